```python
import jax, jax.numpy as jnp
from jax import lax
import numpy as np

D_MODEL = 2048
BATCH = 1
SEQ = 8192
DEPTH = 1
DEC_BATCH = 128
DEC_SEQ = 1
PAST_LEN = 2048
PAGE_SIZE = 128

N_META = 16
ATTN_WIDTH = D_MODEL // 2
POOL_WIDTH = D_MODEL - ATTN_WIDTH
HEAD_DIM = 128
N_HEADS = ATTN_WIDTH // HEAD_DIM
N_KV_HEADS = 2
N_IDX_HEADS = 16
IDX_DIM = 64
TOPK_MAX = 256
POOL_WINDOWS = (2, 4, 8, 16)
N_POOL_GROUPS = len(POOL_WINDOWS)
POOL_GROUP_WIDTH = POOL_WIDTH // N_POOL_GROUPS
POOL_HIST = max(POOL_WINDOWS) - 1
D_FF = 4 * D_MODEL
ROPE_THETA = 10000.0
EPS = 1e-6
Q_BLOCK = 128
NEG = -1e30
SPLIT_SIZES = (N_HEADS * HEAD_DIM, N_KV_HEADS * HEAD_DIM, N_KV_HEADS * HEAD_DIM,
               N_IDX_HEADS * IDX_DIM, IDX_DIM, N_IDX_HEADS, POOL_WIDTH)
IN_WIDTH = sum(SPLIT_SIZES)

kernel_name = "hymba_dsa_pool_decode_step"


def rmsnorm(x, g):
    xf = x.astype(jnp.float32)
    y = xf * lax.rsqrt(jnp.mean(xf * xf, axis=-1, keepdims=True) + EPS)
    return (y * g.astype(jnp.float32)).astype(x.dtype)


def rope(x, pos):
    d = x.shape[-1]
    inv = ROPE_THETA ** (-jnp.arange(0, d, 2, dtype=jnp.float32) / d)
    ang = pos.astype(jnp.float32)[:, None] * inv[None, :]
    cos = jnp.cos(ang)[:, None, :]
    sin = jnp.sin(ang)[:, None, :]
    xf = x.astype(jnp.float32)
    x1, x2 = xf[..., : d // 2], xf[..., d // 2:]
    return jnp.concatenate([x1 * cos - x2 * sin, x1 * sin + x2 * cos], axis=-1).astype(x.dtype)


def mixer_inputs(x, pos, g_mix, w_in):
    b, t, _ = x.shape
    z = rmsnorm(x, g_mix) @ w_in
    idx = [int(v) for v in np.cumsum(SPLIT_SIZES)[:-1]]
    q, k, v, qi, ki, wi, u = jnp.split(z, idx, axis=-1)
    q = rope(q.reshape(b, t, N_HEADS, HEAD_DIM), pos)
    k = rope(k.reshape(b, t, N_KV_HEADS, HEAD_DIM), pos)
    v = v.reshape(b, t, N_KV_HEADS, HEAD_DIM)
    qi = rope(qi.reshape(b, t, N_IDX_HEADS, IDX_DIM), pos)
    ki = rope(ki[:, :, None, :], pos)[:, :, 0, :]
    wi = wi * (N_IDX_HEADS ** -0.5)
    return q, k, v, qi, ki, wi, u


def indexer_scores(q_idx, w_idx, k_idx, q_pos, k_pos):
    dots = jnp.einsum('bthd,bsd->bths', q_idx.astype(jnp.float32), k_idx.astype(jnp.float32))
    s = jnp.einsum('bths,bth->bts', jax.nn.relu(dots) * (IDX_DIM ** -0.5), w_idx.astype(jnp.float32))
    causal = k_pos[None, :] <= q_pos[:, None]
    return jnp.where(causal[None], s, NEG)


def sparse_attend(q, k_sel, v_sel, valid):
    b, t = q.shape[:2]
    qg = q.reshape(b, t, N_KV_HEADS, N_HEADS // N_KV_HEADS, HEAD_DIM).astype(jnp.float32)
    sc = jnp.einsum('btngd,btsnd->btngs', qg, k_sel.astype(jnp.float32)) * (HEAD_DIM ** -0.5)
    sc = jnp.where(valid[:, :, None, None, :], sc, NEG)
    p = jax.nn.softmax(sc, axis=-1)
    o = jnp.einsum('btngs,btsnd->btngd', p, v_sel.astype(jnp.float32))
    return o.reshape(b, t, ATTN_WIDTH).astype(q.dtype)


def gather_rows(a, idx):
    return jax.vmap(lambda ab, ib: ab[ib])(a, idx)


def prompt_attention(q, k, v, q_idx, w_idx, k_idx):
    b, t_len = q.shape[:2]
    n_blk = -(-t_len // Q_BLOCK)
    t_pad = n_blk * Q_BLOCK
    pad = t_pad - t_len
    padt = lambda a: jnp.pad(a, [(0, 0), (0, pad)] + [(0, 0)] * (a.ndim - 2))
    q, k, v, q_idx, w_idx, k_idx = (padt(a) for a in (q, k, v, q_idx, w_idx, k_idx))
    pos_pad = jnp.arange(t_pad, dtype=jnp.int32)
    top_k = min(TOPK_MAX, t_len // 4)

    def to_blocks(a):
        return jnp.moveaxis(a.reshape((b, n_blk, Q_BLOCK) + a.shape[2:]), 1, 0)

    def block(args):
        qb, qib, wb, pb = args
        sc = indexer_scores(qib, wb, k_idx, pb, pos_pad)
        sel = lax.top_k(sc, top_k)[1]
        valid = sel <= pb[None, :, None]
        return sparse_attend(qb, gather_rows(k, sel), gather_rows(v, sel), valid)

    out = lax.map(block, (to_blocks(q), to_blocks(q_idx), to_blocks(w_idx), pos_pad.reshape(n_blk, Q_BLOCK)))
    return jnp.moveaxis(out, 0, 1).reshape(b, t_pad, ATTN_WIDTH)[:, :t_len]


def sample_attention(q, k_new, v_new, q_idx, w_idx, kidx_new, pos, cache_k, cache_v, cache_kidx, page_table):
    db, s_len = q.shape[:2]
    past = page_table.shape[1] * PAGE_SIZE
    L = past + s_len
    kidx_past = cache_kidx[page_table].reshape(db, past, IDX_DIM)
    kidx_all = jnp.concatenate([kidx_past, kidx_new.astype(kidx_past.dtype)], axis=1)
    sc = indexer_scores(q_idx, w_idx, kidx_all, pos, jnp.arange(L, dtype=jnp.int32))
    sel = lax.top_k(sc, min(TOPK_MAX, L // 4))[1]
    valid = sel <= pos[None, :, None]
    in_past = sel < past
    sel_p = jnp.minimum(sel, past - 1)
    phys = jnp.take_along_axis(page_table, (sel_p // PAGE_SIZE).reshape(db, -1), axis=1).reshape(sel.shape)
    off = sel_p % PAGE_SIZE
    sel_n = jnp.clip(sel - past, 0, s_len - 1)

    def pick(cache, new):
        from_past = cache[phys, off]
        from_new = gather_rows(new, sel_n).astype(from_past.dtype)
        return jnp.where(in_past[..., None, None], from_past, from_new)

    return sparse_attend(q, pick(cache_k, k_new), pick(cache_v, v_new), valid)


def pool_mixer(u_ext, pos, w_pool, pool_scale):
    b = u_ext.shape[0]
    t = pos.shape[0]
    uf = u_ext.astype(jnp.float32)
    cs = jnp.concatenate([jnp.zeros((b, 1, POOL_WIDTH), jnp.float32), jnp.cumsum(uf, axis=1)], axis=1)
    end = cs[:, POOL_HIST + 1:]
    cur = uf[:, POOL_HIST:]
    outs = []
    for g, w in enumerate(POOL_WINDOWS):
        sl = slice(g * POOL_GROUP_WIDTH, (g + 1) * POOL_GROUP_WIDTH)
        start = cs[:, POOL_HIST + 1 - w: POOL_HIST + 1 - w + t, sl]
        cnt = jnp.minimum(w, pos + 1).astype(jnp.float32)[None, :, None]
        outs.append((end[..., sl] - start) / cnt - cur[..., sl])
    d = jnp.stack(outs, axis=2)
    y = jnp.einsum('btgc,gcd->btgd', d, w_pool.astype(jnp.float32)).reshape(b, t, POOL_WIDTH)
    return (y * pool_scale.astype(jnp.float32)).astype(u_ext.dtype)


def squared_relu_mlp(h, g_mlp, w_up, w_down):
    a = jax.nn.relu(rmsnorm(h, g_mlp) @ w_up)
    return (a * a) @ w_down


def setup_inputs(seed: int = 0) -> dict:
    key = jax.random.key(seed)
    ks = jax.random.split(key, 17)
    n_pages = PAST_LEN // PAGE_SIZE
    n_used = DEC_BATCH * n_pages
    n_phys = n_used + max(1, n_used // 4)
    nrm = lambda k, shape, scale=1.0: jax.random.normal(k, shape, jnp.float32) * scale
    perm = jax.random.permutation(ks[0], n_phys)
    page_table = perm[:n_used].reshape(DEC_BATCH, n_pages).astype(jnp.int32)
    return {
        "x_prompt": nrm(ks[1], (BATCH, SEQ, D_MODEL)),
        "x_sample": nrm(ks[2], (DEC_BATCH, DEC_SEQ, D_MODEL)),
        "cache_k": nrm(ks[3], (DEPTH, n_phys, PAGE_SIZE, N_KV_HEADS, HEAD_DIM)),
        "cache_v": nrm(ks[4], (DEPTH, n_phys, PAGE_SIZE, N_KV_HEADS, HEAD_DIM)),
        "cache_kidx": nrm(ks[5], (DEPTH, n_phys, PAGE_SIZE, IDX_DIM)),
        "state_pool": nrm(ks[6], (DEPTH, DEC_BATCH, POOL_HIST, POOL_WIDTH)),
        "page_table": page_table,
        "meta_tokens": nrm(ks[7], (N_META, D_MODEL)),
        "g_mix": 1.0 + nrm(ks[8], (DEPTH, D_MODEL), 0.02),
        "w_in": nrm(ks[9], (DEPTH, D_MODEL, IN_WIDTH), D_MODEL ** -0.5),
        "w_pool": nrm(ks[10], (DEPTH, N_POOL_GROUPS, POOL_GROUP_WIDTH, POOL_GROUP_WIDTH), POOL_GROUP_WIDTH ** -0.5),
        "pool_scale": 1.0 + nrm(ks[11], (DEPTH, POOL_WIDTH), 0.1),
        "w_out": nrm(ks[12], (DEPTH, D_MODEL, D_MODEL), D_MODEL ** -0.5),
        "g_mlp": 1.0 + nrm(ks[13], (DEPTH, D_MODEL), 0.02),
        "w_up": nrm(ks[14], (DEPTH, D_MODEL, D_FF), D_MODEL ** -0.5),
        "w_down": nrm(ks[15], (DEPTH, D_FF, D_MODEL), D_FF ** -0.5),
        "g_final": 1.0 + nrm(ks[16], (D_MODEL,), 0.02),
    }


def reference(x_prompt, x_sample, cache_k, cache_v, cache_kidx, state_pool, page_table,
              meta_tokens, g_mix, w_in, w_pool, pool_scale, w_out, g_mlp, w_up, w_down, g_final):
    b = x_prompt.shape[0]
    meta = jnp.broadcast_to(meta_tokens[None].astype(x_prompt.dtype), (b, N_META, D_MODEL))
    hp = jnp.concatenate([meta, x_prompt], axis=1)
    pos_p = jnp.arange(hp.shape[1], dtype=jnp.int32)
    hs = x_sample
    past = page_table.shape[1] * PAGE_SIZE
    pos_s = past + jnp.arange(hs.shape[1], dtype=jnp.int32)
    kp_l, vp_l, kip_l, pp_l, ks_l, vs_l, kis_l, ps_l = [], [], [], [], [], [], [], []
    for l in range(DEPTH):
        q, k, v, qi, ki, wi, u = mixer_inputs(hp, pos_p, g_mix[l], w_in[l])
        att = prompt_attention(q, k, v, qi, wi, ki)
        u_ext = jnp.concatenate([jnp.zeros((b, POOL_HIST, POOL_WIDTH), u.dtype), u], axis=1)
        pool = pool_mixer(u_ext, pos_p, w_pool[l], pool_scale[l])
        hp = hp + jnp.concatenate([att, pool], axis=-1) @ w_out[l]
        hp = hp + squared_relu_mlp(hp, g_mlp[l], w_up[l], w_down[l])
        kp_l.append(k); vp_l.append(v); kip_l.append(ki); pp_l.append(u_ext[:, -POOL_HIST:])
        q, k, v, qi, ki, wi, u = mixer_inputs(hs, pos_s, g_mix[l], w_in[l])
        att = sample_attention(q, k, v, qi, wi, ki, pos_s, cache_k[l], cache_v[l], cache_kidx[l], page_table)
        u_ext = jnp.concatenate([state_pool[l].astype(u.dtype), u], axis=1)
        pool = pool_mixer(u_ext, pos_s, w_pool[l], pool_scale[l])
        hs = hs + jnp.concatenate([att, pool], axis=-1) @ w_out[l]
        hs = hs + squared_relu_mlp(hs, g_mlp[l], w_up[l], w_down[l])
        ks_l.append(k); vs_l.append(v); kis_l.append(ki); ps_l.append(u_ext[:, -POOL_HIST:])
    y_prompt = rmsnorm(hp, g_final)[:, N_META:]
    y_sample = rmsnorm(hs, g_final)
    return (y_prompt, y_sample,
            jnp.stack(kp_l), jnp.stack(vp_l), jnp.stack(kip_l), jnp.stack(pp_l),
            jnp.stack(ks_l), jnp.stack(vs_l), jnp.stack(kis_l), jnp.stack(ps_l))
```

```python
import functools

import jax
import jax.numpy as jnp
from jax import lax
from jax.experimental import pallas as pl
from jax.experimental.pallas import tpu as pltpu

F32 = jnp.float32
BF16 = jnp.bfloat16

D_MODEL = 2048
N_META = 16
ATTN_WIDTH = 1024
POOL_WIDTH = 1024
HEAD_DIM = 128
N_HEADS = 8
N_KV_HEADS = 2
GROUP = N_HEADS // N_KV_HEADS
N_IDX_HEADS = 16
IDX_DIM = 64
TOPK = 256
POOL_WINDOWS = (2, 4, 8, 16)
POOL_GROUP_WIDTH = 256
POOL_HIST = 15
D_FF = 8192
PAGE_SIZE = 128
ROPE_THETA = 10000.0
EPS = 1e-6
NEG = -1e30
NEG_INF = float("-inf")
POS_INF = float("inf")
F32_LOWEST = -3.0e38

LANES = 128
VMEM_LIMIT = 56 * 1024 * 1024

Q_BLOCK = 128
KEY_CHUNK = 256
MAX_BISECT = 320


def _const_spec(shape):
    nd = len(shape)
    return pl.BlockSpec(shape, lambda *_: (0,) * nd, pipeline_mode=pl.Buffered(1))


def _rope128(x, c, s):
    return x * c + pltpu.roll(x, 64, 1) * s


def _rope64(x, c, sa, sb):
    return x * c + pltpu.roll(x, 96, 1) * sa + pltpu.roll(x, 32, 1) * sb


def _proj_kernel(x_ref, g_ref, c128_ref, s128_ref, c64_ref, sa64_ref, sb64_ref,
                 wq_ref, wkv_ref, wqi_ref, wkw_ref, wu_ref,
                 q_ref, kf_ref, vf_ref, kb_ref, vb_ref, qi_ref, kw_ref, u_ref):
    x = x_ref[...]
    ms = jnp.mean(x * x, axis=-1, keepdims=True)
    xn = ((x * lax.rsqrt(ms + EPS)) * g_ref[...]).astype(BF16)
    c128 = c128_ref[...]
    s128 = s128_ref[...]
    c64 = c64_ref[...]
    sa64 = sa64_ref[...]
    sb64 = sb64_ref[...]

    zq = jnp.dot(xn, wq_ref[...], preferred_element_type=F32)
    for h in range(N_HEADS):
        sl = slice(h * LANES, (h + 1) * LANES)
        q_ref[:, sl] = _rope128(zq[:, sl], c128, s128).astype(BF16)

    zkv = jnp.dot(xn, wkv_ref[...], preferred_element_type=F32)
    for h in range(N_KV_HEADS):
        sl = slice(h * LANES, (h + 1) * LANES)
        kr = _rope128(zkv[:, sl], c128, s128)
        kf_ref[:, sl] = kr
        kb_ref[:, sl] = kr.astype(BF16)
    v = zkv[:, N_KV_HEADS * LANES:]
    vf_ref[...] = v
    vb_ref[...] = v.astype(BF16)

    zqi = jnp.dot(xn, wqi_ref[...], preferred_element_type=F32)
    for p in range(N_IDX_HEADS // 2):
        sl = slice(p * LANES, (p + 1) * LANES)
        qi_ref[:, sl] = _rope64(zqi[:, sl], c64, sa64, sb64).astype(BF16)

    zkw = jnp.dot(xn, wkw_ref[...], preferred_element_type=F32)
    lane = lax.broadcasted_iota(jnp.int32, zkw.shape, 1)
    is_key = lane < IDX_DIM
    ckw = jnp.where(is_key, c64, jnp.where(lane < IDX_DIM + N_IDX_HEADS, N_IDX_HEADS ** -0.5, 1.0))
    kw_ref[...] = _rope64(zkw, ckw, jnp.where(is_key, sa64, 0.0), jnp.where(is_key, sb64, 0.0))

    u_ref[...] = jnp.dot(xn, wu_ref[...], preferred_element_type=F32)


def _project(x, pos, g_mix, w_parts, tm):
    rows = x.shape[0]
    assert rows % tm == 0
    c128, s128, c64, sa64, sb64 = _rope_tables(pos)
    wq, wkv, wqi, wkw, wu = w_parts
    row_spec = lambda w: pl.BlockSpec((tm, w), lambda i: (i, 0))
    out_shapes = (
        jax.ShapeDtypeStruct((rows, ATTN_WIDTH), BF16),
        jax.ShapeDtypeStruct((rows, 256), F32),
        jax.ShapeDtypeStruct((rows, 256), F32),
        jax.ShapeDtypeStruct((rows, 256), BF16),
        jax.ShapeDtypeStruct((rows, 256), BF16),
        jax.ShapeDtypeStruct((rows, 1024), BF16),
        jax.ShapeDtypeStruct((rows, LANES), F32),
        jax.ShapeDtypeStruct((rows, POOL_WIDTH), F32),
    )
    return pl.pallas_call(
        _proj_kernel,
        grid=(rows // tm,),
        in_specs=[row_spec(D_MODEL), _const_spec((1, D_MODEL))]
        + [row_spec(LANES)] * 5
        + [_const_spec(w.shape) for w in w_parts],
        out_specs=tuple(row_spec(s.shape[1]) for s in out_shapes),
        out_shape=out_shapes,
        compiler_params=pltpu.CompilerParams(
            dimension_semantics=("arbitrary",), vmem_limit_bytes=VMEM_LIMIT),
        name="proj",
    )(x, g_mix.reshape(1, D_MODEL), c128, s128, c64, sa64, sb64, wq, wkv, wqi, wkw, wu)


def _rope_tables(pos):
    posf = pos.astype(F32)[:, None]
    inv = ROPE_THETA ** (-jnp.arange(0, HEAD_DIM, 2, dtype=F32) / HEAD_DIM)
    ang = posf * inv[None, :]
    c, s = jnp.cos(ang), jnp.sin(ang)
    c128 = jnp.concatenate([c, c], axis=-1)
    s128 = jnp.concatenate([-s, s], axis=-1)
    inv = ROPE_THETA ** (-jnp.arange(0, IDX_DIM, 2, dtype=F32) / IDX_DIM)
    ang = posf * inv[None, :]
    c, s = jnp.cos(ang), jnp.sin(ang)
    z = jnp.zeros_like(s)
    c64 = jnp.concatenate([c, c, c, c], axis=-1)
    sa64 = jnp.concatenate([-s, z, -s, z], axis=-1)
    sb64 = jnp.concatenate([z, s, z, s], axis=-1)
    return c128, s128, c64, sa64, sb64


def _row_count(s_ref, nchunks, chunk, pred):
    rows = s_ref.shape[0]

    def body(c, part):
        c0 = pl.multiple_of(c * chunk, chunk)
        hit = pred(s_ref[:, pl.ds(c0, chunk)], c0).astype(F32)
        for j in range(chunk // LANES):
            part = part + hit[:, j * LANES:(j + 1) * LANES]
        return part

    part = lax.fori_loop(0, nchunks, body, jnp.zeros((rows, LANES), F32))
    return jnp.sum(part, axis=-1, keepdims=True)


def _topk_threshold(s_ref, nchunks, chunk, n_valid, row_min, row_max):
    rows = s_ref.shape[0]
    kf = float(TOPK)
    take_all = n_valid <= kf

    def count_ge(t):
        return _row_count(s_ref, nchunks, chunk, lambda v, c0: v >= t)

    def cond(st):
        it, _, _, done, _ = st
        return jnp.logical_and(it < MAX_BISECT, jnp.min(done) < 0.5)

    def body(st):
        it, lo, hi, done, tau = st
        mid = lo + (hi - lo) * 0.5
        collapsed = jnp.logical_or(mid <= lo, mid >= hi)
        cnt = count_ge(mid)
        found = cnt == kf
        active = done < 0.5
        go_up = jnp.logical_and(active, cnt > kf)
        go_dn = jnp.logical_and(active, cnt < kf)
        lo = jnp.where(go_up, mid, lo)
        hi = jnp.where(go_dn, mid, hi)
        tau = jnp.where(jnp.logical_and(active, found), mid, tau)
        done = jnp.where(jnp.logical_or(found, collapsed), 1.0, done)
        return it + 1, lo, hi, done, tau

    done0 = jnp.where(take_all, 1.0, 0.0).astype(F32)
    tau0 = jnp.full((rows, 1), POS_INF, F32)
    _, lo, hi, _, tau = lax.while_loop(
        cond, body, (jnp.int32(0), row_min, row_max, done0, tau0))

    unresolved = jnp.logical_and(tau == POS_INF, jnp.logical_not(take_all))
    n_unres = jnp.sum(unresolved.astype(F32))
    tau = jnp.where(take_all, F32_LOWEST, tau)

    def tie_path(tau):
        cut = jnp.where(count_ge(hi) >= kf, hi, lo)
        cut = jnp.where(unresolved, cut, POS_INF)
        c_gt = _row_count(s_ref, nchunks, chunk, lambda v, c0: v > cut)
        keep = kf - c_gt

        def col_iota(c0):
            return c0 + lax.broadcasted_iota(jnp.int32, (rows, chunk), 1)

        def idx_body(_, st):
            xlo, xhi = st
            xmid = (xlo + xhi) // 2
            cnt = _row_count(
                s_ref, nchunks, chunk,
                lambda v, c0: jnp.logical_and(v == cut, col_iota(c0) < xmid))
            ok = cnt >= keep
            return jnp.where(ok, xlo, xmid), jnp.where(ok, xmid, xhi)

        xlo0 = jnp.zeros((rows, 1), jnp.int32)
        xhi0 = jnp.full((rows, 1), nchunks * chunk, jnp.int32)
        n_idx_steps = 15
        _, xcut = lax.fori_loop(0, n_idx_steps, idx_body, (xlo0, xhi0))

        def drop_body(c, carry):
            c0 = pl.multiple_of(c * chunk, chunk)
            v = s_ref[:, pl.ds(c0, chunk)]
            drop = jnp.logical_and(v == cut, col_iota(c0) >= xcut)
            s_ref[:, pl.ds(c0, chunk)] = jnp.where(drop, NEG_INF, v)
            return carry

        lax.fori_loop(0, nchunks, drop_body, 0)
        return jnp.where(unresolved, cut, tau)

    return lax.cond(n_unres > 0.0, tie_path, lambda t: t, tau)


def _prompt_attn_kernel(qi_ref, kw_ref, q_ref, kilo_ref, kihi_ref, k_ref, v_ref, o_ref,
                        s_ref, wb_ref, m_ref, l_ref, acc_ref):
    i = pl.program_id(0)
    tq, ck = Q_BLOCK, KEY_CHUNK
    n_blocks = i + 2
    nchunks = (n_blocks * Q_BLOCK + ck - 1) // ck

    for h in range(N_IDX_HEADS):
        wcol = kw_ref[:, IDX_DIM + h:IDX_DIM + h + 1] * (IDX_DIM ** -0.5)
        wb_ref[h] = jnp.broadcast_to(wcol, (tq, LANES))

    qcat = jnp.concatenate(
        [qi_ref[:, p * LANES:(p + 1) * LANES] for p in range(N_IDX_HEADS // 2)], axis=0)
    row = lax.broadcasted_iota(jnp.int32, (tq, ck), 0)
    lane = lax.broadcasted_iota(jnp.int32, (tq, ck), 1)
    q_idx = i * tq + row
    contract_last = (((1,), (1,)), ((), ()))

    def idx_body(c, st):
        rmin, rmax = st
        c0 = pl.multiple_of(c * ck, ck)
        dlo = lax.dot_general(qcat, kilo_ref[pl.ds(c0, ck), :], contract_last,
                              preferred_element_type=F32)
        dhi = lax.dot_general(qcat, kihi_ref[pl.ds(c0, ck), :], contract_last,
                              preferred_element_type=F32)
        halves = []
        for j in range(ck // LANES):
            cs = slice(j * LANES, (j + 1) * LANES)
            acc = jnp.zeros((tq, LANES), F32)
            for p in range(N_IDX_HEADS // 2):
                rs = slice(p * tq, (p + 1) * tq)
                acc = acc + wb_ref[2 * p] * jnp.maximum(dlo[rs, cs], 0.0)
                acc = acc + wb_ref[2 * p + 1] * jnp.maximum(dhi[rs, cs], 0.0)
            halves.append(acc)
        sc = jnp.concatenate(halves, axis=1)
        col = c0 + lane
        valid = jnp.logical_or(
            col < N_META,
            jnp.logical_and(col >= Q_BLOCK, col - Q_BLOCK <= q_idx))
        s_ref[:, pl.ds(c0, ck)] = jnp.where(valid, sc, NEG_INF)
        rmin = jnp.minimum(rmin, jnp.min(jnp.where(valid, sc, POS_INF), axis=-1, keepdims=True))
        rmax = jnp.maximum(rmax, jnp.max(jnp.where(valid, sc, NEG_INF), axis=-1, keepdims=True))
        return rmin, rmax

    rmin, rmax = lax.fori_loop(
        0, nchunks, idx_body,
        (jnp.full((tq, 1), POS_INF, F32), jnp.full((tq, 1), NEG_INF, F32)))

    n_valid = (N_META + 1 + i * tq + lax.broadcasted_iota(jnp.int32, (tq, 1), 0)).astype(F32)
    tau = _topk_threshold(s_ref, nchunks, ck, n_valid, rmin, rmax)

    m_ref[...] = jnp.full(m_ref.shape, NEG, F32)
    l_ref[...] = jnp.zeros(l_ref.shape, F32)
    acc_ref[...] = jnp.zeros(acc_ref.shape, F32)
    scale = HEAD_DIM ** -0.5

    def att_body(c, carry):
        c0 = pl.multiple_of(c * ck, ck)
        bias = jnp.where(s_ref[:, pl.ds(c0, ck)] >= tau, 0.0, NEG)
        bias = jnp.concatenate([bias] * GROUP, axis=0)
        for n in range(N_KV_HEADS):
            qg = jnp.concatenate(
                [q_ref[:, (n * GROUP + g) * LANES:(n * GROUP + g + 1) * LANES]
                 for g in range(GROUP)], axis=0)
            kc = k_ref[pl.ds(c0, ck), n * LANES:(n + 1) * LANES]
            vc = v_ref[pl.ds(c0, ck), n * LANES:(n + 1) * LANES]
            sc = lax.dot_general(qg, kc, contract_last, preferred_element_type=F32)
            sc = sc * scale + bias
            m_old = m_ref[n]
            m_new = jnp.maximum(m_old, jnp.max(sc, axis=-1, keepdims=True))
            alpha = jnp.exp(m_old - m_new)
            p = jnp.exp(sc - m_new)
            l_ref[n] = alpha * l_ref[n] + jnp.sum(p, axis=-1, keepdims=True)
            acc_ref[n] = alpha * acc_ref[n] + jnp.dot(
                p.astype(BF16), vc, preferred_element_type=F32)
            m_ref[n] = m_new
        return carry

    lax.fori_loop(0, nchunks, att_body, 0)

    for n in range(N_KV_HEADS):
        o = acc_ref[n] / l_ref[n]
        for g in range(GROUP):
            h = n * GROUP + g
            o_ref[:, h * LANES:(h + 1) * LANES] = o[g * tq:(g + 1) * tq].astype(o_ref.dtype)


def _prompt_attention(qi, kw, q, ki_lo, ki_hi, k_all, v_all):
    rows = q.shape[0]
    nk = k_all.shape[0]
    assert rows % Q_BLOCK == 0 and nk % KEY_CHUNK == 0
    assert nk >= (rows // Q_BLOCK + 1) * Q_BLOCK
    row_spec = lambda w: pl.BlockSpec((Q_BLOCK, w), lambda i: (i, 0))
    return pl.pallas_call(
        _prompt_attn_kernel,
        grid=(rows // Q_BLOCK,),
        in_specs=[row_spec(1024), row_spec(LANES), row_spec(ATTN_WIDTH),
                  _const_spec(ki_lo.shape), _const_spec(ki_hi.shape),
                  _const_spec(k_all.shape), _const_spec(v_all.shape)],
        out_specs=row_spec(ATTN_WIDTH),
        out_shape=jax.ShapeDtypeStruct((rows, ATTN_WIDTH), BF16),
        scratch_shapes=[
            pltpu.VMEM((Q_BLOCK, nk), F32),
            pltpu.VMEM((N_IDX_HEADS, Q_BLOCK, LANES), F32),
            pltpu.VMEM((N_KV_HEADS, GROUP * Q_BLOCK, 1), F32),
            pltpu.VMEM((N_KV_HEADS, GROUP * Q_BLOCK, 1), F32),
            pltpu.VMEM((N_KV_HEADS, GROUP * Q_BLOCK, HEAD_DIM), F32),
        ],
        compiler_params=pltpu.CompilerParams(
            dimension_semantics=("arbitrary",), vmem_limit_bytes=VMEM_LIMIT),
        name="prompt_attn",
    )(qi, kw, q, ki_lo, ki_hi, k_all, v_all)


def _decode_scores_kernel(pt_ref, qi_ref, w_ref, kinew_ref, *rest):
    n_pages = len(rest) - 1
    page_refs, s_ref = rest[:n_pages], rest[n_pages]
    qi = qi_ref[...]
    w = w_ref[...] * (IDX_DIM ** -0.5)
    contract_last = (((1,), (1,)), ((), ()))
    for j in range(n_pages):
        kp = page_refs[j][...].astype(BF16)
        d = lax.dot_general(qi, kp, contract_last, preferred_element_type=F32)
        s_ref[:, j * PAGE_SIZE:(j + 1) * PAGE_SIZE] = jnp.sum(
            jnp.maximum(d, 0.0) * w, axis=0, keepdims=True)
    d_new = jnp.sum(qi.astype(F32) * kinew_ref[...].astype(F32), axis=-1, keepdims=True)
    s_new = jnp.sum(jnp.maximum(d_new, 0.0) * w, axis=0, keepdims=True)
    past = n_pages * PAGE_SIZE
    tail = s_ref.shape[1] - past
    lane = lax.broadcasted_iota(jnp.int32, (1, tail), 1)
    s_ref[:, past:] = jnp.where(lane == 0, s_new, NEG_INF)


def _decode_scores(page_table, qi_s, w_s, kinew_s, cache_kidx, width):
    db, n_pages = page_table.shape
    page_specs = [
        pl.BlockSpec((None, PAGE_SIZE, IDX_DIM), functools.partial(
            lambda b, pt, j: (pt[b, j], 0, 0), j=j)) for j in range(n_pages)]
    seq_spec = lambda a, c: pl.BlockSpec((None, a, c), lambda b, pt: (b, 0, 0))
    return pl.pallas_call(
        _decode_scores_kernel,
        grid_spec=pltpu.PrefetchScalarGridSpec(
            num_scalar_prefetch=1,
            grid=(db,),
            in_specs=[seq_spec(N_IDX_HEADS, IDX_DIM), seq_spec(N_IDX_HEADS, 1),
                      seq_spec(1, IDX_DIM)] + page_specs,
            out_specs=seq_spec(1, width),
        ),
        out_shape=jax.ShapeDtypeStruct((db, 1, width), F32),
        compiler_params=pltpu.CompilerParams(dimension_semantics=("arbitrary",)),
        name="decode_scores",
    )(page_table, qi_s, w_s, kinew_s, *([cache_kidx] * n_pages))


def _decode_mask_kernel(s_in_ref, bias_ref, s_ref):
    rows, width = s_in_ref.shape
    chunk = LANES
    nchunks = width // chunk
    s = s_in_ref[...]
    s_ref[...] = s
    valid = s > NEG_INF
    rmin = jnp.min(jnp.where(valid, s, POS_INF), axis=-1, keepdims=True)
    rmax = jnp.max(s, axis=-1, keepdims=True)
    n_valid = jnp.sum(valid.astype(F32), axis=-1, keepdims=True)
    tau = _topk_threshold(s_ref, nchunks, chunk, n_valid, rmin, rmax)
    bias_ref[...] = jnp.where(s_ref[...] >= tau, 0.0, NEG)


def _decode_mask(scores):
    return pl.pallas_call(
        _decode_mask_kernel,
        out_shape=jax.ShapeDtypeStruct(scores.shape, F32),
        scratch_shapes=[pltpu.VMEM(scores.shape, F32)],
        name="decode_mask",
    )(scores)


def _decode_attn_kernel(pt_ref, q_ref, bias_ref, knew_ref, vnew_ref, *rest):
    n_pages = (len(rest) - 1) // 2
    k_pages, v_pages, o_ref = rest[:n_pages], rest[n_pages:2 * n_pages], rest[2 * n_pages]
    past = n_pages * PAGE_SIZE
    qbd = q_ref[...]
    contract_last = (((1,), (1,)), ((), ()))
    k_all = jnp.concatenate([r[...].astype(BF16) for r in k_pages], axis=0)
    v_all = jnp.concatenate([r[...].astype(BF16) for r in v_pages], axis=0)
    scale = HEAD_DIM ** -0.5
    bias = bias_ref[...]
    sc = lax.dot_general(qbd, k_all, contract_last, preferred_element_type=F32)
    sc = sc * scale + bias[:, :past]
    sc_new = jnp.sum(qbd.astype(F32) * knew_ref[...].astype(F32), axis=-1, keepdims=True)
    sc_new = sc_new * scale + bias[:, past:past + 1]
    m = jnp.maximum(jnp.max(sc, axis=-1, keepdims=True), sc_new)
    p = jnp.exp(sc - m)
    p_new = jnp.exp(sc_new - m)
    denom = jnp.sum(p, axis=-1, keepdims=True) + p_new
    o = jnp.dot(p.astype(BF16), v_all, preferred_element_type=F32)
    o = o + p_new.astype(BF16).astype(F32) * vnew_ref[...].astype(F32)
    o = o / denom
    head = lax.broadcasted_iota(jnp.int32, (N_HEADS, HEAD_DIM), 0)
    o_ref[...] = jnp.where(head < GROUP, o[:, :HEAD_DIM], o[:, HEAD_DIM:]).astype(o_ref.dtype)


def _decode_attention(page_table, qbd_s, bias, knew_s, vnew_s, cache_k, cache_v):
    db, n_pages = page_table.shape
    width = bias.shape[-1]
    kvw = N_KV_HEADS * HEAD_DIM
    page_specs = [
        pl.BlockSpec((None, PAGE_SIZE, kvw), functools.partial(
            lambda b, pt, j: (pt[b, j], 0, 0), j=j)) for j in range(n_pages)]
    seq_spec = lambda a, c: pl.BlockSpec((None, a, c), lambda b, pt: (b, 0, 0))
    return pl.pallas_call(
        _decode_attn_kernel,
        grid_spec=pltpu.PrefetchScalarGridSpec(
            num_scalar_prefetch=1,
            grid=(db,),
            in_specs=[seq_spec(N_HEADS, kvw), seq_spec(1, width), seq_spec(1, kvw),
                      seq_spec(1, kvw)] + page_specs + page_specs,
            out_specs=seq_spec(N_HEADS, HEAD_DIM),
        ),
        out_shape=jax.ShapeDtypeStruct((db, N_HEADS, HEAD_DIM), BF16),
        compiler_params=pltpu.CompilerParams(
            dimension_semantics=("arbitrary",), vmem_limit_bytes=VMEM_LIMIT),
        name="decode_attn",
    )(page_table, qbd_s, bias, knew_s, vnew_s, *([cache_k] * n_pages), *([cache_v] * n_pages))


POOL_TILE = 512
POOL_HALO = 16


def _pool_prompt_kernel(u_ref, uprev_ref, umeta_ref, d_ref, ubuf):
    i = pl.program_id(0)
    tp = u_ref.shape[0]
    ubuf[0:POOL_HALO, :] = jnp.where(i == 0, umeta_ref[...], uprev_ref[...])
    ubuf[POOL_HALO:, :] = u_ref[...]
    for g, w in enumerate(POOL_WINDOWS):
        cols = slice(g * POOL_GROUP_WIDTH, (g + 1) * POOL_GROUP_WIDTH)
        cur = ubuf[POOL_HALO:POOL_HALO + tp, cols]
        tot = cur
        for j in range(1, w):
            tot = tot + ubuf[POOL_HALO - j:POOL_HALO - j + tp, cols]
        d_ref[:, cols] = tot * (1.0 / w) - cur


def _pool_prompt(u, u_meta):
    rows = u.shape[0]
    tp = POOL_TILE
    per = tp // POOL_HALO
    return pl.pallas_call(
        _pool_prompt_kernel,
        grid=(rows // tp,),
        in_specs=[pl.BlockSpec((tp, POOL_WIDTH), lambda i: (i, 0)),
                  pl.BlockSpec((POOL_HALO, POOL_WIDTH), lambda i: (jnp.maximum(i * per - 1, 0), 0)),
                  _const_spec((POOL_HALO, POOL_WIDTH))],
        out_specs=pl.BlockSpec((tp, POOL_WIDTH), lambda i: (i, 0)),
        out_shape=jax.ShapeDtypeStruct((rows, POOL_WIDTH), F32),
        scratch_shapes=[pltpu.VMEM((tp + POOL_HALO, POOL_WIDTH), F32)],
        compiler_params=pltpu.CompilerParams(dimension_semantics=("arbitrary",)),
        name="pool_prompt",
    )(u, u, u_meta)


def _pool_decode_kernel(hist_ref, u_ref, d_ref):
    for g, w in enumerate(POOL_WINDOWS):
        cols = slice(g * POOL_GROUP_WIDTH, (g + 1) * POOL_GROUP_WIDTH)
        cur = u_ref[:, cols]
        tot = cur
        for j in range(1, w):
            tot = tot + hist_ref[POOL_HIST - j, :, cols]
        d_ref[:, cols] = tot * (1.0 / w) - cur


def _pool_decode(hist, u):
    return pl.pallas_call(
        _pool_decode_kernel,
        out_shape=jax.ShapeDtypeStruct(u.shape, F32),
        name="pool_decode",
    )(hist, u)


FF_TILE = 512


def _rmsnorm(x, g):
    ms = jnp.mean(x * x, axis=-1, keepdims=True)
    return (x * lax.rsqrt(ms + EPS)) * g


def _post_kernel(x_ref, att_ref, d_ref, woa_ref, wop_ref, wpool_ref, pscale_ref,
                 gmlp_ref, gfin_ref, wup_ref, wdn_ref, y_ref, h_ref, hn_ref, acc_ref):
    f = pl.program_id(1)

    @pl.when(f == 0)
    def _():
        d = d_ref[...]
        pooled = jnp.concatenate(
            [jnp.dot(d[:, g * POOL_GROUP_WIDTH:(g + 1) * POOL_GROUP_WIDTH].astype(BF16),
                     wpool_ref[g], preferred_element_type=F32)
             for g in range(len(POOL_WINDOWS))], axis=1) * pscale_ref[...]
        h = (x_ref[...]
             + jnp.dot(att_ref[...], woa_ref[...], preferred_element_type=F32)
             + jnp.dot(pooled.astype(BF16), wop_ref[...], preferred_element_type=F32))
        h_ref[...] = h
        hn_ref[...] = _rmsnorm(h, gmlp_ref[...]).astype(BF16)
        acc_ref[...] = jnp.zeros(acc_ref.shape, F32)

    a = jnp.maximum(jnp.dot(hn_ref[...], wup_ref[...], preferred_element_type=F32), 0.0)
    acc_ref[...] += jnp.dot((a * a).astype(BF16), wdn_ref[...], preferred_element_type=F32)

    @pl.when(f == pl.num_programs(1) - 1)
    def _():
        y_ref[...] = _rmsnorm(h_ref[...] + acc_ref[...], gfin_ref[...])


def _post(x, att, d, wo_a, wo_p, w_pool, pool_scale, g_mlp, g_final, w_up, w_down, tm):
    rows = x.shape[0]
    assert rows % tm == 0 and D_FF % FF_TILE == 0
    row_spec = lambda w: pl.BlockSpec((tm, w), lambda i, f: (i, 0))
    return pl.pallas_call(
        _post_kernel,
        grid=(rows // tm, D_FF // FF_TILE),
        in_specs=[row_spec(D_MODEL), row_spec(ATTN_WIDTH), row_spec(POOL_WIDTH),
                  _const_spec(wo_a.shape), _const_spec(wo_p.shape), _const_spec(w_pool.shape),
                  _const_spec((1, POOL_WIDTH)), _const_spec((1, D_MODEL)), _const_spec((1, D_MODEL)),
                  pl.BlockSpec((D_MODEL, FF_TILE), lambda i, f: (0, f)),
                  pl.BlockSpec((FF_TILE, D_MODEL), lambda i, f: (f, 0))],
        out_specs=row_spec(D_MODEL),
        out_shape=jax.ShapeDtypeStruct((rows, D_MODEL), F32),
        scratch_shapes=[pltpu.VMEM((tm, D_MODEL), F32), pltpu.VMEM((tm, D_MODEL), BF16),
                        pltpu.VMEM((tm, D_MODEL), F32)],
        compiler_params=pltpu.CompilerParams(
            dimension_semantics=("arbitrary", "arbitrary"), vmem_limit_bytes=VMEM_LIMIT),
        name="post",
    )(x, att, d, wo_a, wo_p, w_pool, pool_scale.reshape(1, POOL_WIDTH),
      g_mlp.reshape(1, D_MODEL), g_final.reshape(1, D_MODEL), w_up, w_down)


def kernel(x_prompt, x_sample, cache_k, cache_v, cache_kidx, state_pool, page_table, meta_tokens,
           g_mix, w_in, w_pool, pool_scale, w_out, g_mlp, w_up, w_down, g_final):
    assert x_prompt.shape[0] == 1 and x_sample.shape[1] == 1 and g_mix.shape[0] == 1
    seq = x_prompt.shape[1]
    db = x_sample.shape[0]
    n_pages = page_table.shape[1]
    past = n_pages * PAGE_SIZE
    n_phys = cache_k.shape[1]

    w = w_in[0]
    o = 0
    parts = []
    for width in (ATTN_WIDTH, 2 * N_KV_HEADS * HEAD_DIM, N_IDX_HEADS * IDX_DIM,
                  IDX_DIM + N_IDX_HEADS, POOL_WIDTH):
        parts.append(w[:, o:o + width].astype(BF16))
        o += width
    parts[3] = jnp.pad(parts[3], ((0, 0), (0, LANES - parts[3].shape[1])))
    wo_a = w_out[0, :ATTN_WIDTH].astype(BF16)
    wo_p = w_out[0, ATTN_WIDTH:].astype(BF16)
    w_pool_b = w_pool[0].astype(BF16)
    w_up_b = w_up[0].astype(BF16)
    w_down_b = w_down[0].astype(BF16)

    xp = x_prompt[0]
    pos_p = N_META + jnp.arange(seq, dtype=jnp.int32)
    q_p, kf_p, vf_p, kb_p, vb_p, qi_p, kw_p, u_p = _project(xp, pos_p, g_mix[0], parts, 256)
    xs = jnp.concatenate([meta_tokens.astype(F32), x_sample[:, 0]], axis=0)
    pos_s = jnp.concatenate([jnp.arange(N_META, dtype=jnp.int32),
                             jnp.full((db,), past, jnp.int32)])
    q_s, kf_s, vf_s, kb_s, vb_s, qi_s, kw_s, u_s = _project(xs, pos_s, g_mix[0], parts, xs.shape[0])

    nk = -(-(seq + 2 * Q_BLOCK) // KEY_CHUNK) * KEY_CHUNK

    def key_rows(meta_part, prompt_part):
        width = meta_part.shape[1]
        return jnp.concatenate(
            [meta_part, jnp.zeros((Q_BLOCK - N_META, width), meta_part.dtype), prompt_part,
             jnp.zeros((nk - Q_BLOCK - seq, width), meta_part.dtype)], axis=0)

    ki_all = key_rows(kw_s[:N_META, :IDX_DIM], kw_p[:, :IDX_DIM]).astype(BF16)
    zk = jnp.zeros_like(ki_all)
    ki_lo = jnp.concatenate([ki_all, zk], axis=1)
    ki_hi = jnp.concatenate([zk, ki_all], axis=1)
    k_all = key_rows(kb_s[:N_META], kb_p)
    v_all = key_rows(vb_s[:N_META], vb_p)
    att_p = _prompt_attention(qi_p, kw_p, q_p, ki_lo, ki_hi, k_all, v_all)

    width = past + 2 * LANES
    qi_d = qi_s[N_META:].reshape(db, N_IDX_HEADS, IDX_DIM)
    w_d = kw_s[N_META:, IDX_DIM:IDX_DIM + N_IDX_HEADS].reshape(db, N_IDX_HEADS, 1)
    kinew = kw_s[N_META:, :IDX_DIM].astype(BF16).reshape(db, 1, IDX_DIM)
    scores = _decode_scores(page_table, qi_d, w_d, kinew,
                            cache_kidx[0], width)
    bias = _decode_mask(scores.reshape(db, width)).reshape(db, 1, width)
    q_d = q_s[N_META:].reshape(db, N_HEADS, HEAD_DIM)
    zq = jnp.zeros_like(q_d)
    top = lax.broadcasted_iota(jnp.int32, q_d.shape, 1) < GROUP
    qbd = jnp.concatenate([jnp.where(top, q_d, zq), jnp.where(top, zq, q_d)], axis=-1)
    kvw = N_KV_HEADS * HEAD_DIM
    att_s = _decode_attention(
        page_table, qbd, bias, kb_s[N_META:].reshape(db, 1, kvw), vb_s[N_META:].reshape(db, 1, kvw),
        cache_k[0].reshape(n_phys, PAGE_SIZE, kvw), cache_v[0].reshape(n_phys, PAGE_SIZE, kvw))
    att_s = att_s.reshape(db, ATTN_WIDTH)

    d_p = _pool_prompt(u_p, u_s[:N_META])
    d_s = _pool_decode(jnp.swapaxes(state_pool[0], 0, 1), u_s[N_META:])

    post = functools.partial(
        _post, wo_a=wo_a, wo_p=wo_p, w_pool=w_pool_b, pool_scale=pool_scale[0],
        g_mlp=g_mlp[0], g_final=g_final, w_up=w_up_b, w_down=w_down_b)
    y_p = post(xp, att_p, d_p, tm=512)
    y_s = post(x_sample[:, 0], att_s, d_s, tm=db)

    k_prompt = jnp.concatenate([kf_s[:N_META], kf_p], axis=0).reshape(1, 1, N_META + seq, N_KV_HEADS, HEAD_DIM)
    v_prompt = jnp.concatenate([vf_s[:N_META], vf_p], axis=0).reshape(1, 1, N_META + seq, N_KV_HEADS, HEAD_DIM)
    kidx_prompt = jnp.concatenate([kw_s[:N_META, :IDX_DIM], kw_p[:, :IDX_DIM]], axis=0).reshape(
        1, 1, N_META + seq, IDX_DIM)
    pool_prompt = u_p[seq - POOL_HIST:].reshape(1, 1, POOL_HIST, POOL_WIDTH)
    k_sample = kf_s[N_META:].reshape(1, db, 1, N_KV_HEADS, HEAD_DIM)
    v_sample = vf_s[N_META:].reshape(1, db, 1, N_KV_HEADS, HEAD_DIM)
    kidx_sample = kw_s[N_META:, :IDX_DIM].reshape(1, db, 1, IDX_DIM)
    pool_sample = jnp.concatenate(
        [state_pool[0][:, 1:], u_s[N_META:][:, None, :]], axis=1)[None]
    return (y_p.reshape(1, seq, D_MODEL), y_s.reshape(db, 1, D_MODEL),
            k_prompt, v_prompt, kidx_prompt, pool_prompt,
            k_sample, v_sample, kidx_sample, pool_sample)
```

```python
import functools

import jax
import jax.numpy as jnp
from jax import lax
from jax.experimental import pallas as pl
from jax.experimental.pallas import tpu as pltpu

F32 = jnp.float32
BF16 = jnp.bfloat16

D_MODEL = 2048
N_META = 16
ATTN_WIDTH = 1024
POOL_WIDTH = 1024
HEAD_DIM = 128
N_HEADS = 8
N_KV_HEADS = 2
GROUP = N_HEADS // N_KV_HEADS
N_IDX_HEADS = 16
IDX_DIM = 64
TOPK = 256
POOL_WINDOWS = (2, 4, 8, 16)
POOL_GROUP_WIDTH = 256
POOL_HIST = 15
D_FF = 8192
PAGE_SIZE = 128
ROPE_THETA = 10000.0
EPS = 1e-6
NEG = -1e30
NEG_INF = float("-inf")
POS_INF = float("inf")
F32_LOWEST = -3.0e38

LANES = 128
SUBLANES = 8
VMEM_LIMIT = 56 * 1024 * 1024

Q_BLOCK = 128
KEY_CHUNK = 256
MAX_BISECT = 320

CONTRACT_LAST = (((1,), (1,)), ((), ()))


def _const_spec(shape):
    nd = len(shape)
    return pl.BlockSpec(shape, lambda *_: (0,) * nd, pipeline_mode=pl.Buffered(1))


def _rope128(x, c, s):
    return x * c + pltpu.roll(x, 64, 1) * s


def _rope64(x, c, sa, sb):
    return x * c + pltpu.roll(x, 96, 1) * sa + pltpu.roll(x, 32, 1) * sb


def _proj_kernel(x_ref, g_ref, c128_ref, s128_ref, c64_ref, sa64_ref, sb64_ref,
                 wq_ref, wkv_ref, wqi_ref, wkw_ref, wu_ref,
                 q_ref, kf_ref, vf_ref, kb_ref, vb_ref, qi_ref, kw_ref, u_ref):
    x = x_ref[...]
    ms = jnp.mean(x * x, axis=-1, keepdims=True)
    xn = ((x * lax.rsqrt(ms + EPS)) * g_ref[...]).astype(BF16)
    c128 = c128_ref[...]
    s128 = s128_ref[...]
    c64 = c64_ref[...]
    sa64 = sa64_ref[...]
    sb64 = sb64_ref[...]

    zq = jnp.dot(xn, wq_ref[...], preferred_element_type=F32)
    for h in range(N_HEADS):
        sl = slice(h * LANES, (h + 1) * LANES)
        q_ref[:, sl] = _rope128(zq[:, sl], c128, s128).astype(BF16)

    zkv = jnp.dot(xn, wkv_ref[...], preferred_element_type=F32)
    for h in range(N_KV_HEADS):
        sl = slice(h * LANES, (h + 1) * LANES)
        kr = _rope128(zkv[:, sl], c128, s128)
        kf_ref[:, sl] = kr
        kb_ref[:, sl] = kr.astype(BF16)
    v = zkv[:, N_KV_HEADS * LANES:]
    vf_ref[...] = v
    vb_ref[...] = v.astype(BF16)

    zqi = jnp.dot(xn, wqi_ref[...], preferred_element_type=F32)
    for p in range(N_IDX_HEADS // 2):
        sl = slice(p * LANES, (p + 1) * LANES)
        qi_ref[:, sl] = _rope64(zqi[:, sl], c64, sa64, sb64).astype(BF16)

    zkw = jnp.dot(xn, wkw_ref[...], preferred_element_type=F32)
    lane = lax.broadcasted_iota(jnp.int32, zkw.shape, 1)
    is_key = lane < IDX_DIM
    ckw = jnp.where(is_key, c64, jnp.where(lane < IDX_DIM + N_IDX_HEADS, N_IDX_HEADS ** -0.5, 1.0))
    kw_ref[...] = _rope64(zkw, ckw, jnp.where(is_key, sa64, 0.0), jnp.where(is_key, sb64, 0.0))

    u_ref[...] = jnp.dot(xn, wu_ref[...], preferred_element_type=F32)


def _project(x, pos, g_mix, w_parts, tm):
    rows = x.shape[0]
    assert rows % tm == 0
    c128, s128, c64, sa64, sb64 = _rope_tables(pos)
    wq, wkv, wqi, wkw, wu = w_parts
    row_spec = lambda w: pl.BlockSpec((tm, w), lambda i: (i, 0))
    out_shapes = (
        jax.ShapeDtypeStruct((rows, ATTN_WIDTH), BF16),
        jax.ShapeDtypeStruct((rows, 256), F32),
        jax.ShapeDtypeStruct((rows, 256), F32),
        jax.ShapeDtypeStruct((rows, 256), BF16),
        jax.ShapeDtypeStruct((rows, 256), BF16),
        jax.ShapeDtypeStruct((rows, 1024), BF16),
        jax.ShapeDtypeStruct((rows, LANES), F32),
        jax.ShapeDtypeStruct((rows, POOL_WIDTH), F32),
    )
    return pl.pallas_call(
        _proj_kernel,
        grid=(rows // tm,),
        in_specs=[row_spec(D_MODEL), _const_spec((1, D_MODEL))]
        + [row_spec(LANES)] * 5
        + [_const_spec(w.shape) for w in w_parts],
        out_specs=tuple(row_spec(s.shape[1]) for s in out_shapes),
        out_shape=out_shapes,
        compiler_params=pltpu.CompilerParams(
            dimension_semantics=("arbitrary",), vmem_limit_bytes=VMEM_LIMIT),
        name="proj",
    )(x, g_mix.reshape(1, D_MODEL), c128, s128, c64, sa64, sb64, wq, wkv, wqi, wkw, wu)


def _rope_tables(pos):
    posf = pos.astype(F32)[:, None]
    inv = ROPE_THETA ** (-jnp.arange(0, HEAD_DIM, 2, dtype=F32) / HEAD_DIM)
    ang = posf * inv[None, :]
    c, s = jnp.cos(ang), jnp.sin(ang)
    c128 = jnp.concatenate([c, c], axis=-1)
    s128 = jnp.concatenate([-s, s], axis=-1)
    inv = ROPE_THETA ** (-jnp.arange(0, IDX_DIM, 2, dtype=F32) / IDX_DIM)
    ang = posf * inv[None, :]
    c, s = jnp.cos(ang), jnp.sin(ang)
    z = jnp.zeros_like(s)
    c64 = jnp.concatenate([c, c, c, c], axis=-1)
    sa64 = jnp.concatenate([-s, z, -s, z], axis=-1)
    sb64 = jnp.concatenate([z, s, z, s], axis=-1)
    return c128, s128, c64, sa64, sb64


def _key_count(st_ref, nchunks, chunk, pred):
    acc_rows = 8 * SUBLANES

    def body(c, part):
        c0 = pl.multiple_of(c * chunk, chunk)
        hit = pred(st_ref[pl.ds(c0, chunk), :], c0).astype(F32)
        return part + jnp.sum(hit.reshape(chunk // acc_rows, acc_rows, LANES), axis=0)

    part = lax.fori_loop(0, nchunks, body, jnp.zeros((acc_rows, LANES), F32))
    return jnp.sum(part, axis=0, keepdims=True)


def _topk_threshold(st_ref, nchunks, chunk, n_valid, s_min, s_max):
    kf = float(TOPK)
    take_all = n_valid <= kf

    def count_ge(t):
        return _key_count(st_ref, nchunks, chunk, lambda v, c0: v >= t)

    def cond(st):
        it, _, _, done, _ = st
        return jnp.logical_and(it < MAX_BISECT, jnp.min(done) < 0.5)

    def body(st):
        it, lo, hi, done, tau = st
        mid = lo + (hi - lo) * 0.5
        collapsed = jnp.logical_or(mid <= lo, mid >= hi)
        cnt = count_ge(mid)
        found = cnt == kf
        active = done < 0.5
        go_up = jnp.logical_and(active, cnt > kf)
        go_dn = jnp.logical_and(active, cnt < kf)
        lo = jnp.where(go_up, mid, lo)
        hi = jnp.where(go_dn, mid, hi)
        tau = jnp.where(jnp.logical_and(active, found), mid, tau)
        done = jnp.where(jnp.logical_or(found, collapsed), 1.0, done)
        return it + 1, lo, hi, done, tau

    done0 = jnp.where(take_all, 1.0, 0.0).astype(F32)
    tau0 = jnp.full((1, LANES), POS_INF, F32)
    _, lo, hi, _, tau = lax.while_loop(
        cond, body, (jnp.int32(0), s_min, s_max, done0, tau0))

    unresolved = jnp.logical_and(tau == POS_INF, jnp.logical_not(take_all))
    n_unres = jnp.sum(unresolved.astype(F32))
    tau = jnp.where(take_all, F32_LOWEST, tau)

    def tie_path(tau):
        cut = jnp.where(count_ge(hi) >= kf, hi, lo)
        cut = jnp.where(unresolved, cut, POS_INF)
        c_gt = _key_count(st_ref, nchunks, chunk, lambda v, c0: v > cut)
        keep = kf - c_gt

        def key_iota(c0):
            return c0 + lax.broadcasted_iota(jnp.int32, (chunk, LANES), 0)

        def idx_body(_, st):
            xlo, xhi = st
            xmid = (xlo + xhi) // 2
            cnt = _key_count(
                st_ref, nchunks, chunk,
                lambda v, c0: jnp.logical_and(v == cut, key_iota(c0) < xmid))
            ok = cnt >= keep
            return jnp.where(ok, xlo, xmid), jnp.where(ok, xmid, xhi)

        xlo0 = jnp.zeros((1, LANES), jnp.int32)
        xhi0 = jnp.full((1, LANES), nchunks * chunk, jnp.int32)
        n_idx_steps = 15
        _, xcut = lax.fori_loop(0, n_idx_steps, idx_body, (xlo0, xhi0))

        def drop_body(c, carry):
            c0 = pl.multiple_of(c * chunk, chunk)
            v = st_ref[pl.ds(c0, chunk), :]
            drop = jnp.logical_and(v == cut, key_iota(c0) >= xcut)
            st_ref[pl.ds(c0, chunk), :] = jnp.where(drop, NEG_INF, v)
            return carry

        lax.fori_loop(0, nchunks, drop_body, 0)
        return jnp.where(unresolved, cut, tau)

    return lax.cond(n_unres > 0.0, tie_path, lambda t: t, tau)


def _prompt_attn_kernel(qi_ref, kw_ref, q_ref, kilo_ref, kihi_ref, k_ref, vt_ref, o_ref,
                        st_ref, m_ref, l_ref, acc_ref):
    i = pl.program_id(0)
    tq, ck = Q_BLOCK, KEY_CHUNK
    n_blocks = i + 2
    nchunks = (n_blocks * Q_BLOCK + ck - 1) // ck

    wt = jnp.transpose(kw_ref[...])[IDX_DIM:IDX_DIM + N_IDX_HEADS, :] * (IDX_DIM ** -0.5)
    qcat = jnp.concatenate(
        [qi_ref[:, p * LANES:(p + 1) * LANES] for p in range(N_IDX_HEADS // 2)], axis=0)
    key = lax.broadcasted_iota(jnp.int32, (ck, tq), 0)
    qry = i * tq + lax.broadcasted_iota(jnp.int32, (ck, tq), 1)

    def idx_body(c, st):
        smin, smax = st
        c0 = pl.multiple_of(c * ck, ck)
        dlo = lax.dot_general(kilo_ref[pl.ds(c0, ck), :], qcat, CONTRACT_LAST,
                              preferred_element_type=F32)
        dhi = lax.dot_general(kihi_ref[pl.ds(c0, ck), :], qcat, CONTRACT_LAST,
                              preferred_element_type=F32)
        acc = jnp.zeros((ck, tq), F32)
        for p in range(N_IDX_HEADS // 2):
            cs = slice(p * tq, (p + 1) * tq)
            acc = acc + wt[2 * p:2 * p + 1, :] * jnp.maximum(dlo[:, cs], 0.0)
            acc = acc + wt[2 * p + 1:2 * p + 2, :] * jnp.maximum(dhi[:, cs], 0.0)
        col = c0 + key
        valid = jnp.logical_or(
            col < N_META,
            jnp.logical_and(col >= Q_BLOCK, col - Q_BLOCK <= qry))
        st_ref[pl.ds(c0, ck), :] = jnp.where(valid, acc, NEG_INF)
        smin = jnp.minimum(smin, jnp.min(jnp.where(valid, acc, POS_INF), axis=0, keepdims=True))
        smax = jnp.maximum(smax, jnp.max(jnp.where(valid, acc, NEG_INF), axis=0, keepdims=True))
        return smin, smax

    smin, smax = lax.fori_loop(
        0, nchunks, idx_body,
        (jnp.full((1, tq), POS_INF, F32), jnp.full((1, tq), NEG_INF, F32)))

    n_valid = (N_META + 1 + i * tq + lax.broadcasted_iota(jnp.int32, (1, tq), 1)).astype(F32)
    tau = _topk_threshold(st_ref, nchunks, ck, n_valid, smin, smax)

    m_ref[...] = jnp.full(m_ref.shape, NEG, F32)
    l_ref[...] = jnp.zeros(l_ref.shape, F32)
    acc_ref[...] = jnp.zeros(acc_ref.shape, F32)
    scale = HEAD_DIM ** -0.5

    def att_body(c, carry):
        c0 = pl.multiple_of(c * ck, ck)
        bias = jnp.where(st_ref[pl.ds(c0, ck), :] >= tau, 0.0, NEG)
        bias = jnp.concatenate([bias] * GROUP, axis=1)
        for n in range(N_KV_HEADS):
            qg = jnp.concatenate(
                [q_ref[:, (n * GROUP + g) * LANES:(n * GROUP + g + 1) * LANES]
                 for g in range(GROUP)], axis=0)
            kc = k_ref[pl.ds(c0, ck), n * LANES:(n + 1) * LANES]
            sc = lax.dot_general(kc, qg, CONTRACT_LAST, preferred_element_type=F32)
            sc = sc * scale + bias
            m_old = m_ref[n]
            m_new = jnp.maximum(m_old, jnp.max(sc, axis=0, keepdims=True))
            alpha = jnp.exp(m_old - m_new)
            p = jnp.exp(sc - m_new)
            l_ref[n] = alpha * l_ref[n] + jnp.sum(
                p.reshape(ck // SUBLANES, SUBLANES, GROUP * tq), axis=0)
            vt = vt_ref[c, n * HEAD_DIM:(n + 1) * HEAD_DIM, :]
            acc_ref[n] = alpha * acc_ref[n] + jnp.dot(
                vt, p.astype(BF16), preferred_element_type=F32)
            m_ref[n] = m_new
        return carry

    lax.fori_loop(0, nchunks, att_body, 0)

    for n in range(N_KV_HEADS):
        ot = acc_ref[n] / jnp.sum(l_ref[n], axis=0, keepdims=True)
        for g in range(GROUP):
            h = n * GROUP + g
            o_ref[:, h * LANES:(h + 1) * LANES] = jnp.transpose(
                ot[:, g * tq:(g + 1) * tq]).astype(o_ref.dtype)


def _prompt_attention(qi, kw, q, ki_lo, ki_hi, k_all, v_t):
    rows = q.shape[0]
    nk = k_all.shape[0]
    assert rows % Q_BLOCK == 0 and nk % KEY_CHUNK == 0
    assert nk >= (rows // Q_BLOCK + 1) * Q_BLOCK
    row_spec = lambda w: pl.BlockSpec((Q_BLOCK, w), lambda i: (i, 0))
    return pl.pallas_call(
        _prompt_attn_kernel,
        grid=(rows // Q_BLOCK,),
        in_specs=[row_spec(1024), row_spec(LANES), row_spec(ATTN_WIDTH),
                  _const_spec(ki_lo.shape), _const_spec(ki_hi.shape),
                  _const_spec(k_all.shape), _const_spec(v_t.shape)],
        out_specs=row_spec(ATTN_WIDTH),
        out_shape=jax.ShapeDtypeStruct((rows, ATTN_WIDTH), BF16),
        scratch_shapes=[
            pltpu.VMEM((nk, Q_BLOCK), F32),
            pltpu.VMEM((N_KV_HEADS, 1, GROUP * Q_BLOCK), F32),
            pltpu.VMEM((N_KV_HEADS, SUBLANES, GROUP * Q_BLOCK), F32),
            pltpu.VMEM((N_KV_HEADS, HEAD_DIM, GROUP * Q_BLOCK), F32),
        ],
        compiler_params=pltpu.CompilerParams(
            dimension_semantics=("arbitrary",), vmem_limit_bytes=VMEM_LIMIT),
        name="prompt_attn",
    )(qi, kw, q, ki_lo, ki_hi, k_all, v_t)


def _decode_scores_kernel(pt_ref, qi_ref, w_ref, kinew_ref, *rest):
    n_pages = len(rest) - 1
    page_refs, s_ref = rest[:n_pages], rest[n_pages]
    qi = qi_ref[...]
    w = w_ref[...] * (IDX_DIM ** -0.5)
    for j in range(n_pages):
        kp = page_refs[j][...].astype(BF16)
        d = lax.dot_general(qi, kp, CONTRACT_LAST, preferred_element_type=F32)
        s_ref[:, j * PAGE_SIZE:(j + 1) * PAGE_SIZE] = jnp.sum(
            jnp.maximum(d, 0.0) * w, axis=0, keepdims=True)
    d_new = jnp.sum(qi.astype(F32) * kinew_ref[...].astype(F32), axis=-1, keepdims=True)
    s_new = jnp.sum(jnp.maximum(d_new, 0.0) * w, axis=0, keepdims=True)
    past = n_pages * PAGE_SIZE
    tail = s_ref.shape[1] - past
    lane = lax.broadcasted_iota(jnp.int32, (1, tail), 1)
    s_ref[:, past:] = jnp.where(lane == 0, s_new, NEG_INF)


def _page_specs(block, n_pages):
    nd = len(block)
    return [
        pl.BlockSpec((None, None) + block, functools.partial(
            lambda b, pt, j: (0, pt[b, j]) + (0,) * nd, j=j)) for j in range(n_pages)]


def _decode_scores(page_table, qi_s, w_s, kinew_s, cache_kidx, width):
    db, n_pages = page_table.shape
    seq_spec = lambda a, c: pl.BlockSpec((None, a, c), lambda b, pt: (b, 0, 0))
    return pl.pallas_call(
        _decode_scores_kernel,
        grid_spec=pltpu.PrefetchScalarGridSpec(
            num_scalar_prefetch=1,
            grid=(db,),
            in_specs=[seq_spec(N_IDX_HEADS, IDX_DIM), seq_spec(N_IDX_HEADS, 1),
                      seq_spec(1, IDX_DIM)] + _page_specs((PAGE_SIZE, IDX_DIM), n_pages),
            out_specs=seq_spec(1, width),
        ),
        out_shape=jax.ShapeDtypeStruct((db, 1, width), F32),
        compiler_params=pltpu.CompilerParams(dimension_semantics=("arbitrary",)),
        name="decode_scores",
    )(page_table, qi_s, w_s, kinew_s, *([cache_kidx] * n_pages))


def _decode_mask_kernel(s_in_ref, bias_ref, st_ref):
    nk = s_in_ref.shape[0]
    chunk = KEY_CHUNK
    s = s_in_ref[...]
    st_ref[...] = s
    valid = s > NEG_INF
    smin = jnp.min(jnp.where(valid, s, POS_INF), axis=0, keepdims=True)
    smax = jnp.max(s, axis=0, keepdims=True)
    n_valid = jnp.sum(valid.astype(F32), axis=0, keepdims=True)
    tau = _topk_threshold(st_ref, nk // chunk, chunk, n_valid, smin, smax)
    bias_ref[...] = jnp.where(st_ref[...] >= tau, 0.0, NEG)


def _decode_mask(scores_t):
    assert scores_t.shape[0] % KEY_CHUNK == 0 and scores_t.shape[1] == LANES
    return pl.pallas_call(
        _decode_mask_kernel,
        out_shape=jax.ShapeDtypeStruct(scores_t.shape, F32),
        scratch_shapes=[pltpu.VMEM(scores_t.shape, F32)],
        name="decode_mask",
    )(scores_t)


def _decode_attn_kernel(pt_ref, q_ref, bias_ref, knew_ref, vnew_ref, *rest):
    n_pages = (len(rest) - 1) // 2
    k_pages, v_pages, o_ref = rest[:n_pages], rest[n_pages:2 * n_pages], rest[2 * n_pages]
    past = n_pages * PAGE_SIZE
    scale = HEAD_DIM ** -0.5
    bias = bias_ref[...]
    outs = []
    for n in range(N_KV_HEADS):
        qn = q_ref[n * GROUP:(n + 1) * GROUP, :]
        k_n = jnp.concatenate([r[:, n, :].astype(BF16) for r in k_pages], axis=0)
        v_n = jnp.concatenate([r[:, n, :].astype(BF16) for r in v_pages], axis=0)
        sc = lax.dot_general(qn, k_n, CONTRACT_LAST, preferred_element_type=F32)
        sc = sc * scale + bias[:, :past]
        sc_new = jnp.sum(qn.astype(F32) * knew_ref[n:n + 1, :].astype(F32), axis=-1, keepdims=True)
        sc_new = sc_new * scale + bias[:, past:past + 1]
        m = jnp.maximum(jnp.max(sc, axis=-1, keepdims=True), sc_new)
        p = jnp.exp(sc - m)
        p_new = jnp.exp(sc_new - m)
        denom = jnp.sum(p, axis=-1, keepdims=True) + p_new
        o = jnp.dot(p.astype(BF16), v_n, preferred_element_type=F32)
        o = o + p_new.astype(BF16).astype(F32) * vnew_ref[n:n + 1, :].astype(F32)
        outs.append(o / denom)
    o_ref[...] = jnp.concatenate(outs, axis=0).astype(o_ref.dtype)


def _decode_attention(page_table, q_s, bias, knew_s, vnew_s, cache_k, cache_v):
    db, n_pages = page_table.shape
    width = bias.shape[-1]
    seq_spec = lambda a, c: pl.BlockSpec((None, a, c), lambda b, pt: (b, 0, 0))
    page_specs = _page_specs((PAGE_SIZE, N_KV_HEADS, HEAD_DIM), n_pages)
    return pl.pallas_call(
        _decode_attn_kernel,
        grid_spec=pltpu.PrefetchScalarGridSpec(
            num_scalar_prefetch=1,
            grid=(db,),
            in_specs=[seq_spec(N_HEADS, HEAD_DIM), seq_spec(1, width),
                      seq_spec(N_KV_HEADS, HEAD_DIM), seq_spec(N_KV_HEADS, HEAD_DIM)]
            + page_specs + page_specs,
            out_specs=seq_spec(N_HEADS, HEAD_DIM),
        ),
        out_shape=jax.ShapeDtypeStruct((db, N_HEADS, HEAD_DIM), BF16),
        compiler_params=pltpu.CompilerParams(
            dimension_semantics=("arbitrary",), vmem_limit_bytes=VMEM_LIMIT),
        name="decode_attn",
    )(page_table, q_s, bias, knew_s, vnew_s, *([cache_k] * n_pages), *([cache_v] * n_pages))


POOL_TILE = 512
POOL_HALO = 16


def _pool_prompt_kernel(u_ref, uprev_ref, umeta_ref, d_ref, ubuf):
    i = pl.program_id(0)
    tp = u_ref.shape[0]
    ubuf[0:POOL_HALO, :] = jnp.where(i == 0, umeta_ref[...], uprev_ref[...])
    ubuf[POOL_HALO:, :] = u_ref[...]
    for g, w in enumerate(POOL_WINDOWS):
        cols = slice(g * POOL_GROUP_WIDTH, (g + 1) * POOL_GROUP_WIDTH)
        cur = ubuf[POOL_HALO:POOL_HALO + tp, cols]
        tot = cur
        for j in range(1, w):
            tot = tot + ubuf[POOL_HALO - j:POOL_HALO - j + tp, cols]
        d_ref[:, cols] = tot * (1.0 / w) - cur


def _pool_prompt(u, u_meta):
    rows = u.shape[0]
    tp = POOL_TILE
    per = tp // POOL_HALO
    return pl.pallas_call(
        _pool_prompt_kernel,
        grid=(rows // tp,),
        in_specs=[pl.BlockSpec((tp, POOL_WIDTH), lambda i: (i, 0)),
                  pl.BlockSpec((POOL_HALO, POOL_WIDTH), lambda i: (jnp.maximum(i * per - 1, 0), 0)),
                  _const_spec((POOL_HALO, POOL_WIDTH))],
        out_specs=pl.BlockSpec((tp, POOL_WIDTH), lambda i: (i, 0)),
        out_shape=jax.ShapeDtypeStruct((rows, POOL_WIDTH), F32),
        scratch_shapes=[pltpu.VMEM((tp + POOL_HALO, POOL_WIDTH), F32)],
        compiler_params=pltpu.CompilerParams(dimension_semantics=("arbitrary",)),
        name="pool_prompt",
    )(u, u, u_meta)


def _pool_decode_kernel(hist_ref, u_ref, d_ref):
    for g, w in enumerate(POOL_WINDOWS):
        cols = slice(g * POOL_GROUP_WIDTH, (g + 1) * POOL_GROUP_WIDTH)
        cur = u_ref[:, cols]
        tot = cur
        for j in range(1, w):
            tot = tot + hist_ref[POOL_HIST - j, :, cols]
        d_ref[:, cols] = tot * (1.0 / w) - cur


def _pool_decode(hist, u):
    return pl.pallas_call(
        _pool_decode_kernel,
        out_shape=jax.ShapeDtypeStruct(u.shape, F32),
        name="pool_decode",
    )(hist, u)


FF_TILE = 512


def _rmsnorm(x, g):
    ms = jnp.mean(x * x, axis=-1, keepdims=True)
    return (x * lax.rsqrt(ms + EPS)) * g


def _post_kernel(x_ref, att_ref, d_ref, woa_ref, wop_ref, wpool_ref, pscale_ref,
                 gmlp_ref, gfin_ref, wup_ref, wdn_ref, y_ref, h_ref, hn_ref, acc_ref):
    f = pl.program_id(1)

    @pl.when(f == 0)
    def _():
        d = d_ref[...]
        pooled = jnp.concatenate(
            [jnp.dot(d[:, g * POOL_GROUP_WIDTH:(g + 1) * POOL_GROUP_WIDTH].astype(BF16),
                     wpool_ref[g], preferred_element_type=F32)
             for g in range(len(POOL_WINDOWS))], axis=1) * pscale_ref[...]
        h = (x_ref[...]
             + jnp.dot(att_ref[...], woa_ref[...], preferred_element_type=F32)
             + jnp.dot(pooled.astype(BF16), wop_ref[...], preferred_element_type=F32))
        h_ref[...] = h
        hn_ref[...] = _rmsnorm(h, gmlp_ref[...]).astype(BF16)
        acc_ref[...] = jnp.zeros(acc_ref.shape, F32)

    a = jnp.maximum(jnp.dot(hn_ref[...], wup_ref[...], preferred_element_type=F32), 0.0)
    acc_ref[...] += jnp.dot((a * a).astype(BF16), wdn_ref[...], preferred_element_type=F32)

    @pl.when(f == pl.num_programs(1) - 1)
    def _():
        y_ref[...] = _rmsnorm(h_ref[...] + acc_ref[...], gfin_ref[...])


def _post(x, att, d, wo_a, wo_p, w_pool, pool_scale, g_mlp, g_final, w_up, w_down, tm):
    rows = x.shape[0]
    assert rows % tm == 0 and D_FF % FF_TILE == 0
    row_spec = lambda w: pl.BlockSpec((tm, w), lambda i, f: (i, 0))
    return pl.pallas_call(
        _post_kernel,
        grid=(rows // tm, D_FF // FF_TILE),
        in_specs=[row_spec(D_MODEL), row_spec(ATTN_WIDTH), row_spec(POOL_WIDTH),
                  _const_spec(wo_a.shape), _const_spec(wo_p.shape), _const_spec(w_pool.shape),
                  _const_spec((1, POOL_WIDTH)), _const_spec((1, D_MODEL)), _const_spec((1, D_MODEL)),
                  pl.BlockSpec((D_MODEL, FF_TILE), lambda i, f: (0, f)),
                  pl.BlockSpec((FF_TILE, D_MODEL), lambda i, f: (f, 0))],
        out_specs=row_spec(D_MODEL),
        out_shape=jax.ShapeDtypeStruct((rows, D_MODEL), F32),
        scratch_shapes=[pltpu.VMEM((tm, D_MODEL), F32), pltpu.VMEM((tm, D_MODEL), BF16),
                        pltpu.VMEM((tm, D_MODEL), F32)],
        compiler_params=pltpu.CompilerParams(
            dimension_semantics=("arbitrary", "arbitrary"), vmem_limit_bytes=VMEM_LIMIT),
        name="post",
    )(x, att, d, wo_a, wo_p, w_pool, pool_scale.reshape(1, POOL_WIDTH),
      g_mlp.reshape(1, D_MODEL), g_final.reshape(1, D_MODEL), w_up, w_down)


def kernel(x_prompt, x_sample, cache_k, cache_v, cache_kidx, state_pool, page_table, meta_tokens,
           g_mix, w_in, w_pool, pool_scale, w_out, g_mlp, w_up, w_down, g_final):
    assert x_prompt.shape[0] == 1 and x_sample.shape[1] == 1 and g_mix.shape[0] == 1
    seq = x_prompt.shape[1]
    db = x_sample.shape[0]
    n_pages = page_table.shape[1]
    past = n_pages * PAGE_SIZE
    assert db == LANES

    w = w_in[0]
    o = 0
    parts = []
    for width in (ATTN_WIDTH, 2 * N_KV_HEADS * HEAD_DIM, N_IDX_HEADS * IDX_DIM,
                  IDX_DIM + N_IDX_HEADS, POOL_WIDTH):
        parts.append(w[:, o:o + width].astype(BF16))
        o += width
    parts[3] = jnp.pad(parts[3], ((0, 0), (0, LANES - parts[3].shape[1])))
    wo_a = w_out[0, :ATTN_WIDTH].astype(BF16)
    wo_p = w_out[0, ATTN_WIDTH:].astype(BF16)
    w_pool_b = w_pool[0].astype(BF16)
    w_up_b = w_up[0].astype(BF16)
    w_down_b = w_down[0].astype(BF16)

    xp = x_prompt[0]
    pos_p = N_META + jnp.arange(seq, dtype=jnp.int32)
    q_p, kf_p, vf_p, kb_p, vb_p, qi_p, kw_p, u_p = _project(xp, pos_p, g_mix[0], parts, 256)
    xs = jnp.concatenate([meta_tokens.astype(F32), x_sample[:, 0]], axis=0)
    pos_s = jnp.concatenate([jnp.arange(N_META, dtype=jnp.int32),
                             jnp.full((db,), past, jnp.int32)])
    q_s, kf_s, vf_s, kb_s, vb_s, qi_s, kw_s, u_s = _project(xs, pos_s, g_mix[0], parts, xs.shape[0])

    nk = -(-(seq + 2 * Q_BLOCK) // KEY_CHUNK) * KEY_CHUNK

    def key_rows(meta_part, prompt_part):
        width = meta_part.shape[1]
        return jnp.concatenate(
            [meta_part, jnp.zeros((Q_BLOCK - N_META, width), meta_part.dtype), prompt_part,
             jnp.zeros((nk - Q_BLOCK - seq, width), meta_part.dtype)], axis=0)

    ki_all = key_rows(kw_s[:N_META, :IDX_DIM], kw_p[:, :IDX_DIM]).astype(BF16)
    zk = jnp.zeros_like(ki_all)
    ki_lo = jnp.concatenate([ki_all, zk], axis=1)
    ki_hi = jnp.concatenate([zk, ki_all], axis=1)
    k_all = key_rows(kb_s[:N_META], kb_p)
    v_all = key_rows(vb_s[:N_META], vb_p)
    v_t = jnp.swapaxes(v_all.reshape(nk // KEY_CHUNK, KEY_CHUNK, N_KV_HEADS * HEAD_DIM), 1, 2)
    att_p = _prompt_attention(qi_p, kw_p, q_p, ki_lo, ki_hi, k_all, v_t)

    width = -(-(past + 1) // KEY_CHUNK) * KEY_CHUNK
    qi_d = qi_s[N_META:].reshape(db, N_IDX_HEADS, IDX_DIM)
    w_d = kw_s[N_META:, IDX_DIM:IDX_DIM + N_IDX_HEADS].reshape(db, N_IDX_HEADS, 1)
    kinew = kw_s[N_META:, :IDX_DIM].astype(BF16).reshape(db, 1, IDX_DIM)
    scores = _decode_scores(page_table, qi_d, w_d, kinew, cache_kidx, width)
    bias_t = _decode_mask(jnp.transpose(scores.reshape(db, width)))
    bias = jnp.transpose(bias_t).reshape(db, 1, width)
    att_s = _decode_attention(
        page_table, q_s[N_META:].reshape(db, N_HEADS, HEAD_DIM), bias,
        kb_s[N_META:].reshape(db, N_KV_HEADS, HEAD_DIM), vb_s[N_META:].reshape(db, N_KV_HEADS, HEAD_DIM),
        cache_k, cache_v)
    att_s = att_s.reshape(db, ATTN_WIDTH)

    d_p = _pool_prompt(u_p, u_s[:N_META])
    d_s = _pool_decode(jnp.swapaxes(state_pool[0], 0, 1), u_s[N_META:])

    post = functools.partial(
        _post, wo_a=wo_a, wo_p=wo_p, w_pool=w_pool_b, pool_scale=pool_scale[0],
        g_mlp=g_mlp[0], g_final=g_final, w_up=w_up_b, w_down=w_down_b)
    y_p = post(xp, att_p, d_p, tm=512)
    y_s = post(x_sample[:, 0], att_s, d_s, tm=db)

    k_prompt = jnp.concatenate([kf_s[:N_META], kf_p], axis=0).reshape(1, 1, N_META + seq, N_KV_HEADS, HEAD_DIM)
    v_prompt = jnp.concatenate([vf_s[:N_META], vf_p], axis=0).reshape(1, 1, N_META + seq, N_KV_HEADS, HEAD_DIM)
    kidx_prompt = jnp.concatenate([kw_s[:N_META, :IDX_DIM], kw_p[:, :IDX_DIM]], axis=0).reshape(
        1, 1, N_META + seq, IDX_DIM)
    pool_prompt = u_p[seq - POOL_HIST:].reshape(1, 1, POOL_HIST, POOL_WIDTH)
    k_sample = kf_s[N_META:].reshape(1, db, 1, N_KV_HEADS, HEAD_DIM)
    v_sample = vf_s[N_META:].reshape(1, db, 1, N_KV_HEADS, HEAD_DIM)
    kidx_sample = kw_s[N_META:, :IDX_DIM].reshape(1, db, 1, IDX_DIM)
    pool_sample = jnp.concatenate(
        [state_pool[0][:, 1:], u_s[N_META:][:, None, :]], axis=1)[None]
    return (y_p.reshape(1, seq, D_MODEL), y_s.reshape(db, 1, D_MODEL),
            k_prompt, v_prompt, kidx_prompt, pool_prompt,
            k_sample, v_sample, kidx_sample, pool_sample)
```

```python
import functools

import jax
import jax.numpy as jnp
from jax import lax
from jax.experimental import pallas as pl
from jax.experimental.pallas import tpu as pltpu

F32 = jnp.float32
BF16 = jnp.bfloat16

D_MODEL = 2048
N_META = 16
ATTN_WIDTH = 1024
POOL_WIDTH = 1024
HEAD_DIM = 128
N_HEADS = 8
N_KV_HEADS = 2
GROUP = N_HEADS // N_KV_HEADS
N_IDX_HEADS = 16
IDX_DIM = 64
TOPK = 256
POOL_WINDOWS = (2, 4, 8, 16)
POOL_GROUP_WIDTH = 256
POOL_HIST = 15
D_FF = 8192
PAGE_SIZE = 128
ROPE_THETA = 10000.0
EPS = 1e-6
NEG = -1e30
NEG_INF = float("-inf")
POS_INF = float("inf")
F32_LOWEST = -3.0e38

LANES = 128
SUBLANES = 8
VMEM_LIMIT = 56 * 1024 * 1024

Q_BLOCK = 128
KEY_CHUNK = 512
MAX_BISECT = 320

CONTRACT_LAST = (((1,), (1,)), ((), ()))


def _const_spec(shape):
    nd = len(shape)
    return pl.BlockSpec(shape, lambda *_: (0,) * nd, pipeline_mode=pl.Buffered(1))


def _rope128(x, c, s):
    return x * c + pltpu.roll(x, 64, 1) * s


def _rope64(x, c, sa, sb):
    return x * c + pltpu.roll(x, 96, 1) * sa + pltpu.roll(x, 32, 1) * sb


def _proj_kernel(x_ref, g_ref, c128_ref, s128_ref, c64_ref, sa64_ref, sb64_ref,
                 wq_ref, wkv_ref, wqi_ref, wkw_ref, wu_ref,
                 q_ref, kf_ref, vf_ref, kb_ref, vb_ref, qi_ref, kw_ref, u_ref):
    x = x_ref[...]
    ms = jnp.mean(x * x, axis=-1, keepdims=True)
    xn = ((x * lax.rsqrt(ms + EPS)) * g_ref[...]).astype(BF16)
    c128 = c128_ref[...]
    s128 = s128_ref[...]
    c64 = c64_ref[...]
    sa64 = sa64_ref[...]
    sb64 = sb64_ref[...]

    zq = jnp.dot(xn, wq_ref[...], preferred_element_type=F32)
    for h in range(N_HEADS):
        sl = slice(h * LANES, (h + 1) * LANES)
        q_ref[:, sl] = _rope128(zq[:, sl], c128, s128).astype(BF16)

    zkv = jnp.dot(xn, wkv_ref[...], preferred_element_type=F32)
    for h in range(N_KV_HEADS):
        sl = slice(h * LANES, (h + 1) * LANES)
        kr = _rope128(zkv[:, sl], c128, s128)
        kf_ref[:, sl] = kr
        kb_ref[:, sl] = kr.astype(BF16)
    v = zkv[:, N_KV_HEADS * LANES:]
    vf_ref[...] = v
    vb_ref[...] = v.astype(BF16)

    zqi = jnp.dot(xn, wqi_ref[...], preferred_element_type=F32)
    for p in range(N_IDX_HEADS // 2):
        sl = slice(p * LANES, (p + 1) * LANES)
        qi_ref[:, sl] = _rope64(zqi[:, sl], c64, sa64, sb64).astype(BF16)

    zkw = jnp.dot(xn, wkw_ref[...], preferred_element_type=F32)
    lane = lax.broadcasted_iota(jnp.int32, zkw.shape, 1)
    is_key = lane < IDX_DIM
    ckw = jnp.where(is_key, c64, jnp.where(lane < IDX_DIM + N_IDX_HEADS, N_IDX_HEADS ** -0.5, 1.0))
    kw_ref[...] = _rope64(zkw, ckw, jnp.where(is_key, sa64, 0.0), jnp.where(is_key, sb64, 0.0))

    u_ref[...] = jnp.dot(xn, wu_ref[...], preferred_element_type=F32)


def _project(x, pos, g_mix, w_parts, tm):
    rows = x.shape[0]
    assert rows % tm == 0
    c128, s128, c64, sa64, sb64 = _rope_tables(pos)
    wq, wkv, wqi, wkw, wu = w_parts
    row_spec = lambda w: pl.BlockSpec((tm, w), lambda i: (i, 0))
    out_shapes = (
        jax.ShapeDtypeStruct((rows, ATTN_WIDTH), BF16),
        jax.ShapeDtypeStruct((rows, 256), F32),
        jax.ShapeDtypeStruct((rows, 256), F32),
        jax.ShapeDtypeStruct((rows, 256), BF16),
        jax.ShapeDtypeStruct((rows, 256), BF16),
        jax.ShapeDtypeStruct((rows, 1024), BF16),
        jax.ShapeDtypeStruct((rows, LANES), F32),
        jax.ShapeDtypeStruct((rows, POOL_WIDTH), F32),
    )
    return pl.pallas_call(
        _proj_kernel,
        grid=(rows // tm,),
        in_specs=[row_spec(D_MODEL), _const_spec((1, D_MODEL))]
        + [row_spec(LANES)] * 5
        + [_const_spec(w.shape) for w in w_parts],
        out_specs=tuple(row_spec(s.shape[1]) for s in out_shapes),
        out_shape=out_shapes,
        compiler_params=pltpu.CompilerParams(
            dimension_semantics=("arbitrary",), vmem_limit_bytes=VMEM_LIMIT),
        name="proj",
    )(x, g_mix.reshape(1, D_MODEL), c128, s128, c64, sa64, sb64, wq, wkv, wqi, wkw, wu)


def _rope_tables(pos):
    posf = pos.astype(F32)[:, None]
    inv = ROPE_THETA ** (-jnp.arange(0, HEAD_DIM, 2, dtype=F32) / HEAD_DIM)
    ang = posf * inv[None, :]
    c, s = jnp.cos(ang), jnp.sin(ang)
    c128 = jnp.concatenate([c, c], axis=-1)
    s128 = jnp.concatenate([-s, s], axis=-1)
    inv = ROPE_THETA ** (-jnp.arange(0, IDX_DIM, 2, dtype=F32) / IDX_DIM)
    ang = posf * inv[None, :]
    c, s = jnp.cos(ang), jnp.sin(ang)
    z = jnp.zeros_like(s)
    c64 = jnp.concatenate([c, c, c, c], axis=-1)
    sa64 = jnp.concatenate([-s, z, -s, z], axis=-1)
    sb64 = jnp.concatenate([z, s, z, s], axis=-1)
    return c128, s128, c64, sa64, sb64


def _key_count(st_ref, nchunks, chunk, pred):
    acc_rows = 8 * SUBLANES

    def body(c, part):
        c0 = pl.multiple_of(c * chunk, chunk)
        hit = pred(st_ref[pl.ds(c0, chunk), :], c0).astype(F32)
        return part + jnp.sum(hit.reshape(chunk // acc_rows, acc_rows, LANES), axis=0)

    part = lax.fori_loop(0, nchunks, body, jnp.zeros((acc_rows, LANES), F32))
    return jnp.sum(part, axis=0, keepdims=True)


def _topk_threshold(st_ref, nchunks, chunk, n_valid, s_min, s_max):
    kf = float(TOPK)
    take_all = n_valid <= kf

    def count_ge(t):
        return _key_count(st_ref, nchunks, chunk, lambda v, c0: v >= t)

    def cond(st):
        it, _, _, done, _ = st
        return jnp.logical_and(it < MAX_BISECT, jnp.min(done) < 0.5)

    def body(st):
        it, lo, hi, done, tau = st
        mid = lo + (hi - lo) * 0.5
        collapsed = jnp.logical_or(mid <= lo, mid >= hi)
        cnt = count_ge(mid)
        found = cnt == kf
        active = done < 0.5
        go_up = jnp.logical_and(active, cnt > kf)
        go_dn = jnp.logical_and(active, cnt < kf)
        lo = jnp.where(go_up, mid, lo)
        hi = jnp.where(go_dn, mid, hi)
        tau = jnp.where(jnp.logical_and(active, found), mid, tau)
        done = jnp.where(jnp.logical_or(found, collapsed), 1.0, done)
        return it + 1, lo, hi, done, tau

    done0 = jnp.where(take_all, 1.0, 0.0).astype(F32)
    tau0 = jnp.full((1, LANES), POS_INF, F32)
    _, lo, hi, _, tau = lax.while_loop(
        cond, body, (jnp.int32(0), s_min, s_max, done0, tau0))

    unresolved = jnp.logical_and(tau == POS_INF, jnp.logical_not(take_all))
    n_unres = jnp.sum(unresolved.astype(F32))
    tau = jnp.where(take_all, F32_LOWEST, tau)

    def tie_path(tau):
        cut = jnp.where(count_ge(hi) >= kf, hi, lo)
        cut = jnp.where(unresolved, cut, POS_INF)
        c_gt = _key_count(st_ref, nchunks, chunk, lambda v, c0: v > cut)
        keep = kf - c_gt

        def key_iota(c0):
            return c0 + lax.broadcasted_iota(jnp.int32, (chunk, LANES), 0)

        def idx_body(_, st):
            xlo, xhi = st
            xmid = (xlo + xhi) // 2
            cnt = _key_count(
                st_ref, nchunks, chunk,
                lambda v, c0: jnp.logical_and(v == cut, key_iota(c0) < xmid))
            ok = cnt >= keep
            return jnp.where(ok, xlo, xmid), jnp.where(ok, xmid, xhi)

        xlo0 = jnp.zeros((1, LANES), jnp.int32)
        xhi0 = jnp.full((1, LANES), nchunks * chunk, jnp.int32)
        n_idx_steps = 15
        _, xcut = lax.fori_loop(0, n_idx_steps, idx_body, (xlo0, xhi0))

        def drop_body(c, carry):
            c0 = pl.multiple_of(c * chunk, chunk)
            v = st_ref[pl.ds(c0, chunk), :]
            drop = jnp.logical_and(v == cut, key_iota(c0) >= xcut)
            st_ref[pl.ds(c0, chunk), :] = jnp.where(drop, NEG_INF, v)
            return carry

        lax.fori_loop(0, nchunks, drop_body, 0)
        return jnp.where(unresolved, cut, tau)

    return lax.cond(n_unres > 0.0, tie_path, lambda t: t, tau)


def _prompt_attn_kernel(qi_ref, kw_ref, q_ref, kilo_ref, kihi_ref, k_ref, vt_ref, o_ref,
                        st_ref, m_ref, l_ref, acc_ref):
    i = pl.program_id(0)
    tq, ck = Q_BLOCK, KEY_CHUNK
    n_blocks = i + 2
    nchunks = (n_blocks * Q_BLOCK + ck - 1) // ck

    wt = jnp.transpose(kw_ref[...])[IDX_DIM:IDX_DIM + N_IDX_HEADS, :] * (IDX_DIM ** -0.5)
    qcat = jnp.concatenate(
        [qi_ref[:, p * LANES:(p + 1) * LANES] for p in range(N_IDX_HEADS // 2)], axis=0)
    key = lax.broadcasted_iota(jnp.int32, (ck, tq), 0)
    qry = i * tq + lax.broadcasted_iota(jnp.int32, (ck, tq), 1)

    def idx_body(c, st):
        smin, smax = st
        c0 = pl.multiple_of(c * ck, ck)
        klo = kilo_ref[pl.ds(c0, ck), :]
        khi = kihi_ref[pl.ds(c0, ck), :]
        acc = jnp.zeros((ck, tq), F32)
        for t in range(N_IDX_HEADS // 4):
            rhs = qcat[2 * t * tq:(2 * t + 2) * tq]
            dlo = lax.dot_general(klo, rhs, CONTRACT_LAST, preferred_element_type=F32)
            dhi = lax.dot_general(khi, rhs, CONTRACT_LAST, preferred_element_type=F32)
            for j in range(2):
                p = 2 * t + j
                cs = slice(j * tq, (j + 1) * tq)
                acc = acc + wt[2 * p:2 * p + 1, :] * jnp.maximum(dlo[:, cs], 0.0)
                acc = acc + wt[2 * p + 1:2 * p + 2, :] * jnp.maximum(dhi[:, cs], 0.0)
        col = c0 + key
        valid = jnp.logical_or(
            col < N_META,
            jnp.logical_and(col >= Q_BLOCK, col - Q_BLOCK <= qry))
        st_ref[pl.ds(c0, ck), :] = jnp.where(valid, acc, NEG_INF)
        smin = jnp.minimum(smin, jnp.min(jnp.where(valid, acc, POS_INF), axis=0, keepdims=True))
        smax = jnp.maximum(smax, jnp.max(jnp.where(valid, acc, NEG_INF), axis=0, keepdims=True))
        return smin, smax

    smin, smax = lax.fori_loop(
        0, nchunks, idx_body,
        (jnp.full((1, tq), POS_INF, F32), jnp.full((1, tq), NEG_INF, F32)))

    n_valid = (N_META + 1 + i * tq + lax.broadcasted_iota(jnp.int32, (1, tq), 1)).astype(F32)
    tau = _topk_threshold(st_ref, nchunks, ck, n_valid, smin, smax)

    m_ref[...] = jnp.full(m_ref.shape, NEG, F32)
    l_ref[...] = jnp.zeros(l_ref.shape, F32)
    acc_ref[...] = jnp.zeros(acc_ref.shape, F32)
    scale = HEAD_DIM ** -0.5

    def att_body(c, carry):
        c0 = pl.multiple_of(c * ck, ck)
        bias = jnp.where(st_ref[pl.ds(c0, ck), :] >= tau, 0.0, NEG)
        bias = jnp.concatenate([bias] * GROUP, axis=1)
        for n in range(N_KV_HEADS):
            qg = jnp.concatenate(
                [q_ref[:, (n * GROUP + g) * LANES:(n * GROUP + g + 1) * LANES]
                 for g in range(GROUP)], axis=0)
            kc = k_ref[pl.ds(c0, ck), n * LANES:(n + 1) * LANES]
            sc = lax.dot_general(kc, qg, CONTRACT_LAST, preferred_element_type=F32)
            sc = sc * scale + bias
            m_old = m_ref[n]
            m_new = jnp.maximum(m_old, jnp.max(sc, axis=0, keepdims=True))
            alpha = jnp.exp(m_old - m_new)
            p = jnp.exp(sc - m_new)
            l_ref[n] = alpha * l_ref[n] + jnp.sum(
                p.reshape(ck // SUBLANES, SUBLANES, GROUP * tq), axis=0)
            vt = vt_ref[c, n * HEAD_DIM:(n + 1) * HEAD_DIM, :]
            acc_ref[n] = alpha * acc_ref[n] + jnp.dot(
                vt, p.astype(BF16), preferred_element_type=F32)
            m_ref[n] = m_new
        return carry

    lax.fori_loop(0, nchunks, att_body, 0)

    for n in range(N_KV_HEADS):
        ot = acc_ref[n] / jnp.sum(l_ref[n], axis=0, keepdims=True)
        for g in range(GROUP):
            h = n * GROUP + g
            o_ref[:, h * LANES:(h + 1) * LANES] = jnp.transpose(
                ot[:, g * tq:(g + 1) * tq]).astype(o_ref.dtype)


def _prompt_attention(qi, kw, q, ki_lo, ki_hi, k_all, v_t):
    rows = q.shape[0]
    nk = k_all.shape[0]
    assert rows % Q_BLOCK == 0 and nk % KEY_CHUNK == 0
    assert nk >= (rows // Q_BLOCK + 1) * Q_BLOCK
    row_spec = lambda w: pl.BlockSpec((Q_BLOCK, w), lambda i: (i, 0))
    return pl.pallas_call(
        _prompt_attn_kernel,
        grid=(rows // Q_BLOCK,),
        in_specs=[row_spec(1024), row_spec(LANES), row_spec(ATTN_WIDTH),
                  _const_spec(ki_lo.shape), _const_spec(ki_hi.shape),
                  _const_spec(k_all.shape), _const_spec(v_t.shape)],
        out_specs=row_spec(ATTN_WIDTH),
        out_shape=jax.ShapeDtypeStruct((rows, ATTN_WIDTH), BF16),
        scratch_shapes=[
            pltpu.VMEM((nk, Q_BLOCK), F32),
            pltpu.VMEM((N_KV_HEADS, 1, GROUP * Q_BLOCK), F32),
            pltpu.VMEM((N_KV_HEADS, SUBLANES, GROUP * Q_BLOCK), F32),
            pltpu.VMEM((N_KV_HEADS, HEAD_DIM, GROUP * Q_BLOCK), F32),
        ],
        compiler_params=pltpu.CompilerParams(
            dimension_semantics=("arbitrary",), vmem_limit_bytes=VMEM_LIMIT),
        name="prompt_attn",
    )(qi, kw, q, ki_lo, ki_hi, k_all, v_t)


def _decode_scores_kernel(pt_ref, qi_ref, w_ref, kinew_ref, *rest):
    n_pages = len(rest) - 1
    page_refs, s_ref = rest[:n_pages], rest[n_pages]
    qi = qi_ref[...]
    w = w_ref[...] * (IDX_DIM ** -0.5)
    for j in range(n_pages):
        kpt = page_refs[j][...].astype(BF16)
        d = jnp.dot(qi, kpt, preferred_element_type=F32)
        s_ref[:, j * PAGE_SIZE:(j + 1) * PAGE_SIZE] = jnp.sum(
            jnp.maximum(d, 0.0) * w, axis=0, keepdims=True)
    d_new = jnp.sum(qi.astype(F32) * kinew_ref[...].astype(F32), axis=-1, keepdims=True)
    s_new = jnp.sum(jnp.maximum(d_new, 0.0) * w, axis=0, keepdims=True)
    past = n_pages * PAGE_SIZE
    tail = s_ref.shape[1] - past
    lane = lax.broadcasted_iota(jnp.int32, (1, tail), 1)
    s_ref[:, past:] = jnp.where(lane == 0, s_new, NEG_INF)


def _page_specs(block, n_pages):
    nd = len(block)
    return [
        pl.BlockSpec((None, None) + block, functools.partial(
            lambda b, pt, j: (0, pt[b, j]) + (0,) * nd, j=j)) for j in range(n_pages)]


def _decode_scores(page_table, qi_s, w_s, kinew_s, cache_kidx, width):
    db, n_pages = page_table.shape
    seq_spec = lambda a, c: pl.BlockSpec((None, a, c), lambda b, pt: (b, 0, 0))
    return pl.pallas_call(
        _decode_scores_kernel,
        grid_spec=pltpu.PrefetchScalarGridSpec(
            num_scalar_prefetch=1,
            grid=(db,),
            in_specs=[seq_spec(N_IDX_HEADS, IDX_DIM), seq_spec(N_IDX_HEADS, 1),
                      seq_spec(1, IDX_DIM)] + _page_specs((IDX_DIM, PAGE_SIZE), n_pages),
            out_specs=seq_spec(1, width),
        ),
        out_shape=jax.ShapeDtypeStruct((db, 1, width), F32),
        compiler_params=pltpu.CompilerParams(dimension_semantics=("arbitrary",)),
        name="decode_scores",
    )(page_table, qi_s, w_s, kinew_s, *([cache_kidx] * n_pages))


def _decode_mask_kernel(s_in_ref, bias_ref, st_ref):
    nk = s_in_ref.shape[0]
    chunk = KEY_CHUNK
    s = s_in_ref[...]
    st_ref[...] = s
    valid = s > NEG_INF
    smin = jnp.min(jnp.where(valid, s, POS_INF), axis=0, keepdims=True)
    smax = jnp.max(s, axis=0, keepdims=True)
    n_valid = jnp.sum(valid.astype(F32), axis=0, keepdims=True)
    tau = _topk_threshold(st_ref, nk // chunk, chunk, n_valid, smin, smax)
    bias_ref[...] = jnp.where(st_ref[...] >= tau, 0.0, NEG)


def _decode_mask(scores_t):
    assert scores_t.shape[0] % KEY_CHUNK == 0 and scores_t.shape[1] == LANES
    return pl.pallas_call(
        _decode_mask_kernel,
        out_shape=jax.ShapeDtypeStruct(scores_t.shape, F32),
        scratch_shapes=[pltpu.VMEM(scores_t.shape, F32)],
        name="decode_mask",
    )(scores_t)


def _decode_attn_kernel(pt_ref, q_ref, bias_ref, knew_ref, vnew_ref, *rest):
    n_pages = (len(rest) - 1) // 2
    k_pages, v_pages, o_ref = rest[:n_pages], rest[n_pages:2 * n_pages], rest[2 * n_pages]
    rows = N_KV_HEADS * n_pages * PAGE_SIZE
    scale = HEAD_DIM ** -0.5
    q = q_ref[...]
    bias = jnp.concatenate(
        [jnp.broadcast_to(bias_ref[n:n + 1, :], (GROUP, bias_ref.shape[1]))
         for n in range(N_KV_HEADS)], axis=0)
    k_il = jnp.concatenate([r[...].astype(BF16) for r in k_pages], axis=0)
    v_il = jnp.concatenate([r[...].astype(BF16) for r in v_pages], axis=0)
    sc = lax.dot_general(q, k_il, CONTRACT_LAST, preferred_element_type=F32)
    sc = sc * scale + bias[:, :rows]
    sc_new = jnp.sum(q.astype(F32) * knew_ref[...].astype(F32), axis=-1, keepdims=True)
    sc_new = sc_new * scale + jnp.max(bias[:, rows:rows + N_KV_HEADS], axis=-1, keepdims=True)
    m = jnp.maximum(jnp.max(sc, axis=-1, keepdims=True), sc_new)
    p = jnp.exp(sc - m)
    p_new = jnp.exp(sc_new - m)
    denom = jnp.sum(p, axis=-1, keepdims=True) + p_new
    o = jnp.dot(p.astype(BF16), v_il, preferred_element_type=F32)
    o = o + p_new.astype(BF16).astype(F32) * vnew_ref[...].astype(F32)
    o_ref[...] = (o / denom).astype(o_ref.dtype)


def _decode_attention(page_table, q_s, bias_il, knew_s, vnew_s, cache_k, cache_v):
    db, n_pages = page_table.shape
    width2 = bias_il.shape[-1]
    seq_spec = lambda a, c: pl.BlockSpec((None, a, c), lambda b, pt: (b, 0, 0))
    page_specs = _page_specs((N_KV_HEADS * PAGE_SIZE, HEAD_DIM), n_pages)
    return pl.pallas_call(
        _decode_attn_kernel,
        grid_spec=pltpu.PrefetchScalarGridSpec(
            num_scalar_prefetch=1,
            grid=(db,),
            in_specs=[seq_spec(N_HEADS, HEAD_DIM), seq_spec(N_KV_HEADS, width2),
                      seq_spec(N_HEADS, HEAD_DIM), seq_spec(N_HEADS, HEAD_DIM)]
            + page_specs + page_specs,
            out_specs=seq_spec(N_HEADS, HEAD_DIM),
        ),
        out_shape=jax.ShapeDtypeStruct((db, N_HEADS, HEAD_DIM), BF16),
        compiler_params=pltpu.CompilerParams(
            dimension_semantics=("arbitrary",), vmem_limit_bytes=VMEM_LIMIT),
        name="decode_attn",
    )(page_table, q_s, bias_il, knew_s, vnew_s, *([cache_k] * n_pages), *([cache_v] * n_pages))


POOL_TILE = 512
POOL_HALO = 16


def _pool_prompt_kernel(u_ref, uprev_ref, umeta_ref, d_ref, ubuf):
    i = pl.program_id(0)
    tp = u_ref.shape[0]
    ubuf[0:POOL_HALO, :] = jnp.where(i == 0, umeta_ref[...], uprev_ref[...])
    ubuf[POOL_HALO:, :] = u_ref[...]
    for g, w in enumerate(POOL_WINDOWS):
        cols = slice(g * POOL_GROUP_WIDTH, (g + 1) * POOL_GROUP_WIDTH)
        cur = ubuf[POOL_HALO:POOL_HALO + tp, cols]
        tot = cur
        for j in range(1, w):
            tot = tot + ubuf[POOL_HALO - j:POOL_HALO - j + tp, cols]
        d_ref[:, cols] = tot * (1.0 / w) - cur


def _pool_prompt(u, u_meta):
    rows = u.shape[0]
    tp = POOL_TILE
    per = tp // POOL_HALO
    return pl.pallas_call(
        _pool_prompt_kernel,
        grid=(rows // tp,),
        in_specs=[pl.BlockSpec((tp, POOL_WIDTH), lambda i: (i, 0)),
                  pl.BlockSpec((POOL_HALO, POOL_WIDTH), lambda i: (jnp.maximum(i * per - 1, 0), 0)),
                  _const_spec((POOL_HALO, POOL_WIDTH))],
        out_specs=pl.BlockSpec((tp, POOL_WIDTH), lambda i: (i, 0)),
        out_shape=jax.ShapeDtypeStruct((rows, POOL_WIDTH), F32),
        scratch_shapes=[pltpu.VMEM((tp + POOL_HALO, POOL_WIDTH), F32)],
        compiler_params=pltpu.CompilerParams(dimension_semantics=("arbitrary",)),
        name="pool_prompt",
    )(u, u, u_meta)


def _pool_decode_kernel(hist_ref, u_ref, d_ref):
    for g, w in enumerate(POOL_WINDOWS):
        cols = slice(g * POOL_GROUP_WIDTH, (g + 1) * POOL_GROUP_WIDTH)
        cur = u_ref[:, cols]
        tot = cur
        for j in range(1, w):
            tot = tot + hist_ref[POOL_HIST - j, :, cols]
        d_ref[:, cols] = tot * (1.0 / w) - cur


def _pool_decode(hist, u):
    return pl.pallas_call(
        _pool_decode_kernel,
        out_shape=jax.ShapeDtypeStruct(u.shape, F32),
        name="pool_decode",
    )(hist, u)


FF_TILE = 512


def _rmsnorm(x, g):
    ms = jnp.mean(x * x, axis=-1, keepdims=True)
    return (x * lax.rsqrt(ms + EPS)) * g


def _post_kernel(x_ref, att_ref, d_ref, woa_ref, wop_ref, wpool_ref, pscale_ref,
                 gmlp_ref, gfin_ref, wup_ref, wdn_ref, y_ref, h_ref, hn_ref, acc_ref):
    f = pl.program_id(1)

    @pl.when(f == 0)
    def _():
        d = d_ref[...]
        pooled = jnp.concatenate(
            [jnp.dot(d[:, g * POOL_GROUP_WIDTH:(g + 1) * POOL_GROUP_WIDTH].astype(BF16),
                     wpool_ref[g], preferred_element_type=F32)
             for g in range(len(POOL_WINDOWS))], axis=1) * pscale_ref[...]
        h = (x_ref[...]
             + jnp.dot(att_ref[...], woa_ref[...], preferred_element_type=F32)
             + jnp.dot(pooled.astype(BF16), wop_ref[...], preferred_element_type=F32))
        h_ref[...] = h
        hn_ref[...] = _rmsnorm(h, gmlp_ref[...]).astype(BF16)
        acc_ref[...] = jnp.zeros(acc_ref.shape, F32)

    a = jnp.maximum(jnp.dot(hn_ref[...], wup_ref[...], preferred_element_type=F32), 0.0)
    acc_ref[...] += jnp.dot((a * a).astype(BF16), wdn_ref[...], preferred_element_type=F32)

    @pl.when(f == pl.num_programs(1) - 1)
    def _():
        y_ref[...] = _rmsnorm(h_ref[...] + acc_ref[...], gfin_ref[...])


def _post(x, att, d, wo_a, wo_p, w_pool, pool_scale, g_mlp, g_final, w_up, w_down, tm):
    rows = x.shape[0]
    assert rows % tm == 0 and D_FF % FF_TILE == 0
    row_spec = lambda w: pl.BlockSpec((tm, w), lambda i, f: (i, 0))
    return pl.pallas_call(
        _post_kernel,
        grid=(rows // tm, D_FF // FF_TILE),
        in_specs=[row_spec(D_MODEL), row_spec(ATTN_WIDTH), row_spec(POOL_WIDTH),
                  _const_spec(wo_a.shape), _const_spec(wo_p.shape), _const_spec(w_pool.shape),
                  _const_spec((1, POOL_WIDTH)), _const_spec((1, D_MODEL)), _const_spec((1, D_MODEL)),
                  pl.BlockSpec((D_MODEL, FF_TILE), lambda i, f: (0, f)),
                  pl.BlockSpec((FF_TILE, D_MODEL), lambda i, f: (f, 0))],
        out_specs=row_spec(D_MODEL),
        out_shape=jax.ShapeDtypeStruct((rows, D_MODEL), F32),
        scratch_shapes=[pltpu.VMEM((tm, D_MODEL), F32), pltpu.VMEM((tm, D_MODEL), BF16),
                        pltpu.VMEM((tm, D_MODEL), F32)],
        compiler_params=pltpu.CompilerParams(
            dimension_semantics=("arbitrary", "arbitrary"), vmem_limit_bytes=VMEM_LIMIT),
        name="post",
    )(x, att, d, wo_a, wo_p, w_pool, pool_scale.reshape(1, POOL_WIDTH),
      g_mlp.reshape(1, D_MODEL), g_final.reshape(1, D_MODEL), w_up, w_down)


def kernel(x_prompt, x_sample, cache_k, cache_v, cache_kidx, state_pool, page_table, meta_tokens,
           g_mix, w_in, w_pool, pool_scale, w_out, g_mlp, w_up, w_down, g_final):
    assert x_prompt.shape[0] == 1 and x_sample.shape[1] == 1 and g_mix.shape[0] == 1
    seq = x_prompt.shape[1]
    db = x_sample.shape[0]
    n_pages = page_table.shape[1]
    past = n_pages * PAGE_SIZE
    assert db == LANES

    w = w_in[0]
    o = 0
    parts = []
    for width in (ATTN_WIDTH, 2 * N_KV_HEADS * HEAD_DIM, N_IDX_HEADS * IDX_DIM,
                  IDX_DIM + N_IDX_HEADS, POOL_WIDTH):
        parts.append(w[:, o:o + width].astype(BF16))
        o += width
    parts[3] = jnp.pad(parts[3], ((0, 0), (0, LANES - parts[3].shape[1])))
    wo_a = w_out[0, :ATTN_WIDTH].astype(BF16)
    wo_p = w_out[0, ATTN_WIDTH:].astype(BF16)
    w_pool_b = w_pool[0].astype(BF16)
    w_up_b = w_up[0].astype(BF16)
    w_down_b = w_down[0].astype(BF16)

    xp = x_prompt[0]
    pos_p = N_META + jnp.arange(seq, dtype=jnp.int32)
    q_p, kf_p, vf_p, kb_p, vb_p, qi_p, kw_p, u_p = _project(xp, pos_p, g_mix[0], parts, 256)
    xs = jnp.concatenate([meta_tokens.astype(F32), x_sample[:, 0]], axis=0)
    pos_s = jnp.concatenate([jnp.arange(N_META, dtype=jnp.int32),
                             jnp.full((db,), past, jnp.int32)])
    q_s, kf_s, vf_s, kb_s, vb_s, qi_s, kw_s, u_s = _project(xs, pos_s, g_mix[0], parts, xs.shape[0])

    nk = -(-(seq + 2 * Q_BLOCK) // KEY_CHUNK) * KEY_CHUNK

    def key_rows(meta_part, prompt_part):
        width = meta_part.shape[1]
        return jnp.concatenate(
            [meta_part, jnp.zeros((Q_BLOCK - N_META, width), meta_part.dtype), prompt_part,
             jnp.zeros((nk - Q_BLOCK - seq, width), meta_part.dtype)], axis=0)

    ki_all = key_rows(kw_s[:N_META, :IDX_DIM], kw_p[:, :IDX_DIM]).astype(BF16)
    zk = jnp.zeros_like(ki_all)
    ki_lo = jnp.concatenate([ki_all, zk], axis=1)
    ki_hi = jnp.concatenate([zk, ki_all], axis=1)
    k_all = key_rows(kb_s[:N_META], kb_p)
    v_all = key_rows(vb_s[:N_META], vb_p)
    v_t = jnp.swapaxes(v_all.reshape(nk // KEY_CHUNK, KEY_CHUNK, N_KV_HEADS * HEAD_DIM), 1, 2)
    att_p = _prompt_attention(qi_p, kw_p, q_p, ki_lo, ki_hi, k_all, v_t)

    width = -(-(past + 1) // KEY_CHUNK) * KEY_CHUNK
    qi_d = qi_s[N_META:].reshape(db, N_IDX_HEADS, IDX_DIM)
    w_d = kw_s[N_META:, IDX_DIM:IDX_DIM + N_IDX_HEADS].reshape(db, N_IDX_HEADS, 1)
    kinew = kw_s[N_META:, :IDX_DIM].astype(BF16).reshape(db, 1, IDX_DIM)
    scores = _decode_scores(page_table, qi_d, w_d, kinew, jnp.swapaxes(cache_kidx, 2, 3), width)
    bias_t = _decode_mask(jnp.transpose(scores.reshape(db, width)))
    bias = jnp.transpose(bias_t)
    negs = jnp.full_like(bias, NEG)
    bias_il = jnp.stack(
        [jnp.stack([bias, negs], axis=-1), jnp.stack([negs, bias], axis=-1)], axis=1,
    ).reshape(db, N_KV_HEADS, N_KV_HEADS * width)
    n_phys = cache_k.shape[1]
    il_shape = (1, n_phys, N_KV_HEADS * PAGE_SIZE, HEAD_DIM)
    per_head = lambda a: jnp.repeat(a.reshape(db, N_KV_HEADS, HEAD_DIM), GROUP, axis=1)
    att_s = _decode_attention(
        page_table, q_s[N_META:].reshape(db, N_HEADS, HEAD_DIM), bias_il,
        per_head(kb_s[N_META:]), per_head(vb_s[N_META:]),
        cache_k.reshape(il_shape), cache_v.reshape(il_shape))
    att_s = att_s.reshape(db, ATTN_WIDTH)

    d_p = _pool_prompt(u_p, u_s[:N_META])
    d_s = _pool_decode(jnp.swapaxes(state_pool[0], 0, 1), u_s[N_META:])

    post = functools.partial(
        _post, wo_a=wo_a, wo_p=wo_p, w_pool=w_pool_b, pool_scale=pool_scale[0],
        g_mlp=g_mlp[0], g_final=g_final, w_up=w_up_b, w_down=w_down_b)
    y_p = post(xp, att_p, d_p, tm=512)
    y_s = post(x_sample[:, 0], att_s, d_s, tm=db)

    k_prompt = jnp.concatenate([kf_s[:N_META], kf_p], axis=0).reshape(1, 1, N_META + seq, N_KV_HEADS, HEAD_DIM)
    v_prompt = jnp.concatenate([vf_s[:N_META], vf_p], axis=0).reshape(1, 1, N_META + seq, N_KV_HEADS, HEAD_DIM)
    kidx_prompt = jnp.concatenate([kw_s[:N_META, :IDX_DIM], kw_p[:, :IDX_DIM]], axis=0).reshape(
        1, 1, N_META + seq, IDX_DIM)
    pool_prompt = u_p[seq - POOL_HIST:].reshape(1, 1, POOL_HIST, POOL_WIDTH)
    k_sample = kf_s[N_META:].reshape(1, db, 1, N_KV_HEADS, HEAD_DIM)
    v_sample = vf_s[N_META:].reshape(1, db, 1, N_KV_HEADS, HEAD_DIM)
    kidx_sample = kw_s[N_META:, :IDX_DIM].reshape(1, db, 1, IDX_DIM)
    pool_sample = jnp.concatenate(
        [state_pool[0][:, 1:], u_s[N_META:][:, None, :]], axis=1)[None]
    return (y_p.reshape(1, seq, D_MODEL), y_s.reshape(db, 1, D_MODEL),
            k_prompt, v_prompt, kidx_prompt, pool_prompt,
            k_sample, v_sample, kidx_sample, pool_sample)
```

```python
import functools

import jax
import jax.numpy as jnp
from jax import lax
from jax.experimental import pallas as pl
from jax.experimental.pallas import tpu as pltpu

F32 = jnp.float32
BF16 = jnp.bfloat16

D_MODEL = 2048
N_META = 16
ATTN_WIDTH = 1024
POOL_WIDTH = 1024
HEAD_DIM = 128
N_HEADS = 8
N_KV_HEADS = 2
GROUP = N_HEADS // N_KV_HEADS
N_IDX_HEADS = 16
IDX_DIM = 64
TOPK = 256
POOL_WINDOWS = (2, 4, 8, 16)
POOL_GROUP_WIDTH = 256
POOL_HIST = 15
D_FF = 8192
PAGE_SIZE = 128
ROPE_THETA = 10000.0
EPS = 1e-6
NEG = -1e30
NEG_INF = float("-inf")
POS_INF = float("inf")
F32_LOWEST = -3.0e38

LANES = 128
SUBLANES = 8
VMEM_LIMIT = 56 * 1024 * 1024

Q_BLOCK = 128
KEY_CHUNK = 512
MAX_BISECT = 320
LOG2E = 1.4426950408889634
MAX_EXP2_GAP = 80.0
NORM_BOUND_SLACK = 1.02
ONES_ROWS = 16

CONTRACT_LAST = (((1,), (1,)), ((), ()))


def _const_spec(shape):
    nd = len(shape)
    return pl.BlockSpec(shape, lambda *_: (0,) * nd, pipeline_mode=pl.Buffered(1))


def _rope128(x, c, s):
    return x * c + pltpu.roll(x, 64, 1) * s


def _rope64(x, c, sa, sb):
    return x * c + pltpu.roll(x, 96, 1) * sa + pltpu.roll(x, 32, 1) * sb


def _proj_kernel(x_ref, g_ref, c128_ref, s128_ref, c64_ref, sa64_ref, sb64_ref,
                 wq_ref, wkv_ref, wqi_ref, wkw_ref, wu_ref,
                 q_ref, kf_ref, vf_ref, kb_ref, vb_ref, qi_ref, kw_ref, u_ref):
    x = x_ref[...]
    ms = jnp.mean(x * x, axis=-1, keepdims=True)
    xn = ((x * lax.rsqrt(ms + EPS)) * g_ref[...]).astype(BF16)
    c128 = c128_ref[...]
    s128 = s128_ref[...]
    c64 = c64_ref[...]
    sa64 = sa64_ref[...]
    sb64 = sb64_ref[...]

    zq = jnp.dot(xn, wq_ref[...], preferred_element_type=F32)
    for h in range(N_HEADS):
        sl = slice(h * LANES, (h + 1) * LANES)
        q_ref[:, sl] = _rope128(zq[:, sl], c128, s128).astype(BF16)

    zkv = jnp.dot(xn, wkv_ref[...], preferred_element_type=F32)
    for h in range(N_KV_HEADS):
        sl = slice(h * LANES, (h + 1) * LANES)
        kr = _rope128(zkv[:, sl], c128, s128)
        kf_ref[:, sl] = kr
        kb_ref[:, sl] = kr.astype(BF16)
    v = zkv[:, N_KV_HEADS * LANES:]
    vf_ref[...] = v
    vb_ref[...] = v.astype(BF16)

    zqi = jnp.dot(xn, wqi_ref[...], preferred_element_type=F32)
    for p in range(N_IDX_HEADS // 2):
        sl = slice(p * LANES, (p + 1) * LANES)
        qi_ref[:, sl] = _rope64(zqi[:, sl], c64, sa64, sb64).astype(BF16)

    zkw = jnp.dot(xn, wkw_ref[...], preferred_element_type=F32)
    lane = lax.broadcasted_iota(jnp.int32, zkw.shape, 1)
    is_key = lane < IDX_DIM
    ckw = jnp.where(is_key, c64, jnp.where(lane < IDX_DIM + N_IDX_HEADS, N_IDX_HEADS ** -0.5, 1.0))
    kw_ref[...] = _rope64(zkw, ckw, jnp.where(is_key, sa64, 0.0), jnp.where(is_key, sb64, 0.0))

    u_ref[...] = jnp.dot(xn, wu_ref[...], preferred_element_type=F32)


def _project(x, pos, g_mix, w_parts, tm):
    rows = x.shape[0]
    assert rows % tm == 0
    c128, s128, c64, sa64, sb64 = _rope_tables(pos)
    wq, wkv, wqi, wkw, wu = w_parts
    row_spec = lambda w: pl.BlockSpec((tm, w), lambda i: (i, 0))
    out_shapes = (
        jax.ShapeDtypeStruct((rows, ATTN_WIDTH), BF16),
        jax.ShapeDtypeStruct((rows, 256), F32),
        jax.ShapeDtypeStruct((rows, 256), F32),
        jax.ShapeDtypeStruct((rows, 256), BF16),
        jax.ShapeDtypeStruct((rows, 256), BF16),
        jax.ShapeDtypeStruct((rows, 1024), BF16),
        jax.ShapeDtypeStruct((rows, LANES), F32),
        jax.ShapeDtypeStruct((rows, POOL_WIDTH), F32),
    )
    return pl.pallas_call(
        _proj_kernel,
        grid=(rows // tm,),
        in_specs=[row_spec(D_MODEL), _const_spec((1, D_MODEL))]
        + [row_spec(LANES)] * 5
        + [_const_spec(w.shape) for w in w_parts],
        out_specs=tuple(row_spec(s.shape[1]) for s in out_shapes),
        out_shape=out_shapes,
        compiler_params=pltpu.CompilerParams(
            dimension_semantics=("arbitrary",), vmem_limit_bytes=VMEM_LIMIT),
        name="proj",
    )(x, g_mix.reshape(1, D_MODEL), c128, s128, c64, sa64, sb64, wq, wkv, wqi, wkw, wu)


def _rope_tables(pos):
    posf = pos.astype(F32)[:, None]
    inv = ROPE_THETA ** (-jnp.arange(0, HEAD_DIM, 2, dtype=F32) / HEAD_DIM)
    ang = posf * inv[None, :]
    c, s = jnp.cos(ang), jnp.sin(ang)
    c128 = jnp.concatenate([c, c], axis=-1)
    s128 = jnp.concatenate([-s, s], axis=-1)
    inv = ROPE_THETA ** (-jnp.arange(0, IDX_DIM, 2, dtype=F32) / IDX_DIM)
    ang = posf * inv[None, :]
    c, s = jnp.cos(ang), jnp.sin(ang)
    z = jnp.zeros_like(s)
    c64 = jnp.concatenate([c, c, c, c], axis=-1)
    sa64 = jnp.concatenate([-s, z, -s, z], axis=-1)
    sb64 = jnp.concatenate([z, s, z, s], axis=-1)
    return c128, s128, c64, sa64, sb64


def _key_count(st_ref, nchunks, chunk, pred):
    acc_rows = 8 * SUBLANES

    def body(c, part):
        c0 = pl.multiple_of(c * chunk, chunk)
        hit = pred(st_ref[pl.ds(c0, chunk), :], c0).astype(F32)
        return part + jnp.sum(hit.reshape(chunk // acc_rows, acc_rows, LANES), axis=0)

    part = lax.fori_loop(0, nchunks, body, jnp.zeros((acc_rows, LANES), F32))
    return jnp.sum(part, axis=0, keepdims=True)


def _topk_threshold(st_ref, nchunks, chunk, n_valid, s_min, s_max):
    kf = float(TOPK)
    take_all = n_valid <= kf

    def count_ge(t):
        return _key_count(st_ref, nchunks, chunk, lambda v, c0: v >= t)

    def cond(st):
        it, done = st[0], st[5]
        return jnp.logical_and(it < MAX_BISECT, jnp.min(done) < 0.5)

    def step(st, interpolate):
        lo, hi, flo, fhi, done, tau = st
        half = lo + (hi - lo) * 0.5
        collapsed = jnp.logical_or(half <= lo, half >= hi)
        mid = half
        if interpolate:
            frac = (flo - (kf + 0.5)) / jnp.maximum(flo - fhi, 1.0)
            guess = lo + (hi - lo) * jnp.clip(frac, 1.0 / 32, 31.0 / 32)
            mid = jnp.where(jnp.logical_and(guess > lo, guess < hi), guess, half)
        cnt = count_ge(mid)
        found = cnt == kf
        active = done < 0.5
        go_up = jnp.logical_and(active, cnt > kf)
        go_dn = jnp.logical_and(active, cnt < kf)
        lo = jnp.where(go_up, mid, lo)
        hi = jnp.where(go_dn, mid, hi)
        flo = jnp.where(go_up, cnt, flo)
        fhi = jnp.where(go_dn, cnt, fhi)
        tau = jnp.where(jnp.logical_and(active, found), mid, tau)
        done = jnp.where(jnp.logical_or(found, collapsed), 1.0, done)
        return lo, hi, flo, fhi, done, tau

    def body(st):
        inner = step(step(st[1:], True), False)
        return (st[0] + 1,) + inner

    done0 = jnp.where(take_all, 1.0, 0.0).astype(F32)
    tau0 = jnp.full((1, LANES), POS_INF, F32)
    fhi0 = jnp.ones((1, LANES), F32)
    _, lo, hi, _, _, _, tau = lax.while_loop(
        cond, body, (jnp.int32(0), s_min, s_max, n_valid, fhi0, done0, tau0))

    unresolved = jnp.logical_and(tau == POS_INF, jnp.logical_not(take_all))
    n_unres = jnp.sum(unresolved.astype(F32))
    tau = jnp.where(take_all, F32_LOWEST, tau)

    def tie_path(tau):
        cut = jnp.where(count_ge(hi) >= kf, hi, lo)
        cut = jnp.where(unresolved, cut, POS_INF)
        c_gt = _key_count(st_ref, nchunks, chunk, lambda v, c0: v > cut)
        keep = kf - c_gt

        def key_iota(c0):
            return c0 + lax.broadcasted_iota(jnp.int32, (chunk, LANES), 0)

        def idx_body(_, st):
            xlo, xhi = st
            xmid = (xlo + xhi) // 2
            cnt = _key_count(
                st_ref, nchunks, chunk,
                lambda v, c0: jnp.logical_and(v == cut, key_iota(c0) < xmid))
            ok = cnt >= keep
            return jnp.where(ok, xlo, xmid), jnp.where(ok, xmid, xhi)

        xlo0 = jnp.zeros((1, LANES), jnp.int32)
        xhi0 = jnp.full((1, LANES), nchunks * chunk, jnp.int32)
        n_idx_steps = 15
        _, xcut = lax.fori_loop(0, n_idx_steps, idx_body, (xlo0, xhi0))

        def drop_body(c, carry):
            c0 = pl.multiple_of(c * chunk, chunk)
            v = st_ref[pl.ds(c0, chunk), :]
            drop = jnp.logical_and(v == cut, key_iota(c0) >= xcut)
            st_ref[pl.ds(c0, chunk), :] = jnp.where(drop, NEG_INF, v)
            return carry

        lax.fori_loop(0, nchunks, drop_body, 0)
        return jnp.where(unresolved, cut, tau)

    return lax.cond(n_unres > 0.0, tie_path, lambda t: t, tau)


def _prompt_attn_kernel(kmax2_ref, qi_ref, kw_ref, q_ref, kilo_ref, kihi_ref, k_ref, vt_ref, o_ref,
                        st_ref, m_ref, l_ref, acc_ref, accf_ref):
    i = pl.program_id(0)
    tq, ck = Q_BLOCK, KEY_CHUNK
    n_blocks = i + 2
    nchunks = (n_blocks * Q_BLOCK + ck - 1) // ck

    wt = jnp.transpose(kw_ref[...])[IDX_DIM:IDX_DIM + N_IDX_HEADS, :] * (IDX_DIM ** -0.5)
    qcat = jnp.concatenate(
        [qi_ref[:, p * LANES:(p + 1) * LANES] for p in range(N_IDX_HEADS // 2)], axis=0)
    key = lax.broadcasted_iota(jnp.int32, (ck, tq), 0)
    qry = i * tq + lax.broadcasted_iota(jnp.int32, (ck, tq), 1)

    def idx_body(c, st):
        smin, smax = st
        c0 = pl.multiple_of(c * ck, ck)
        klo = kilo_ref[pl.ds(c0, ck), :]
        khi = kihi_ref[pl.ds(c0, ck), :]
        acc = jnp.zeros((ck, tq), F32)
        for t in range(N_IDX_HEADS // 4):
            rhs = qcat[2 * t * tq:(2 * t + 2) * tq]
            dlo = lax.dot_general(klo, rhs, CONTRACT_LAST, preferred_element_type=F32)
            dhi = lax.dot_general(khi, rhs, CONTRACT_LAST, preferred_element_type=F32)
            for j in range(2):
                p = 2 * t + j
                cs = slice(j * tq, (j + 1) * tq)
                acc = acc + wt[2 * p:2 * p + 1, :] * jnp.maximum(dlo[:, cs], 0.0)
                acc = acc + wt[2 * p + 1:2 * p + 2, :] * jnp.maximum(dhi[:, cs], 0.0)
        col = c0 + key
        valid = jnp.logical_or(
            col < N_META,
            jnp.logical_and(col >= Q_BLOCK, col - Q_BLOCK <= qry))
        st_ref[pl.ds(c0, ck), :] = jnp.where(valid, acc, NEG_INF)
        smin = jnp.minimum(smin, jnp.min(jnp.where(valid, acc, POS_INF), axis=0, keepdims=True))
        smax = jnp.maximum(smax, jnp.max(jnp.where(valid, acc, NEG_INF), axis=0, keepdims=True))
        return smin, smax

    smin, smax = lax.fori_loop(
        0, nchunks, idx_body,
        (jnp.full((1, tq), POS_INF, F32), jnp.full((1, tq), NEG_INF, F32)))

    n_valid = (N_META + 1 + i * tq + lax.broadcasted_iota(jnp.int32, (1, tq), 1)).astype(F32)
    tau = _topk_threshold(st_ref, nchunks, ck, n_valid, smin, smax)

    scale = HEAD_DIM ** -0.5

    def q_group(n):
        return jnp.concatenate(
            [q_ref[:, (n * GROUP + g) * LANES:(n * GROUP + g + 1) * LANES]
             for g in range(GROUP)], axis=0)

    def k_chunk(c0, n):
        return k_ref[pl.ds(c0, ck), n * LANES:(n + 1) * LANES]

    def write_out(n, ot):
        for g in range(GROUP):
            h = n * GROUP + g
            o_ref[:, h * LANES:(h + 1) * LANES] = jnp.transpose(
                ot[:, g * tq:(g + 1) * tq]).astype(o_ref.dtype)

    n_ref_chunks = jnp.minimum(nchunks, 2)

    def ref_body(c, mx):
        c0 = pl.multiple_of(c * ck, ck)
        sel = st_ref[pl.ds(c0, ck), :] >= tau
        out = []
        for n in range(N_KV_HEADS):
            raw = lax.dot_general(k_chunk(c0, n), q_group(n), CONTRACT_LAST,
                                  preferred_element_type=F32)
            cur = jnp.concatenate(
                [jnp.max(jnp.where(sel, raw[:, g * tq:(g + 1) * tq], NEG), axis=0, keepdims=True)
                 for g in range(GROUP)], axis=1)
            out.append(jnp.maximum(mx[n], cur))
        return tuple(out)

    mx = lax.fori_loop(0, n_ref_chunks, ref_body,
                       tuple(jnp.full((1, GROUP * tq), NEG, F32) for _ in range(N_KV_HEADS)))
    c2 = scale * LOG2E
    ones = jnp.ones((SUBLANES, HEAD_DIM), BF16)
    gap = jnp.float32(0.0)
    for n in range(N_KV_HEADS):
        qf = q_group(n).astype(F32)
        qn2 = lax.dot_general(ones, (qf * qf).astype(BF16), CONTRACT_LAST,
                              preferred_element_type=F32)[0:1]
        bound = jnp.sqrt(qn2 * kmax2_ref[n]) * (c2 * NORM_BOUND_SLACK)
        gap = jnp.maximum(gap, jnp.max(bound - mx[n] * c2))

    def fixed_reference_path():
        accf_ref[...] = jnp.zeros(accf_ref.shape, F32)

        def body(c, carry):
            c0 = pl.multiple_of(c * ck, ck)
            sel = st_ref[pl.ds(c0, ck), :] >= tau
            for n in range(N_KV_HEADS):
                raw = lax.dot_general(k_chunk(c0, n), q_group(n), CONTRACT_LAST,
                                      preferred_element_type=F32)
                neg_ref = mx[n] * (-c2)
                p = jnp.concatenate(
                    [jnp.exp2(raw[:, g * tq:(g + 1) * tq] * c2
                              + jnp.where(sel, neg_ref[:, g * tq:(g + 1) * tq], NEG)).astype(BF16)
                     for g in range(GROUP)], axis=1)
                accf_ref[n] += jnp.dot(vt_ref[c, n], p, preferred_element_type=F32)
            return carry

        lax.fori_loop(0, nchunks, body, 0)
        for n in range(N_KV_HEADS):
            write_out(n, accf_ref[n, 0:HEAD_DIM, :] / accf_ref[n, HEAD_DIM:HEAD_DIM + 1, :])

    def running_max_path():
        m_ref[...] = jnp.full(m_ref.shape, NEG, F32)
        l_ref[...] = jnp.zeros(l_ref.shape, F32)
        acc_ref[...] = jnp.zeros(acc_ref.shape, F32)

        def body(c, carry):
            c0 = pl.multiple_of(c * ck, ck)
            bias = jnp.where(st_ref[pl.ds(c0, ck), :] >= tau, 0.0, NEG)
            bias = jnp.concatenate([bias] * GROUP, axis=1)
            for n in range(N_KV_HEADS):
                sc = lax.dot_general(k_chunk(c0, n), q_group(n), CONTRACT_LAST,
                                     preferred_element_type=F32)
                sc = sc * scale + bias
                m_old = m_ref[n]
                m_new = jnp.maximum(m_old, jnp.max(sc, axis=0, keepdims=True))
                alpha = jnp.exp(m_old - m_new)
                p = jnp.exp(sc - m_new)
                l_ref[n] = alpha * l_ref[n] + jnp.sum(
                    p.reshape(ck // SUBLANES, SUBLANES, GROUP * tq), axis=0)
                acc_ref[n] = alpha * acc_ref[n] + jnp.dot(
                    vt_ref[c, n, 0:HEAD_DIM, :], p.astype(BF16), preferred_element_type=F32)
                m_ref[n] = m_new
            return carry

        lax.fori_loop(0, nchunks, body, 0)
        for n in range(N_KV_HEADS):
            write_out(n, acc_ref[n] / jnp.sum(l_ref[n], axis=0, keepdims=True))

    lax.cond(gap <= MAX_EXP2_GAP, fixed_reference_path, running_max_path)


def _prompt_attention(kmax2, qi, kw, q, ki_lo, ki_hi, k_all, v_t):
    rows = q.shape[0]
    nk = k_all.shape[0]
    assert rows % Q_BLOCK == 0 and nk % KEY_CHUNK == 0
    assert nk >= (rows // Q_BLOCK + 1) * Q_BLOCK
    assert v_t.shape == (nk // KEY_CHUNK, N_KV_HEADS, HEAD_DIM + ONES_ROWS, KEY_CHUNK)
    row_spec = lambda w: pl.BlockSpec((Q_BLOCK, w), lambda i: (i, 0))
    return pl.pallas_call(
        _prompt_attn_kernel,
        grid=(rows // Q_BLOCK,),
        in_specs=[pl.BlockSpec(memory_space=pltpu.SMEM),
                  row_spec(1024), row_spec(LANES), row_spec(ATTN_WIDTH),
                  _const_spec(ki_lo.shape), _const_spec(ki_hi.shape),
                  _const_spec(k_all.shape), _const_spec(v_t.shape)],
        out_specs=row_spec(ATTN_WIDTH),
        out_shape=jax.ShapeDtypeStruct((rows, ATTN_WIDTH), BF16),
        scratch_shapes=[
            pltpu.VMEM((nk, Q_BLOCK), F32),
            pltpu.VMEM((N_KV_HEADS, 1, GROUP * Q_BLOCK), F32),
            pltpu.VMEM((N_KV_HEADS, SUBLANES, GROUP * Q_BLOCK), F32),
            pltpu.VMEM((N_KV_HEADS, HEAD_DIM, GROUP * Q_BLOCK), F32),
            pltpu.VMEM((N_KV_HEADS, HEAD_DIM + ONES_ROWS, GROUP * Q_BLOCK), F32),
        ],
        compiler_params=pltpu.CompilerParams(
            dimension_semantics=("arbitrary",), vmem_limit_bytes=VMEM_LIMIT),
        name="prompt_attn",
    )(kmax2, qi, kw, q, ki_lo, ki_hi, k_all, v_t)


def _decode_scores_kernel(pt_ref, qi_ref, w_ref, kinew_ref, *rest):
    n_pages = len(rest) - 1
    page_refs, s_ref = rest[:n_pages], rest[n_pages]
    qi = qi_ref[...]
    w = w_ref[...] * (IDX_DIM ** -0.5)
    for j in range(n_pages):
        kpt = page_refs[j][...].astype(BF16)
        d = jnp.dot(qi, kpt, preferred_element_type=F32)
        s_ref[:, j * PAGE_SIZE:(j + 1) * PAGE_SIZE] = jnp.sum(
            jnp.maximum(d, 0.0) * w, axis=0, keepdims=True)
    d_new = jnp.sum(qi.astype(F32) * kinew_ref[...].astype(F32), axis=-1, keepdims=True)
    s_new = jnp.sum(jnp.maximum(d_new, 0.0) * w, axis=0, keepdims=True)
    past = n_pages * PAGE_SIZE
    tail = s_ref.shape[1] - past
    lane = lax.broadcasted_iota(jnp.int32, (1, tail), 1)
    s_ref[:, past:] = jnp.where(lane == 0, s_new, NEG_INF)


def _page_specs(block, n_pages):
    nd = len(block)
    return [
        pl.BlockSpec((None, None) + block, functools.partial(
            lambda b, pt, j: (0, pt[b, j]) + (0,) * nd, j=j)) for j in range(n_pages)]


def _decode_scores(page_table, qi_s, w_s, kinew_s, cache_kidx, width):
    db, n_pages = page_table.shape
    seq_spec = lambda a, c: pl.BlockSpec((None, a, c), lambda b, pt: (b, 0, 0))
    return pl.pallas_call(
        _decode_scores_kernel,
        grid_spec=pltpu.PrefetchScalarGridSpec(
            num_scalar_prefetch=1,
            grid=(db,),
            in_specs=[seq_spec(N_IDX_HEADS, IDX_DIM), seq_spec(N_IDX_HEADS, 1),
                      seq_spec(1, IDX_DIM)] + _page_specs((IDX_DIM, PAGE_SIZE), n_pages),
            out_specs=seq_spec(1, width),
        ),
        out_shape=jax.ShapeDtypeStruct((db, 1, width), F32),
        compiler_params=pltpu.CompilerParams(dimension_semantics=("arbitrary",)),
        name="decode_scores",
    )(page_table, qi_s, w_s, kinew_s, *([cache_kidx] * n_pages))


def _decode_mask_kernel(s_in_ref, bias_ref, st_ref):
    nk = s_in_ref.shape[0]
    chunk = KEY_CHUNK
    s = s_in_ref[...]
    st_ref[...] = s
    valid = s > NEG_INF
    smin = jnp.min(jnp.where(valid, s, POS_INF), axis=0, keepdims=True)
    smax = jnp.max(s, axis=0, keepdims=True)
    n_valid = jnp.sum(valid.astype(F32), axis=0, keepdims=True)
    tau = _topk_threshold(st_ref, nk // chunk, chunk, n_valid, smin, smax)
    bias_ref[...] = jnp.where(st_ref[...] >= tau, 0.0, NEG)


def _decode_mask(scores_t):
    assert scores_t.shape[0] % KEY_CHUNK == 0 and scores_t.shape[1] == LANES
    return pl.pallas_call(
        _decode_mask_kernel,
        out_shape=jax.ShapeDtypeStruct(scores_t.shape, F32),
        scratch_shapes=[pltpu.VMEM(scores_t.shape, F32)],
        name="decode_mask",
    )(scores_t)


def _decode_attn_kernel(pt_ref, q_ref, bias_ref, knew_ref, vnew_ref, *rest):
    n_pages = (len(rest) - 1) // 2
    k_pages, v_pages, o_ref = rest[:n_pages], rest[n_pages:2 * n_pages], rest[2 * n_pages]
    rows = N_KV_HEADS * n_pages * PAGE_SIZE
    scale = HEAD_DIM ** -0.5
    q = q_ref[...]
    bias = jnp.concatenate(
        [jnp.broadcast_to(bias_ref[n:n + 1, :], (GROUP, bias_ref.shape[1]))
         for n in range(N_KV_HEADS)], axis=0)
    k_il = jnp.concatenate([r[...].astype(BF16) for r in k_pages], axis=0)
    v_il = jnp.concatenate([r[...].astype(BF16) for r in v_pages], axis=0)
    sc = lax.dot_general(q, k_il, CONTRACT_LAST, preferred_element_type=F32)
    sc = sc * scale + bias[:, :rows]
    sc_new = jnp.sum(q.astype(F32) * knew_ref[...].astype(F32), axis=-1, keepdims=True)
    sc_new = sc_new * scale + jnp.max(bias[:, rows:rows + N_KV_HEADS], axis=-1, keepdims=True)
    m = jnp.maximum(jnp.max(sc, axis=-1, keepdims=True), sc_new)
    p = jnp.exp(sc - m)
    p_new = jnp.exp(sc_new - m)
    denom = jnp.sum(p, axis=-1, keepdims=True) + p_new
    o = jnp.dot(p.astype(BF16), v_il, preferred_element_type=F32)
    o = o + p_new.astype(BF16).astype(F32) * vnew_ref[...].astype(F32)
    o_ref[...] = (o / denom).astype(o_ref.dtype)


def _decode_attention(page_table, q_s, bias_il, knew_s, vnew_s, cache_k, cache_v):
    db, n_pages = page_table.shape
    width2 = bias_il.shape[-1]
    seq_spec = lambda a, c: pl.BlockSpec((None, a, c), lambda b, pt: (b, 0, 0))
    page_specs = _page_specs((N_KV_HEADS * PAGE_SIZE, HEAD_DIM), n_pages)
    return pl.pallas_call(
        _decode_attn_kernel,
        grid_spec=pltpu.PrefetchScalarGridSpec(
            num_scalar_prefetch=1,
            grid=(db,),
            in_specs=[seq_spec(N_HEADS, HEAD_DIM), seq_spec(N_KV_HEADS, width2),
                      seq_spec(N_HEADS, HEAD_DIM), seq_spec(N_HEADS, HEAD_DIM)]
            + page_specs + page_specs,
            out_specs=seq_spec(N_HEADS, HEAD_DIM),
        ),
        out_shape=jax.ShapeDtypeStruct((db, N_HEADS, HEAD_DIM), BF16),
        compiler_params=pltpu.CompilerParams(
            dimension_semantics=("arbitrary",), vmem_limit_bytes=VMEM_LIMIT),
        name="decode_attn",
    )(page_table, q_s, bias_il, knew_s, vnew_s, *([cache_k] * n_pages), *([cache_v] * n_pages))


POOL_TILE = 512
POOL_HALO = 16


def _pool_prompt_kernel(u_ref, uprev_ref, umeta_ref, d_ref, ubuf):
    i = pl.program_id(0)
    tp = u_ref.shape[0]
    ubuf[0:POOL_HALO, :] = jnp.where(i == 0, umeta_ref[...], uprev_ref[...])
    ubuf[POOL_HALO:, :] = u_ref[...]
    for g, w in enumerate(POOL_WINDOWS):
        cols = slice(g * POOL_GROUP_WIDTH, (g + 1) * POOL_GROUP_WIDTH)
        cur = ubuf[POOL_HALO:POOL_HALO + tp, cols]
        tot = cur
        for j in range(1, w):
            tot = tot + ubuf[POOL_HALO - j:POOL_HALO - j + tp, cols]
        d_ref[:, cols] = tot * (1.0 / w) - cur


def _pool_prompt(u, u_meta):
    rows = u.shape[0]
    tp = POOL_TILE
    per = tp // POOL_HALO
    return pl.pallas_call(
        _pool_prompt_kernel,
        grid=(rows // tp,),
        in_specs=[pl.BlockSpec((tp, POOL_WIDTH), lambda i: (i, 0)),
                  pl.BlockSpec((POOL_HALO, POOL_WIDTH), lambda i: (jnp.maximum(i * per - 1, 0), 0)),
                  _const_spec((POOL_HALO, POOL_WIDTH))],
        out_specs=pl.BlockSpec((tp, POOL_WIDTH), lambda i: (i, 0)),
        out_shape=jax.ShapeDtypeStruct((rows, POOL_WIDTH), F32),
        scratch_shapes=[pltpu.VMEM((tp + POOL_HALO, POOL_WIDTH), F32)],
        compiler_params=pltpu.CompilerParams(dimension_semantics=("arbitrary",)),
        name="pool_prompt",
    )(u, u, u_meta)


def _pool_decode_kernel(hist_ref, u_ref, d_ref):
    for g, w in enumerate(POOL_WINDOWS):
        cols = slice(g * POOL_GROUP_WIDTH, (g + 1) * POOL_GROUP_WIDTH)
        cur = u_ref[:, cols]
        tot = cur
        for j in range(1, w):
            tot = tot + hist_ref[POOL_HIST - j, :, cols]
        d_ref[:, cols] = tot * (1.0 / w) - cur


def _pool_decode(hist, u):
    return pl.pallas_call(
        _pool_decode_kernel,
        out_shape=jax.ShapeDtypeStruct(u.shape, F32),
        name="pool_decode",
    )(hist, u)


FF_TILE = 512


def _rmsnorm(x, g):
    ms = jnp.mean(x * x, axis=-1, keepdims=True)
    return (x * lax.rsqrt(ms + EPS)) * g


def _post_kernel(x_ref, att_ref, d_ref, woa_ref, wop_ref, wpool_ref, pscale_ref,
                 gmlp_ref, gfin_ref, wup_ref, wdn_ref, y_ref, h_ref, hn_ref, acc_ref):
    f = pl.program_id(1)

    @pl.when(f == 0)
    def _():
        d = d_ref[...]
        pooled = jnp.concatenate(
            [jnp.dot(d[:, g * POOL_GROUP_WIDTH:(g + 1) * POOL_GROUP_WIDTH].astype(BF16),
                     wpool_ref[g], preferred_element_type=F32)
             for g in range(len(POOL_WINDOWS))], axis=1) * pscale_ref[...]
        h = (x_ref[...]
             + jnp.dot(att_ref[...], woa_ref[...], preferred_element_type=F32)
             + jnp.dot(pooled.astype(BF16), wop_ref[...], preferred_element_type=F32))
        h_ref[...] = h
        hn_ref[...] = _rmsnorm(h, gmlp_ref[...]).astype(BF16)
        acc_ref[...] = jnp.zeros(acc_ref.shape, F32)

    a = jnp.maximum(jnp.dot(hn_ref[...], wup_ref[...], preferred_element_type=F32), 0.0)
    acc_ref[...] += jnp.dot((a * a).astype(BF16), wdn_ref[...], preferred_element_type=F32)

    @pl.when(f == pl.num_programs(1) - 1)
    def _():
        y_ref[...] = _rmsnorm(h_ref[...] + acc_ref[...], gfin_ref[...])


def _post(x, att, d, wo_a, wo_p, w_pool, pool_scale, g_mlp, g_final, w_up, w_down, tm):
    rows = x.shape[0]
    assert rows % tm == 0 and D_FF % FF_TILE == 0
    row_spec = lambda w: pl.BlockSpec((tm, w), lambda i, f: (i, 0))
    return pl.pallas_call(
        _post_kernel,
        grid=(rows // tm, D_FF // FF_TILE),
        in_specs=[row_spec(D_MODEL), row_spec(ATTN_WIDTH), row_spec(POOL_WIDTH),
                  _const_spec(wo_a.shape), _const_spec(wo_p.shape), _const_spec(w_pool.shape),
                  _const_spec((1, POOL_WIDTH)), _const_spec((1, D_MODEL)), _const_spec((1, D_MODEL)),
                  pl.BlockSpec((D_MODEL, FF_TILE), lambda i, f: (0, f)),
                  pl.BlockSpec((FF_TILE, D_MODEL), lambda i, f: (f, 0))],
        out_specs=row_spec(D_MODEL),
        out_shape=jax.ShapeDtypeStruct((rows, D_MODEL), F32),
        scratch_shapes=[pltpu.VMEM((tm, D_MODEL), F32), pltpu.VMEM((tm, D_MODEL), BF16),
                        pltpu.VMEM((tm, D_MODEL), F32)],
        compiler_params=pltpu.CompilerParams(
            dimension_semantics=("arbitrary", "arbitrary"), vmem_limit_bytes=VMEM_LIMIT),
        name="post",
    )(x, att, d, wo_a, wo_p, w_pool, pool_scale.reshape(1, POOL_WIDTH),
      g_mlp.reshape(1, D_MODEL), g_final.reshape(1, D_MODEL), w_up, w_down)


def kernel(x_prompt, x_sample, cache_k, cache_v, cache_kidx, state_pool, page_table, meta_tokens,
           g_mix, w_in, w_pool, pool_scale, w_out, g_mlp, w_up, w_down, g_final):
    assert x_prompt.shape[0] == 1 and x_sample.shape[1] == 1 and g_mix.shape[0] == 1
    seq = x_prompt.shape[1]
    db = x_sample.shape[0]
    n_pages = page_table.shape[1]
    past = n_pages * PAGE_SIZE
    assert db == LANES

    w = w_in[0]
    o = 0
    parts = []
    for width in (ATTN_WIDTH, 2 * N_KV_HEADS * HEAD_DIM, N_IDX_HEADS * IDX_DIM,
                  IDX_DIM + N_IDX_HEADS, POOL_WIDTH):
        parts.append(w[:, o:o + width].astype(BF16))
        o += width
    parts[3] = jnp.pad(parts[3], ((0, 0), (0, LANES - parts[3].shape[1])))
    wo_a = w_out[0, :ATTN_WIDTH].astype(BF16)
    wo_p = w_out[0, ATTN_WIDTH:].astype(BF16)
    w_pool_b = w_pool[0].astype(BF16)
    w_up_b = w_up[0].astype(BF16)
    w_down_b = w_down[0].astype(BF16)

    xp = x_prompt[0]
    pos_p = N_META + jnp.arange(seq, dtype=jnp.int32)
    q_p, kf_p, vf_p, kb_p, vb_p, qi_p, kw_p, u_p = _project(xp, pos_p, g_mix[0], parts, 256)
    xs = jnp.concatenate([meta_tokens.astype(F32), x_sample[:, 0]], axis=0)
    pos_s = jnp.concatenate([jnp.arange(N_META, dtype=jnp.int32),
                             jnp.full((db,), past, jnp.int32)])
    q_s, kf_s, vf_s, kb_s, vb_s, qi_s, kw_s, u_s = _project(xs, pos_s, g_mix[0], parts, xs.shape[0])

    nk = -(-(seq + 2 * Q_BLOCK) // KEY_CHUNK) * KEY_CHUNK

    def key_rows(meta_part, prompt_part):
        width = meta_part.shape[1]
        return jnp.concatenate(
            [meta_part, jnp.zeros((Q_BLOCK - N_META, width), meta_part.dtype), prompt_part,
             jnp.zeros((nk - Q_BLOCK - seq, width), meta_part.dtype)], axis=0)

    ki_all = key_rows(kw_s[:N_META, :IDX_DIM], kw_p[:, :IDX_DIM]).astype(BF16)
    zk = jnp.zeros_like(ki_all)
    ki_lo = jnp.concatenate([ki_all, zk], axis=1)
    ki_hi = jnp.concatenate([zk, ki_all], axis=1)
    k_all = key_rows(kb_s[:N_META], kb_p)
    v_all = key_rows(vb_s[:N_META], vb_p)
    v_t = jnp.transpose(
        v_all.reshape(nk // KEY_CHUNK, KEY_CHUNK, N_KV_HEADS, HEAD_DIM), (0, 2, 3, 1))
    v_t = jnp.concatenate(
        [v_t, jnp.ones((nk // KEY_CHUNK, N_KV_HEADS, ONES_ROWS, KEY_CHUNK), BF16)], axis=2)
    kf = k_all.astype(F32).reshape(nk, N_KV_HEADS, HEAD_DIM)
    kmax2 = jnp.max(jnp.sum(kf * kf, axis=-1), axis=0)
    att_p = _prompt_attention(kmax2, qi_p, kw_p, q_p, ki_lo, ki_hi, k_all, v_t)

    width = -(-(past + 1) // KEY_CHUNK) * KEY_CHUNK
    qi_d = qi_s[N_META:].reshape(db, N_IDX_HEADS, IDX_DIM)
    w_d = kw_s[N_META:, IDX_DIM:IDX_DIM + N_IDX_HEADS].reshape(db, N_IDX_HEADS, 1)
    kinew = kw_s[N_META:, :IDX_DIM].astype(BF16).reshape(db, 1, IDX_DIM)
    scores = _decode_scores(page_table, qi_d, w_d, kinew, jnp.swapaxes(cache_kidx, 2, 3), width)
    bias_t = _decode_mask(jnp.transpose(scores.reshape(db, width)))
    bias = jnp.transpose(bias_t)
    negs = jnp.full_like(bias, NEG)
    bias_il = jnp.stack(
        [jnp.stack([bias, negs], axis=-1), jnp.stack([negs, bias], axis=-1)], axis=1,
    ).reshape(db, N_KV_HEADS, N_KV_HEADS * width)
    n_phys = cache_k.shape[1]
    il_shape = (1, n_phys, N_KV_HEADS * PAGE_SIZE, HEAD_DIM)
    per_head = lambda a: jnp.repeat(a.reshape(db, N_KV_HEADS, HEAD_DIM), GROUP, axis=1)
    att_s = _decode_attention(
        page_table, q_s[N_META:].reshape(db, N_HEADS, HEAD_DIM), bias_il,
        per_head(kb_s[N_META:]), per_head(vb_s[N_META:]),
        cache_k.reshape(il_shape), cache_v.reshape(il_shape))
    att_s = att_s.reshape(db, ATTN_WIDTH)

    d_p = _pool_prompt(u_p, u_s[:N_META])
    d_s = _pool_decode(jnp.swapaxes(state_pool[0], 0, 1), u_s[N_META:])

    post = functools.partial(
        _post, wo_a=wo_a, wo_p=wo_p, w_pool=w_pool_b, pool_scale=pool_scale[0],
        g_mlp=g_mlp[0], g_final=g_final, w_up=w_up_b, w_down=w_down_b)
    y_p = post(xp, att_p, d_p, tm=512)
    y_s = post(x_sample[:, 0], att_s, d_s, tm=db)

    k_prompt = jnp.concatenate([kf_s[:N_META], kf_p], axis=0).reshape(1, 1, N_META + seq, N_KV_HEADS, HEAD_DIM)
    v_prompt = jnp.concatenate([vf_s[:N_META], vf_p], axis=0).reshape(1, 1, N_META + seq, N_KV_HEADS, HEAD_DIM)
    kidx_prompt = jnp.concatenate([kw_s[:N_META, :IDX_DIM], kw_p[:, :IDX_DIM]], axis=0).reshape(
        1, 1, N_META + seq, IDX_DIM)
    pool_prompt = u_p[seq - POOL_HIST:].reshape(1, 1, POOL_HIST, POOL_WIDTH)
    k_sample = kf_s[N_META:].reshape(1, db, 1, N_KV_HEADS, HEAD_DIM)
    v_sample = vf_s[N_META:].reshape(1, db, 1, N_KV_HEADS, HEAD_DIM)
    kidx_sample = kw_s[N_META:, :IDX_DIM].reshape(1, db, 1, IDX_DIM)
    pool_sample = jnp.concatenate(
        [state_pool[0][:, 1:], u_s[N_META:][:, None, :]], axis=1)[None]
    return (y_p.reshape(1, seq, D_MODEL), y_s.reshape(db, 1, D_MODEL),
            k_prompt, v_prompt, kidx_prompt, pool_prompt,
            k_sample, v_sample, kidx_sample, pool_sample)
```

```python
import functools

import jax
import jax.numpy as jnp
from jax import lax
from jax.experimental import pallas as pl
from jax.experimental.pallas import tpu as pltpu

F32 = jnp.float32
BF16 = jnp.bfloat16

D_MODEL = 2048
N_META = 16
ATTN_WIDTH = 1024
POOL_WIDTH = 1024
HEAD_DIM = 128
N_HEADS = 8
N_KV_HEADS = 2
GROUP = N_HEADS // N_KV_HEADS
N_IDX_HEADS = 16
IDX_DIM = 64
TOPK = 256
POOL_WINDOWS = (2, 4, 8, 16)
POOL_GROUP_WIDTH = 256
POOL_HIST = 15
D_FF = 8192
PAGE_SIZE = 128
ROPE_THETA = 10000.0
EPS = 1e-6
NEG = -1e30
NEG_INF = float("-inf")
POS_INF = float("inf")
F32_LOWEST = -3.0e38

LANES = 128
SUBLANES = 8
VMEM_LIMIT = 56 * 1024 * 1024

Q_BLOCK = 128
KEY_CHUNK = 512
MAX_BISECT = 320
LOG2E = 1.4426950408889634
MAX_EXP2_GAP = 80.0
NORM_BOUND_SLACK = 1.02
ONES_ROWS = 16
KNORM_LANE = IDX_DIM + N_IDX_HEADS

CONTRACT_LAST = (((1,), (1,)), ((), ()))


def _const_spec(shape):
    nd = len(shape)
    return pl.BlockSpec(shape, lambda *_: (0,) * nd, pipeline_mode=pl.Buffered(1))


def _rope128(x, c, s):
    return x * c + pltpu.roll(x, 64, 1) * s


def _rope64(x, c, sa, sb):
    return x * c + pltpu.roll(x, 96, 1) * sa + pltpu.roll(x, 32, 1) * sb


def _proj_kernel(x_ref, g_ref, c128_ref, s128_ref, c64_ref, sa64_ref, sb64_ref,
                 wq_ref, wkv_ref, wqi_ref, wkw_ref, wu_ref,
                 q_ref, kf_ref, vf_ref, kb_ref, vb_ref, qi_ref, kw_ref, u_ref):
    x = x_ref[...]
    ms = jnp.mean(x * x, axis=-1, keepdims=True)
    xn = ((x * lax.rsqrt(ms + EPS)) * g_ref[...]).astype(BF16)
    c128 = c128_ref[...]
    s128 = s128_ref[...]
    c64 = c64_ref[...]
    sa64 = sa64_ref[...]
    sb64 = sb64_ref[...]

    zq = jnp.dot(xn, wq_ref[...], preferred_element_type=F32)
    for h in range(N_HEADS):
        sl = slice(h * LANES, (h + 1) * LANES)
        q_ref[:, sl] = _rope128(zq[:, sl], c128, s128).astype(BF16)

    zkv = jnp.dot(xn, wkv_ref[...], preferred_element_type=F32)
    k_norm2 = []
    for h in range(N_KV_HEADS):
        sl = slice(h * LANES, (h + 1) * LANES)
        kr = _rope128(zkv[:, sl], c128, s128)
        kf_ref[:, sl] = kr
        kb = kr.astype(BF16)
        kb_ref[:, sl] = kb
        k_norm2.append(jnp.sum(kb.astype(F32) * kb.astype(F32), axis=-1, keepdims=True))
    v = zkv[:, N_KV_HEADS * LANES:]
    vf_ref[...] = v
    vb_ref[...] = v.astype(BF16)

    zqi = jnp.dot(xn, wqi_ref[...], preferred_element_type=F32)
    for p in range(N_IDX_HEADS // 2):
        sl = slice(p * LANES, (p + 1) * LANES)
        qi_ref[:, sl] = _rope64(zqi[:, sl], c64, sa64, sb64).astype(BF16)

    zkw = jnp.dot(xn, wkw_ref[...], preferred_element_type=F32)
    lane = lax.broadcasted_iota(jnp.int32, zkw.shape, 1)
    is_key = lane < IDX_DIM
    ckw = jnp.where(is_key, c64, jnp.where(lane < IDX_DIM + N_IDX_HEADS, N_IDX_HEADS ** -0.5, 1.0))
    kw = _rope64(zkw, ckw, jnp.where(is_key, sa64, 0.0), jnp.where(is_key, sb64, 0.0))
    for h in range(N_KV_HEADS):
        kw = jnp.where(lane == KNORM_LANE + h, k_norm2[h], kw)
    kw_ref[...] = kw

    u_ref[...] = jnp.dot(xn, wu_ref[...], preferred_element_type=F32)


def _project(x, pos, g_mix, w_parts, tm):
    rows = x.shape[0]
    assert rows % tm == 0
    c128, s128, c64, sa64, sb64 = _rope_tables(pos)
    wq, wkv, wqi, wkw, wu = w_parts
    row_spec = lambda w: pl.BlockSpec((tm, w), lambda i: (i, 0))
    out_shapes = (
        jax.ShapeDtypeStruct((rows, ATTN_WIDTH), BF16),
        jax.ShapeDtypeStruct((rows, 256), F32),
        jax.ShapeDtypeStruct((rows, 256), F32),
        jax.ShapeDtypeStruct((rows, 256), BF16),
        jax.ShapeDtypeStruct((rows, 256), BF16),
        jax.ShapeDtypeStruct((rows, 1024), BF16),
        jax.ShapeDtypeStruct((rows, LANES), F32),
        jax.ShapeDtypeStruct((rows, POOL_WIDTH), F32),
    )
    return pl.pallas_call(
        _proj_kernel,
        grid=(rows // tm,),
        in_specs=[row_spec(D_MODEL), _const_spec((1, D_MODEL))]
        + [row_spec(LANES)] * 5
        + [_const_spec(w.shape) for w in w_parts],
        out_specs=tuple(row_spec(s.shape[1]) for s in out_shapes),
        out_shape=out_shapes,
        compiler_params=pltpu.CompilerParams(
            dimension_semantics=("arbitrary",), vmem_limit_bytes=VMEM_LIMIT),
        name="proj",
    )(x, g_mix.reshape(1, D_MODEL), c128, s128, c64, sa64, sb64, wq, wkv, wqi, wkw, wu)


def _rope_tables(pos):
    posf = pos.astype(F32)[:, None]
    inv = ROPE_THETA ** (-jnp.arange(0, HEAD_DIM, 2, dtype=F32) / HEAD_DIM)
    ang = posf * inv[None, :]
    c, s = jnp.cos(ang), jnp.sin(ang)
    c128 = jnp.concatenate([c, c], axis=-1)
    s128 = jnp.concatenate([-s, s], axis=-1)
    inv = ROPE_THETA ** (-jnp.arange(0, IDX_DIM, 2, dtype=F32) / IDX_DIM)
    ang = posf * inv[None, :]
    c, s = jnp.cos(ang), jnp.sin(ang)
    z = jnp.zeros_like(s)
    c64 = jnp.concatenate([c, c, c, c], axis=-1)
    sa64 = jnp.concatenate([-s, z, -s, z], axis=-1)
    sb64 = jnp.concatenate([z, s, z, s], axis=-1)
    return c128, s128, c64, sa64, sb64


def _key_count(st_ref, nchunks, chunk, pred):
    acc_rows = 8 * SUBLANES

    def body(c, part):
        c0 = pl.multiple_of(c * chunk, chunk)
        hit = pred(st_ref[pl.ds(c0, chunk), :], c0).astype(F32)
        return part + jnp.sum(hit.reshape(chunk // acc_rows, acc_rows, LANES), axis=0)

    part = lax.fori_loop(0, nchunks, body, jnp.zeros((acc_rows, LANES), F32))
    return jnp.sum(part, axis=0, keepdims=True)


def _topk_threshold(st_ref, nchunks, chunk, n_valid, s_min, s_max):
    kf = float(TOPK)
    take_all = n_valid <= kf

    def count_ge(t):
        return _key_count(st_ref, nchunks, chunk, lambda v, c0: v >= t)

    def cond(st):
        it, done = st[0], st[5]
        return jnp.logical_and(it < MAX_BISECT, jnp.min(done) < 0.5)

    def step(st, interpolate):
        lo, hi, flo, fhi, done, tau = st
        half = lo + (hi - lo) * 0.5
        collapsed = jnp.logical_or(half <= lo, half >= hi)
        mid = half
        if interpolate:
            frac = (flo - (kf + 0.5)) / jnp.maximum(flo - fhi, 1.0)
            guess = lo + (hi - lo) * jnp.clip(frac, 1.0 / 32, 31.0 / 32)
            mid = jnp.where(jnp.logical_and(guess > lo, guess < hi), guess, half)
        cnt = count_ge(mid)
        found = cnt == kf
        active = done < 0.5
        go_up = jnp.logical_and(active, cnt > kf)
        go_dn = jnp.logical_and(active, cnt < kf)
        lo = jnp.where(go_up, mid, lo)
        hi = jnp.where(go_dn, mid, hi)
        flo = jnp.where(go_up, cnt, flo)
        fhi = jnp.where(go_dn, cnt, fhi)
        tau = jnp.where(jnp.logical_and(active, found), mid, tau)
        done = jnp.where(jnp.logical_or(found, collapsed), 1.0, done)
        return lo, hi, flo, fhi, done, tau

    def body(st):
        inner = step(step(st[1:], True), False)
        return (st[0] + 1,) + inner

    done0 = jnp.where(take_all, 1.0, 0.0).astype(F32)
    tau0 = jnp.full((1, LANES), POS_INF, F32)
    fhi0 = jnp.ones((1, LANES), F32)
    _, lo, hi, _, _, _, tau = lax.while_loop(
        cond, body, (jnp.int32(0), s_min, s_max, n_valid, fhi0, done0, tau0))

    unresolved = jnp.logical_and(tau == POS_INF, jnp.logical_not(take_all))
    n_unres = jnp.sum(unresolved.astype(F32))
    tau = jnp.where(take_all, F32_LOWEST, tau)

    def tie_path(tau):
        cut = jnp.where(count_ge(hi) >= kf, hi, lo)
        cut = jnp.where(unresolved, cut, POS_INF)
        c_gt = _key_count(st_ref, nchunks, chunk, lambda v, c0: v > cut)
        keep = kf - c_gt

        def key_iota(c0):
            return c0 + lax.broadcasted_iota(jnp.int32, (chunk, LANES), 0)

        def idx_body(_, st):
            xlo, xhi = st
            xmid = (xlo + xhi) // 2
            cnt = _key_count(
                st_ref, nchunks, chunk,
                lambda v, c0: jnp.logical_and(v == cut, key_iota(c0) < xmid))
            ok = cnt >= keep
            return jnp.where(ok, xlo, xmid), jnp.where(ok, xmid, xhi)

        xlo0 = jnp.zeros((1, LANES), jnp.int32)
        xhi0 = jnp.full((1, LANES), nchunks * chunk, jnp.int32)
        n_idx_steps = 15
        _, xcut = lax.fori_loop(0, n_idx_steps, idx_body, (xlo0, xhi0))

        def drop_body(c, carry):
            c0 = pl.multiple_of(c * chunk, chunk)
            v = st_ref[pl.ds(c0, chunk), :]
            drop = jnp.logical_and(v == cut, key_iota(c0) >= xcut)
            st_ref[pl.ds(c0, chunk), :] = jnp.where(drop, NEG_INF, v)
            return carry

        lax.fori_loop(0, nchunks, drop_body, 0)
        return jnp.where(unresolved, cut, tau)

    return lax.cond(n_unres > 0.0, tie_path, lambda t: t, tau)


def _prompt_attn_kernel(kmax2_ref, qi_ref, kw_ref, q_ref, kilo_ref, kihi_ref, k_ref, vt_ref, o_ref,
                        st_ref, m_ref, l_ref, acc_ref, accf_ref):
    i = pl.program_id(0)
    tq, ck = Q_BLOCK, KEY_CHUNK
    n_blocks = i + 2
    nchunks = (n_blocks * Q_BLOCK + ck - 1) // ck

    wt = jnp.transpose(kw_ref[...])[IDX_DIM:IDX_DIM + N_IDX_HEADS, :] * (IDX_DIM ** -0.5)
    qcat = jnp.concatenate(
        [qi_ref[:, p * LANES:(p + 1) * LANES] for p in range(N_IDX_HEADS // 2)], axis=0)
    key = lax.broadcasted_iota(jnp.int32, (ck, tq), 0)
    qry = i * tq + lax.broadcasted_iota(jnp.int32, (ck, tq), 1)

    def idx_body(c, st):
        smin, smax = st
        c0 = pl.multiple_of(c * ck, ck)
        klo = kilo_ref[pl.ds(c0, ck), :]
        khi = kihi_ref[pl.ds(c0, ck), :]
        acc = jnp.zeros((ck, tq), F32)
        for t in range(N_IDX_HEADS // 4):
            rhs = qcat[2 * t * tq:(2 * t + 2) * tq]
            dlo = lax.dot_general(klo, rhs, CONTRACT_LAST, preferred_element_type=F32)
            dhi = lax.dot_general(khi, rhs, CONTRACT_LAST, preferred_element_type=F32)
            for j in range(2):
                p = 2 * t + j
                cs = slice(j * tq, (j + 1) * tq)
                acc = acc + wt[2 * p:2 * p + 1, :] * jnp.maximum(dlo[:, cs], 0.0)
                acc = acc + wt[2 * p + 1:2 * p + 2, :] * jnp.maximum(dhi[:, cs], 0.0)
        col = c0 + key
        valid = jnp.logical_or(
            col < N_META,
            jnp.logical_and(col >= Q_BLOCK, col - Q_BLOCK <= qry))
        st_ref[pl.ds(c0, ck), :] = jnp.where(valid, acc, NEG_INF)
        smin = jnp.minimum(smin, jnp.min(jnp.where(valid, acc, POS_INF), axis=0, keepdims=True))
        smax = jnp.maximum(smax, jnp.max(jnp.where(valid, acc, NEG_INF), axis=0, keepdims=True))
        return smin, smax

    smin, smax = lax.fori_loop(
        0, nchunks, idx_body,
        (jnp.full((1, tq), POS_INF, F32), jnp.full((1, tq), NEG_INF, F32)))

    n_valid = (N_META + 1 + i * tq + lax.broadcasted_iota(jnp.int32, (1, tq), 1)).astype(F32)
    tau = _topk_threshold(st_ref, nchunks, ck, n_valid, smin, smax)

    scale = HEAD_DIM ** -0.5

    def q_group(n):
        return jnp.concatenate(
            [q_ref[:, (n * GROUP + g) * LANES:(n * GROUP + g + 1) * LANES]
             for g in range(GROUP)], axis=0)

    def k_chunk(c0, n):
        return k_ref[pl.ds(c0, ck), n * LANES:(n + 1) * LANES]

    def write_out(n, ot):
        for g in range(GROUP):
            h = n * GROUP + g
            o_ref[:, h * LANES:(h + 1) * LANES] = jnp.transpose(
                ot[:, g * tq:(g + 1) * tq]).astype(o_ref.dtype)

    n_ref_chunks = jnp.minimum(nchunks, 2)

    def ref_body(c, mx):
        c0 = pl.multiple_of(c * ck, ck)
        sel = st_ref[pl.ds(c0, ck), :] >= tau
        out = []
        for n in range(N_KV_HEADS):
            raw = lax.dot_general(k_chunk(c0, n), q_group(n), CONTRACT_LAST,
                                  preferred_element_type=F32)
            cur = jnp.concatenate(
                [jnp.max(jnp.where(sel, raw[:, g * tq:(g + 1) * tq], NEG), axis=0, keepdims=True)
                 for g in range(GROUP)], axis=1)
            out.append(jnp.maximum(mx[n], cur))
        return tuple(out)

    mx = lax.fori_loop(0, n_ref_chunks, ref_body,
                       tuple(jnp.full((1, GROUP * tq), NEG, F32) for _ in range(N_KV_HEADS)))
    c2 = scale * LOG2E
    ones = jnp.ones((SUBLANES, HEAD_DIM), BF16)
    gap = jnp.float32(0.0)
    for n in range(N_KV_HEADS):
        qf = q_group(n).astype(F32)
        qn2 = lax.dot_general(ones, (qf * qf).astype(BF16), CONTRACT_LAST,
                              preferred_element_type=F32)[0:1]
        bound = jnp.sqrt(qn2 * kmax2_ref[n]) * (c2 * NORM_BOUND_SLACK)
        gap = jnp.maximum(gap, jnp.max(bound - mx[n] * c2))

    def fixed_reference_path():
        accf_ref[...] = jnp.zeros(accf_ref.shape, F32)

        def chunk_update(c, n):
            c0 = pl.multiple_of(c * ck, ck)
            sel = st_ref[pl.ds(c0, ck), :] >= tau
            raw = lax.dot_general(k_chunk(c0, n), q_group(n), CONTRACT_LAST,
                                  preferred_element_type=F32)
            neg_ref = mx[n] * (-c2)
            p = jnp.concatenate(
                [jnp.exp2(raw[:, g * tq:(g + 1) * tq] * c2
                          + jnp.where(sel, neg_ref[:, g * tq:(g + 1) * tq], NEG)).astype(BF16)
                 for g in range(GROUP)], axis=1)
            return jnp.dot(vt_ref[c, n], p, preferred_element_type=F32)

        def pair_body(j, carry):
            for n in range(N_KV_HEADS):
                accf_ref[n] += chunk_update(2 * j, n) + chunk_update(2 * j + 1, n)
            return carry

        lax.fori_loop(0, nchunks // 2, pair_body, 0)

        @pl.when(nchunks % 2 == 1)
        def _():
            for n in range(N_KV_HEADS):
                accf_ref[n] += chunk_update(nchunks - 1, n)

        for n in range(N_KV_HEADS):
            write_out(n, accf_ref[n, 0:HEAD_DIM, :] / accf_ref[n, HEAD_DIM:HEAD_DIM + 1, :])

    def running_max_path():
        m_ref[...] = jnp.full(m_ref.shape, NEG, F32)
        l_ref[...] = jnp.zeros(l_ref.shape, F32)
        acc_ref[...] = jnp.zeros(acc_ref.shape, F32)

        def body(c, carry):
            c0 = pl.multiple_of(c * ck, ck)
            bias = jnp.where(st_ref[pl.ds(c0, ck), :] >= tau, 0.0, NEG)
            bias = jnp.concatenate([bias] * GROUP, axis=1)
            for n in range(N_KV_HEADS):
                sc = lax.dot_general(k_chunk(c0, n), q_group(n), CONTRACT_LAST,
                                     preferred_element_type=F32)
                sc = sc * scale + bias
                m_old = m_ref[n]
                m_new = jnp.maximum(m_old, jnp.max(sc, axis=0, keepdims=True))
                alpha = jnp.exp(m_old - m_new)
                p = jnp.exp(sc - m_new)
                l_ref[n] = alpha * l_ref[n] + jnp.sum(
                    p.reshape(ck // SUBLANES, SUBLANES, GROUP * tq), axis=0)
                acc_ref[n] = alpha * acc_ref[n] + jnp.dot(
                    vt_ref[c, n, 0:HEAD_DIM, :], p.astype(BF16), preferred_element_type=F32)
                m_ref[n] = m_new
            return carry

        lax.fori_loop(0, nchunks, body, 0)
        for n in range(N_KV_HEADS):
            write_out(n, acc_ref[n] / jnp.sum(l_ref[n], axis=0, keepdims=True))

    lax.cond(gap <= MAX_EXP2_GAP, fixed_reference_path, running_max_path)


def _prompt_attention(kmax2, qi, kw, q, ki_lo, ki_hi, k_all, v_t):
    rows = q.shape[0]
    nk = k_all.shape[0]
    assert rows % Q_BLOCK == 0 and nk % KEY_CHUNK == 0
    assert nk >= (rows // Q_BLOCK + 1) * Q_BLOCK
    assert v_t.shape == (nk // KEY_CHUNK, N_KV_HEADS, HEAD_DIM + ONES_ROWS, KEY_CHUNK)
    row_spec = lambda w: pl.BlockSpec((Q_BLOCK, w), lambda i: (i, 0))
    return pl.pallas_call(
        _prompt_attn_kernel,
        grid=(rows // Q_BLOCK,),
        in_specs=[pl.BlockSpec(memory_space=pltpu.SMEM),
                  row_spec(1024), row_spec(LANES), row_spec(ATTN_WIDTH),
                  _const_spec(ki_lo.shape), _const_spec(ki_hi.shape),
                  _const_spec(k_all.shape), _const_spec(v_t.shape)],
        out_specs=row_spec(ATTN_WIDTH),
        out_shape=jax.ShapeDtypeStruct((rows, ATTN_WIDTH), BF16),
        scratch_shapes=[
            pltpu.VMEM((nk, Q_BLOCK), F32),
            pltpu.VMEM((N_KV_HEADS, 1, GROUP * Q_BLOCK), F32),
            pltpu.VMEM((N_KV_HEADS, SUBLANES, GROUP * Q_BLOCK), F32),
            pltpu.VMEM((N_KV_HEADS, HEAD_DIM, GROUP * Q_BLOCK), F32),
            pltpu.VMEM((N_KV_HEADS, HEAD_DIM + ONES_ROWS, GROUP * Q_BLOCK), F32),
        ],
        compiler_params=pltpu.CompilerParams(
            dimension_semantics=("arbitrary",), vmem_limit_bytes=VMEM_LIMIT),
        name="prompt_attn",
    )(kmax2, qi, kw, q, ki_lo, ki_hi, k_all, v_t)


def _decode_scores_kernel(pt_ref, qi_ref, w_ref, kinew_ref, *rest):
    n_pages = len(rest) - 1
    page_refs, s_ref = rest[:n_pages], rest[n_pages]
    qi = qi_ref[...]
    w = w_ref[...] * (IDX_DIM ** -0.5)
    for j in range(n_pages):
        kpt = page_refs[j][...].astype(BF16)
        d = jnp.dot(qi, kpt, preferred_element_type=F32)
        s_ref[:, j * PAGE_SIZE:(j + 1) * PAGE_SIZE] = jnp.sum(
            jnp.maximum(d, 0.0) * w, axis=0, keepdims=True)
    d_new = jnp.sum(qi.astype(F32) * kinew_ref[...].astype(F32), axis=-1, keepdims=True)
    s_new = jnp.sum(jnp.maximum(d_new, 0.0) * w, axis=0, keepdims=True)
    past = n_pages * PAGE_SIZE
    tail = s_ref.shape[1] - past
    lane = lax.broadcasted_iota(jnp.int32, (1, tail), 1)
    s_ref[:, past:] = jnp.where(lane == 0, s_new, NEG_INF)


def _page_specs(block, n_pages):
    nd = len(block)
    return [
        pl.BlockSpec((None, None) + block, functools.partial(
            lambda b, pt, j: (0, pt[b, j]) + (0,) * nd, j=j)) for j in range(n_pages)]


def _decode_scores(page_table, qi_s, w_s, kinew_s, cache_kidx, width):
    db, n_pages = page_table.shape
    seq_spec = lambda a, c: pl.BlockSpec((None, a, c), lambda b, pt: (b, 0, 0))
    return pl.pallas_call(
        _decode_scores_kernel,
        grid_spec=pltpu.PrefetchScalarGridSpec(
            num_scalar_prefetch=1,
            grid=(db,),
            in_specs=[seq_spec(N_IDX_HEADS, IDX_DIM), seq_spec(N_IDX_HEADS, 1),
                      seq_spec(1, IDX_DIM)] + _page_specs((IDX_DIM, PAGE_SIZE), n_pages),
            out_specs=seq_spec(1, width),
        ),
        out_shape=jax.ShapeDtypeStruct((db, 1, width), F32),
        compiler_params=pltpu.CompilerParams(dimension_semantics=("arbitrary",)),
        name="decode_scores",
    )(page_table, qi_s, w_s, kinew_s, *([cache_kidx] * n_pages))


def _decode_mask_kernel(s_in_ref, bias_ref, st_ref):
    nk = s_in_ref.shape[0]
    chunk = KEY_CHUNK
    s = s_in_ref[...]
    st_ref[...] = s
    valid = s > NEG_INF
    smin = jnp.min(jnp.where(valid, s, POS_INF), axis=0, keepdims=True)
    smax = jnp.max(s, axis=0, keepdims=True)
    n_valid = jnp.sum(valid.astype(F32), axis=0, keepdims=True)
    tau = _topk_threshold(st_ref, nk // chunk, chunk, n_valid, smin, smax)
    bias_ref[...] = jnp.where(st_ref[...] >= tau, 0.0, NEG)


def _decode_mask(scores_t):
    assert scores_t.shape[0] % KEY_CHUNK == 0 and scores_t.shape[1] == LANES
    return pl.pallas_call(
        _decode_mask_kernel,
        out_shape=jax.ShapeDtypeStruct(scores_t.shape, F32),
        scratch_shapes=[pltpu.VMEM(scores_t.shape, F32)],
        name="decode_mask",
    )(scores_t)


def _decode_attn_kernel(pt_ref, q_ref, bias_ref, knew_ref, vnew_ref, *rest):
    n_pages = (len(rest) - 1) // 2
    k_pages, v_pages, o_ref = rest[:n_pages], rest[n_pages:2 * n_pages], rest[2 * n_pages]
    rows = N_KV_HEADS * n_pages * PAGE_SIZE
    scale = HEAD_DIM ** -0.5
    q = q_ref[...]
    bias = jnp.concatenate(
        [jnp.broadcast_to(bias_ref[n:n + 1, :], (GROUP, bias_ref.shape[1]))
         for n in range(N_KV_HEADS)], axis=0)
    k_il = jnp.concatenate([r[...].astype(BF16) for r in k_pages], axis=0)
    v_il = jnp.concatenate([r[...].astype(BF16) for r in v_pages], axis=0)
    sc = lax.dot_general(q, k_il, CONTRACT_LAST, preferred_element_type=F32)
    sc = sc * scale + bias[:, :rows]
    sc_new = jnp.sum(q.astype(F32) * knew_ref[...].astype(F32), axis=-1, keepdims=True)
    sc_new = sc_new * scale + jnp.max(bias[:, rows:rows + N_KV_HEADS], axis=-1, keepdims=True)
    m = jnp.maximum(jnp.max(sc, axis=-1, keepdims=True), sc_new)
    p = jnp.exp(sc - m)
    p_new = jnp.exp(sc_new - m)
    denom = jnp.sum(p, axis=-1, keepdims=True) + p_new
    o = jnp.dot(p.astype(BF16), v_il, preferred_element_type=F32)
    o = o + p_new.astype(BF16).astype(F32) * vnew_ref[...].astype(F32)
    o_ref[...] = (o / denom).astype(o_ref.dtype)


def _decode_attention(page_table, q_s, bias_il, knew_s, vnew_s, cache_k, cache_v):
    db, n_pages = page_table.shape
    width2 = bias_il.shape[-1]
    seq_spec = lambda a, c: pl.BlockSpec((None, a, c), lambda b, pt: (b, 0, 0))
    page_specs = _page_specs((N_KV_HEADS * PAGE_SIZE, HEAD_DIM), n_pages)
    return pl.pallas_call(
        _decode_attn_kernel,
        grid_spec=pltpu.PrefetchScalarGridSpec(
            num_scalar_prefetch=1,
            grid=(db,),
            in_specs=[seq_spec(N_HEADS, HEAD_DIM), seq_spec(N_KV_HEADS, width2),
                      seq_spec(N_HEADS, HEAD_DIM), seq_spec(N_HEADS, HEAD_DIM)]
            + page_specs + page_specs,
            out_specs=seq_spec(N_HEADS, HEAD_DIM),
        ),
        out_shape=jax.ShapeDtypeStruct((db, N_HEADS, HEAD_DIM), BF16),
        compiler_params=pltpu.CompilerParams(
            dimension_semantics=("arbitrary",), vmem_limit_bytes=VMEM_LIMIT),
        name="decode_attn",
    )(page_table, q_s, bias_il, knew_s, vnew_s, *([cache_k] * n_pages), *([cache_v] * n_pages))


POOL_TILE = 512
POOL_HALO = 16


def _pool_prompt_kernel(u_ref, uprev_ref, umeta_ref, d_ref, ubuf):
    i = pl.program_id(0)
    tp = u_ref.shape[0]
    ubuf[0:POOL_HALO, :] = jnp.where(i == 0, umeta_ref[...], uprev_ref[...])
    ubuf[POOL_HALO:, :] = u_ref[...]
    for g, w in enumerate(POOL_WINDOWS):
        cols = slice(g * POOL_GROUP_WIDTH, (g + 1) * POOL_GROUP_WIDTH)
        cur = ubuf[POOL_HALO:POOL_HALO + tp, cols]
        tot = cur
        for j in range(1, w):
            tot = tot + ubuf[POOL_HALO - j:POOL_HALO - j + tp, cols]
        d_ref[:, cols] = tot * (1.0 / w) - cur


def _pool_prompt(u, u_meta):
    rows = u.shape[0]
    tp = POOL_TILE
    per = tp // POOL_HALO
    return pl.pallas_call(
        _pool_prompt_kernel,
        grid=(rows // tp,),
        in_specs=[pl.BlockSpec((tp, POOL_WIDTH), lambda i: (i, 0)),
                  pl.BlockSpec((POOL_HALO, POOL_WIDTH), lambda i: (jnp.maximum(i * per - 1, 0), 0)),
                  _const_spec((POOL_HALO, POOL_WIDTH))],
        out_specs=pl.BlockSpec((tp, POOL_WIDTH), lambda i: (i, 0)),
        out_shape=jax.ShapeDtypeStruct((rows, POOL_WIDTH), F32),
        scratch_shapes=[pltpu.VMEM((tp + POOL_HALO, POOL_WIDTH), F32)],
        compiler_params=pltpu.CompilerParams(dimension_semantics=("arbitrary",)),
        name="pool_prompt",
    )(u, u, u_meta)


def _pool_decode_kernel(hist_ref, u_ref, d_ref):
    for g, w in enumerate(POOL_WINDOWS):
        cols = slice(g * POOL_GROUP_WIDTH, (g + 1) * POOL_GROUP_WIDTH)
        cur = u_ref[:, cols]
        tot = cur
        for j in range(1, w):
            tot = tot + hist_ref[POOL_HIST - j, :, cols]
        d_ref[:, cols] = tot * (1.0 / w) - cur


def _pool_decode(hist, u):
    return pl.pallas_call(
        _pool_decode_kernel,
        out_shape=jax.ShapeDtypeStruct(u.shape, F32),
        name="pool_decode",
    )(hist, u)


FF_TILE = 512


def _rmsnorm(x, g):
    ms = jnp.mean(x * x, axis=-1, keepdims=True)
    return (x * lax.rsqrt(ms + EPS)) * g


def _post_kernel(x_ref, att_ref, d_ref, woa_ref, wop_ref, wpool_ref, pscale_ref,
                 gmlp_ref, gfin_ref, wup_ref, wdn_ref, y_ref, h_ref, hn_ref, acc_ref):
    f = pl.program_id(1)

    @pl.when(f == 0)
    def _():
        d = d_ref[...]
        pooled = jnp.concatenate(
            [jnp.dot(d[:, g * POOL_GROUP_WIDTH:(g + 1) * POOL_GROUP_WIDTH].astype(BF16),
                     wpool_ref[g], preferred_element_type=F32)
             for g in range(len(POOL_WINDOWS))], axis=1) * pscale_ref[...]
        h = (x_ref[...]
             + jnp.dot(att_ref[...], woa_ref[...], preferred_element_type=F32)
             + jnp.dot(pooled.astype(BF16), wop_ref[...], preferred_element_type=F32))
        h_ref[...] = h
        hn_ref[...] = _rmsnorm(h, gmlp_ref[...]).astype(BF16)
        acc_ref[...] = jnp.zeros(acc_ref.shape, F32)

    a = jnp.maximum(jnp.dot(hn_ref[...], wup_ref[...], preferred_element_type=F32), 0.0)
    acc_ref[...] += jnp.dot((a * a).astype(BF16), wdn_ref[...], preferred_element_type=F32)

    @pl.when(f == pl.num_programs(1) - 1)
    def _():
        y_ref[...] = _rmsnorm(h_ref[...] + acc_ref[...], gfin_ref[...])


def _post(x, att, d, wo_a, wo_p, w_pool, pool_scale, g_mlp, g_final, w_up, w_down, tm):
    rows = x.shape[0]
    assert rows % tm == 0 and D_FF % FF_TILE == 0
    row_spec = lambda w: pl.BlockSpec((tm, w), lambda i, f: (i, 0))
    return pl.pallas_call(
        _post_kernel,
        grid=(rows // tm, D_FF // FF_TILE),
        in_specs=[row_spec(D_MODEL), row_spec(ATTN_WIDTH), row_spec(POOL_WIDTH),
                  _const_spec(wo_a.shape), _const_spec(wo_p.shape), _const_spec(w_pool.shape),
                  _const_spec((1, POOL_WIDTH)), _const_spec((1, D_MODEL)), _const_spec((1, D_MODEL)),
                  pl.BlockSpec((D_MODEL, FF_TILE), lambda i, f: (0, f)),
                  pl.BlockSpec((FF_TILE, D_MODEL), lambda i, f: (f, 0))],
        out_specs=row_spec(D_MODEL),
        out_shape=jax.ShapeDtypeStruct((rows, D_MODEL), F32),
        scratch_shapes=[pltpu.VMEM((tm, D_MODEL), F32), pltpu.VMEM((tm, D_MODEL), BF16),
                        pltpu.VMEM((tm, D_MODEL), F32)],
        compiler_params=pltpu.CompilerParams(
            dimension_semantics=("arbitrary", "arbitrary"), vmem_limit_bytes=VMEM_LIMIT),
        name="post",
    )(x, att, d, wo_a, wo_p, w_pool, pool_scale.reshape(1, POOL_WIDTH),
      g_mlp.reshape(1, D_MODEL), g_final.reshape(1, D_MODEL), w_up, w_down)


def kernel(x_prompt, x_sample, cache_k, cache_v, cache_kidx, state_pool, page_table, meta_tokens,
           g_mix, w_in, w_pool, pool_scale, w_out, g_mlp, w_up, w_down, g_final):
    assert x_prompt.shape[0] == 1 and x_sample.shape[1] == 1 and g_mix.shape[0] == 1
    seq = x_prompt.shape[1]
    db = x_sample.shape[0]
    n_pages = page_table.shape[1]
    past = n_pages * PAGE_SIZE
    assert db == LANES

    w = w_in[0]
    o = 0
    parts = []
    for width in (ATTN_WIDTH, 2 * N_KV_HEADS * HEAD_DIM, N_IDX_HEADS * IDX_DIM,
                  IDX_DIM + N_IDX_HEADS, POOL_WIDTH):
        parts.append(w[:, o:o + width].astype(BF16))
        o += width
    parts[3] = jnp.pad(parts[3], ((0, 0), (0, LANES - parts[3].shape[1])))
    wo_a = w_out[0, :ATTN_WIDTH].astype(BF16)
    wo_p = w_out[0, ATTN_WIDTH:].astype(BF16)
    w_pool_b = w_pool[0].astype(BF16)
    w_up_b = w_up[0].astype(BF16)
    w_down_b = w_down[0].astype(BF16)

    xp = x_prompt[0]
    pos_p = N_META + jnp.arange(seq, dtype=jnp.int32)
    q_p, kf_p, vf_p, kb_p, vb_p, qi_p, kw_p, u_p = _project(xp, pos_p, g_mix[0], parts, 256)
    xs = jnp.concatenate([meta_tokens.astype(F32), x_sample[:, 0]], axis=0)
    pos_s = jnp.concatenate([jnp.arange(N_META, dtype=jnp.int32),
                             jnp.full((db,), past, jnp.int32)])
    q_s, kf_s, vf_s, kb_s, vb_s, qi_s, kw_s, u_s = _project(xs, pos_s, g_mix[0], parts, xs.shape[0])

    nk = -(-(seq + 2 * Q_BLOCK) // KEY_CHUNK) * KEY_CHUNK

    def key_rows(meta_part, prompt_part):
        width = meta_part.shape[1]
        return jnp.concatenate(
            [meta_part, jnp.zeros((Q_BLOCK - N_META, width), meta_part.dtype), prompt_part,
             jnp.zeros((nk - Q_BLOCK - seq, width), meta_part.dtype)], axis=0)

    ki_all = key_rows(kw_s[:N_META, :IDX_DIM], kw_p[:, :IDX_DIM]).astype(BF16)
    zk = jnp.zeros_like(ki_all)
    ki_lo = jnp.concatenate([ki_all, zk], axis=1)
    ki_hi = jnp.concatenate([zk, ki_all], axis=1)
    k_all = key_rows(kb_s[:N_META], kb_p)
    v_all = key_rows(vb_s[:N_META], vb_p)
    v_t = jnp.transpose(
        v_all.reshape(nk // KEY_CHUNK, KEY_CHUNK, N_KV_HEADS, HEAD_DIM), (0, 2, 3, 1))
    v_t = jnp.concatenate(
        [v_t, jnp.ones((nk // KEY_CHUNK, N_KV_HEADS, ONES_ROWS, KEY_CHUNK), BF16)], axis=2)
    knorm = lambda kw: kw[:, KNORM_LANE:KNORM_LANE + N_KV_HEADS]
    kmax2 = jnp.maximum(jnp.max(knorm(kw_p), axis=0), jnp.max(knorm(kw_s[:N_META]), axis=0))
    att_p = _prompt_attention(kmax2, qi_p, kw_p, q_p, ki_lo, ki_hi, k_all, v_t)

    width = -(-(past + 1) // KEY_CHUNK) * KEY_CHUNK
    qi_d = qi_s[N_META:].reshape(db, N_IDX_HEADS, IDX_DIM)
    w_d = kw_s[N_META:, IDX_DIM:IDX_DIM + N_IDX_HEADS].reshape(db, N_IDX_HEADS, 1)
    kinew = kw_s[N_META:, :IDX_DIM].astype(BF16).reshape(db, 1, IDX_DIM)
    scores = _decode_scores(page_table, qi_d, w_d, kinew, jnp.swapaxes(cache_kidx, 2, 3), width)
    bias_t = _decode_mask(jnp.transpose(scores.reshape(db, width)))
    bias = jnp.transpose(bias_t)
    negs = jnp.full_like(bias, NEG)
    bias_il = jnp.stack(
        [jnp.stack([bias, negs], axis=-1), jnp.stack([negs, bias], axis=-1)], axis=1,
    ).reshape(db, N_KV_HEADS, N_KV_HEADS * width)
    n_phys = cache_k.shape[1]
    il_shape = (1, n_phys, N_KV_HEADS * PAGE_SIZE, HEAD_DIM)
    per_head = lambda a: jnp.repeat(a.reshape(db, N_KV_HEADS, HEAD_DIM), GROUP, axis=1)
    att_s = _decode_attention(
        page_table, q_s[N_META:].reshape(db, N_HEADS, HEAD_DIM), bias_il,
        per_head(kb_s[N_META:]), per_head(vb_s[N_META:]),
        cache_k.reshape(il_shape), cache_v.reshape(il_shape))
    att_s = att_s.reshape(db, ATTN_WIDTH)

    d_p = _pool_prompt(u_p, u_s[:N_META])
    d_s = _pool_decode(jnp.swapaxes(state_pool[0], 0, 1), u_s[N_META:])

    post = functools.partial(
        _post, wo_a=wo_a, wo_p=wo_p, w_pool=w_pool_b, pool_scale=pool_scale[0],
        g_mlp=g_mlp[0], g_final=g_final, w_up=w_up_b, w_down=w_down_b)
    y_p = post(xp, att_p, d_p, tm=512)
    y_s = post(x_sample[:, 0], att_s, d_s, tm=db)

    k_prompt = jnp.concatenate([kf_s[:N_META], kf_p], axis=0).reshape(1, 1, N_META + seq, N_KV_HEADS, HEAD_DIM)
    v_prompt = jnp.concatenate([vf_s[:N_META], vf_p], axis=0).reshape(1, 1, N_META + seq, N_KV_HEADS, HEAD_DIM)
    kidx_prompt = jnp.concatenate([kw_s[:N_META, :IDX_DIM], kw_p[:, :IDX_DIM]], axis=0).reshape(
        1, 1, N_META + seq, IDX_DIM)
    pool_prompt = u_p[seq - POOL_HIST:].reshape(1, 1, POOL_HIST, POOL_WIDTH)
    k_sample = kf_s[N_META:].reshape(1, db, 1, N_KV_HEADS, HEAD_DIM)
    v_sample = vf_s[N_META:].reshape(1, db, 1, N_KV_HEADS, HEAD_DIM)
    kidx_sample = kw_s[N_META:, :IDX_DIM].reshape(1, db, 1, IDX_DIM)
    pool_sample = jnp.concatenate(
        [state_pool[0][:, 1:], u_s[N_META:][:, None, :]], axis=1)[None]
    return (y_p.reshape(1, seq, D_MODEL), y_s.reshape(db, 1, D_MODEL),
            k_prompt, v_prompt, kidx_prompt, pool_prompt,
            k_sample, v_sample, kidx_sample, pool_sample)
```

```python
import functools

import jax
import jax.numpy as jnp
from jax import lax
from jax.experimental import pallas as pl
from jax.experimental.pallas import tpu as pltpu

F32 = jnp.float32
BF16 = jnp.bfloat16

D_MODEL = 2048
N_META = 16
ATTN_WIDTH = 1024
POOL_WIDTH = 1024
HEAD_DIM = 128
N_HEADS = 8
N_KV_HEADS = 2
GROUP = N_HEADS // N_KV_HEADS
N_IDX_HEADS = 16
IDX_DIM = 64
TOPK = 256
POOL_WINDOWS = (2, 4, 8, 16)
POOL_GROUP_WIDTH = 256
POOL_HIST = 15
D_FF = 8192
PAGE_SIZE = 128
ROPE_THETA = 10000.0
EPS = 1e-6
NEG = -1e30
NEG_INF = float("-inf")
POS_INF = float("inf")
F32_LOWEST = -3.0e38

LANES = 128
SUBLANES = 8
VMEM_LIMIT = 56 * 1024 * 1024

Q_BLOCK = 128
KEY_CHUNK = 512
MAX_BISECT = 320
N_INTERPOLATE_ROUNDS = 6
LOG2E = 1.4426950408889634
MAX_EXP2_GAP = 80.0
NORM_BOUND_SLACK = 1.02
ONES_ROWS = 16
KNORM_LANE = IDX_DIM + N_IDX_HEADS

CONTRACT_LAST = (((1,), (1,)), ((), ()))


def _const_spec(shape):
    nd = len(shape)
    return pl.BlockSpec(shape, lambda *_: (0,) * nd, pipeline_mode=pl.Buffered(1))


def _rope128(x, c, s):
    return x * c + pltpu.roll(x, 64, 1) * s


def _rope64(x, c, sa, sb):
    return x * c + pltpu.roll(x, 96, 1) * sa + pltpu.roll(x, 32, 1) * sb


def _proj_kernel(x_ref, g_ref, c128_ref, s128_ref, c64_ref, sa64_ref, sb64_ref,
                 wq_ref, wkv_ref, wqi_ref, wkw_ref, wu_ref,
                 q_ref, kf_ref, vf_ref, kb_ref, vb_ref, qi_ref, kw_ref, u_ref):
    x = x_ref[...]
    ms = jnp.mean(x * x, axis=-1, keepdims=True)
    xn = ((x * lax.rsqrt(ms + EPS)) * g_ref[...]).astype(BF16)
    c128 = c128_ref[...]
    s128 = s128_ref[...]
    c64 = c64_ref[...]
    sa64 = sa64_ref[...]
    sb64 = sb64_ref[...]

    zq = jnp.dot(xn, wq_ref[...], preferred_element_type=F32)
    for h in range(N_HEADS):
        sl = slice(h * LANES, (h + 1) * LANES)
        q_ref[:, sl] = _rope128(zq[:, sl], c128, s128).astype(BF16)

    zkv = jnp.dot(xn, wkv_ref[...], preferred_element_type=F32)
    k_norm2 = []
    for h in range(N_KV_HEADS):
        sl = slice(h * LANES, (h + 1) * LANES)
        kr = _rope128(zkv[:, sl], c128, s128)
        kf_ref[:, sl] = kr
        kb = kr.astype(BF16)
        kb_ref[:, sl] = kb
        k_norm2.append(jnp.sum(kb.astype(F32) * kb.astype(F32), axis=-1, keepdims=True))
    v = zkv[:, N_KV_HEADS * LANES:]
    vf_ref[...] = v
    vb_ref[...] = v.astype(BF16)

    zqi = jnp.dot(xn, wqi_ref[...], preferred_element_type=F32)
    for p in range(N_IDX_HEADS // 2):
        sl = slice(p * LANES, (p + 1) * LANES)
        qi_ref[:, sl] = _rope64(zqi[:, sl], c64, sa64, sb64).astype(BF16)

    zkw = jnp.dot(xn, wkw_ref[...], preferred_element_type=F32)
    lane = lax.broadcasted_iota(jnp.int32, zkw.shape, 1)
    is_key = lane < IDX_DIM
    ckw = jnp.where(is_key, c64, jnp.where(lane < IDX_DIM + N_IDX_HEADS, N_IDX_HEADS ** -0.5, 1.0))
    kw = _rope64(zkw, ckw, jnp.where(is_key, sa64, 0.0), jnp.where(is_key, sb64, 0.0))
    for h in range(N_KV_HEADS):
        kw = jnp.where(lane == KNORM_LANE + h, k_norm2[h], kw)
    kw_ref[...] = kw

    u_ref[...] = jnp.dot(xn, wu_ref[...], preferred_element_type=F32)


def _project(x, pos, g_mix, w_parts, tm):
    rows = x.shape[0]
    assert rows % tm == 0
    c128, s128, c64, sa64, sb64 = _rope_tables(pos)
    wq, wkv, wqi, wkw, wu = w_parts
    row_spec = lambda w: pl.BlockSpec((tm, w), lambda i: (i, 0))
    out_shapes = (
        jax.ShapeDtypeStruct((rows, ATTN_WIDTH), BF16),
        jax.ShapeDtypeStruct((rows, 256), F32),
        jax.ShapeDtypeStruct((rows, 256), F32),
        jax.ShapeDtypeStruct((rows, 256), BF16),
        jax.ShapeDtypeStruct((rows, 256), BF16),
        jax.ShapeDtypeStruct((rows, 1024), BF16),
        jax.ShapeDtypeStruct((rows, LANES), F32),
        jax.ShapeDtypeStruct((rows, POOL_WIDTH), F32),
    )
    return pl.pallas_call(
        _proj_kernel,
        grid=(rows // tm,),
        in_specs=[row_spec(D_MODEL), _const_spec((1, D_MODEL))]
        + [row_spec(LANES)] * 5
        + [_const_spec(w.shape) for w in w_parts],
        out_specs=tuple(row_spec(s.shape[1]) for s in out_shapes),
        out_shape=out_shapes,
        compiler_params=pltpu.CompilerParams(
            dimension_semantics=("arbitrary",), vmem_limit_bytes=VMEM_LIMIT),
        name="proj",
    )(x, g_mix.reshape(1, D_MODEL), c128, s128, c64, sa64, sb64, wq, wkv, wqi, wkw, wu)


def _rope_tables(pos):
    posf = pos.astype(F32)[:, None]
    inv = ROPE_THETA ** (-jnp.arange(0, HEAD_DIM, 2, dtype=F32) / HEAD_DIM)
    ang = posf * inv[None, :]
    c, s = jnp.cos(ang), jnp.sin(ang)
    c128 = jnp.concatenate([c, c], axis=-1)
    s128 = jnp.concatenate([-s, s], axis=-1)
    inv = ROPE_THETA ** (-jnp.arange(0, IDX_DIM, 2, dtype=F32) / IDX_DIM)
    ang = posf * inv[None, :]
    c, s = jnp.cos(ang), jnp.sin(ang)
    z = jnp.zeros_like(s)
    c64 = jnp.concatenate([c, c, c, c], axis=-1)
    sa64 = jnp.concatenate([-s, z, -s, z], axis=-1)
    sb64 = jnp.concatenate([z, s, z, s], axis=-1)
    return c128, s128, c64, sa64, sb64


def _key_count(st_ref, nchunks, chunk, pred):
    acc_rows = 8 * SUBLANES

    def body(c, part):
        c0 = pl.multiple_of(c * chunk, chunk)
        hit = pred(st_ref[pl.ds(c0, chunk), :], c0).astype(F32)
        return part + jnp.sum(hit.reshape(chunk // acc_rows, acc_rows, LANES), axis=0)

    part = lax.fori_loop(0, nchunks, body, jnp.zeros((acc_rows, LANES), F32))
    return jnp.sum(part, axis=0, keepdims=True)


def _topk_threshold(st_ref, nchunks, chunk, n_valid, s_min, s_max):
    kf = float(TOPK)
    take_all = n_valid <= kf

    def count_ge(t):
        return _key_count(st_ref, nchunks, chunk, lambda v, c0: v >= t)

    def cond(st):
        it, done = st[0], st[5]
        return jnp.logical_and(it < MAX_BISECT, jnp.min(done) < 0.5)

    def largest_below(hi):
        def body(c, part):
            c0 = pl.multiple_of(c * chunk, chunk)
            v = st_ref[pl.ds(c0, chunk), :]
            cand = jnp.where(v < hi, v, NEG_INF)
            return jnp.maximum(
                part, jnp.max(cand.reshape(chunk // acc_rows, acc_rows, LANES), axis=0))

        acc_rows = 8 * SUBLANES
        part = lax.fori_loop(0, nchunks, body, jnp.full((acc_rows, LANES), NEG_INF, F32))
        return jnp.max(part, axis=0, keepdims=True)

    def step(st, mode):
        lo, hi, flo, fhi, done, tau = st
        half = lo + (hi - lo) * 0.5
        collapsed = jnp.logical_or(half <= lo, half >= hi)
        mid = half
        if mode == "interpolate":
            frac = (flo - (kf + 0.5)) / jnp.maximum(flo - fhi, 1.0)
            guess = lo + (hi - lo) * jnp.clip(frac, 1.0 / 32, 31.0 / 32)
            mid = jnp.where(jnp.logical_and(guess > lo, guess < hi), guess, half)
        elif mode == "extract":
            below = largest_below(hi)
            mid = jnp.where(below >= lo, below, half)
        cnt = count_ge(mid)
        found = cnt == kf
        active = done < 0.5
        go_up = jnp.logical_and(active, cnt > kf)
        go_dn = jnp.logical_and(active, cnt < kf)
        lo = jnp.where(go_up, mid, lo)
        hi = jnp.where(go_dn, mid, hi)
        flo = jnp.where(go_up, cnt, flo)
        fhi = jnp.where(go_dn, cnt, fhi)
        tau = jnp.where(jnp.logical_and(active, found), mid, tau)
        done = jnp.where(jnp.logical_or(found, collapsed), 1.0, done)
        return lo, hi, flo, fhi, done, tau

    def body(st):
        inner = lax.cond(st[0] < N_INTERPOLATE_ROUNDS,
                         lambda s: step(s, "interpolate"), lambda s: step(s, "extract"), st[1:])
        return (st[0] + 1,) + step(inner, "halve")

    done0 = jnp.where(take_all, 1.0, 0.0).astype(F32)
    tau0 = jnp.full((1, LANES), POS_INF, F32)
    fhi0 = jnp.ones((1, LANES), F32)
    _, lo, hi, _, _, _, tau = lax.while_loop(
        cond, body, (jnp.int32(0), s_min, s_max, n_valid, fhi0, done0, tau0))

    unresolved = jnp.logical_and(tau == POS_INF, jnp.logical_not(take_all))
    n_unres = jnp.sum(unresolved.astype(F32))
    tau = jnp.where(take_all, F32_LOWEST, tau)

    def tie_path(tau):
        cut = jnp.where(count_ge(hi) >= kf, hi, lo)
        cut = jnp.where(unresolved, cut, POS_INF)
        c_gt = _key_count(st_ref, nchunks, chunk, lambda v, c0: v > cut)
        keep = kf - c_gt

        def key_iota(c0):
            return c0 + lax.broadcasted_iota(jnp.int32, (chunk, LANES), 0)

        def idx_body(_, st):
            xlo, xhi = st
            xmid = (xlo + xhi) // 2
            cnt = _key_count(
                st_ref, nchunks, chunk,
                lambda v, c0: jnp.logical_and(v == cut, key_iota(c0) < xmid))
            ok = cnt >= keep
            return jnp.where(ok, xlo, xmid), jnp.where(ok, xmid, xhi)

        xlo0 = jnp.zeros((1, LANES), jnp.int32)
        xhi0 = jnp.full((1, LANES), nchunks * chunk, jnp.int32)
        n_idx_steps = 15
        _, xcut = lax.fori_loop(0, n_idx_steps, idx_body, (xlo0, xhi0))

        def drop_body(c, carry):
            c0 = pl.multiple_of(c * chunk, chunk)
            v = st_ref[pl.ds(c0, chunk), :]
            drop = jnp.logical_and(v == cut, key_iota(c0) >= xcut)
            st_ref[pl.ds(c0, chunk), :] = jnp.where(drop, NEG_INF, v)
            return carry

        lax.fori_loop(0, nchunks, drop_body, 0)
        return jnp.where(unresolved, cut, tau)

    return lax.cond(n_unres > 0.0, tie_path, lambda t: t, tau)


def _prompt_attn_kernel(kmax2_ref, qi_ref, kw_ref, q_ref, kilo_ref, kihi_ref, k_ref, vt_ref, o_ref,
                        st_ref, m_ref, l_ref, acc_ref, accf_ref):
    i = pl.program_id(0)
    tq, ck = Q_BLOCK, KEY_CHUNK
    n_blocks = i + 2
    nchunks = (n_blocks * Q_BLOCK + ck - 1) // ck

    wt = jnp.transpose(kw_ref[...])[IDX_DIM:IDX_DIM + N_IDX_HEADS, :] * (IDX_DIM ** -0.5)
    qcat = jnp.concatenate(
        [qi_ref[:, p * LANES:(p + 1) * LANES] for p in range(N_IDX_HEADS // 2)], axis=0)
    key = lax.broadcasted_iota(jnp.int32, (ck, tq), 0)
    qry = i * tq + lax.broadcasted_iota(jnp.int32, (ck, tq), 1)

    def idx_chunk(c, st):
        smin, smax = st
        c0 = pl.multiple_of(c * ck, ck)
        klo = kilo_ref[pl.ds(c0, ck), :]
        khi = kihi_ref[pl.ds(c0, ck), :]
        acc = jnp.zeros((ck, tq), F32)
        for t in range(N_IDX_HEADS // 4):
            rhs = qcat[2 * t * tq:(2 * t + 2) * tq]
            dlo = lax.dot_general(klo, rhs, CONTRACT_LAST, preferred_element_type=F32)
            dhi = lax.dot_general(khi, rhs, CONTRACT_LAST, preferred_element_type=F32)
            for j in range(2):
                p = 2 * t + j
                cs = slice(j * tq, (j + 1) * tq)
                acc = acc + wt[2 * p:2 * p + 1, :] * jnp.maximum(dlo[:, cs], 0.0)
                acc = acc + wt[2 * p + 1:2 * p + 2, :] * jnp.maximum(dhi[:, cs], 0.0)
        col = c0 + key
        valid = jnp.logical_or(
            col < N_META,
            jnp.logical_and(col >= Q_BLOCK, col - Q_BLOCK <= qry))
        st_ref[pl.ds(c0, ck), :] = jnp.where(valid, acc, NEG_INF)
        smin = jnp.minimum(smin, jnp.min(jnp.where(valid, acc, POS_INF), axis=0, keepdims=True))
        smax = jnp.maximum(smax, jnp.max(jnp.where(valid, acc, NEG_INF), axis=0, keepdims=True))
        return smin, smax

    st = lax.fori_loop(
        0, nchunks // 2, lambda j, st: idx_chunk(2 * j + 1, idx_chunk(2 * j, st)),
        (jnp.full((1, tq), POS_INF, F32), jnp.full((1, tq), NEG_INF, F32)))
    smin, smax = lax.cond(nchunks % 2 == 1, lambda st: idx_chunk(nchunks - 1, st), lambda st: st, st)

    n_valid = (N_META + 1 + i * tq + lax.broadcasted_iota(jnp.int32, (1, tq), 1)).astype(F32)
    tau = _topk_threshold(st_ref, nchunks, ck, n_valid, smin, smax)

    scale = HEAD_DIM ** -0.5

    def q_group(n):
        return jnp.concatenate(
            [q_ref[:, (n * GROUP + g) * LANES:(n * GROUP + g + 1) * LANES]
             for g in range(GROUP)], axis=0)

    def k_chunk(c0, n):
        return k_ref[pl.ds(c0, ck), n * LANES:(n + 1) * LANES]

    def write_out(n, ot):
        for g in range(GROUP):
            h = n * GROUP + g
            o_ref[:, h * LANES:(h + 1) * LANES] = jnp.transpose(
                ot[:, g * tq:(g + 1) * tq]).astype(o_ref.dtype)

    n_ref_chunks = jnp.minimum(nchunks, 2)

    def ref_body(c, mx):
        c0 = pl.multiple_of(c * ck, ck)
        sel = st_ref[pl.ds(c0, ck), :] >= tau
        out = []
        for n in range(N_KV_HEADS):
            raw = lax.dot_general(k_chunk(c0, n), q_group(n), CONTRACT_LAST,
                                  preferred_element_type=F32)
            cur = jnp.concatenate(
                [jnp.max(jnp.where(sel, raw[:, g * tq:(g + 1) * tq], NEG), axis=0, keepdims=True)
                 for g in range(GROUP)], axis=1)
            out.append(jnp.maximum(mx[n], cur))
        return tuple(out)

    mx = lax.fori_loop(0, n_ref_chunks, ref_body,
                       tuple(jnp.full((1, GROUP * tq), NEG, F32) for _ in range(N_KV_HEADS)))
    c2 = scale * LOG2E
    ones = jnp.ones((SUBLANES, HEAD_DIM), BF16)
    gap = jnp.float32(0.0)
    for n in range(N_KV_HEADS):
        qf = q_group(n).astype(F32)
        qn2 = lax.dot_general(ones, (qf * qf).astype(BF16), CONTRACT_LAST,
                              preferred_element_type=F32)[0:1]
        bound = jnp.sqrt(qn2 * kmax2_ref[n]) * (c2 * NORM_BOUND_SLACK)
        gap = jnp.maximum(gap, jnp.max(bound - mx[n] * c2))

    def fixed_reference_path():
        accf_ref[...] = jnp.zeros(accf_ref.shape, F32)

        def chunk_update(c, n):
            c0 = pl.multiple_of(c * ck, ck)
            sel = st_ref[pl.ds(c0, ck), :] >= tau
            raw = lax.dot_general(k_chunk(c0, n), q_group(n), CONTRACT_LAST,
                                  preferred_element_type=F32)
            neg_ref = mx[n] * (-c2)
            p = jnp.concatenate(
                [jnp.exp2(raw[:, g * tq:(g + 1) * tq] * c2
                          + jnp.where(sel, neg_ref[:, g * tq:(g + 1) * tq], NEG)).astype(BF16)
                 for g in range(GROUP)], axis=1)
            return jnp.dot(vt_ref[c, n], p, preferred_element_type=F32)

        def pair_body(j, carry):
            for n in range(N_KV_HEADS):
                accf_ref[n] += chunk_update(2 * j, n) + chunk_update(2 * j + 1, n)
            return carry

        lax.fori_loop(0, nchunks // 2, pair_body, 0)

        @pl.when(nchunks % 2 == 1)
        def _():
            for n in range(N_KV_HEADS):
                accf_ref[n] += chunk_update(nchunks - 1, n)

        for n in range(N_KV_HEADS):
            write_out(n, accf_ref[n, 0:HEAD_DIM, :] / accf_ref[n, HEAD_DIM:HEAD_DIM + 1, :])

    def running_max_path():
        m_ref[...] = jnp.full(m_ref.shape, NEG, F32)
        l_ref[...] = jnp.zeros(l_ref.shape, F32)
        acc_ref[...] = jnp.zeros(acc_ref.shape, F32)

        def body(c, carry):
            c0 = pl.multiple_of(c * ck, ck)
            bias = jnp.where(st_ref[pl.ds(c0, ck), :] >= tau, 0.0, NEG)
            bias = jnp.concatenate([bias] * GROUP, axis=1)
            for n in range(N_KV_HEADS):
                sc = lax.dot_general(k_chunk(c0, n), q_group(n), CONTRACT_LAST,
                                     preferred_element_type=F32)
                sc = sc * scale + bias
                m_old = m_ref[n]
                m_new = jnp.maximum(m_old, jnp.max(sc, axis=0, keepdims=True))
                alpha = jnp.exp(m_old - m_new)
                p = jnp.exp(sc - m_new)
                l_ref[n] = alpha * l_ref[n] + jnp.sum(
                    p.reshape(ck // SUBLANES, SUBLANES, GROUP * tq), axis=0)
                acc_ref[n] = alpha * acc_ref[n] + jnp.dot(
                    vt_ref[c, n, 0:HEAD_DIM, :], p.astype(BF16), preferred_element_type=F32)
                m_ref[n] = m_new
            return carry

        lax.fori_loop(0, nchunks, body, 0)
        for n in range(N_KV_HEADS):
            write_out(n, acc_ref[n] / jnp.sum(l_ref[n], axis=0, keepdims=True))

    lax.cond(gap <= MAX_EXP2_GAP, fixed_reference_path, running_max_path)


def _prompt_attention(kmax2, qi, kw, q, ki_lo, ki_hi, k_all, v_t):
    rows = q.shape[0]
    nk = k_all.shape[0]
    assert rows % Q_BLOCK == 0 and nk % KEY_CHUNK == 0
    assert nk >= (rows // Q_BLOCK + 1) * Q_BLOCK
    assert v_t.shape == (nk // KEY_CHUNK, N_KV_HEADS, HEAD_DIM + ONES_ROWS, KEY_CHUNK)
    row_spec = lambda w: pl.BlockSpec((Q_BLOCK, w), lambda i: (i, 0))
    return pl.pallas_call(
        _prompt_attn_kernel,
        grid=(rows // Q_BLOCK,),
        in_specs=[pl.BlockSpec(memory_space=pltpu.SMEM),
                  row_spec(1024), row_spec(LANES), row_spec(ATTN_WIDTH),
                  _const_spec(ki_lo.shape), _const_spec(ki_hi.shape),
                  _const_spec(k_all.shape), _const_spec(v_t.shape)],
        out_specs=row_spec(ATTN_WIDTH),
        out_shape=jax.ShapeDtypeStruct((rows, ATTN_WIDTH), BF16),
        scratch_shapes=[
            pltpu.VMEM((nk, Q_BLOCK), F32),
            pltpu.VMEM((N_KV_HEADS, 1, GROUP * Q_BLOCK), F32),
            pltpu.VMEM((N_KV_HEADS, SUBLANES, GROUP * Q_BLOCK), F32),
            pltpu.VMEM((N_KV_HEADS, HEAD_DIM, GROUP * Q_BLOCK), F32),
            pltpu.VMEM((N_KV_HEADS, HEAD_DIM + ONES_ROWS, GROUP * Q_BLOCK), F32),
        ],
        compiler_params=pltpu.CompilerParams(
            dimension_semantics=("arbitrary",), vmem_limit_bytes=VMEM_LIMIT),
        name="prompt_attn",
    )(kmax2, qi, kw, q, ki_lo, ki_hi, k_all, v_t)


def _decode_scores_kernel(pt_ref, qi_ref, w_ref, kinew_ref, *rest):
    n_pages = len(rest) - 1
    page_refs, s_ref = rest[:n_pages], rest[n_pages]
    qi = qi_ref[...]
    w = w_ref[...] * (IDX_DIM ** -0.5)
    for j in range(n_pages):
        kpt = page_refs[j][...].astype(BF16)
        d = jnp.dot(qi, kpt, preferred_element_type=F32)
        s_ref[:, j * PAGE_SIZE:(j + 1) * PAGE_SIZE] = jnp.sum(
            jnp.maximum(d, 0.0) * w, axis=0, keepdims=True)
    d_new = jnp.sum(qi.astype(F32) * kinew_ref[...].astype(F32), axis=-1, keepdims=True)
    s_new = jnp.sum(jnp.maximum(d_new, 0.0) * w, axis=0, keepdims=True)
    past = n_pages * PAGE_SIZE
    tail = s_ref.shape[1] - past
    lane = lax.broadcasted_iota(jnp.int32, (1, tail), 1)
    s_ref[:, past:] = jnp.where(lane == 0, s_new, NEG_INF)


def _page_specs(block, n_pages):
    nd = len(block)
    return [
        pl.BlockSpec((None, None) + block, functools.partial(
            lambda b, pt, j: (0, pt[b, j]) + (0,) * nd, j=j)) for j in range(n_pages)]


def _decode_scores(page_table, qi_s, w_s, kinew_s, cache_kidx, width):
    db, n_pages = page_table.shape
    seq_spec = lambda a, c: pl.BlockSpec((None, a, c), lambda b, pt: (b, 0, 0))
    return pl.pallas_call(
        _decode_scores_kernel,
        grid_spec=pltpu.PrefetchScalarGridSpec(
            num_scalar_prefetch=1,
            grid=(db,),
            in_specs=[seq_spec(N_IDX_HEADS, IDX_DIM), seq_spec(N_IDX_HEADS, 1),
                      seq_spec(1, IDX_DIM)] + _page_specs((IDX_DIM, PAGE_SIZE), n_pages),
            out_specs=seq_spec(1, width),
        ),
        out_shape=jax.ShapeDtypeStruct((db, 1, width), F32),
        compiler_params=pltpu.CompilerParams(dimension_semantics=("arbitrary",)),
        name="decode_scores",
    )(page_table, qi_s, w_s, kinew_s, *([cache_kidx] * n_pages))


def _decode_mask_kernel(s_in_ref, bias_ref, st_ref):
    nk = s_in_ref.shape[0]
    chunk = KEY_CHUNK
    s = s_in_ref[...]
    st_ref[...] = s
    valid = s > NEG_INF
    smin = jnp.min(jnp.where(valid, s, POS_INF), axis=0, keepdims=True)
    smax = jnp.max(s, axis=0, keepdims=True)
    n_valid = jnp.sum(valid.astype(F32), axis=0, keepdims=True)
    tau = _topk_threshold(st_ref, nk // chunk, chunk, n_valid, smin, smax)
    bias_ref[...] = jnp.where(st_ref[...] >= tau, 0.0, NEG)


def _decode_mask(scores_t):
    assert scores_t.shape[0] % KEY_CHUNK == 0 and scores_t.shape[1] == LANES
    return pl.pallas_call(
        _decode_mask_kernel,
        out_shape=jax.ShapeDtypeStruct(scores_t.shape, F32),
        scratch_shapes=[pltpu.VMEM(scores_t.shape, F32)],
        name="decode_mask",
    )(scores_t)


def _decode_attn_kernel(pt_ref, q_ref, bias_ref, knew_ref, vnew_ref, *rest):
    n_pages = (len(rest) - 1) // 2
    k_pages, v_pages, o_ref = rest[:n_pages], rest[n_pages:2 * n_pages], rest[2 * n_pages]
    rows = N_KV_HEADS * n_pages * PAGE_SIZE
    scale = HEAD_DIM ** -0.5
    q = q_ref[...]
    bias = jnp.concatenate(
        [jnp.broadcast_to(bias_ref[n:n + 1, :], (GROUP, bias_ref.shape[1]))
         for n in range(N_KV_HEADS)], axis=0)
    k_il = jnp.concatenate([r[...].astype(BF16) for r in k_pages], axis=0)
    v_il = jnp.concatenate([r[...].astype(BF16) for r in v_pages], axis=0)
    sc = lax.dot_general(q, k_il, CONTRACT_LAST, preferred_element_type=F32)
    sc = sc * scale + bias[:, :rows]
    sc_new = jnp.sum(q.astype(F32) * knew_ref[...].astype(F32), axis=-1, keepdims=True)
    sc_new = sc_new * scale + jnp.max(bias[:, rows:rows + N_KV_HEADS], axis=-1, keepdims=True)
    m = jnp.maximum(jnp.max(sc, axis=-1, keepdims=True), sc_new)
    p = jnp.exp(sc - m)
    p_new = jnp.exp(sc_new - m)
    denom = jnp.sum(p, axis=-1, keepdims=True) + p_new
    o = jnp.dot(p.astype(BF16), v_il, preferred_element_type=F32)
    o = o + p_new.astype(BF16).astype(F32) * vnew_ref[...].astype(F32)
    o_ref[...] = (o / denom).astype(o_ref.dtype)


def _decode_attention(page_table, q_s, bias_il, knew_s, vnew_s, cache_k, cache_v):
    db, n_pages = page_table.shape
    width2 = bias_il.shape[-1]
    seq_spec = lambda a, c: pl.BlockSpec((None, a, c), lambda b, pt: (b, 0, 0))
    page_specs = _page_specs((N_KV_HEADS * PAGE_SIZE, HEAD_DIM), n_pages)
    return pl.pallas_call(
        _decode_attn_kernel,
        grid_spec=pltpu.PrefetchScalarGridSpec(
            num_scalar_prefetch=1,
            grid=(db,),
            in_specs=[seq_spec(N_HEADS, HEAD_DIM), seq_spec(N_KV_HEADS, width2),
                      seq_spec(N_HEADS, HEAD_DIM), seq_spec(N_HEADS, HEAD_DIM)]
            + page_specs + page_specs,
            out_specs=seq_spec(N_HEADS, HEAD_DIM),
        ),
        out_shape=jax.ShapeDtypeStruct((db, N_HEADS, HEAD_DIM), BF16),
        compiler_params=pltpu.CompilerParams(
            dimension_semantics=("arbitrary",), vmem_limit_bytes=VMEM_LIMIT),
        name="decode_attn",
    )(page_table, q_s, bias_il, knew_s, vnew_s, *([cache_k] * n_pages), *([cache_v] * n_pages))


POOL_TILE = 512
POOL_HALO = 16


def _pool_prompt_kernel(u_ref, uprev_ref, umeta_ref, d_ref, ubuf):
    i = pl.program_id(0)
    tp = u_ref.shape[0]
    ubuf[0:POOL_HALO, :] = jnp.where(i == 0, umeta_ref[...], uprev_ref[...])
    ubuf[POOL_HALO:, :] = u_ref[...]
    for g, w in enumerate(POOL_WINDOWS):
        cols = slice(g * POOL_GROUP_WIDTH, (g + 1) * POOL_GROUP_WIDTH)
        cur = ubuf[POOL_HALO:POOL_HALO + tp, cols]
        tot = cur
        for j in range(1, w):
            tot = tot + ubuf[POOL_HALO - j:POOL_HALO - j + tp, cols]
        d_ref[:, cols] = tot * (1.0 / w) - cur


def _pool_prompt(u, u_meta):
    rows = u.shape[0]
    tp = POOL_TILE
    per = tp // POOL_HALO
    return pl.pallas_call(
        _pool_prompt_kernel,
        grid=(rows // tp,),
        in_specs=[pl.BlockSpec((tp, POOL_WIDTH), lambda i: (i, 0)),
                  pl.BlockSpec((POOL_HALO, POOL_WIDTH), lambda i: (jnp.maximum(i * per - 1, 0), 0)),
                  _const_spec((POOL_HALO, POOL_WIDTH))],
        out_specs=pl.BlockSpec((tp, POOL_WIDTH), lambda i: (i, 0)),
        out_shape=jax.ShapeDtypeStruct((rows, POOL_WIDTH), F32),
        scratch_shapes=[pltpu.VMEM((tp + POOL_HALO, POOL_WIDTH), F32)],
        compiler_params=pltpu.CompilerParams(dimension_semantics=("arbitrary",)),
        name="pool_prompt",
    )(u, u, u_meta)


def _pool_decode_kernel(hist_ref, u_ref, d_ref):
    for g, w in enumerate(POOL_WINDOWS):
        cols = slice(g * POOL_GROUP_WIDTH, (g + 1) * POOL_GROUP_WIDTH)
        cur = u_ref[:, cols]
        tot = cur
        for j in range(1, w):
            tot = tot + hist_ref[POOL_HIST - j, :, cols]
        d_ref[:, cols] = tot * (1.0 / w) - cur


def _pool_decode(hist, u):
    return pl.pallas_call(
        _pool_decode_kernel,
        out_shape=jax.ShapeDtypeStruct(u.shape, F32),
        name="pool_decode",
    )(hist, u)


FF_TILE = 512


def _rmsnorm(x, g):
    ms = jnp.mean(x * x, axis=-1, keepdims=True)
    return (x * lax.rsqrt(ms + EPS)) * g


def _post_kernel(x_ref, att_ref, d_ref, woa_ref, wop_ref, wpool_ref, pscale_ref,
                 gmlp_ref, gfin_ref, wup_ref, wdn_ref, y_ref, h_ref, hn_ref, acc_ref):
    f = pl.program_id(1)

    @pl.when(f == 0)
    def _():
        d = d_ref[...]
        pooled = jnp.concatenate(
            [jnp.dot(d[:, g * POOL_GROUP_WIDTH:(g + 1) * POOL_GROUP_WIDTH].astype(BF16),
                     wpool_ref[g], preferred_element_type=F32)
             for g in range(len(POOL_WINDOWS))], axis=1) * pscale_ref[...]
        h = (x_ref[...]
             + jnp.dot(att_ref[...], woa_ref[...], preferred_element_type=F32)
             + jnp.dot(pooled.astype(BF16), wop_ref[...], preferred_element_type=F32))
        h_ref[...] = h
        hn_ref[...] = _rmsnorm(h, gmlp_ref[...]).astype(BF16)
        acc_ref[...] = jnp.zeros(acc_ref.shape, F32)

    a = jnp.maximum(jnp.dot(hn_ref[...], wup_ref[...], preferred_element_type=F32), 0.0)
    acc_ref[...] += jnp.dot((a * a).astype(BF16), wdn_ref[...], preferred_element_type=F32)

    @pl.when(f == pl.num_programs(1) - 1)
    def _():
        y_ref[...] = _rmsnorm(h_ref[...] + acc_ref[...], gfin_ref[...])


def _post(x, att, d, wo_a, wo_p, w_pool, pool_scale, g_mlp, g_final, w_up, w_down, tm):
    rows = x.shape[0]
    assert rows % tm == 0 and D_FF % FF_TILE == 0
    row_spec = lambda w: pl.BlockSpec((tm, w), lambda i, f: (i, 0))
    return pl.pallas_call(
        _post_kernel,
        grid=(rows // tm, D_FF // FF_TILE),
        in_specs=[row_spec(D_MODEL), row_spec(ATTN_WIDTH), row_spec(POOL_WIDTH),
                  _const_spec(wo_a.shape), _const_spec(wo_p.shape), _const_spec(w_pool.shape),
                  _const_spec((1, POOL_WIDTH)), _const_spec((1, D_MODEL)), _const_spec((1, D_MODEL)),
                  pl.BlockSpec((D_MODEL, FF_TILE), lambda i, f: (0, f)),
                  pl.BlockSpec((FF_TILE, D_MODEL), lambda i, f: (f, 0))],
        out_specs=row_spec(D_MODEL),
        out_shape=jax.ShapeDtypeStruct((rows, D_MODEL), F32),
        scratch_shapes=[pltpu.VMEM((tm, D_MODEL), F32), pltpu.VMEM((tm, D_MODEL), BF16),
                        pltpu.VMEM((tm, D_MODEL), F32)],
        compiler_params=pltpu.CompilerParams(
            dimension_semantics=("arbitrary", "arbitrary"), vmem_limit_bytes=VMEM_LIMIT),
        name="post",
    )(x, att, d, wo_a, wo_p, w_pool, pool_scale.reshape(1, POOL_WIDTH),
      g_mlp.reshape(1, D_MODEL), g_final.reshape(1, D_MODEL), w_up, w_down)


def kernel(x_prompt, x_sample, cache_k, cache_v, cache_kidx, state_pool, page_table, meta_tokens,
           g_mix, w_in, w_pool, pool_scale, w_out, g_mlp, w_up, w_down, g_final):
    assert x_prompt.shape[0] == 1 and x_sample.shape[1] == 1 and g_mix.shape[0] == 1
    seq = x_prompt.shape[1]
    db = x_sample.shape[0]
    n_pages = page_table.shape[1]
    past = n_pages * PAGE_SIZE
    assert db == LANES

    w = w_in[0]
    o = 0
    parts = []
    for width in (ATTN_WIDTH, 2 * N_KV_HEADS * HEAD_DIM, N_IDX_HEADS * IDX_DIM,
                  IDX_DIM + N_IDX_HEADS, POOL_WIDTH):
        parts.append(w[:, o:o + width].astype(BF16))
        o += width
    parts[3] = jnp.pad(parts[3], ((0, 0), (0, LANES - parts[3].shape[1])))
    wo_a = w_out[0, :ATTN_WIDTH].astype(BF16)
    wo_p = w_out[0, ATTN_WIDTH:].astype(BF16)
    w_pool_b = w_pool[0].astype(BF16)
    w_up_b = w_up[0].astype(BF16)
    w_down_b = w_down[0].astype(BF16)

    xp = x_prompt[0]
    pos_p = N_META + jnp.arange(seq, dtype=jnp.int32)
    q_p, kf_p, vf_p, kb_p, vb_p, qi_p, kw_p, u_p = _project(xp, pos_p, g_mix[0], parts, 256)
    xs = jnp.concatenate([meta_tokens.astype(F32), x_sample[:, 0]], axis=0)
    pos_s = jnp.concatenate([jnp.arange(N_META, dtype=jnp.int32),
                             jnp.full((db,), past, jnp.int32)])
    q_s, kf_s, vf_s, kb_s, vb_s, qi_s, kw_s, u_s = _project(xs, pos_s, g_mix[0], parts, xs.shape[0])

    nk = -(-(seq + 2 * Q_BLOCK) // KEY_CHUNK) * KEY_CHUNK

    def key_rows(meta_part, prompt_part):
        width = meta_part.shape[1]
        return jnp.concatenate(
            [meta_part, jnp.zeros((Q_BLOCK - N_META, width), meta_part.dtype), prompt_part,
             jnp.zeros((nk - Q_BLOCK - seq, width), meta_part.dtype)], axis=0)

    ki_all = key_rows(kw_s[:N_META, :IDX_DIM], kw_p[:, :IDX_DIM]).astype(BF16)
    zk = jnp.zeros_like(ki_all)
    ki_lo = jnp.concatenate([ki_all, zk], axis=1)
    ki_hi = jnp.concatenate([zk, ki_all], axis=1)
    k_all = key_rows(kb_s[:N_META], kb_p)
    v_all = key_rows(vb_s[:N_META], vb_p)
    v_t = jnp.transpose(
        v_all.reshape(nk // KEY_CHUNK, KEY_CHUNK, N_KV_HEADS, HEAD_DIM), (0, 2, 3, 1))
    v_t = jnp.concatenate(
        [v_t, jnp.ones((nk // KEY_CHUNK, N_KV_HEADS, ONES_ROWS, KEY_CHUNK), BF16)], axis=2)
    knorm = lambda kw: kw[:, KNORM_LANE:KNORM_LANE + N_KV_HEADS]
    kmax2 = jnp.maximum(jnp.max(knorm(kw_p), axis=0), jnp.max(knorm(kw_s[:N_META]), axis=0))
    att_p = _prompt_attention(kmax2, qi_p, kw_p, q_p, ki_lo, ki_hi, k_all, v_t)

    width = -(-(past + 1) // KEY_CHUNK) * KEY_CHUNK
    qi_d = qi_s[N_META:].reshape(db, N_IDX_HEADS, IDX_DIM)
    w_d = kw_s[N_META:, IDX_DIM:IDX_DIM + N_IDX_HEADS].reshape(db, N_IDX_HEADS, 1)
    kinew = kw_s[N_META:, :IDX_DIM].astype(BF16).reshape(db, 1, IDX_DIM)
    scores = _decode_scores(page_table, qi_d, w_d, kinew, jnp.swapaxes(cache_kidx, 2, 3), width)
    bias_t = _decode_mask(jnp.transpose(scores.reshape(db, width)))
    bias = jnp.transpose(bias_t)
    negs = jnp.full_like(bias, NEG)
    bias_il = jnp.stack(
        [jnp.stack([bias, negs], axis=-1), jnp.stack([negs, bias], axis=-1)], axis=1,
    ).reshape(db, N_KV_HEADS, N_KV_HEADS * width)
    n_phys = cache_k.shape[1]
    il_shape = (1, n_phys, N_KV_HEADS * PAGE_SIZE, HEAD_DIM)
    per_head = lambda a: jnp.repeat(a.reshape(db, N_KV_HEADS, HEAD_DIM), GROUP, axis=1)
    att_s = _decode_attention(
        page_table, q_s[N_META:].reshape(db, N_HEADS, HEAD_DIM), bias_il,
        per_head(kb_s[N_META:]), per_head(vb_s[N_META:]),
        cache_k.reshape(il_shape), cache_v.reshape(il_shape))
    att_s = att_s.reshape(db, ATTN_WIDTH)

    d_p = _pool_prompt(u_p, u_s[:N_META])
    d_s = _pool_decode(jnp.swapaxes(state_pool[0], 0, 1), u_s[N_META:])

    post = functools.partial(
        _post, wo_a=wo_a, wo_p=wo_p, w_pool=w_pool_b, pool_scale=pool_scale[0],
        g_mlp=g_mlp[0], g_final=g_final, w_up=w_up_b, w_down=w_down_b)
    y_p = post(xp, att_p, d_p, tm=512)
    y_s = post(x_sample[:, 0], att_s, d_s, tm=db)

    k_prompt = jnp.concatenate([kf_s[:N_META], kf_p], axis=0).reshape(1, 1, N_META + seq, N_KV_HEADS, HEAD_DIM)
    v_prompt = jnp.concatenate([vf_s[:N_META], vf_p], axis=0).reshape(1, 1, N_META + seq, N_KV_HEADS, HEAD_DIM)
    kidx_prompt = jnp.concatenate([kw_s[:N_META, :IDX_DIM], kw_p[:, :IDX_DIM]], axis=0).reshape(
        1, 1, N_META + seq, IDX_DIM)
    pool_prompt = u_p[seq - POOL_HIST:].reshape(1, 1, POOL_HIST, POOL_WIDTH)
    k_sample = kf_s[N_META:].reshape(1, db, 1, N_KV_HEADS, HEAD_DIM)
    v_sample = vf_s[N_META:].reshape(1, db, 1, N_KV_HEADS, HEAD_DIM)
    kidx_sample = kw_s[N_META:, :IDX_DIM].reshape(1, db, 1, IDX_DIM)
    pool_sample = jnp.concatenate(
        [state_pool[0][:, 1:], u_s[N_META:][:, None, :]], axis=1)[None]
    return (y_p.reshape(1, seq, D_MODEL), y_s.reshape(db, 1, D_MODEL),
            k_prompt, v_prompt, kidx_prompt, pool_prompt,
            k_sample, v_sample, kidx_sample, pool_sample)
```

```python
import functools

import jax
import jax.numpy as jnp
from jax import lax
from jax.experimental import pallas as pl
from jax.experimental.pallas import tpu as pltpu

F32 = jnp.float32
BF16 = jnp.bfloat16

D_MODEL = 2048
N_META = 16
ATTN_WIDTH = 1024
POOL_WIDTH = 1024
HEAD_DIM = 128
N_HEADS = 8
N_KV_HEADS = 2
GROUP = N_HEADS // N_KV_HEADS
N_IDX_HEADS = 16
IDX_DIM = 64
TOPK = 256
POOL_WINDOWS = (2, 4, 8, 16)
POOL_GROUP_WIDTH = 256
POOL_HIST = 15
D_FF = 8192
PAGE_SIZE = 128
ROPE_THETA = 10000.0
EPS = 1e-6
NEG = -1e30
NEG_INF = float("-inf")
POS_INF = float("inf")
F32_LOWEST = -3.0e38

LANES = 128
SUBLANES = 8
VMEM_LIMIT = 56 * 1024 * 1024

Q_BLOCK = 128
KEY_CHUNK = 512
MAX_BISECT = 320
N_INTERPOLATE_ROUNDS = 6
LOG2E = 1.4426950408889634
MAX_EXP2_GAP = 80.0
NORM_BOUND_SLACK = 1.02
ONES_ROWS = 16
KNORM_LANE = IDX_DIM + N_IDX_HEADS

CONTRACT_LAST = (((1,), (1,)), ((), ()))


def _const_spec(shape):
    nd = len(shape)
    return pl.BlockSpec(shape, lambda *_: (0,) * nd, pipeline_mode=pl.Buffered(1))


def _rope128(x, c, s):
    return x * c + pltpu.roll(x, 64, 1) * s


def _rope64(x, c, sa, sb):
    return x * c + pltpu.roll(x, 96, 1) * sa + pltpu.roll(x, 32, 1) * sb


def _proj_kernel(x_ref, g_ref, c128_ref, s128_ref, c64_ref, sa64_ref, sb64_ref,
                 wq_ref, wkv_ref, wqi_ref, wkw_ref, wu_ref,
                 q_ref, kf_ref, vf_ref, kb_ref, vb_ref, qi_ref, kw_ref, u_ref):
    x = x_ref[...]
    ms = jnp.mean(x * x, axis=-1, keepdims=True)
    xn = ((x * lax.rsqrt(ms + EPS)) * g_ref[...]).astype(BF16)
    c128 = c128_ref[...]
    s128 = s128_ref[...]
    c64 = c64_ref[...]
    sa64 = sa64_ref[...]
    sb64 = sb64_ref[...]

    zq = jnp.dot(xn, wq_ref[...], preferred_element_type=F32)
    for h in range(N_HEADS):
        sl = slice(h * LANES, (h + 1) * LANES)
        q_ref[:, sl] = _rope128(zq[:, sl], c128, s128).astype(BF16)

    zkv = jnp.dot(xn, wkv_ref[...], preferred_element_type=F32)
    k_norm2 = []
    for h in range(N_KV_HEADS):
        sl = slice(h * LANES, (h + 1) * LANES)
        kr = _rope128(zkv[:, sl], c128, s128)
        kf_ref[:, sl] = kr
        kb = kr.astype(BF16)
        kb_ref[:, sl] = kb
        k_norm2.append(jnp.sum(kb.astype(F32) * kb.astype(F32), axis=-1, keepdims=True))
    v = zkv[:, N_KV_HEADS * LANES:]
    vf_ref[...] = v
    vb_ref[...] = v.astype(BF16)

    zqi = jnp.dot(xn, wqi_ref[...], preferred_element_type=F32)
    for p in range(N_IDX_HEADS // 2):
        sl = slice(p * LANES, (p + 1) * LANES)
        qi_ref[:, sl] = _rope64(zqi[:, sl], c64, sa64, sb64).astype(BF16)

    zkw = jnp.dot(xn, wkw_ref[...], preferred_element_type=F32)
    lane = lax.broadcasted_iota(jnp.int32, zkw.shape, 1)
    is_key = lane < IDX_DIM
    ckw = jnp.where(is_key, c64, jnp.where(lane < IDX_DIM + N_IDX_HEADS, N_IDX_HEADS ** -0.5, 1.0))
    kw = _rope64(zkw, ckw, jnp.where(is_key, sa64, 0.0), jnp.where(is_key, sb64, 0.0))
    for h in range(N_KV_HEADS):
        kw = jnp.where(lane == KNORM_LANE + h, k_norm2[h], kw)
    kw_ref[...] = kw

    u_ref[...] = jnp.dot(xn, wu_ref[...], preferred_element_type=F32)


def _project(x, pos, g_mix, w_parts, tm):
    rows = x.shape[0]
    assert rows % tm == 0
    c128, s128, c64, sa64, sb64 = _rope_tables(pos)
    wq, wkv, wqi, wkw, wu = w_parts
    row_spec = lambda w: pl.BlockSpec((tm, w), lambda i: (i, 0))
    out_shapes = (
        jax.ShapeDtypeStruct((rows, ATTN_WIDTH), BF16),
        jax.ShapeDtypeStruct((rows, 256), F32),
        jax.ShapeDtypeStruct((rows, 256), F32),
        jax.ShapeDtypeStruct((rows, 256), BF16),
        jax.ShapeDtypeStruct((rows, 256), BF16),
        jax.ShapeDtypeStruct((rows, 1024), BF16),
        jax.ShapeDtypeStruct((rows, LANES), F32),
        jax.ShapeDtypeStruct((rows, POOL_WIDTH), F32),
    )
    return pl.pallas_call(
        _proj_kernel,
        grid=(rows // tm,),
        in_specs=[row_spec(D_MODEL), _const_spec((1, D_MODEL))]
        + [row_spec(LANES)] * 5
        + [_const_spec(w.shape) for w in w_parts],
        out_specs=tuple(row_spec(s.shape[1]) for s in out_shapes),
        out_shape=out_shapes,
        compiler_params=pltpu.CompilerParams(
            dimension_semantics=("arbitrary",), vmem_limit_bytes=VMEM_LIMIT),
        name="proj",
    )(x, g_mix.reshape(1, D_MODEL), c128, s128, c64, sa64, sb64, wq, wkv, wqi, wkw, wu)


def _rope_tables(pos):
    posf = pos.astype(F32)[:, None]
    inv = ROPE_THETA ** (-jnp.arange(0, HEAD_DIM, 2, dtype=F32) / HEAD_DIM)
    ang = posf * inv[None, :]
    c, s = jnp.cos(ang), jnp.sin(ang)
    c128 = jnp.concatenate([c, c], axis=-1)
    s128 = jnp.concatenate([-s, s], axis=-1)
    inv = ROPE_THETA ** (-jnp.arange(0, IDX_DIM, 2, dtype=F32) / IDX_DIM)
    ang = posf * inv[None, :]
    c, s = jnp.cos(ang), jnp.sin(ang)
    z = jnp.zeros_like(s)
    c64 = jnp.concatenate([c, c, c, c], axis=-1)
    sa64 = jnp.concatenate([-s, z, -s, z], axis=-1)
    sb64 = jnp.concatenate([z, s, z, s], axis=-1)
    return c128, s128, c64, sa64, sb64


def _key_count(st_ref, nchunks, chunk, pred):
    acc_rows = 8 * SUBLANES

    def body(c, part):
        c0 = pl.multiple_of(c * chunk, chunk)
        hit = pred(st_ref[pl.ds(c0, chunk), :], c0).astype(F32)
        return part + jnp.sum(hit.reshape(chunk // acc_rows, acc_rows, LANES), axis=0)

    part = lax.fori_loop(0, nchunks, body, jnp.zeros((acc_rows, LANES), F32))
    return jnp.sum(part, axis=0, keepdims=True)


def _topk_threshold(st_ref, nchunks, chunk, n_valid, s_min, s_max):
    kf = float(TOPK)
    take_all = n_valid <= kf

    def count_ge(t):
        return _key_count(st_ref, nchunks, chunk, lambda v, c0: v >= t)

    def cond(st):
        it, done = st[0], st[5]
        return jnp.logical_and(it < MAX_BISECT, jnp.min(done) < 0.5)

    def largest_below(hi):
        def body(c, part):
            c0 = pl.multiple_of(c * chunk, chunk)
            v = st_ref[pl.ds(c0, chunk), :]
            cand = jnp.where(v < hi, v, NEG_INF)
            return jnp.maximum(
                part, jnp.max(cand.reshape(chunk // acc_rows, acc_rows, LANES), axis=0))

        acc_rows = 8 * SUBLANES
        part = lax.fori_loop(0, nchunks, body, jnp.full((acc_rows, LANES), NEG_INF, F32))
        return jnp.max(part, axis=0, keepdims=True)

    def step(st, mode):
        lo, hi, flo, fhi, done, tau = st
        half = lo + (hi - lo) * 0.5
        collapsed = jnp.logical_or(half <= lo, half >= hi)
        mid = half
        if mode == "interpolate":
            frac = (flo - (kf + 0.5)) / jnp.maximum(flo - fhi, 1.0)
            guess = lo + (hi - lo) * jnp.clip(frac, 1.0 / 32, 31.0 / 32)
            mid = jnp.where(jnp.logical_and(guess > lo, guess < hi), guess, half)
        elif mode == "extract":
            below = largest_below(hi)
            mid = jnp.where(below >= lo, below, half)
        cnt = count_ge(mid)
        found = cnt == kf
        active = done < 0.5
        go_up = jnp.logical_and(active, cnt > kf)
        go_dn = jnp.logical_and(active, cnt < kf)
        lo = jnp.where(go_up, mid, lo)
        hi = jnp.where(go_dn, mid, hi)
        flo = jnp.where(go_up, cnt, flo)
        fhi = jnp.where(go_dn, cnt, fhi)
        tau = jnp.where(jnp.logical_and(active, found), mid, tau)
        done = jnp.where(jnp.logical_or(found, collapsed), 1.0, done)
        return lo, hi, flo, fhi, done, tau

    def body(st):
        inner = lax.cond(st[0] < N_INTERPOLATE_ROUNDS,
                         lambda s: step(s, "interpolate"), lambda s: step(s, "extract"), st[1:])
        return (st[0] + 1,) + step(inner, "halve")

    done0 = jnp.where(take_all, 1.0, 0.0).astype(F32)
    tau0 = jnp.full((1, LANES), POS_INF, F32)
    fhi0 = jnp.ones((1, LANES), F32)
    _, lo, hi, _, _, _, tau = lax.while_loop(
        cond, body, (jnp.int32(0), s_min, s_max, n_valid, fhi0, done0, tau0))

    unresolved = jnp.logical_and(tau == POS_INF, jnp.logical_not(take_all))
    n_unres = jnp.sum(unresolved.astype(F32))
    tau = jnp.where(take_all, F32_LOWEST, tau)

    def tie_path(tau):
        cut = jnp.where(count_ge(hi) >= kf, hi, lo)
        cut = jnp.where(unresolved, cut, POS_INF)
        c_gt = _key_count(st_ref, nchunks, chunk, lambda v, c0: v > cut)
        keep = kf - c_gt

        def key_iota(c0):
            return c0 + lax.broadcasted_iota(jnp.int32, (chunk, LANES), 0)

        def idx_body(_, st):
            xlo, xhi = st
            xmid = (xlo + xhi) // 2
            cnt = _key_count(
                st_ref, nchunks, chunk,
                lambda v, c0: jnp.logical_and(v == cut, key_iota(c0) < xmid))
            ok = cnt >= keep
            return jnp.where(ok, xlo, xmid), jnp.where(ok, xmid, xhi)

        xlo0 = jnp.zeros((1, LANES), jnp.int32)
        xhi0 = jnp.full((1, LANES), nchunks * chunk, jnp.int32)
        n_idx_steps = 15
        _, xcut = lax.fori_loop(0, n_idx_steps, idx_body, (xlo0, xhi0))

        def drop_body(c, carry):
            c0 = pl.multiple_of(c * chunk, chunk)
            v = st_ref[pl.ds(c0, chunk), :]
            drop = jnp.logical_and(v == cut, key_iota(c0) >= xcut)
            st_ref[pl.ds(c0, chunk), :] = jnp.where(drop, NEG_INF, v)
            return carry

        lax.fori_loop(0, nchunks, drop_body, 0)
        return jnp.where(unresolved, cut, tau)

    return lax.cond(n_unres > 0.0, tie_path, lambda t: t, tau)


def _prompt_attn_kernel(kmax2_ref, qi_ref, kw_ref, q_ref, kilo_ref, kihi_ref, k_ref, vt_ref, o_ref,
                        st_ref, m_ref, l_ref, acc_ref, accf_ref):
    i = pl.program_id(0)
    tq, ck = Q_BLOCK, KEY_CHUNK
    n_blocks = i + 2
    nchunks = (n_blocks * Q_BLOCK + ck - 1) // ck

    wt = jnp.transpose(kw_ref[...])[IDX_DIM:IDX_DIM + N_IDX_HEADS, :] * (IDX_DIM ** -0.5)
    qcat = jnp.concatenate(
        [qi_ref[:, p * LANES:(p + 1) * LANES] for p in range(N_IDX_HEADS // 2)], axis=0)
    key = lax.broadcasted_iota(jnp.int32, (ck, tq), 0)
    qry = i * tq + lax.broadcasted_iota(jnp.int32, (ck, tq), 1)

    def idx_chunk(c, st):
        smin, smax = st
        c0 = pl.multiple_of(c * ck, ck)
        klo = kilo_ref[pl.ds(c0, ck), :]
        khi = kihi_ref[pl.ds(c0, ck), :]
        acc = jnp.zeros((ck, tq), F32)
        for t in range(N_IDX_HEADS // 4):
            rhs = qcat[2 * t * tq:(2 * t + 2) * tq]
            dlo = lax.dot_general(klo, rhs, CONTRACT_LAST, preferred_element_type=F32)
            dhi = lax.dot_general(khi, rhs, CONTRACT_LAST, preferred_element_type=F32)
            for j in range(2):
                p = 2 * t + j
                cs = slice(j * tq, (j + 1) * tq)
                acc = acc + wt[2 * p:2 * p + 1, :] * jnp.maximum(dlo[:, cs], 0.0)
                acc = acc + wt[2 * p + 1:2 * p + 2, :] * jnp.maximum(dhi[:, cs], 0.0)
        col = c0 + key
        valid = jnp.logical_or(
            col < N_META,
            jnp.logical_and(col >= Q_BLOCK, col - Q_BLOCK <= qry))
        st_ref[pl.ds(c0, ck), :] = jnp.where(valid, acc, NEG_INF)
        smin = jnp.minimum(smin, jnp.min(jnp.where(valid, acc, POS_INF), axis=0, keepdims=True))
        smax = jnp.maximum(smax, jnp.max(jnp.where(valid, acc, NEG_INF), axis=0, keepdims=True))
        return smin, smax

    st = lax.fori_loop(
        0, nchunks // 2, lambda j, st: idx_chunk(2 * j + 1, idx_chunk(2 * j, st)),
        (jnp.full((1, tq), POS_INF, F32), jnp.full((1, tq), NEG_INF, F32)))
    smin, smax = lax.cond(nchunks % 2 == 1, lambda st: idx_chunk(nchunks - 1, st), lambda st: st, st)

    n_valid = (N_META + 1 + i * tq + lax.broadcasted_iota(jnp.int32, (1, tq), 1)).astype(F32)
    tau = _topk_threshold(st_ref, nchunks, ck, n_valid, smin, smax)

    scale = HEAD_DIM ** -0.5

    def q_group(n):
        return jnp.concatenate(
            [q_ref[:, (n * GROUP + g) * LANES:(n * GROUP + g + 1) * LANES]
             for g in range(GROUP)], axis=0)

    def k_chunk(c0, n):
        return k_ref[pl.ds(c0, ck), n * LANES:(n + 1) * LANES]

    def write_out(n, ot):
        for g in range(GROUP):
            h = n * GROUP + g
            o_ref[:, h * LANES:(h + 1) * LANES] = jnp.transpose(
                ot[:, g * tq:(g + 1) * tq]).astype(o_ref.dtype)

    n_ref_chunks = jnp.minimum(nchunks, 2)

    def ref_body(c, mx):
        c0 = pl.multiple_of(c * ck, ck)
        sel = st_ref[pl.ds(c0, ck), :] >= tau
        out = []
        for n in range(N_KV_HEADS):
            raw = lax.dot_general(k_chunk(c0, n), q_group(n), CONTRACT_LAST,
                                  preferred_element_type=F32)
            cur = jnp.concatenate(
                [jnp.max(jnp.where(sel, raw[:, g * tq:(g + 1) * tq], NEG), axis=0, keepdims=True)
                 for g in range(GROUP)], axis=1)
            out.append(jnp.maximum(mx[n], cur))
        return tuple(out)

    mx = lax.fori_loop(0, n_ref_chunks, ref_body,
                       tuple(jnp.full((1, GROUP * tq), NEG, F32) for _ in range(N_KV_HEADS)))
    c2 = scale * LOG2E
    ones = jnp.ones((SUBLANES, HEAD_DIM), BF16)
    gap = jnp.float32(0.0)
    for n in range(N_KV_HEADS):
        qf = q_group(n).astype(F32)
        qn2 = lax.dot_general(ones, (qf * qf).astype(BF16), CONTRACT_LAST,
                              preferred_element_type=F32)[0:1]
        bound = jnp.sqrt(qn2 * kmax2_ref[n]) * (c2 * NORM_BOUND_SLACK)
        gap = jnp.maximum(gap, jnp.max(bound - mx[n] * c2))

    def fixed_reference_path():
        accf_ref[...] = jnp.zeros(accf_ref.shape, F32)

        def chunk_update(c, n):
            c0 = pl.multiple_of(c * ck, ck)
            sel = st_ref[pl.ds(c0, ck), :] >= tau
            raw = lax.dot_general(k_chunk(c0, n), q_group(n), CONTRACT_LAST,
                                  preferred_element_type=F32)
            neg_ref = mx[n] * (-c2)
            p = jnp.concatenate(
                [jnp.exp2(raw[:, g * tq:(g + 1) * tq] * c2
                          + jnp.where(sel, neg_ref[:, g * tq:(g + 1) * tq], NEG)).astype(BF16)
                 for g in range(GROUP)], axis=1)
            return jnp.dot(vt_ref[c, n], p, preferred_element_type=F32)

        def pair_body(j, carry):
            for n in range(N_KV_HEADS):
                accf_ref[n] += chunk_update(2 * j, n) + chunk_update(2 * j + 1, n)
            return carry

        lax.fori_loop(0, nchunks // 2, pair_body, 0)

        @pl.when(nchunks % 2 == 1)
        def _():
            for n in range(N_KV_HEADS):
                accf_ref[n] += chunk_update(nchunks - 1, n)

        for n in range(N_KV_HEADS):
            write_out(n, accf_ref[n, 0:HEAD_DIM, :] / accf_ref[n, HEAD_DIM:HEAD_DIM + 1, :])

    def running_max_path():
        m_ref[...] = jnp.full(m_ref.shape, NEG, F32)
        l_ref[...] = jnp.zeros(l_ref.shape, F32)
        acc_ref[...] = jnp.zeros(acc_ref.shape, F32)

        def body(c, carry):
            c0 = pl.multiple_of(c * ck, ck)
            bias = jnp.where(st_ref[pl.ds(c0, ck), :] >= tau, 0.0, NEG)
            bias = jnp.concatenate([bias] * GROUP, axis=1)
            for n in range(N_KV_HEADS):
                sc = lax.dot_general(k_chunk(c0, n), q_group(n), CONTRACT_LAST,
                                     preferred_element_type=F32)
                sc = sc * scale + bias
                m_old = m_ref[n]
                m_new = jnp.maximum(m_old, jnp.max(sc, axis=0, keepdims=True))
                alpha = jnp.exp(m_old - m_new)
                p = jnp.exp(sc - m_new)
                l_ref[n] = alpha * l_ref[n] + jnp.sum(
                    p.reshape(ck // SUBLANES, SUBLANES, GROUP * tq), axis=0)
                acc_ref[n] = alpha * acc_ref[n] + jnp.dot(
                    vt_ref[c, n, 0:HEAD_DIM, :], p.astype(BF16), preferred_element_type=F32)
                m_ref[n] = m_new
            return carry

        lax.fori_loop(0, nchunks, body, 0)
        for n in range(N_KV_HEADS):
            write_out(n, acc_ref[n] / jnp.sum(l_ref[n], axis=0, keepdims=True))

    lax.cond(gap <= MAX_EXP2_GAP, fixed_reference_path, running_max_path)


def _prompt_attention(kmax2, qi, kw, q, ki_lo, ki_hi, k_all, v_t):
    rows = q.shape[0]
    nk = k_all.shape[0]
    assert rows % Q_BLOCK == 0 and nk % KEY_CHUNK == 0
    assert nk >= (rows // Q_BLOCK + 1) * Q_BLOCK
    assert v_t.shape == (nk // KEY_CHUNK, N_KV_HEADS, HEAD_DIM + ONES_ROWS, KEY_CHUNK)
    row_spec = lambda w: pl.BlockSpec((Q_BLOCK, w), lambda i: (i, 0))
    return pl.pallas_call(
        _prompt_attn_kernel,
        grid=(rows // Q_BLOCK,),
        in_specs=[pl.BlockSpec(memory_space=pltpu.SMEM),
                  row_spec(1024), row_spec(LANES), row_spec(ATTN_WIDTH),
                  _const_spec(ki_lo.shape), _const_spec(ki_hi.shape),
                  _const_spec(k_all.shape), _const_spec(v_t.shape)],
        out_specs=row_spec(ATTN_WIDTH),
        out_shape=jax.ShapeDtypeStruct((rows, ATTN_WIDTH), BF16),
        scratch_shapes=[
            pltpu.VMEM((nk, Q_BLOCK), F32),
            pltpu.VMEM((N_KV_HEADS, 1, GROUP * Q_BLOCK), F32),
            pltpu.VMEM((N_KV_HEADS, SUBLANES, GROUP * Q_BLOCK), F32),
            pltpu.VMEM((N_KV_HEADS, HEAD_DIM, GROUP * Q_BLOCK), F32),
            pltpu.VMEM((N_KV_HEADS, HEAD_DIM + ONES_ROWS, GROUP * Q_BLOCK), F32),
        ],
        compiler_params=pltpu.CompilerParams(
            dimension_semantics=("arbitrary",), vmem_limit_bytes=VMEM_LIMIT),
        name="prompt_attn",
    )(kmax2, qi, kw, q, ki_lo, ki_hi, k_all, v_t)


def _prefetch_pages(copies, n_pages, b, n_seq):
    @pl.when(b == 0)
    def _():
        for j in range(n_pages):
            for cp in copies(0, 0, j):
                cp.start()

    @pl.when(b + 1 < n_seq)
    def _():
        for j in range(n_pages):
            for cp in copies(b + 1, (b + 1) % 2, j):
                cp.start()

    for j in range(n_pages):
        for cp in copies(b, b % 2, j):
            cp.wait()


def _decode_scores_kernel(pt_ref, qi_ref, w_ref, kinew_ref, cache_ref, s_ref, buf, sem):
    b = pl.program_id(0)
    n_pages = pt_ref.shape[1]

    def copies(seq, slot, j):
        return (pltpu.make_async_copy(
            cache_ref.at[0, pt_ref[seq, j]], buf.at[slot, j], sem.at[slot]),)

    _prefetch_pages(copies, n_pages, b, pl.num_programs(0))
    slot = b % 2

    qi = qi_ref[...]
    w = w_ref[...] * (IDX_DIM ** -0.5)
    for j in range(n_pages):
        kpt = buf[slot, j].astype(BF16)
        d = jnp.dot(qi, kpt, preferred_element_type=F32)
        s_ref[:, j * PAGE_SIZE:(j + 1) * PAGE_SIZE] = jnp.sum(
            jnp.maximum(d, 0.0) * w, axis=0, keepdims=True)
    d_new = jnp.sum(qi.astype(F32) * kinew_ref[...].astype(F32), axis=-1, keepdims=True)
    s_new = jnp.sum(jnp.maximum(d_new, 0.0) * w, axis=0, keepdims=True)
    past = n_pages * PAGE_SIZE
    tail = s_ref.shape[1] - past
    lane = lax.broadcasted_iota(jnp.int32, (1, tail), 1)
    s_ref[:, past:] = jnp.where(lane == 0, s_new, NEG_INF)


def _decode_scores(page_table, qi_s, w_s, kinew_s, cache_kidx_t, width):
    db, n_pages = page_table.shape
    seq_spec = lambda a, c: pl.BlockSpec((None, a, c), lambda b, pt: (b, 0, 0))
    return pl.pallas_call(
        _decode_scores_kernel,
        grid_spec=pltpu.PrefetchScalarGridSpec(
            num_scalar_prefetch=1,
            grid=(db,),
            in_specs=[seq_spec(N_IDX_HEADS, IDX_DIM), seq_spec(N_IDX_HEADS, 1),
                      seq_spec(1, IDX_DIM), pl.BlockSpec(memory_space=pl.ANY)],
            out_specs=seq_spec(1, width),
            scratch_shapes=[pltpu.VMEM((2, n_pages, IDX_DIM, PAGE_SIZE), F32),
                            pltpu.SemaphoreType.DMA((2,))],
        ),
        out_shape=jax.ShapeDtypeStruct((db, 1, width), F32),
        compiler_params=pltpu.CompilerParams(dimension_semantics=("arbitrary",)),
        name="decode_scores",
    )(page_table, qi_s, w_s, kinew_s, cache_kidx_t)


def _decode_mask_kernel(s_in_ref, bias_ref, st_ref):
    nk = s_in_ref.shape[0]
    chunk = KEY_CHUNK
    s = s_in_ref[...]
    st_ref[...] = s
    valid = s > NEG_INF
    smin = jnp.min(jnp.where(valid, s, POS_INF), axis=0, keepdims=True)
    smax = jnp.max(s, axis=0, keepdims=True)
    n_valid = jnp.sum(valid.astype(F32), axis=0, keepdims=True)
    tau = _topk_threshold(st_ref, nk // chunk, chunk, n_valid, smin, smax)
    bias_ref[...] = jnp.where(st_ref[...] >= tau, 0.0, NEG)


def _decode_mask(scores_t):
    assert scores_t.shape[0] % KEY_CHUNK == 0 and scores_t.shape[1] == LANES
    return pl.pallas_call(
        _decode_mask_kernel,
        out_shape=jax.ShapeDtypeStruct(scores_t.shape, F32),
        scratch_shapes=[pltpu.VMEM(scores_t.shape, F32)],
        name="decode_mask",
    )(scores_t)


def _decode_attn_kernel(pt_ref, q_ref, bias_ref, knew_ref, vnew_ref, ck_ref, cv_ref, o_ref,
                        kbuf, vbuf, sem):
    b = pl.program_id(0)
    n_pages = pt_ref.shape[1]
    page_rows = N_KV_HEADS * PAGE_SIZE

    def copies(seq, slot, j):
        rows_j = pl.ds(j * page_rows, page_rows)
        page = pt_ref[seq, j]
        return (pltpu.make_async_copy(ck_ref.at[0, page], kbuf.at[slot, rows_j], sem.at[slot, 0]),
                pltpu.make_async_copy(cv_ref.at[0, page], vbuf.at[slot, rows_j], sem.at[slot, 1]))

    _prefetch_pages(copies, n_pages, b, pl.num_programs(0))
    slot = b % 2

    rows = n_pages * page_rows
    scale = HEAD_DIM ** -0.5
    q = q_ref[...]
    bias = jnp.concatenate(
        [jnp.broadcast_to(bias_ref[n:n + 1, :], (GROUP, bias_ref.shape[1]))
         for n in range(N_KV_HEADS)], axis=0)
    k_il = kbuf[slot].astype(BF16)
    v_il = vbuf[slot].astype(BF16)
    sc = lax.dot_general(q, k_il, CONTRACT_LAST, preferred_element_type=F32)
    sc = sc * scale + bias[:, :rows]
    sc_new = jnp.sum(q.astype(F32) * knew_ref[...].astype(F32), axis=-1, keepdims=True)
    sc_new = sc_new * scale + jnp.max(bias[:, rows:rows + N_KV_HEADS], axis=-1, keepdims=True)
    m = jnp.maximum(jnp.max(sc, axis=-1, keepdims=True), sc_new)
    p = jnp.exp(sc - m)
    p_new = jnp.exp(sc_new - m)
    denom = jnp.sum(p, axis=-1, keepdims=True) + p_new
    o = jnp.dot(p.astype(BF16), v_il, preferred_element_type=F32)
    o = o + p_new.astype(BF16).astype(F32) * vnew_ref[...].astype(F32)
    o_ref[...] = (o / denom).astype(o_ref.dtype)


def _decode_attention(page_table, q_s, bias_il, knew_s, vnew_s, cache_k, cache_v):
    db, n_pages = page_table.shape
    width2 = bias_il.shape[-1]
    rows = n_pages * N_KV_HEADS * PAGE_SIZE
    seq_spec = lambda a, c: pl.BlockSpec((None, a, c), lambda b, pt: (b, 0, 0))
    return pl.pallas_call(
        _decode_attn_kernel,
        grid_spec=pltpu.PrefetchScalarGridSpec(
            num_scalar_prefetch=1,
            grid=(db,),
            in_specs=[seq_spec(N_HEADS, HEAD_DIM), seq_spec(N_KV_HEADS, width2),
                      seq_spec(N_HEADS, HEAD_DIM), seq_spec(N_HEADS, HEAD_DIM),
                      pl.BlockSpec(memory_space=pl.ANY), pl.BlockSpec(memory_space=pl.ANY)],
            out_specs=seq_spec(N_HEADS, HEAD_DIM),
            scratch_shapes=[pltpu.VMEM((2, rows, HEAD_DIM), F32), pltpu.VMEM((2, rows, HEAD_DIM), F32),
                            pltpu.SemaphoreType.DMA((2, 2))],
        ),
        out_shape=jax.ShapeDtypeStruct((db, N_HEADS, HEAD_DIM), BF16),
        compiler_params=pltpu.CompilerParams(
            dimension_semantics=("arbitrary",), vmem_limit_bytes=VMEM_LIMIT),
        name="decode_attn",
    )(page_table, q_s, bias_il, knew_s, vnew_s, cache_k, cache_v)


POOL_TILE = 512
POOL_HALO = 16


def _pool_prompt_kernel(u_ref, uprev_ref, umeta_ref, d_ref, ubuf):
    i = pl.program_id(0)
    tp = u_ref.shape[0]
    ubuf[0:POOL_HALO, :] = jnp.where(i == 0, umeta_ref[...], uprev_ref[...])
    ubuf[POOL_HALO:, :] = u_ref[...]
    for g, w in enumerate(POOL_WINDOWS):
        cols = slice(g * POOL_GROUP_WIDTH, (g + 1) * POOL_GROUP_WIDTH)
        cur = ubuf[POOL_HALO:POOL_HALO + tp, cols]
        tot = cur
        for j in range(1, w):
            tot = tot + ubuf[POOL_HALO - j:POOL_HALO - j + tp, cols]
        d_ref[:, cols] = tot * (1.0 / w) - cur


def _pool_prompt(u, u_meta):
    rows = u.shape[0]
    tp = POOL_TILE
    per = tp // POOL_HALO
    return pl.pallas_call(
        _pool_prompt_kernel,
        grid=(rows // tp,),
        in_specs=[pl.BlockSpec((tp, POOL_WIDTH), lambda i: (i, 0)),
                  pl.BlockSpec((POOL_HALO, POOL_WIDTH), lambda i: (jnp.maximum(i * per - 1, 0), 0)),
                  _const_spec((POOL_HALO, POOL_WIDTH))],
        out_specs=pl.BlockSpec((tp, POOL_WIDTH), lambda i: (i, 0)),
        out_shape=jax.ShapeDtypeStruct((rows, POOL_WIDTH), F32),
        scratch_shapes=[pltpu.VMEM((tp + POOL_HALO, POOL_WIDTH), F32)],
        compiler_params=pltpu.CompilerParams(dimension_semantics=("arbitrary",)),
        name="pool_prompt",
    )(u, u, u_meta)


def _pool_decode_kernel(hist_ref, u_ref, d_ref):
    for g, w in enumerate(POOL_WINDOWS):
        cols = slice(g * POOL_GROUP_WIDTH, (g + 1) * POOL_GROUP_WIDTH)
        cur = u_ref[:, cols]
        tot = cur
        for j in range(1, w):
            tot = tot + hist_ref[POOL_HIST - j, :, cols]
        d_ref[:, cols] = tot * (1.0 / w) - cur


def _pool_decode(hist, u):
    return pl.pallas_call(
        _pool_decode_kernel,
        out_shape=jax.ShapeDtypeStruct(u.shape, F32),
        name="pool_decode",
    )(hist, u)


FF_TILE = 512


def _rmsnorm(x, g):
    ms = jnp.mean(x * x, axis=-1, keepdims=True)
    return (x * lax.rsqrt(ms + EPS)) * g


def _post_kernel(x_ref, att_ref, d_ref, woa_ref, wop_ref, wpool_ref, pscale_ref,
                 gmlp_ref, gfin_ref, wup_ref, wdn_ref, y_ref, h_ref, hn_ref, acc_ref):
    f = pl.program_id(1)

    @pl.when(f == 0)
    def _():
        d = d_ref[...]
        pooled = jnp.concatenate(
            [jnp.dot(d[:, g * POOL_GROUP_WIDTH:(g + 1) * POOL_GROUP_WIDTH].astype(BF16),
                     wpool_ref[g], preferred_element_type=F32)
             for g in range(len(POOL_WINDOWS))], axis=1) * pscale_ref[...]
        h = (x_ref[...]
             + jnp.dot(att_ref[...], woa_ref[...], preferred_element_type=F32)
             + jnp.dot(pooled.astype(BF16), wop_ref[...], preferred_element_type=F32))
        h_ref[...] = h
        hn_ref[...] = _rmsnorm(h, gmlp_ref[...]).astype(BF16)
        acc_ref[...] = jnp.zeros(acc_ref.shape, F32)

    a = jnp.maximum(jnp.dot(hn_ref[...], wup_ref[...], preferred_element_type=F32), 0.0)
    acc_ref[...] += jnp.dot((a * a).astype(BF16), wdn_ref[...], preferred_element_type=F32)

    @pl.when(f == pl.num_programs(1) - 1)
    def _():
        y_ref[...] = _rmsnorm(h_ref[...] + acc_ref[...], gfin_ref[...])


def _post(x, att, d, wo_a, wo_p, w_pool, pool_scale, g_mlp, g_final, w_up, w_down, tm):
    rows = x.shape[0]
    assert rows % tm == 0 and D_FF % FF_TILE == 0
    row_spec = lambda w: pl.BlockSpec((tm, w), lambda i, f: (i, 0))
    return pl.pallas_call(
        _post_kernel,
        grid=(rows // tm, D_FF // FF_TILE),
        in_specs=[row_spec(D_MODEL), row_spec(ATTN_WIDTH), row_spec(POOL_WIDTH),
                  _const_spec(wo_a.shape), _const_spec(wo_p.shape), _const_spec(w_pool.shape),
                  _const_spec((1, POOL_WIDTH)), _const_spec((1, D_MODEL)), _const_spec((1, D_MODEL)),
                  pl.BlockSpec((D_MODEL, FF_TILE), lambda i, f: (0, f)),
                  pl.BlockSpec((FF_TILE, D_MODEL), lambda i, f: (f, 0))],
        out_specs=row_spec(D_MODEL),
        out_shape=jax.ShapeDtypeStruct((rows, D_MODEL), F32),
        scratch_shapes=[pltpu.VMEM((tm, D_MODEL), F32), pltpu.VMEM((tm, D_MODEL), BF16),
                        pltpu.VMEM((tm, D_MODEL), F32)],
        compiler_params=pltpu.CompilerParams(
            dimension_semantics=("arbitrary", "arbitrary"), vmem_limit_bytes=VMEM_LIMIT),
        name="post",
    )(x, att, d, wo_a, wo_p, w_pool, pool_scale.reshape(1, POOL_WIDTH),
      g_mlp.reshape(1, D_MODEL), g_final.reshape(1, D_MODEL), w_up, w_down)


def kernel(x_prompt, x_sample, cache_k, cache_v, cache_kidx, state_pool, page_table, meta_tokens,
           g_mix, w_in, w_pool, pool_scale, w_out, g_mlp, w_up, w_down, g_final):
    assert x_prompt.shape[0] == 1 and x_sample.shape[1] == 1 and g_mix.shape[0] == 1
    seq = x_prompt.shape[1]
    db = x_sample.shape[0]
    n_pages = page_table.shape[1]
    past = n_pages * PAGE_SIZE
    assert db == LANES

    w = w_in[0]
    o = 0
    parts = []
    for width in (ATTN_WIDTH, 2 * N_KV_HEADS * HEAD_DIM, N_IDX_HEADS * IDX_DIM,
                  IDX_DIM + N_IDX_HEADS, POOL_WIDTH):
        parts.append(w[:, o:o + width].astype(BF16))
        o += width
    parts[3] = jnp.pad(parts[3], ((0, 0), (0, LANES - parts[3].shape[1])))
    wo_a = w_out[0, :ATTN_WIDTH].astype(BF16)
    wo_p = w_out[0, ATTN_WIDTH:].astype(BF16)
    w_pool_b = w_pool[0].astype(BF16)
    w_up_b = w_up[0].astype(BF16)
    w_down_b = w_down[0].astype(BF16)

    xp = x_prompt[0]
    pos_p = N_META + jnp.arange(seq, dtype=jnp.int32)
    q_p, kf_p, vf_p, kb_p, vb_p, qi_p, kw_p, u_p = _project(xp, pos_p, g_mix[0], parts, 256)
    xs = jnp.concatenate([meta_tokens.astype(F32), x_sample[:, 0]], axis=0)
    pos_s = jnp.concatenate([jnp.arange(N_META, dtype=jnp.int32),
                             jnp.full((db,), past, jnp.int32)])
    q_s, kf_s, vf_s, kb_s, vb_s, qi_s, kw_s, u_s = _project(xs, pos_s, g_mix[0], parts, xs.shape[0])

    nk = -(-(seq + 2 * Q_BLOCK) // KEY_CHUNK) * KEY_CHUNK

    def key_rows(meta_part, prompt_part):
        width = meta_part.shape[1]
        return jnp.concatenate(
            [meta_part, jnp.zeros((Q_BLOCK - N_META, width), meta_part.dtype), prompt_part,
             jnp.zeros((nk - Q_BLOCK - seq, width), meta_part.dtype)], axis=0)

    ki_all = key_rows(kw_s[:N_META, :IDX_DIM], kw_p[:, :IDX_DIM]).astype(BF16)
    zk = jnp.zeros_like(ki_all)
    ki_lo = jnp.concatenate([ki_all, zk], axis=1)
    ki_hi = jnp.concatenate([zk, ki_all], axis=1)
    k_all = key_rows(kb_s[:N_META], kb_p)
    v_all = key_rows(vb_s[:N_META], vb_p)
    v_t = jnp.transpose(
        v_all.reshape(nk // KEY_CHUNK, KEY_CHUNK, N_KV_HEADS, HEAD_DIM), (0, 2, 3, 1))
    v_t = jnp.concatenate(
        [v_t, jnp.ones((nk // KEY_CHUNK, N_KV_HEADS, ONES_ROWS, KEY_CHUNK), BF16)], axis=2)
    knorm = lambda kw: kw[:, KNORM_LANE:KNORM_LANE + N_KV_HEADS]
    kmax2 = jnp.maximum(jnp.max(knorm(kw_p), axis=0), jnp.max(knorm(kw_s[:N_META]), axis=0))
    att_p = _prompt_attention(kmax2, qi_p, kw_p, q_p, ki_lo, ki_hi, k_all, v_t)

    width = -(-(past + 1) // KEY_CHUNK) * KEY_CHUNK
    qi_d = qi_s[N_META:].reshape(db, N_IDX_HEADS, IDX_DIM)
    w_d = kw_s[N_META:, IDX_DIM:IDX_DIM + N_IDX_HEADS].reshape(db, N_IDX_HEADS, 1)
    kinew = kw_s[N_META:, :IDX_DIM].astype(BF16).reshape(db, 1, IDX_DIM)
    scores = _decode_scores(page_table, qi_d, w_d, kinew, jnp.swapaxes(cache_kidx, 2, 3), width)
    bias_t = _decode_mask(jnp.transpose(scores.reshape(db, width)))
    bias = jnp.transpose(bias_t)
    negs = jnp.full_like(bias, NEG)
    bias_il = jnp.stack(
        [jnp.stack([bias, negs], axis=-1), jnp.stack([negs, bias], axis=-1)], axis=1,
    ).reshape(db, N_KV_HEADS, N_KV_HEADS * width)
    n_phys = cache_k.shape[1]
    il_shape = (1, n_phys, N_KV_HEADS * PAGE_SIZE, HEAD_DIM)
    per_head = lambda a: jnp.repeat(a.reshape(db, N_KV_HEADS, HEAD_DIM), GROUP, axis=1)
    att_s = _decode_attention(
        page_table, q_s[N_META:].reshape(db, N_HEADS, HEAD_DIM), bias_il,
        per_head(kb_s[N_META:]), per_head(vb_s[N_META:]),
        cache_k.reshape(il_shape), cache_v.reshape(il_shape))
    att_s = att_s.reshape(db, ATTN_WIDTH)

    d_p = _pool_prompt(u_p, u_s[:N_META])
    d_s = _pool_decode(jnp.swapaxes(state_pool[0], 0, 1), u_s[N_META:])

    post = functools.partial(
        _post, wo_a=wo_a, wo_p=wo_p, w_pool=w_pool_b, pool_scale=pool_scale[0],
        g_mlp=g_mlp[0], g_final=g_final, w_up=w_up_b, w_down=w_down_b)
    y_p = post(xp, att_p, d_p, tm=512)
    y_s = post(x_sample[:, 0], att_s, d_s, tm=db)

    k_prompt = jnp.concatenate([kf_s[:N_META], kf_p], axis=0).reshape(1, 1, N_META + seq, N_KV_HEADS, HEAD_DIM)
    v_prompt = jnp.concatenate([vf_s[:N_META], vf_p], axis=0).reshape(1, 1, N_META + seq, N_KV_HEADS, HEAD_DIM)
    kidx_prompt = jnp.concatenate([kw_s[:N_META, :IDX_DIM], kw_p[:, :IDX_DIM]], axis=0).reshape(
        1, 1, N_META + seq, IDX_DIM)
    pool_prompt = u_p[seq - POOL_HIST:].reshape(1, 1, POOL_HIST, POOL_WIDTH)
    k_sample = kf_s[N_META:].reshape(1, db, 1, N_KV_HEADS, HEAD_DIM)
    v_sample = vf_s[N_META:].reshape(1, db, 1, N_KV_HEADS, HEAD_DIM)
    kidx_sample = kw_s[N_META:, :IDX_DIM].reshape(1, db, 1, IDX_DIM)
    pool_sample = jnp.concatenate(
        [state_pool[0][:, 1:], u_s[N_META:][:, None, :]], axis=1)[None]
    return (y_p.reshape(1, seq, D_MODEL), y_s.reshape(db, 1, D_MODEL),
            k_prompt, v_prompt, kidx_prompt, pool_prompt,
            k_sample, v_sample, kidx_sample, pool_sample)
```

```python
import functools

import jax
import jax.numpy as jnp
from jax import lax
from jax.experimental import pallas as pl
from jax.experimental.pallas import tpu as pltpu

F32 = jnp.float32
BF16 = jnp.bfloat16

D_MODEL = 2048
N_META = 16
ATTN_WIDTH = 1024
POOL_WIDTH = 1024
HEAD_DIM = 128
N_HEADS = 8
N_KV_HEADS = 2
GROUP = N_HEADS // N_KV_HEADS
N_IDX_HEADS = 16
IDX_DIM = 64
TOPK = 256
POOL_WINDOWS = (2, 4, 8, 16)
POOL_GROUP_WIDTH = 256
POOL_HIST = 15
D_FF = 8192
PAGE_SIZE = 128
ROPE_THETA = 10000.0
EPS = 1e-6
NEG = -1e30
NEG_INF = float("-inf")
POS_INF = float("inf")
F32_LOWEST = -3.0e38

LANES = 128
SUBLANES = 8
VMEM_LIMIT = 56 * 1024 * 1024

Q_BLOCK = 128
KEY_CHUNK = 512
MAX_BISECT = 320
N_INTERPOLATE_ROUNDS = 6
LOG2E = 1.4426950408889634
MAX_EXP2_GAP = 80.0
NORM_BOUND_SLACK = 1.02
ONES_ROWS = 16
KNORM_LANE = IDX_DIM + N_IDX_HEADS

CONTRACT_LAST = (((1,), (1,)), ((), ()))


def _const_spec(shape):
    nd = len(shape)
    return pl.BlockSpec(shape, lambda *_: (0,) * nd, pipeline_mode=pl.Buffered(1))


def _rope128(x, c, s):
    return x * c + pltpu.roll(x, 64, 1) * s


def _rope64(x, c, sa, sb):
    return x * c + pltpu.roll(x, 96, 1) * sa + pltpu.roll(x, 32, 1) * sb


def _proj_kernel(x_ref, g_ref, c128_ref, s128_ref, c64_ref, sa64_ref, sb64_ref,
                 wq_ref, wkv_ref, wqi_ref, wkw_ref, wu_ref,
                 q_ref, kf_ref, vf_ref, kb_ref, vb_ref, qi_ref, kw_ref, u_ref):
    x = x_ref[...]
    ms = jnp.mean(x * x, axis=-1, keepdims=True)
    xn = ((x * lax.rsqrt(ms + EPS)) * g_ref[...]).astype(BF16)
    c128 = c128_ref[...]
    s128 = s128_ref[...]
    c64 = c64_ref[...]
    sa64 = sa64_ref[...]
    sb64 = sb64_ref[...]

    zq = jnp.dot(xn, wq_ref[...], preferred_element_type=F32)
    for h in range(N_HEADS):
        sl = slice(h * LANES, (h + 1) * LANES)
        q_ref[:, sl] = _rope128(zq[:, sl], c128, s128).astype(BF16)

    zkv = jnp.dot(xn, wkv_ref[...], preferred_element_type=F32)
    k_norm2 = []
    for h in range(N_KV_HEADS):
        sl = slice(h * LANES, (h + 1) * LANES)
        kr = _rope128(zkv[:, sl], c128, s128)
        kf_ref[:, sl] = kr
        kb = kr.astype(BF16)
        kb_ref[:, sl] = kb
        k_norm2.append(jnp.sum(kb.astype(F32) * kb.astype(F32), axis=-1, keepdims=True))
    v = zkv[:, N_KV_HEADS * LANES:]
    vf_ref[...] = v
    vb_ref[...] = v.astype(BF16)

    zqi = jnp.dot(xn, wqi_ref[...], preferred_element_type=F32)
    for p in range(N_IDX_HEADS // 2):
        sl = slice(p * LANES, (p + 1) * LANES)
        qi_ref[:, sl] = _rope64(zqi[:, sl], c64, sa64, sb64).astype(BF16)

    zkw = jnp.dot(xn, wkw_ref[...], preferred_element_type=F32)
    lane = lax.broadcasted_iota(jnp.int32, zkw.shape, 1)
    is_key = lane < IDX_DIM
    ckw = jnp.where(is_key, c64, jnp.where(lane < IDX_DIM + N_IDX_HEADS, N_IDX_HEADS ** -0.5, 1.0))
    kw = _rope64(zkw, ckw, jnp.where(is_key, sa64, 0.0), jnp.where(is_key, sb64, 0.0))
    for h in range(N_KV_HEADS):
        kw = jnp.where(lane == KNORM_LANE + h, k_norm2[h], kw)
    kw_ref[...] = kw

    u_ref[...] = jnp.dot(xn, wu_ref[...], preferred_element_type=F32)


def _project(x, pos, g_mix, w_parts, tm):
    rows = x.shape[0]
    assert rows % tm == 0
    c128, s128, c64, sa64, sb64 = _rope_tables(pos)
    wq, wkv, wqi, wkw, wu = w_parts
    row_spec = lambda w: pl.BlockSpec((tm, w), lambda i: (i, 0))
    out_shapes = (
        jax.ShapeDtypeStruct((rows, ATTN_WIDTH), BF16),
        jax.ShapeDtypeStruct((rows, 256), F32),
        jax.ShapeDtypeStruct((rows, 256), F32),
        jax.ShapeDtypeStruct((rows, 256), BF16),
        jax.ShapeDtypeStruct((rows, 256), BF16),
        jax.ShapeDtypeStruct((rows, 1024), BF16),
        jax.ShapeDtypeStruct((rows, LANES), F32),
        jax.ShapeDtypeStruct((rows, POOL_WIDTH), F32),
    )
    return pl.pallas_call(
        _proj_kernel,
        grid=(rows // tm,),
        in_specs=[row_spec(D_MODEL), _const_spec((1, D_MODEL))]
        + [row_spec(LANES)] * 5
        + [_const_spec(w.shape) for w in w_parts],
        out_specs=tuple(row_spec(s.shape[1]) for s in out_shapes),
        out_shape=out_shapes,
        compiler_params=pltpu.CompilerParams(
            dimension_semantics=("arbitrary",), vmem_limit_bytes=VMEM_LIMIT),
        name="proj",
    )(x, g_mix.reshape(1, D_MODEL), c128, s128, c64, sa64, sb64, wq, wkv, wqi, wkw, wu)


def _rope_tables(pos):
    posf = pos.astype(F32)[:, None]
    inv = ROPE_THETA ** (-jnp.arange(0, HEAD_DIM, 2, dtype=F32) / HEAD_DIM)
    ang = posf * inv[None, :]
    c, s = jnp.cos(ang), jnp.sin(ang)
    c128 = jnp.concatenate([c, c], axis=-1)
    s128 = jnp.concatenate([-s, s], axis=-1)
    inv = ROPE_THETA ** (-jnp.arange(0, IDX_DIM, 2, dtype=F32) / IDX_DIM)
    ang = posf * inv[None, :]
    c, s = jnp.cos(ang), jnp.sin(ang)
    z = jnp.zeros_like(s)
    c64 = jnp.concatenate([c, c, c, c], axis=-1)
    sa64 = jnp.concatenate([-s, z, -s, z], axis=-1)
    sb64 = jnp.concatenate([z, s, z, s], axis=-1)
    return c128, s128, c64, sa64, sb64


def _key_count(st_ref, nchunks, chunk, pred):
    acc_rows = 8 * SUBLANES

    def body(c, part):
        c0 = pl.multiple_of(c * chunk, chunk)
        hit = pred(st_ref[pl.ds(c0, chunk), :], c0).astype(F32)
        return part + jnp.sum(hit.reshape(chunk // acc_rows, acc_rows, LANES), axis=0)

    part = lax.fori_loop(0, nchunks, body, jnp.zeros((acc_rows, LANES), F32))
    return jnp.sum(part, axis=0, keepdims=True)


def _topk_threshold(st_ref, nchunks, chunk, n_valid, s_min, s_max):
    kf = float(TOPK)
    take_all = n_valid <= kf

    def count_ge(t):
        return _key_count(st_ref, nchunks, chunk, lambda v, c0: v >= t)

    def cond(st):
        it, done = st[0], st[5]
        return jnp.logical_and(it < MAX_BISECT, jnp.min(done) < 0.5)

    def largest_below(hi):
        def body(c, part):
            c0 = pl.multiple_of(c * chunk, chunk)
            v = st_ref[pl.ds(c0, chunk), :]
            cand = jnp.where(v < hi, v, NEG_INF)
            return jnp.maximum(
                part, jnp.max(cand.reshape(chunk // acc_rows, acc_rows, LANES), axis=0))

        acc_rows = 8 * SUBLANES
        part = lax.fori_loop(0, nchunks, body, jnp.full((acc_rows, LANES), NEG_INF, F32))
        return jnp.max(part, axis=0, keepdims=True)

    def step(st, mode):
        lo, hi, flo, fhi, done, tau = st
        half = lo + (hi - lo) * 0.5
        collapsed = jnp.logical_or(half <= lo, half >= hi)
        mid = half
        if mode == "interpolate":
            frac = (flo - (kf + 0.5)) / jnp.maximum(flo - fhi, 1.0)
            guess = lo + (hi - lo) * jnp.clip(frac, 1.0 / 32, 31.0 / 32)
            mid = jnp.where(jnp.logical_and(guess > lo, guess < hi), guess, half)
        elif mode == "extract":
            below = largest_below(hi)
            mid = jnp.where(below >= lo, below, half)
        cnt = count_ge(mid)
        found = cnt == kf
        active = done < 0.5
        go_up = jnp.logical_and(active, cnt > kf)
        go_dn = jnp.logical_and(active, cnt < kf)
        lo = jnp.where(go_up, mid, lo)
        hi = jnp.where(go_dn, mid, hi)
        flo = jnp.where(go_up, cnt, flo)
        fhi = jnp.where(go_dn, cnt, fhi)
        tau = jnp.where(jnp.logical_and(active, found), mid, tau)
        done = jnp.where(jnp.logical_or(found, collapsed), 1.0, done)
        return lo, hi, flo, fhi, done, tau

    def body(st):
        inner = lax.cond(st[0] < N_INTERPOLATE_ROUNDS,
                         lambda s: step(s, "interpolate"), lambda s: step(s, "extract"), st[1:])
        return (st[0] + 1,) + step(inner, "halve")

    done0 = jnp.where(take_all, 1.0, 0.0).astype(F32)
    tau0 = jnp.full((1, LANES), POS_INF, F32)
    fhi0 = jnp.ones((1, LANES), F32)
    _, lo, hi, _, _, _, tau = lax.while_loop(
        cond, body, (jnp.int32(0), s_min, s_max, n_valid, fhi0, done0, tau0))

    unresolved = jnp.logical_and(tau == POS_INF, jnp.logical_not(take_all))
    n_unres = jnp.sum(unresolved.astype(F32))
    tau = jnp.where(take_all, F32_LOWEST, tau)

    def tie_path(tau):
        cut = jnp.where(count_ge(hi) >= kf, hi, lo)
        cut = jnp.where(unresolved, cut, POS_INF)
        c_gt = _key_count(st_ref, nchunks, chunk, lambda v, c0: v > cut)
        keep = kf - c_gt

        def key_iota(c0):
            return c0 + lax.broadcasted_iota(jnp.int32, (chunk, LANES), 0)

        def idx_body(_, st):
            xlo, xhi = st
            xmid = (xlo + xhi) // 2
            cnt = _key_count(
                st_ref, nchunks, chunk,
                lambda v, c0: jnp.logical_and(v == cut, key_iota(c0) < xmid))
            ok = cnt >= keep
            return jnp.where(ok, xlo, xmid), jnp.where(ok, xmid, xhi)

        xlo0 = jnp.zeros((1, LANES), jnp.int32)
        xhi0 = jnp.full((1, LANES), nchunks * chunk, jnp.int32)
        n_idx_steps = 15
        _, xcut = lax.fori_loop(0, n_idx_steps, idx_body, (xlo0, xhi0))

        def drop_body(c, carry):
            c0 = pl.multiple_of(c * chunk, chunk)
            v = st_ref[pl.ds(c0, chunk), :]
            drop = jnp.logical_and(v == cut, key_iota(c0) >= xcut)
            st_ref[pl.ds(c0, chunk), :] = jnp.where(drop, NEG_INF, v)
            return carry

        lax.fori_loop(0, nchunks, drop_body, 0)
        return jnp.where(unresolved, cut, tau)

    return lax.cond(n_unres > 0.0, tie_path, lambda t: t, tau)


def _prompt_attn_kernel(kmax2_ref, qi_ref, kw_ref, q_ref, kilo_ref, kihi_ref, k_ref, vt_ref, o_ref,
                        st_ref, m_ref, l_ref, acc_ref, accf_ref):
    i = pl.program_id(0)
    tq, ck = Q_BLOCK, KEY_CHUNK
    n_blocks = i + 2
    nchunks = (n_blocks * Q_BLOCK + ck - 1) // ck

    wt = jnp.transpose(kw_ref[...])[IDX_DIM:IDX_DIM + N_IDX_HEADS, :] * (IDX_DIM ** -0.5)
    qcat = jnp.concatenate(
        [qi_ref[:, p * LANES:(p + 1) * LANES] for p in range(N_IDX_HEADS // 2)], axis=0)
    key = lax.broadcasted_iota(jnp.int32, (ck, tq), 0)
    qry = i * tq + lax.broadcasted_iota(jnp.int32, (ck, tq), 1)

    def idx_chunk(c, st):
        smin, smax = st
        c0 = pl.multiple_of(c * ck, ck)
        klo = kilo_ref[pl.ds(c0, ck), :]
        khi = kihi_ref[pl.ds(c0, ck), :]
        acc = jnp.zeros((ck, tq), F32)
        for t in range(N_IDX_HEADS // 4):
            rhs = qcat[2 * t * tq:(2 * t + 2) * tq]
            dlo = lax.dot_general(klo, rhs, CONTRACT_LAST, preferred_element_type=F32)
            dhi = lax.dot_general(khi, rhs, CONTRACT_LAST, preferred_element_type=F32)
            for j in range(2):
                p = 2 * t + j
                cs = slice(j * tq, (j + 1) * tq)
                acc = acc + wt[2 * p:2 * p + 1, :] * jnp.maximum(dlo[:, cs], 0.0)
                acc = acc + wt[2 * p + 1:2 * p + 2, :] * jnp.maximum(dhi[:, cs], 0.0)
        col = c0 + key
        valid = jnp.logical_or(
            col < N_META,
            jnp.logical_and(col >= Q_BLOCK, col - Q_BLOCK <= qry))
        st_ref[pl.ds(c0, ck), :] = jnp.where(valid, acc, NEG_INF)
        smin = jnp.minimum(smin, jnp.min(jnp.where(valid, acc, POS_INF), axis=0, keepdims=True))
        smax = jnp.maximum(smax, jnp.max(jnp.where(valid, acc, NEG_INF), axis=0, keepdims=True))
        return smin, smax

    st = lax.fori_loop(
        0, nchunks // 2, lambda j, st: idx_chunk(2 * j + 1, idx_chunk(2 * j, st)),
        (jnp.full((1, tq), POS_INF, F32), jnp.full((1, tq), NEG_INF, F32)))
    smin, smax = lax.cond(nchunks % 2 == 1, lambda st: idx_chunk(nchunks - 1, st), lambda st: st, st)

    n_valid = (N_META + 1 + i * tq + lax.broadcasted_iota(jnp.int32, (1, tq), 1)).astype(F32)
    tau = _topk_threshold(st_ref, nchunks, ck, n_valid, smin, smax)

    scale = HEAD_DIM ** -0.5

    def q_group(n):
        return jnp.concatenate(
            [q_ref[:, (n * GROUP + g) * LANES:(n * GROUP + g + 1) * LANES]
             for g in range(GROUP)], axis=0)

    def k_chunk(c0, n):
        return k_ref[pl.ds(c0, ck), n * LANES:(n + 1) * LANES]

    def write_out(n, ot):
        for g in range(GROUP):
            h = n * GROUP + g
            o_ref[:, h * LANES:(h + 1) * LANES] = jnp.transpose(
                ot[:, g * tq:(g + 1) * tq]).astype(o_ref.dtype)

    n_ref_chunks = jnp.minimum(nchunks, 2)

    def ref_body(c, mx):
        c0 = pl.multiple_of(c * ck, ck)
        sel = st_ref[pl.ds(c0, ck), :] >= tau
        out = []
        for n in range(N_KV_HEADS):
            raw = lax.dot_general(k_chunk(c0, n), q_group(n), CONTRACT_LAST,
                                  preferred_element_type=F32)
            cur = jnp.concatenate(
                [jnp.max(jnp.where(sel, raw[:, g * tq:(g + 1) * tq], NEG), axis=0, keepdims=True)
                 for g in range(GROUP)], axis=1)
            out.append(jnp.maximum(mx[n], cur))
        return tuple(out)

    mx = lax.fori_loop(0, n_ref_chunks, ref_body,
                       tuple(jnp.full((1, GROUP * tq), NEG, F32) for _ in range(N_KV_HEADS)))
    c2 = scale * LOG2E
    ones = jnp.ones((SUBLANES, HEAD_DIM), BF16)
    gap = jnp.float32(0.0)
    for n in range(N_KV_HEADS):
        qf = q_group(n).astype(F32)
        qn2 = lax.dot_general(ones, (qf * qf).astype(BF16), CONTRACT_LAST,
                              preferred_element_type=F32)[0:1]
        bound = jnp.sqrt(qn2 * kmax2_ref[n]) * (c2 * NORM_BOUND_SLACK)
        gap = jnp.maximum(gap, jnp.max(bound - mx[n] * c2))

    def fixed_reference_path():
        accf_ref[...] = jnp.zeros(accf_ref.shape, F32)

        def chunk_update(c, n):
            c0 = pl.multiple_of(c * ck, ck)
            sel = st_ref[pl.ds(c0, ck), :] >= tau
            raw = lax.dot_general(k_chunk(c0, n), q_group(n), CONTRACT_LAST,
                                  preferred_element_type=F32)
            neg_ref = mx[n] * (-c2)
            p = jnp.concatenate(
                [jnp.exp2(raw[:, g * tq:(g + 1) * tq] * c2
                          + jnp.where(sel, neg_ref[:, g * tq:(g + 1) * tq], NEG)).astype(BF16)
                 for g in range(GROUP)], axis=1)
            return jnp.dot(vt_ref[c, n], p, preferred_element_type=F32)

        def pair_body(j, carry):
            for n in range(N_KV_HEADS):
                accf_ref[n] += chunk_update(2 * j, n) + chunk_update(2 * j + 1, n)
            return carry

        lax.fori_loop(0, nchunks // 2, pair_body, 0)

        @pl.when(nchunks % 2 == 1)
        def _():
            for n in range(N_KV_HEADS):
                accf_ref[n] += chunk_update(nchunks - 1, n)

        for n in range(N_KV_HEADS):
            write_out(n, accf_ref[n, 0:HEAD_DIM, :] / accf_ref[n, HEAD_DIM:HEAD_DIM + 1, :])

    def running_max_path():
        m_ref[...] = jnp.full(m_ref.shape, NEG, F32)
        l_ref[...] = jnp.zeros(l_ref.shape, F32)
        acc_ref[...] = jnp.zeros(acc_ref.shape, F32)

        def body(c, carry):
            c0 = pl.multiple_of(c * ck, ck)
            bias = jnp.where(st_ref[pl.ds(c0, ck), :] >= tau, 0.0, NEG)
            bias = jnp.concatenate([bias] * GROUP, axis=1)
            for n in range(N_KV_HEADS):
                sc = lax.dot_general(k_chunk(c0, n), q_group(n), CONTRACT_LAST,
                                     preferred_element_type=F32)
                sc = sc * scale + bias
                m_old = m_ref[n]
                m_new = jnp.maximum(m_old, jnp.max(sc, axis=0, keepdims=True))
                alpha = jnp.exp(m_old - m_new)
                p = jnp.exp(sc - m_new)
                l_ref[n] = alpha * l_ref[n] + jnp.sum(
                    p.reshape(ck // SUBLANES, SUBLANES, GROUP * tq), axis=0)
                acc_ref[n] = alpha * acc_ref[n] + jnp.dot(
                    vt_ref[c, n, 0:HEAD_DIM, :], p.astype(BF16), preferred_element_type=F32)
                m_ref[n] = m_new
            return carry

        lax.fori_loop(0, nchunks, body, 0)
        for n in range(N_KV_HEADS):
            write_out(n, acc_ref[n] / jnp.sum(l_ref[n], axis=0, keepdims=True))

    lax.cond(gap <= MAX_EXP2_GAP, fixed_reference_path, running_max_path)


def _prompt_attention(kmax2, qi, kw, q, ki_lo, ki_hi, k_all, v_t):
    rows = q.shape[0]
    nk = k_all.shape[0]
    assert rows % Q_BLOCK == 0 and nk % KEY_CHUNK == 0
    assert nk >= (rows // Q_BLOCK + 1) * Q_BLOCK
    assert v_t.shape == (nk // KEY_CHUNK, N_KV_HEADS, HEAD_DIM + ONES_ROWS, KEY_CHUNK)
    row_spec = lambda w: pl.BlockSpec((Q_BLOCK, w), lambda i: (i, 0))
    return pl.pallas_call(
        _prompt_attn_kernel,
        grid=(rows // Q_BLOCK,),
        in_specs=[pl.BlockSpec(memory_space=pltpu.SMEM),
                  row_spec(1024), row_spec(LANES), row_spec(ATTN_WIDTH),
                  _const_spec(ki_lo.shape), _const_spec(ki_hi.shape),
                  _const_spec(k_all.shape), _const_spec(v_t.shape)],
        out_specs=row_spec(ATTN_WIDTH),
        out_shape=jax.ShapeDtypeStruct((rows, ATTN_WIDTH), BF16),
        scratch_shapes=[
            pltpu.VMEM((nk, Q_BLOCK), F32),
            pltpu.VMEM((N_KV_HEADS, 1, GROUP * Q_BLOCK), F32),
            pltpu.VMEM((N_KV_HEADS, SUBLANES, GROUP * Q_BLOCK), F32),
            pltpu.VMEM((N_KV_HEADS, HEAD_DIM, GROUP * Q_BLOCK), F32),
            pltpu.VMEM((N_KV_HEADS, HEAD_DIM + ONES_ROWS, GROUP * Q_BLOCK), F32),
        ],
        compiler_params=pltpu.CompilerParams(
            dimension_semantics=("arbitrary",), vmem_limit_bytes=VMEM_LIMIT),
        name="prompt_attn",
    )(kmax2, qi, kw, q, ki_lo, ki_hi, k_all, v_t)


SCORE_PAGE_SLOTS = 4
KV_PAGE_SLOTS = 3


def _prefetch_pages(copies, n_pages, n_slots, b, n_seq):
    depth = n_slots - 1

    def start(seq):
        for j in range(n_pages):
            for cp in copies(seq, seq % n_slots, j):
                cp.start()

    @pl.when(b == 0)
    def _():
        for s in range(depth):
            start(s)

    @pl.when(b + depth < n_seq)
    def _():
        start(b + depth)

    for j in range(n_pages):
        for cp in copies(b, b % n_slots, j):
            cp.wait()
    return b % n_slots


def _decode_scores_kernel(pt_ref, qi_ref, w_ref, kinew_ref, cache_ref, s_ref, buf, sem):
    b = pl.program_id(0)
    n_pages = pt_ref.shape[1]

    def copies(seq, slot, j):
        return (pltpu.make_async_copy(
            cache_ref.at[0, pt_ref[seq, j]], buf.at[slot, j], sem.at[slot]),)

    slot = _prefetch_pages(copies, n_pages, buf.shape[0], b, pl.num_programs(0))

    qi = qi_ref[...]
    w = w_ref[...] * (IDX_DIM ** -0.5)
    for j in range(n_pages):
        kpt = buf[slot, j].astype(BF16)
        d = jnp.dot(qi, kpt, preferred_element_type=F32)
        s_ref[:, j * PAGE_SIZE:(j + 1) * PAGE_SIZE] = jnp.sum(
            jnp.maximum(d, 0.0) * w, axis=0, keepdims=True)
    d_new = jnp.sum(qi.astype(F32) * kinew_ref[...].astype(F32), axis=-1, keepdims=True)
    s_new = jnp.sum(jnp.maximum(d_new, 0.0) * w, axis=0, keepdims=True)
    past = n_pages * PAGE_SIZE
    tail = s_ref.shape[1] - past
    lane = lax.broadcasted_iota(jnp.int32, (1, tail), 1)
    s_ref[:, past:] = jnp.where(lane == 0, s_new, NEG_INF)


def _decode_scores(page_table, qi_s, w_s, kinew_s, cache_kidx_t, width):
    db, n_pages = page_table.shape
    seq_spec = lambda a, c: pl.BlockSpec((None, a, c), lambda b, pt: (b, 0, 0))
    return pl.pallas_call(
        _decode_scores_kernel,
        grid_spec=pltpu.PrefetchScalarGridSpec(
            num_scalar_prefetch=1,
            grid=(db,),
            in_specs=[seq_spec(N_IDX_HEADS, IDX_DIM), seq_spec(N_IDX_HEADS, 1),
                      seq_spec(1, IDX_DIM), pl.BlockSpec(memory_space=pl.ANY)],
            out_specs=seq_spec(1, width),
            scratch_shapes=[pltpu.VMEM((SCORE_PAGE_SLOTS, n_pages, IDX_DIM, PAGE_SIZE), F32),
                            pltpu.SemaphoreType.DMA((SCORE_PAGE_SLOTS,))],
        ),
        out_shape=jax.ShapeDtypeStruct((db, 1, width), F32),
        compiler_params=pltpu.CompilerParams(dimension_semantics=("arbitrary",)),
        name="decode_scores",
    )(page_table, qi_s, w_s, kinew_s, cache_kidx_t)


def _decode_mask_kernel(s_in_ref, bias_ref, st_ref):
    nk = s_in_ref.shape[0]
    chunk = KEY_CHUNK
    s = s_in_ref[...]
    st_ref[...] = s
    valid = s > NEG_INF
    smin = jnp.min(jnp.where(valid, s, POS_INF), axis=0, keepdims=True)
    smax = jnp.max(s, axis=0, keepdims=True)
    n_valid = jnp.sum(valid.astype(F32), axis=0, keepdims=True)
    tau = _topk_threshold(st_ref, nk // chunk, chunk, n_valid, smin, smax)
    bias_ref[...] = jnp.where(st_ref[...] >= tau, 0.0, NEG)


def _decode_mask(scores_t):
    assert scores_t.shape[0] % KEY_CHUNK == 0 and scores_t.shape[1] == LANES
    return pl.pallas_call(
        _decode_mask_kernel,
        out_shape=jax.ShapeDtypeStruct(scores_t.shape, F32),
        scratch_shapes=[pltpu.VMEM(scores_t.shape, F32)],
        name="decode_mask",
    )(scores_t)


def _decode_attn_kernel(pt_ref, q_ref, bias_ref, knew_ref, vnew_ref, ck_ref, cv_ref, o_ref,
                        kbuf, vbuf, sem):
    b = pl.program_id(0)
    n_pages = pt_ref.shape[1]
    page_rows = N_KV_HEADS * PAGE_SIZE

    def copies(seq, slot, j):
        rows_j = pl.ds(j * page_rows, page_rows)
        page = pt_ref[seq, j]
        return (pltpu.make_async_copy(ck_ref.at[0, page], kbuf.at[slot, rows_j], sem.at[slot, 0]),
                pltpu.make_async_copy(cv_ref.at[0, page], vbuf.at[slot, rows_j], sem.at[slot, 1]))

    slot = _prefetch_pages(copies, n_pages, kbuf.shape[0], b, pl.num_programs(0))

    rows = n_pages * page_rows
    scale = HEAD_DIM ** -0.5
    q = q_ref[...]
    bias = jnp.concatenate(
        [jnp.broadcast_to(bias_ref[n:n + 1, :], (GROUP, bias_ref.shape[1]))
         for n in range(N_KV_HEADS)], axis=0)
    k_il = kbuf[slot].astype(BF16)
    v_il = vbuf[slot].astype(BF16)
    sc = lax.dot_general(q, k_il, CONTRACT_LAST, preferred_element_type=F32)
    sc = sc * scale + bias[:, :rows]
    sc_new = jnp.sum(q.astype(F32) * knew_ref[...].astype(F32), axis=-1, keepdims=True)
    sc_new = sc_new * scale + jnp.max(bias[:, rows:rows + N_KV_HEADS], axis=-1, keepdims=True)
    m = jnp.maximum(jnp.max(sc, axis=-1, keepdims=True), sc_new)
    p = jnp.exp(sc - m)
    p_new = jnp.exp(sc_new - m)
    denom = jnp.sum(p, axis=-1, keepdims=True) + p_new
    o = jnp.dot(p.astype(BF16), v_il, preferred_element_type=F32)
    o = o + p_new.astype(BF16).astype(F32) * vnew_ref[...].astype(F32)
    o_ref[...] = (o / denom).astype(o_ref.dtype)


def _decode_attention(page_table, q_s, bias_il, knew_s, vnew_s, cache_k, cache_v):
    db, n_pages = page_table.shape
    width2 = bias_il.shape[-1]
    rows = n_pages * N_KV_HEADS * PAGE_SIZE
    seq_spec = lambda a, c: pl.BlockSpec((None, a, c), lambda b, pt: (b, 0, 0))
    return pl.pallas_call(
        _decode_attn_kernel,
        grid_spec=pltpu.PrefetchScalarGridSpec(
            num_scalar_prefetch=1,
            grid=(db,),
            in_specs=[seq_spec(N_HEADS, HEAD_DIM), seq_spec(N_KV_HEADS, width2),
                      seq_spec(N_HEADS, HEAD_DIM), seq_spec(N_HEADS, HEAD_DIM),
                      pl.BlockSpec(memory_space=pl.ANY), pl.BlockSpec(memory_space=pl.ANY)],
            out_specs=seq_spec(N_HEADS, HEAD_DIM),
            scratch_shapes=[pltpu.VMEM((KV_PAGE_SLOTS, rows, HEAD_DIM), F32),
                            pltpu.VMEM((KV_PAGE_SLOTS, rows, HEAD_DIM), F32),
                            pltpu.SemaphoreType.DMA((KV_PAGE_SLOTS, 2))],
        ),
        out_shape=jax.ShapeDtypeStruct((db, N_HEADS, HEAD_DIM), BF16),
        compiler_params=pltpu.CompilerParams(
            dimension_semantics=("arbitrary",), vmem_limit_bytes=VMEM_LIMIT),
        name="decode_attn",
    )(page_table, q_s, bias_il, knew_s, vnew_s, cache_k, cache_v)


POOL_TILE = 512
POOL_HALO = 16


def _pool_prompt_kernel(u_ref, uprev_ref, umeta_ref, d_ref, ubuf):
    i = pl.program_id(0)
    tp = u_ref.shape[0]
    ubuf[0:POOL_HALO, :] = jnp.where(i == 0, umeta_ref[...], uprev_ref[...])
    ubuf[POOL_HALO:, :] = u_ref[...]
    for g, w in enumerate(POOL_WINDOWS):
        cols = slice(g * POOL_GROUP_WIDTH, (g + 1) * POOL_GROUP_WIDTH)
        cur = ubuf[POOL_HALO:POOL_HALO + tp, cols]
        tot = cur
        for j in range(1, w):
            tot = tot + ubuf[POOL_HALO - j:POOL_HALO - j + tp, cols]
        d_ref[:, cols] = tot * (1.0 / w) - cur


def _pool_prompt(u, u_meta):
    rows = u.shape[0]
    tp = POOL_TILE
    per = tp // POOL_HALO
    return pl.pallas_call(
        _pool_prompt_kernel,
        grid=(rows // tp,),
        in_specs=[pl.BlockSpec((tp, POOL_WIDTH), lambda i: (i, 0)),
                  pl.BlockSpec((POOL_HALO, POOL_WIDTH), lambda i: (jnp.maximum(i * per - 1, 0), 0)),
                  _const_spec((POOL_HALO, POOL_WIDTH))],
        out_specs=pl.BlockSpec((tp, POOL_WIDTH), lambda i: (i, 0)),
        out_shape=jax.ShapeDtypeStruct((rows, POOL_WIDTH), F32),
        scratch_shapes=[pltpu.VMEM((tp + POOL_HALO, POOL_WIDTH), F32)],
        compiler_params=pltpu.CompilerParams(dimension_semantics=("arbitrary",)),
        name="pool_prompt",
    )(u, u, u_meta)


def _pool_decode_kernel(hist_ref, u_ref, d_ref):
    for g, w in enumerate(POOL_WINDOWS):
        cols = slice(g * POOL_GROUP_WIDTH, (g + 1) * POOL_GROUP_WIDTH)
        cur = u_ref[:, cols]
        tot = cur
        for j in range(1, w):
            tot = tot + hist_ref[POOL_HIST - j, :, cols]
        d_ref[:, cols] = tot * (1.0 / w) - cur


def _pool_decode(hist, u):
    return pl.pallas_call(
        _pool_decode_kernel,
        out_shape=jax.ShapeDtypeStruct(u.shape, F32),
        name="pool_decode",
    )(hist, u)


FF_TILE = 1024


def _rmsnorm(x, g):
    ms = jnp.mean(x * x, axis=-1, keepdims=True)
    return (x * lax.rsqrt(ms + EPS)) * g


def _post_kernel(x_ref, att_ref, d_ref, woa_ref, wop_ref, wpool_ref, pscale_ref,
                 gmlp_ref, gfin_ref, wup_ref, wdn_ref, y_ref, hn_ref):
    f = pl.program_id(1)

    @pl.when(f == 0)
    def _():
        d = d_ref[...]
        pooled = jnp.concatenate(
            [jnp.dot(d[:, g * POOL_GROUP_WIDTH:(g + 1) * POOL_GROUP_WIDTH].astype(BF16),
                     wpool_ref[g], preferred_element_type=F32)
             for g in range(len(POOL_WINDOWS))], axis=1) * pscale_ref[...]
        h = (x_ref[...]
             + jnp.dot(att_ref[...], woa_ref[...], preferred_element_type=F32)
             + jnp.dot(pooled.astype(BF16), wop_ref[...], preferred_element_type=F32))
        y_ref[...] = h
        hn_ref[...] = _rmsnorm(h, gmlp_ref[...]).astype(BF16)

    a = jnp.maximum(jnp.dot(hn_ref[...], wup_ref[...], preferred_element_type=F32), 0.0)
    y_ref[...] += jnp.dot((a * a).astype(BF16), wdn_ref[...], preferred_element_type=F32)

    @pl.when(f == pl.num_programs(1) - 1)
    def _():
        y_ref[...] = _rmsnorm(y_ref[...], gfin_ref[...])


def _post(x, att, d, wo_a, wo_p, w_pool, pool_scale, g_mlp, g_final, w_up, w_down, tm):
    rows = x.shape[0]
    assert rows % tm == 0 and D_FF % FF_TILE == 0
    row_spec = lambda w: pl.BlockSpec((tm, w), lambda i, f: (i, 0))
    return pl.pallas_call(
        _post_kernel,
        grid=(rows // tm, D_FF // FF_TILE),
        in_specs=[row_spec(D_MODEL), row_spec(ATTN_WIDTH), row_spec(POOL_WIDTH),
                  _const_spec(wo_a.shape), _const_spec(wo_p.shape), _const_spec(w_pool.shape),
                  _const_spec((1, POOL_WIDTH)), _const_spec((1, D_MODEL)), _const_spec((1, D_MODEL)),
                  pl.BlockSpec((D_MODEL, FF_TILE), lambda i, f: (0, f)),
                  pl.BlockSpec((FF_TILE, D_MODEL), lambda i, f: (f, 0))],
        out_specs=row_spec(D_MODEL),
        out_shape=jax.ShapeDtypeStruct((rows, D_MODEL), F32),
        scratch_shapes=[pltpu.VMEM((tm, D_MODEL), BF16)],
        compiler_params=pltpu.CompilerParams(
            dimension_semantics=("arbitrary", "arbitrary"), vmem_limit_bytes=VMEM_LIMIT),
        name="post",
    )(x, att, d, wo_a, wo_p, w_pool, pool_scale.reshape(1, POOL_WIDTH),
      g_mlp.reshape(1, D_MODEL), g_final.reshape(1, D_MODEL), w_up, w_down)


def kernel(x_prompt, x_sample, cache_k, cache_v, cache_kidx, state_pool, page_table, meta_tokens,
           g_mix, w_in, w_pool, pool_scale, w_out, g_mlp, w_up, w_down, g_final):
    assert x_prompt.shape[0] == 1 and x_sample.shape[1] == 1 and g_mix.shape[0] == 1
    seq = x_prompt.shape[1]
    db = x_sample.shape[0]
    n_pages = page_table.shape[1]
    past = n_pages * PAGE_SIZE
    assert db == LANES

    w = w_in[0]
    o = 0
    parts = []
    for width in (ATTN_WIDTH, 2 * N_KV_HEADS * HEAD_DIM, N_IDX_HEADS * IDX_DIM,
                  IDX_DIM + N_IDX_HEADS, POOL_WIDTH):
        parts.append(w[:, o:o + width].astype(BF16))
        o += width
    parts[3] = jnp.pad(parts[3], ((0, 0), (0, LANES - parts[3].shape[1])))
    wo_a = w_out[0, :ATTN_WIDTH].astype(BF16)
    wo_p = w_out[0, ATTN_WIDTH:].astype(BF16)
    w_pool_b = w_pool[0].astype(BF16)
    w_up_b = w_up[0].astype(BF16)
    w_down_b = w_down[0].astype(BF16)

    xp = x_prompt[0]
    pos_p = N_META + jnp.arange(seq, dtype=jnp.int32)
    q_p, kf_p, vf_p, kb_p, vb_p, qi_p, kw_p, u_p = _project(xp, pos_p, g_mix[0], parts, 256)
    xs = jnp.concatenate([meta_tokens.astype(F32), x_sample[:, 0]], axis=0)
    pos_s = jnp.concatenate([jnp.arange(N_META, dtype=jnp.int32),
                             jnp.full((db,), past, jnp.int32)])
    q_s, kf_s, vf_s, kb_s, vb_s, qi_s, kw_s, u_s = _project(xs, pos_s, g_mix[0], parts, xs.shape[0])

    nk = -(-(seq + 2 * Q_BLOCK) // KEY_CHUNK) * KEY_CHUNK

    def key_rows(meta_part, prompt_part):
        width = meta_part.shape[1]
        return jnp.concatenate(
            [meta_part, jnp.zeros((Q_BLOCK - N_META, width), meta_part.dtype), prompt_part,
             jnp.zeros((nk - Q_BLOCK - seq, width), meta_part.dtype)], axis=0)

    ki_all = key_rows(kw_s[:N_META, :IDX_DIM], kw_p[:, :IDX_DIM]).astype(BF16)
    zk = jnp.zeros_like(ki_all)
    ki_lo = jnp.concatenate([ki_all, zk], axis=1)
    ki_hi = jnp.concatenate([zk, ki_all], axis=1)
    k_all = key_rows(kb_s[:N_META], kb_p)
    v_all = key_rows(vb_s[:N_META], vb_p)
    v_t = jnp.transpose(
        v_all.reshape(nk // KEY_CHUNK, KEY_CHUNK, N_KV_HEADS, HEAD_DIM), (0, 2, 3, 1))
    v_t = jnp.concatenate(
        [v_t, jnp.ones((nk // KEY_CHUNK, N_KV_HEADS, ONES_ROWS, KEY_CHUNK), BF16)], axis=2)
    knorm = lambda kw: kw[:, KNORM_LANE:KNORM_LANE + N_KV_HEADS]
    kmax2 = jnp.maximum(jnp.max(knorm(kw_p), axis=0), jnp.max(knorm(kw_s[:N_META]), axis=0))
    att_p = _prompt_attention(kmax2, qi_p, kw_p, q_p, ki_lo, ki_hi, k_all, v_t)

    width = -(-(past + 1) // KEY_CHUNK) * KEY_CHUNK
    qi_d = qi_s[N_META:].reshape(db, N_IDX_HEADS, IDX_DIM)
    w_d = kw_s[N_META:, IDX_DIM:IDX_DIM + N_IDX_HEADS].reshape(db, N_IDX_HEADS, 1)
    kinew = kw_s[N_META:, :IDX_DIM].astype(BF16).reshape(db, 1, IDX_DIM)
    scores = _decode_scores(page_table, qi_d, w_d, kinew, jnp.swapaxes(cache_kidx, 2, 3), width)
    bias_t = _decode_mask(jnp.transpose(scores.reshape(db, width)))
    bias = jnp.transpose(bias_t)
    negs = jnp.full_like(bias, NEG)
    bias_il = jnp.stack(
        [jnp.stack([bias, negs], axis=-1), jnp.stack([negs, bias], axis=-1)], axis=1,
    ).reshape(db, N_KV_HEADS, N_KV_HEADS * width)
    n_phys = cache_k.shape[1]
    il_shape = (1, n_phys, N_KV_HEADS * PAGE_SIZE, HEAD_DIM)
    per_head = lambda a: jnp.repeat(a.reshape(db, N_KV_HEADS, HEAD_DIM), GROUP, axis=1)
    att_s = _decode_attention(
        page_table, q_s[N_META:].reshape(db, N_HEADS, HEAD_DIM), bias_il,
        per_head(kb_s[N_META:]), per_head(vb_s[N_META:]),
        cache_k.reshape(il_shape), cache_v.reshape(il_shape))
    att_s = att_s.reshape(db, ATTN_WIDTH)

    d_p = _pool_prompt(u_p, u_s[:N_META])
    d_s = _pool_decode(jnp.swapaxes(state_pool[0], 0, 1), u_s[N_META:])

    post = functools.partial(
        _post, wo_a=wo_a, wo_p=wo_p, w_pool=w_pool_b, pool_scale=pool_scale[0],
        g_mlp=g_mlp[0], g_final=g_final, w_up=w_up_b, w_down=w_down_b)
    y_p = post(xp, att_p, d_p, tm=512)
    y_s = post(x_sample[:, 0], att_s, d_s, tm=db)

    k_prompt = jnp.concatenate([kf_s[:N_META], kf_p], axis=0).reshape(1, 1, N_META + seq, N_KV_HEADS, HEAD_DIM)
    v_prompt = jnp.concatenate([vf_s[:N_META], vf_p], axis=0).reshape(1, 1, N_META + seq, N_KV_HEADS, HEAD_DIM)
    kidx_prompt = jnp.concatenate([kw_s[:N_META, :IDX_DIM], kw_p[:, :IDX_DIM]], axis=0).reshape(
        1, 1, N_META + seq, IDX_DIM)
    pool_prompt = u_p[seq - POOL_HIST:].reshape(1, 1, POOL_HIST, POOL_WIDTH)
    k_sample = kf_s[N_META:].reshape(1, db, 1, N_KV_HEADS, HEAD_DIM)
    v_sample = vf_s[N_META:].reshape(1, db, 1, N_KV_HEADS, HEAD_DIM)
    kidx_sample = kw_s[N_META:, :IDX_DIM].reshape(1, db, 1, IDX_DIM)
    pool_sample = jnp.concatenate(
        [state_pool[0][:, 1:], u_s[N_META:][:, None, :]], axis=1)[None]
    return (y_p.reshape(1, seq, D_MODEL), y_s.reshape(db, 1, D_MODEL),
            k_prompt, v_prompt, kidx_prompt, pool_prompt,
            k_sample, v_sample, kidx_sample, pool_sample)
```

```python
import functools

import jax
import jax.numpy as jnp
from jax import lax
from jax.experimental import pallas as pl
from jax.experimental.pallas import tpu as pltpu

F32 = jnp.float32
BF16 = jnp.bfloat16

D_MODEL = 2048
N_META = 16
ATTN_WIDTH = 1024
POOL_WIDTH = 1024
HEAD_DIM = 128
N_HEADS = 8
N_KV_HEADS = 2
GROUP = N_HEADS // N_KV_HEADS
N_IDX_HEADS = 16
IDX_DIM = 64
TOPK = 256
POOL_WINDOWS = (2, 4, 8, 16)
POOL_GROUP_WIDTH = 256
POOL_HIST = 15
D_FF = 8192
PAGE_SIZE = 128
ROPE_THETA = 10000.0
EPS = 1e-6
NEG = -1e30
NEG_INF = float("-inf")
POS_INF = float("inf")
F32_LOWEST = -3.0e38

LANES = 128
SUBLANES = 8
VMEM_LIMIT = 56 * 1024 * 1024

Q_BLOCK = 128
KEY_CHUNK = 512
MAX_BISECT = 320
LOG2E = 1.4426950408889634
MAX_EXP2_GAP = 80.0
NORM_BOUND_SLACK = 1.02
ONES_ROWS = 16
KNORM_LANE = IDX_DIM + N_IDX_HEADS

CONTRACT_LAST = (((1,), (1,)), ((), ()))


def _const_spec(shape):
    nd = len(shape)
    return pl.BlockSpec(shape, lambda *_: (0,) * nd, pipeline_mode=pl.Buffered(1))


def _rope128(x, c, s):
    return x * c + pltpu.roll(x, 64, 1) * s


def _rope64(x, c, sa, sb):
    return x * c + pltpu.roll(x, 96, 1) * sa + pltpu.roll(x, 32, 1) * sb


def _proj_kernel(x_ref, g_ref, c128_ref, s128_ref, c64_ref, sa64_ref, sb64_ref,
                 wq_ref, wkv_ref, wqi_ref, wkw_ref, wu_ref,
                 q_ref, kf_ref, vf_ref, kb_ref, vb_ref, qi_ref, kw_ref, u_ref):
    x = x_ref[...]
    ms = jnp.mean(x * x, axis=-1, keepdims=True)
    xn = ((x * lax.rsqrt(ms + EPS)) * g_ref[...]).astype(BF16)
    c128 = c128_ref[...]
    s128 = s128_ref[...]
    c64 = c64_ref[...]
    sa64 = sa64_ref[...]
    sb64 = sb64_ref[...]

    zq = jnp.dot(xn, wq_ref[...], preferred_element_type=F32)
    for h in range(N_HEADS):
        sl = slice(h * LANES, (h + 1) * LANES)
        q_ref[:, sl] = _rope128(zq[:, sl], c128, s128).astype(BF16)

    zkv = jnp.dot(xn, wkv_ref[...], preferred_element_type=F32)
    k_norm2 = []
    for h in range(N_KV_HEADS):
        sl = slice(h * LANES, (h + 1) * LANES)
        kr = _rope128(zkv[:, sl], c128, s128)
        kf_ref[:, sl] = kr
        kb = kr.astype(BF16)
        kb_ref[:, sl] = kb
        k_norm2.append(jnp.sum(kb.astype(F32) * kb.astype(F32), axis=-1, keepdims=True))
    v = zkv[:, N_KV_HEADS * LANES:]
    vf_ref[...] = v
    vb_ref[...] = v.astype(BF16)

    zqi = jnp.dot(xn, wqi_ref[...], preferred_element_type=F32)
    for p in range(N_IDX_HEADS // 2):
        sl = slice(p * LANES, (p + 1) * LANES)
        qi_ref[:, sl] = _rope64(zqi[:, sl], c64, sa64, sb64).astype(BF16)

    zkw = jnp.dot(xn, wkw_ref[...], preferred_element_type=F32)
    lane = lax.broadcasted_iota(jnp.int32, zkw.shape, 1)
    is_key = lane < IDX_DIM
    ckw = jnp.where(is_key, c64, jnp.where(lane < IDX_DIM + N_IDX_HEADS, N_IDX_HEADS ** -0.5, 1.0))
    kw = _rope64(zkw, ckw, jnp.where(is_key, sa64, 0.0), jnp.where(is_key, sb64, 0.0))
    for h in range(N_KV_HEADS):
        kw = jnp.where(lane == KNORM_LANE + h, k_norm2[h], kw)
    kw_ref[...] = kw

    u_ref[...] = jnp.dot(xn, wu_ref[...], preferred_element_type=F32)


def _project(x, pos, g_mix, w_parts, tm):
    rows = x.shape[0]
    assert rows % tm == 0
    c128, s128, c64, sa64, sb64 = _rope_tables(pos)
    wq, wkv, wqi, wkw, wu = w_parts
    row_spec = lambda w: pl.BlockSpec((tm, w), lambda i: (i, 0))
    out_shapes = (
        jax.ShapeDtypeStruct((rows, ATTN_WIDTH), BF16),
        jax.ShapeDtypeStruct((rows, 256), F32),
        jax.ShapeDtypeStruct((rows, 256), F32),
        jax.ShapeDtypeStruct((rows, 256), BF16),
        jax.ShapeDtypeStruct((rows, 256), BF16),
        jax.ShapeDtypeStruct((rows, 1024), BF16),
        jax.ShapeDtypeStruct((rows, LANES), F32),
        jax.ShapeDtypeStruct((rows, POOL_WIDTH), F32),
    )
    return pl.pallas_call(
        _proj_kernel,
        grid=(rows // tm,),
        in_specs=[row_spec(D_MODEL), _const_spec((1, D_MODEL))]
        + [row_spec(LANES)] * 5
        + [_const_spec(w.shape) for w in w_parts],
        out_specs=tuple(row_spec(s.shape[1]) for s in out_shapes),
        out_shape=out_shapes,
        compiler_params=pltpu.CompilerParams(
            dimension_semantics=("arbitrary",), vmem_limit_bytes=VMEM_LIMIT),
        name="proj",
    )(x, g_mix.reshape(1, D_MODEL), c128, s128, c64, sa64, sb64, wq, wkv, wqi, wkw, wu)


def _rope_tables(pos):
    posf = pos.astype(F32)[:, None]
    inv = ROPE_THETA ** (-jnp.arange(0, HEAD_DIM, 2, dtype=F32) / HEAD_DIM)
    ang = posf * inv[None, :]
    c, s = jnp.cos(ang), jnp.sin(ang)
    c128 = jnp.concatenate([c, c], axis=-1)
    s128 = jnp.concatenate([-s, s], axis=-1)
    inv = ROPE_THETA ** (-jnp.arange(0, IDX_DIM, 2, dtype=F32) / IDX_DIM)
    ang = posf * inv[None, :]
    c, s = jnp.cos(ang), jnp.sin(ang)
    z = jnp.zeros_like(s)
    c64 = jnp.concatenate([c, c, c, c], axis=-1)
    sa64 = jnp.concatenate([-s, z, -s, z], axis=-1)
    sb64 = jnp.concatenate([z, s, z, s], axis=-1)
    return c128, s128, c64, sa64, sb64


def _key_count(st_ref, nchunks, chunk, pred):
    acc_rows = 8 * SUBLANES

    def body(c, part):
        c0 = pl.multiple_of(c * chunk, chunk)
        hit = pred(st_ref[pl.ds(c0, chunk), :], c0)
        for g in range(chunk // acc_rows):
            part = jnp.where(hit[g * acc_rows:(g + 1) * acc_rows], part + 1.0, part)
        return part

    part = lax.fori_loop(0, nchunks, body, jnp.zeros((acc_rows, LANES), F32))
    return jnp.sum(part, axis=0, keepdims=True)


def _topk_threshold(st_ref, nchunks, chunk, n_valid, s_min, s_max):
    kf = float(TOPK)
    take_all = n_valid <= kf

    def count_ge(t):
        return _key_count(st_ref, nchunks, chunk, lambda v, c0: v >= t)

    def cond(st):
        it, done = st[0], st[3]
        return jnp.logical_and(it < MAX_BISECT, jnp.min(done) < 0.5)

    def step(st):
        lo, hi, done, tau = st
        mid = lo + (hi - lo) * 0.5
        collapsed = jnp.logical_or(mid <= lo, mid >= hi)
        cnt = count_ge(mid)
        found = cnt == kf
        active = done < 0.5
        lo = jnp.where(jnp.logical_and(active, cnt > kf), mid, lo)
        hi = jnp.where(jnp.logical_and(active, cnt < kf), mid, hi)
        tau = jnp.where(jnp.logical_and(active, found), mid, tau)
        done = jnp.where(jnp.logical_or(found, collapsed), 1.0, done)
        return lo, hi, done, tau

    def body(st):
        return (st[0] + 1,) + step(step(st[1:]))

    done0 = jnp.where(take_all, 1.0, 0.0).astype(F32)
    tau0 = jnp.full((1, LANES), POS_INF, F32)
    _, lo, hi, _, tau = lax.while_loop(
        cond, body, (jnp.int32(0), s_min, s_max, done0, tau0))

    unresolved = jnp.logical_and(tau == POS_INF, jnp.logical_not(take_all))
    n_unres = jnp.sum(unresolved.astype(F32))
    tau = jnp.where(take_all, F32_LOWEST, tau)

    def tie_path(tau):
        cut = jnp.where(count_ge(hi) >= kf, hi, lo)
        cut = jnp.where(unresolved, cut, POS_INF)
        c_gt = _key_count(st_ref, nchunks, chunk, lambda v, c0: v > cut)
        keep = kf - c_gt

        def key_iota(c0):
            return c0 + lax.broadcasted_iota(jnp.int32, (chunk, LANES), 0)

        def idx_body(_, st):
            xlo, xhi = st
            xmid = (xlo + xhi) // 2
            cnt = _key_count(
                st_ref, nchunks, chunk,
                lambda v, c0: jnp.logical_and(v == cut, key_iota(c0) < xmid))
            ok = cnt >= keep
            return jnp.where(ok, xlo, xmid), jnp.where(ok, xmid, xhi)

        xlo0 = jnp.zeros((1, LANES), jnp.int32)
        xhi0 = jnp.full((1, LANES), nchunks * chunk, jnp.int32)
        n_idx_steps = 15
        _, xcut = lax.fori_loop(0, n_idx_steps, idx_body, (xlo0, xhi0))

        def drop_body(c, carry):
            c0 = pl.multiple_of(c * chunk, chunk)
            v = st_ref[pl.ds(c0, chunk), :]
            drop = jnp.logical_and(v == cut, key_iota(c0) >= xcut)
            st_ref[pl.ds(c0, chunk), :] = jnp.where(drop, NEG_INF, v)
            return carry

        lax.fori_loop(0, nchunks, drop_body, 0)
        return jnp.where(unresolved, cut, tau)

    return lax.cond(n_unres > 0.0, tie_path, lambda t: t, tau)


def _prompt_attn_kernel(kmax2_ref, qi_ref, kw_ref, q_ref, kilo_ref, kihi_ref, k_ref, vt_ref, o_ref,
                        st_ref, m_ref, l_ref, acc_ref, accf_ref):
    i = pl.program_id(0)
    tq, ck = Q_BLOCK, KEY_CHUNK
    n_blocks = i + 2
    nchunks = (n_blocks * Q_BLOCK + ck - 1) // ck

    wt = jnp.transpose(kw_ref[...])[IDX_DIM:IDX_DIM + N_IDX_HEADS, :] * (IDX_DIM ** -0.5)
    qcat = jnp.concatenate(
        [qi_ref[:, p * LANES:(p + 1) * LANES] for p in range(N_IDX_HEADS // 2)], axis=0)
    key = lax.broadcasted_iota(jnp.int32, (ck, tq), 0)
    qry = i * tq + lax.broadcasted_iota(jnp.int32, (ck, tq), 1)

    def idx_chunk(c, st):
        smin, smax = st
        c0 = pl.multiple_of(c * ck, ck)
        klo = kilo_ref[pl.ds(c0, ck), :]
        khi = kihi_ref[pl.ds(c0, ck), :]
        acc = jnp.zeros((ck, tq), F32)
        for t in range(N_IDX_HEADS // 4):
            rhs = qcat[2 * t * tq:(2 * t + 2) * tq]
            dlo = lax.dot_general(klo, rhs, CONTRACT_LAST, preferred_element_type=F32)
            dhi = lax.dot_general(khi, rhs, CONTRACT_LAST, preferred_element_type=F32)
            for j in range(2):
                p = 2 * t + j
                cs = slice(j * tq, (j + 1) * tq)
                acc = acc + wt[2 * p:2 * p + 1, :] * jnp.maximum(dlo[:, cs], 0.0)
                acc = acc + wt[2 * p + 1:2 * p + 2, :] * jnp.maximum(dhi[:, cs], 0.0)
        col = c0 + key
        valid = jnp.logical_or(
            col < N_META,
            jnp.logical_and(col >= Q_BLOCK, col - Q_BLOCK <= qry))
        st_ref[pl.ds(c0, ck), :] = jnp.where(valid, acc, NEG_INF)
        smin = jnp.minimum(smin, jnp.min(jnp.where(valid, acc, POS_INF), axis=0, keepdims=True))
        smax = jnp.maximum(smax, jnp.max(jnp.where(valid, acc, NEG_INF), axis=0, keepdims=True))
        return smin, smax

    st = lax.fori_loop(
        0, nchunks // 2, lambda j, st: idx_chunk(2 * j + 1, idx_chunk(2 * j, st)),
        (jnp.full((1, tq), POS_INF, F32), jnp.full((1, tq), NEG_INF, F32)))
    smin, smax = lax.cond(nchunks % 2 == 1, lambda st: idx_chunk(nchunks - 1, st), lambda st: st, st)

    n_valid = (N_META + 1 + i * tq + lax.broadcasted_iota(jnp.int32, (1, tq), 1)).astype(F32)
    tau = _topk_threshold(st_ref, nchunks, ck, n_valid, smin, smax)

    scale = HEAD_DIM ** -0.5

    def q_group(n):
        return jnp.concatenate(
            [q_ref[:, (n * GROUP + g) * LANES:(n * GROUP + g + 1) * LANES]
             for g in range(GROUP)], axis=0)

    def k_chunk(c0, n):
        return k_ref[pl.ds(c0, ck), n * LANES:(n + 1) * LANES]

    def write_out(n, ot):
        for g in range(GROUP):
            h = n * GROUP + g
            o_ref[:, h * LANES:(h + 1) * LANES] = jnp.transpose(
                ot[:, g * tq:(g + 1) * tq]).astype(o_ref.dtype)

    def ref_body(c, mx):
        c0 = pl.multiple_of(c * ck, ck)
        sel = st_ref[pl.ds(c0, ck), :] >= tau
        out = []
        for n in range(N_KV_HEADS):
            raw = lax.dot_general(k_chunk(c0, n), q_group(n), CONTRACT_LAST,
                                  preferred_element_type=F32)
            cur = jnp.concatenate(
                [jnp.max(jnp.where(sel, raw[:, g * tq:(g + 1) * tq], NEG), axis=0, keepdims=True)
                 for g in range(GROUP)], axis=1)
            out.append(jnp.maximum(mx[n], cur))
        return tuple(out)

    mx = ref_body(0, tuple(jnp.full((1, GROUP * tq), NEG, F32) for _ in range(N_KV_HEADS)))
    c2 = scale * LOG2E
    ones = jnp.ones((SUBLANES, HEAD_DIM), BF16)
    gap = jnp.float32(0.0)
    for n in range(N_KV_HEADS):
        qf = q_group(n).astype(F32)
        qn2 = lax.dot_general(ones, (qf * qf).astype(BF16), CONTRACT_LAST,
                              preferred_element_type=F32)[0:1]
        bound = jnp.sqrt(qn2 * kmax2_ref[n]) * (c2 * NORM_BOUND_SLACK)
        gap = jnp.maximum(gap, jnp.max(bound - mx[n] * c2))

    def fixed_reference_path():
        accf_ref[...] = jnp.zeros(accf_ref.shape, F32)

        def chunk_update(c, n):
            c0 = pl.multiple_of(c * ck, ck)
            sel = st_ref[pl.ds(c0, ck), :] >= tau
            raw = lax.dot_general(k_chunk(c0, n), q_group(n), CONTRACT_LAST,
                                  preferred_element_type=F32)
            neg_ref = mx[n] * (-c2)
            p = jnp.concatenate(
                [jnp.exp2(raw[:, g * tq:(g + 1) * tq] * c2
                          + jnp.where(sel, neg_ref[:, g * tq:(g + 1) * tq], NEG)).astype(BF16)
                 for g in range(GROUP)], axis=1)
            return jnp.dot(vt_ref[c, n], p, preferred_element_type=F32)

        def pair_body(j, carry):
            for n in range(N_KV_HEADS):
                accf_ref[n] += chunk_update(2 * j, n) + chunk_update(2 * j + 1, n)
            return carry

        lax.fori_loop(0, nchunks // 2, pair_body, 0)

        @pl.when(nchunks % 2 == 1)
        def _():
            for n in range(N_KV_HEADS):
                accf_ref[n] += chunk_update(nchunks - 1, n)

        for n in range(N_KV_HEADS):
            write_out(n, accf_ref[n, 0:HEAD_DIM, :] / accf_ref[n, HEAD_DIM:HEAD_DIM + 1, :])

    def running_max_path():
        m_ref[...] = jnp.full(m_ref.shape, NEG, F32)
        l_ref[...] = jnp.zeros(l_ref.shape, F32)
        acc_ref[...] = jnp.zeros(acc_ref.shape, F32)

        def body(c, carry):
            c0 = pl.multiple_of(c * ck, ck)
            bias = jnp.where(st_ref[pl.ds(c0, ck), :] >= tau, 0.0, NEG)
            bias = jnp.concatenate([bias] * GROUP, axis=1)
            for n in range(N_KV_HEADS):
                sc = lax.dot_general(k_chunk(c0, n), q_group(n), CONTRACT_LAST,
                                     preferred_element_type=F32)
                sc = sc * scale + bias
                m_old = m_ref[n]
                m_new = jnp.maximum(m_old, jnp.max(sc, axis=0, keepdims=True))
                alpha = jnp.exp(m_old - m_new)
                p = jnp.exp(sc - m_new)
                l_ref[n] = alpha * l_ref[n] + jnp.sum(
                    p.reshape(ck // SUBLANES, SUBLANES, GROUP * tq), axis=0)
                acc_ref[n] = alpha * acc_ref[n] + jnp.dot(
                    vt_ref[c, n, 0:HEAD_DIM, :], p.astype(BF16), preferred_element_type=F32)
                m_ref[n] = m_new
            return carry

        lax.fori_loop(0, nchunks, body, 0)
        for n in range(N_KV_HEADS):
            write_out(n, acc_ref[n] / jnp.sum(l_ref[n], axis=0, keepdims=True))

    lax.cond(gap <= MAX_EXP2_GAP, fixed_reference_path, running_max_path)


def _prompt_attention(kmax2, qi, kw, q, ki_lo, ki_hi, k_all, v_t):
    rows = q.shape[0]
    nk = k_all.shape[0]
    assert rows % Q_BLOCK == 0 and nk % KEY_CHUNK == 0
    assert nk >= (rows // Q_BLOCK + 1) * Q_BLOCK
    assert v_t.shape == (nk // KEY_CHUNK, N_KV_HEADS, HEAD_DIM + ONES_ROWS, KEY_CHUNK)
    row_spec = lambda w: pl.BlockSpec((Q_BLOCK, w), lambda i: (i, 0))
    return pl.pallas_call(
        _prompt_attn_kernel,
        grid=(rows // Q_BLOCK,),
        in_specs=[pl.BlockSpec(memory_space=pltpu.SMEM),
                  row_spec(1024), row_spec(LANES), row_spec(ATTN_WIDTH),
                  _const_spec(ki_lo.shape), _const_spec(ki_hi.shape),
                  _const_spec(k_all.shape), _const_spec(v_t.shape)],
        out_specs=row_spec(ATTN_WIDTH),
        out_shape=jax.ShapeDtypeStruct((rows, ATTN_WIDTH), BF16),
        scratch_shapes=[
            pltpu.VMEM((nk, Q_BLOCK), F32),
            pltpu.VMEM((N_KV_HEADS, 1, GROUP * Q_BLOCK), F32),
            pltpu.VMEM((N_KV_HEADS, SUBLANES, GROUP * Q_BLOCK), F32),
            pltpu.VMEM((N_KV_HEADS, HEAD_DIM, GROUP * Q_BLOCK), F32),
            pltpu.VMEM((N_KV_HEADS, HEAD_DIM + ONES_ROWS, GROUP * Q_BLOCK), F32),
        ],
        compiler_params=pltpu.CompilerParams(
            dimension_semantics=("arbitrary",), vmem_limit_bytes=VMEM_LIMIT),
        name="prompt_attn",
    )(kmax2, qi, kw, q, ki_lo, ki_hi, k_all, v_t)


SCORE_PAGE_SLOTS = 4
KV_PAGE_SLOTS = 3


def _prefetch_pages(copies, n_pages, n_slots, b, n_seq):
    depth = n_slots - 1

    def start(seq):
        for j in range(n_pages):
            for cp in copies(seq, seq % n_slots, j):
                cp.start()

    @pl.when(b == 0)
    def _():
        for s in range(depth):
            start(s)

    @pl.when(b + depth < n_seq)
    def _():
        start(b + depth)

    for j in range(n_pages):
        for cp in copies(b, b % n_slots, j):
            cp.wait()
    return b % n_slots


def _decode_scores_kernel(pt_ref, qi_ref, w_ref, kinew_ref, cache_ref, s_ref, buf, sem):
    b = pl.program_id(0)
    n_pages = pt_ref.shape[1]

    def copies(seq, slot, j):
        return (pltpu.make_async_copy(
            cache_ref.at[0, pt_ref[seq, j]], buf.at[slot, j], sem.at[slot]),)

    slot = _prefetch_pages(copies, n_pages, buf.shape[0], b, pl.num_programs(0))

    qi = qi_ref[...]
    w = w_ref[...] * (IDX_DIM ** -0.5)
    for j in range(n_pages):
        kpt = buf[slot, j].astype(BF16)
        d = jnp.dot(qi, kpt, preferred_element_type=F32)
        s_ref[:, j * PAGE_SIZE:(j + 1) * PAGE_SIZE] = jnp.sum(
            jnp.maximum(d, 0.0) * w, axis=0, keepdims=True)
    d_new = jnp.sum(qi.astype(F32) * kinew_ref[...].astype(F32), axis=-1, keepdims=True)
    s_new = jnp.sum(jnp.maximum(d_new, 0.0) * w, axis=0, keepdims=True)
    past = n_pages * PAGE_SIZE
    tail = s_ref.shape[1] - past
    lane = lax.broadcasted_iota(jnp.int32, (1, tail), 1)
    s_ref[:, past:] = jnp.where(lane == 0, s_new, NEG_INF)


def _decode_scores(page_table, qi_s, w_s, kinew_s, cache_kidx_t, width):
    db, n_pages = page_table.shape
    seq_spec = lambda a, c: pl.BlockSpec((None, a, c), lambda b, pt: (b, 0, 0))
    return pl.pallas_call(
        _decode_scores_kernel,
        grid_spec=pltpu.PrefetchScalarGridSpec(
            num_scalar_prefetch=1,
            grid=(db,),
            in_specs=[seq_spec(N_IDX_HEADS, IDX_DIM), seq_spec(N_IDX_HEADS, 1),
                      seq_spec(1, IDX_DIM), pl.BlockSpec(memory_space=pl.ANY)],
            out_specs=seq_spec(1, width),
            scratch_shapes=[pltpu.VMEM((SCORE_PAGE_SLOTS, n_pages, IDX_DIM, PAGE_SIZE), F32),
                            pltpu.SemaphoreType.DMA((SCORE_PAGE_SLOTS,))],
        ),
        out_shape=jax.ShapeDtypeStruct((db, 1, width), F32),
        compiler_params=pltpu.CompilerParams(dimension_semantics=("arbitrary",)),
        name="decode_scores",
    )(page_table, qi_s, w_s, kinew_s, cache_kidx_t)


def _decode_mask_kernel(s_in_ref, bias_ref, st_ref):
    nk = s_in_ref.shape[0]
    chunk = KEY_CHUNK
    s = s_in_ref[...]
    st_ref[...] = s
    valid = s > NEG_INF
    smin = jnp.min(jnp.where(valid, s, POS_INF), axis=0, keepdims=True)
    smax = jnp.max(s, axis=0, keepdims=True)
    n_valid = jnp.sum(valid.astype(F32), axis=0, keepdims=True)
    tau = _topk_threshold(st_ref, nk // chunk, chunk, n_valid, smin, smax)
    bias_ref[...] = jnp.where(st_ref[...] >= tau, 0.0, NEG)


def _decode_mask(scores_t):
    assert scores_t.shape[0] % KEY_CHUNK == 0 and scores_t.shape[1] == LANES
    return pl.pallas_call(
        _decode_mask_kernel,
        out_shape=jax.ShapeDtypeStruct(scores_t.shape, F32),
        scratch_shapes=[pltpu.VMEM(scores_t.shape, F32)],
        name="decode_mask",
    )(scores_t)


def _decode_attn_kernel(pt_ref, q_ref, bias_ref, knew_ref, vnew_ref, ck_ref, cv_ref, o_ref,
                        kbuf, vbuf, sem):
    b = pl.program_id(0)
    n_pages = pt_ref.shape[1]
    page_rows = N_KV_HEADS * PAGE_SIZE

    def copies(seq, slot, j):
        rows_j = pl.ds(j * page_rows, page_rows)
        page = pt_ref[seq, j]
        return (pltpu.make_async_copy(ck_ref.at[0, page], kbuf.at[slot, rows_j], sem.at[slot, 0]),
                pltpu.make_async_copy(cv_ref.at[0, page], vbuf.at[slot, rows_j], sem.at[slot, 1]))

    slot = _prefetch_pages(copies, n_pages, kbuf.shape[0], b, pl.num_programs(0))

    rows = n_pages * page_rows
    scale = HEAD_DIM ** -0.5
    q = q_ref[...]
    bias = jnp.concatenate(
        [jnp.broadcast_to(bias_ref[n:n + 1, :], (GROUP, bias_ref.shape[1]))
         for n in range(N_KV_HEADS)], axis=0)
    k_il = kbuf[slot].astype(BF16)
    v_il = vbuf[slot].astype(BF16)
    sc = lax.dot_general(q, k_il, CONTRACT_LAST, preferred_element_type=F32)
    sc = sc * scale + bias[:, :rows]
    sc_new = jnp.sum(q.astype(F32) * knew_ref[...].astype(F32), axis=-1, keepdims=True)
    sc_new = sc_new * scale + jnp.max(bias[:, rows:rows + N_KV_HEADS], axis=-1, keepdims=True)
    m = jnp.maximum(jnp.max(sc, axis=-1, keepdims=True), sc_new)
    p = jnp.exp(sc - m)
    p_new = jnp.exp(sc_new - m)
    denom = jnp.sum(p, axis=-1, keepdims=True) + p_new
    o = jnp.dot(p.astype(BF16), v_il, preferred_element_type=F32)
    o = o + p_new.astype(BF16).astype(F32) * vnew_ref[...].astype(F32)
    o_ref[...] = (o / denom).astype(o_ref.dtype)


def _decode_attention(page_table, q_s, bias_il, knew_s, vnew_s, cache_k, cache_v):
    db, n_pages = page_table.shape
    width2 = bias_il.shape[-1]
    rows = n_pages * N_KV_HEADS * PAGE_SIZE
    seq_spec = lambda a, c: pl.BlockSpec((None, a, c), lambda b, pt: (b, 0, 0))
    return pl.pallas_call(
        _decode_attn_kernel,
        grid_spec=pltpu.PrefetchScalarGridSpec(
            num_scalar_prefetch=1,
            grid=(db,),
            in_specs=[seq_spec(N_HEADS, HEAD_DIM), seq_spec(N_KV_HEADS, width2),
                      seq_spec(N_HEADS, HEAD_DIM), seq_spec(N_HEADS, HEAD_DIM),
                      pl.BlockSpec(memory_space=pl.ANY), pl.BlockSpec(memory_space=pl.ANY)],
            out_specs=seq_spec(N_HEADS, HEAD_DIM),
            scratch_shapes=[pltpu.VMEM((KV_PAGE_SLOTS, rows, HEAD_DIM), F32),
                            pltpu.VMEM((KV_PAGE_SLOTS, rows, HEAD_DIM), F32),
                            pltpu.SemaphoreType.DMA((KV_PAGE_SLOTS, 2))],
        ),
        out_shape=jax.ShapeDtypeStruct((db, N_HEADS, HEAD_DIM), BF16),
        compiler_params=pltpu.CompilerParams(
            dimension_semantics=("arbitrary",), vmem_limit_bytes=VMEM_LIMIT),
        name="decode_attn",
    )(page_table, q_s, bias_il, knew_s, vnew_s, cache_k, cache_v)


POOL_TILE = 512
POOL_HALO = 16


def _pool_prompt_kernel(u_ref, uprev_ref, umeta_ref, d_ref, ubuf):
    i = pl.program_id(0)
    tp = u_ref.shape[0]
    ubuf[0:POOL_HALO, :] = jnp.where(i == 0, umeta_ref[...], uprev_ref[...])
    ubuf[POOL_HALO:, :] = u_ref[...]
    for g, w in enumerate(POOL_WINDOWS):
        cols = slice(g * POOL_GROUP_WIDTH, (g + 1) * POOL_GROUP_WIDTH)
        cur = ubuf[POOL_HALO:POOL_HALO + tp, cols]
        tot = cur
        for j in range(1, w):
            tot = tot + ubuf[POOL_HALO - j:POOL_HALO - j + tp, cols]
        d_ref[:, cols] = tot * (1.0 / w) - cur


def _pool_prompt(u, u_meta):
    rows = u.shape[0]
    tp = POOL_TILE
    per = tp // POOL_HALO
    return pl.pallas_call(
        _pool_prompt_kernel,
        grid=(rows // tp,),
        in_specs=[pl.BlockSpec((tp, POOL_WIDTH), lambda i: (i, 0)),
                  pl.BlockSpec((POOL_HALO, POOL_WIDTH), lambda i: (jnp.maximum(i * per - 1, 0), 0)),
                  _const_spec((POOL_HALO, POOL_WIDTH))],
        out_specs=pl.BlockSpec((tp, POOL_WIDTH), lambda i: (i, 0)),
        out_shape=jax.ShapeDtypeStruct((rows, POOL_WIDTH), F32),
        scratch_shapes=[pltpu.VMEM((tp + POOL_HALO, POOL_WIDTH), F32)],
        compiler_params=pltpu.CompilerParams(dimension_semantics=("arbitrary",)),
        name="pool_prompt",
    )(u, u, u_meta)


def _pool_decode_kernel(hist_ref, u_ref, d_ref):
    for g, w in enumerate(POOL_WINDOWS):
        cols = slice(g * POOL_GROUP_WIDTH, (g + 1) * POOL_GROUP_WIDTH)
        cur = u_ref[:, cols]
        tot = cur
        for j in range(1, w):
            tot = tot + hist_ref[POOL_HIST - j, :, cols]
        d_ref[:, cols] = tot * (1.0 / w) - cur


def _pool_decode(hist, u):
    return pl.pallas_call(
        _pool_decode_kernel,
        out_shape=jax.ShapeDtypeStruct(u.shape, F32),
        name="pool_decode",
    )(hist, u)


FF_TILE = 1024


def _rmsnorm(x, g):
    ms = jnp.mean(x * x, axis=-1, keepdims=True)
    return (x * lax.rsqrt(ms + EPS)) * g


def _post_kernel(x_ref, att_ref, d_ref, woa_ref, wop_ref, wpool_ref, pscale_ref,
                 gmlp_ref, gfin_ref, wup_ref, wdn_ref, y_ref, hn_ref):
    f = pl.program_id(1)

    @pl.when(f == 0)
    def _():
        d = d_ref[...]
        pooled = jnp.concatenate(
            [jnp.dot(d[:, g * POOL_GROUP_WIDTH:(g + 1) * POOL_GROUP_WIDTH].astype(BF16),
                     wpool_ref[g], preferred_element_type=F32)
             for g in range(len(POOL_WINDOWS))], axis=1) * pscale_ref[...]
        h = (x_ref[...]
             + jnp.dot(att_ref[...], woa_ref[...], preferred_element_type=F32)
             + jnp.dot(pooled.astype(BF16), wop_ref[...], preferred_element_type=F32))
        y_ref[...] = h
        hn_ref[...] = _rmsnorm(h, gmlp_ref[...]).astype(BF16)

    a = jnp.maximum(jnp.dot(hn_ref[...], wup_ref[...], preferred_element_type=F32), 0.0)
    y_ref[...] += jnp.dot((a * a).astype(BF16), wdn_ref[...], preferred_element_type=F32)

    @pl.when(f == pl.num_programs(1) - 1)
    def _():
        y_ref[...] = _rmsnorm(y_ref[...], gfin_ref[...])


def _post(x, att, d, wo_a, wo_p, w_pool, pool_scale, g_mlp, g_final, w_up, w_down, tm):
    rows = x.shape[0]
    assert rows % tm == 0 and D_FF % FF_TILE == 0
    row_spec = lambda w: pl.BlockSpec((tm, w), lambda i, f: (i, 0))
    return pl.pallas_call(
        _post_kernel,
        grid=(rows // tm, D_FF // FF_TILE),
        in_specs=[row_spec(D_MODEL), row_spec(ATTN_WIDTH), row_spec(POOL_WIDTH),
                  _const_spec(wo_a.shape), _const_spec(wo_p.shape), _const_spec(w_pool.shape),
                  _const_spec((1, POOL_WIDTH)), _const_spec((1, D_MODEL)), _const_spec((1, D_MODEL)),
                  pl.BlockSpec((D_MODEL, FF_TILE), lambda i, f: (0, f)),
                  pl.BlockSpec((FF_TILE, D_MODEL), lambda i, f: (f, 0))],
        out_specs=row_spec(D_MODEL),
        out_shape=jax.ShapeDtypeStruct((rows, D_MODEL), F32),
        scratch_shapes=[pltpu.VMEM((tm, D_MODEL), BF16)],
        compiler_params=pltpu.CompilerParams(
            dimension_semantics=("arbitrary", "arbitrary"), vmem_limit_bytes=VMEM_LIMIT),
        name="post",
    )(x, att, d, wo_a, wo_p, w_pool, pool_scale.reshape(1, POOL_WIDTH),
      g_mlp.reshape(1, D_MODEL), g_final.reshape(1, D_MODEL), w_up, w_down)


def kernel(x_prompt, x_sample, cache_k, cache_v, cache_kidx, state_pool, page_table, meta_tokens,
           g_mix, w_in, w_pool, pool_scale, w_out, g_mlp, w_up, w_down, g_final):
    assert x_prompt.shape[0] == 1 and x_sample.shape[1] == 1 and g_mix.shape[0] == 1
    seq = x_prompt.shape[1]
    db = x_sample.shape[0]
    n_pages = page_table.shape[1]
    past = n_pages * PAGE_SIZE
    assert db == LANES

    w = w_in[0]
    o = 0
    parts = []
    for width in (ATTN_WIDTH, 2 * N_KV_HEADS * HEAD_DIM, N_IDX_HEADS * IDX_DIM,
                  IDX_DIM + N_IDX_HEADS, POOL_WIDTH):
        parts.append(w[:, o:o + width].astype(BF16))
        o += width
    parts[3] = jnp.pad(parts[3], ((0, 0), (0, LANES - parts[3].shape[1])))
    wo_a = w_out[0, :ATTN_WIDTH].astype(BF16)
    wo_p = w_out[0, ATTN_WIDTH:].astype(BF16)
    w_pool_b = w_pool[0].astype(BF16)
    w_up_b = w_up[0].astype(BF16)
    w_down_b = w_down[0].astype(BF16)

    xp = x_prompt[0]
    pos_p = N_META + jnp.arange(seq, dtype=jnp.int32)
    q_p, kf_p, vf_p, kb_p, vb_p, qi_p, kw_p, u_p = _project(xp, pos_p, g_mix[0], parts, 256)
    xs = jnp.concatenate([meta_tokens.astype(F32), x_sample[:, 0]], axis=0)
    pos_s = jnp.concatenate([jnp.arange(N_META, dtype=jnp.int32),
                             jnp.full((db,), past, jnp.int32)])
    q_s, kf_s, vf_s, kb_s, vb_s, qi_s, kw_s, u_s = _project(xs, pos_s, g_mix[0], parts, xs.shape[0])

    nk = -(-(seq + 2 * Q_BLOCK) // KEY_CHUNK) * KEY_CHUNK

    def key_rows(meta_part, prompt_part):
        width = meta_part.shape[1]
        return jnp.concatenate(
            [meta_part, jnp.zeros((Q_BLOCK - N_META, width), meta_part.dtype), prompt_part,
             jnp.zeros((nk - Q_BLOCK - seq, width), meta_part.dtype)], axis=0)

    ki_all = key_rows(kw_s[:N_META, :IDX_DIM], kw_p[:, :IDX_DIM]).astype(BF16)
    zk = jnp.zeros_like(ki_all)
    ki_lo = jnp.concatenate([ki_all, zk], axis=1)
    ki_hi = jnp.concatenate([zk, ki_all], axis=1)
    k_all = key_rows(kb_s[:N_META], kb_p)
    v_all = key_rows(vb_s[:N_META], vb_p)
    v_t = jnp.transpose(
        v_all.reshape(nk // KEY_CHUNK, KEY_CHUNK, N_KV_HEADS, HEAD_DIM), (0, 2, 3, 1))
    v_t = jnp.concatenate(
        [v_t, jnp.ones((nk // KEY_CHUNK, N_KV_HEADS, ONES_ROWS, KEY_CHUNK), BF16)], axis=2)
    knorm = lambda kw: kw[:, KNORM_LANE:KNORM_LANE + N_KV_HEADS]
    kmax2 = jnp.maximum(jnp.max(knorm(kw_p), axis=0), jnp.max(knorm(kw_s[:N_META]), axis=0))
    att_p = _prompt_attention(kmax2, qi_p, kw_p, q_p, ki_lo, ki_hi, k_all, v_t)

    width = -(-(past + 1) // KEY_CHUNK) * KEY_CHUNK
    qi_d = qi_s[N_META:].reshape(db, N_IDX_HEADS, IDX_DIM)
    w_d = kw_s[N_META:, IDX_DIM:IDX_DIM + N_IDX_HEADS].reshape(db, N_IDX_HEADS, 1)
    kinew = kw_s[N_META:, :IDX_DIM].astype(BF16).reshape(db, 1, IDX_DIM)
    scores = _decode_scores(page_table, qi_d, w_d, kinew, jnp.swapaxes(cache_kidx, 2, 3), width)
    bias_t = _decode_mask(jnp.transpose(scores.reshape(db, width)))
    bias = jnp.transpose(bias_t)
    negs = jnp.full_like(bias, NEG)
    bias_il = jnp.stack(
        [jnp.stack([bias, negs], axis=-1), jnp.stack([negs, bias], axis=-1)], axis=1,
    ).reshape(db, N_KV_HEADS, N_KV_HEADS * width)
    n_phys = cache_k.shape[1]
    il_shape = (1, n_phys, N_KV_HEADS * PAGE_SIZE, HEAD_DIM)
    per_head = lambda a: jnp.repeat(a.reshape(db, N_KV_HEADS, HEAD_DIM), GROUP, axis=1)
    att_s = _decode_attention(
        page_table, q_s[N_META:].reshape(db, N_HEADS, HEAD_DIM), bias_il,
        per_head(kb_s[N_META:]), per_head(vb_s[N_META:]),
        cache_k.reshape(il_shape), cache_v.reshape(il_shape))
    att_s = att_s.reshape(db, ATTN_WIDTH)

    d_p = _pool_prompt(u_p, u_s[:N_META])
    d_s = _pool_decode(jnp.swapaxes(state_pool[0], 0, 1), u_s[N_META:])

    post = functools.partial(
        _post, wo_a=wo_a, wo_p=wo_p, w_pool=w_pool_b, pool_scale=pool_scale[0],
        g_mlp=g_mlp[0], g_final=g_final, w_up=w_up_b, w_down=w_down_b)
    y_p = post(xp, att_p, d_p, tm=512)
    y_s = post(x_sample[:, 0], att_s, d_s, tm=db)

    k_prompt = jnp.concatenate([kf_s[:N_META], kf_p], axis=0).reshape(1, 1, N_META + seq, N_KV_HEADS, HEAD_DIM)
    v_prompt = jnp.concatenate([vf_s[:N_META], vf_p], axis=0).reshape(1, 1, N_META + seq, N_KV_HEADS, HEAD_DIM)
    kidx_prompt = jnp.concatenate([kw_s[:N_META, :IDX_DIM], kw_p[:, :IDX_DIM]], axis=0).reshape(
        1, 1, N_META + seq, IDX_DIM)
    pool_prompt = u_p[seq - POOL_HIST:].reshape(1, 1, POOL_HIST, POOL_WIDTH)
    k_sample = kf_s[N_META:].reshape(1, db, 1, N_KV_HEADS, HEAD_DIM)
    v_sample = vf_s[N_META:].reshape(1, db, 1, N_KV_HEADS, HEAD_DIM)
    kidx_sample = kw_s[N_META:, :IDX_DIM].reshape(1, db, 1, IDX_DIM)
    pool_sample = jnp.concatenate(
        [state_pool[0][:, 1:], u_s[N_META:][:, None, :]], axis=1)[None]
    return (y_p.reshape(1, seq, D_MODEL), y_s.reshape(db, 1, D_MODEL),
            k_prompt, v_prompt, kidx_prompt, pool_prompt,
            k_sample, v_sample, kidx_sample, pool_sample)
```

```python
import functools

import jax
import jax.numpy as jnp
from jax import lax
from jax.experimental import pallas as pl
from jax.experimental.pallas import tpu as pltpu

F32 = jnp.float32
BF16 = jnp.bfloat16

D_MODEL = 2048
N_META = 16
ATTN_WIDTH = 1024
POOL_WIDTH = 1024
HEAD_DIM = 128
N_HEADS = 8
N_KV_HEADS = 2
GROUP = N_HEADS // N_KV_HEADS
N_IDX_HEADS = 16
IDX_DIM = 64
TOPK = 256
POOL_WINDOWS = (2, 4, 8, 16)
POOL_GROUP_WIDTH = 256
POOL_HIST = 15
D_FF = 8192
PAGE_SIZE = 128
ROPE_THETA = 10000.0
EPS = 1e-6
NEG = -1e30
NEG_INF = float("-inf")
POS_INF = float("inf")
F32_LOWEST = -3.0e38

LANES = 128
SUBLANES = 8
VMEM_LIMIT = 56 * 1024 * 1024

Q_BLOCK = 128
KEY_CHUNK = 512
MAX_BISECT = 320
LOG2E = 1.4426950408889634
MAX_EXP2_GAP = 80.0
NORM_BOUND_SLACK = 1.02
ONES_ROWS = 16
KNORM_LANE = IDX_DIM + N_IDX_HEADS
PROJ_TILE = 256
POOL_HALO = 16

CONTRACT_LAST = (((1,), (1,)), ((), ()))


def _const_spec(shape):
    nd = len(shape)
    return pl.BlockSpec(shape, lambda *_: (0,) * nd, pipeline_mode=pl.Buffered(1))


def _rope128(x, c, s):
    return x * c + pltpu.roll(x, 64, 1) * s


def _rope64(x, c, sa, sb):
    return x * c + pltpu.roll(x, 96, 1) * sa + pltpu.roll(x, 32, 1) * sb


def _angle_sum(cb_ref, sb_ref, cr_ref, sr_ref):
    cb, sb, cr, sr = cb_ref[...], sb_ref[...], cr_ref[...], sr_ref[...]
    return cb * cr - sb * sr, sb * cr + cb * sr


def _proj_kernel(x_ref, g_ref, cb128_ref, sb128_ref, cb64_ref, sb64_ref,
                 cr128_ref, sr128_ref, cr64_ref, sr64_ref,
                 wq_ref, wkv_ref, wqi_ref, wkw_ref, wu_ref, *rest, with_pool):
    if with_pool:
        (halo_ref, q_ref, kf_ref, vf_ref, kb_ref, vb_ref, qi_ref, kw_ref, u_ref, utail_ref,
         ubuf) = rest
    else:
        q_ref, kf_ref, vf_ref, kb_ref, vb_ref, qi_ref, kw_ref, u_ref = rest
    x = x_ref[...]
    ms = jnp.mean(x * x, axis=-1, keepdims=True)
    xn = ((x * lax.rsqrt(ms + EPS)) * g_ref[...]).astype(BF16)
    tm = x.shape[0]
    lane_t = lax.broadcasted_iota(jnp.int32, (tm, LANES), 1)
    c128, sfull = _angle_sum(cb128_ref, sb128_ref, cr128_ref, sr128_ref)
    s128 = jnp.where(lane_t < HEAD_DIM // 2, -sfull, sfull)
    c64, sfull = _angle_sum(cb64_ref, sb64_ref, cr64_ref, sr64_ref)
    first_half = jnp.bitwise_and(lane_t, IDX_DIM - 1) < IDX_DIM // 2
    sa64 = jnp.where(first_half, -sfull, 0.0)
    sb64 = jnp.where(first_half, 0.0, sfull)

    zq = jnp.dot(xn, wq_ref[...], preferred_element_type=F32)
    for h in range(N_HEADS):
        sl = slice(h * LANES, (h + 1) * LANES)
        q_ref[:, sl] = _rope128(zq[:, sl], c128, s128).astype(BF16)

    zkv = jnp.dot(xn, wkv_ref[...], preferred_element_type=F32)
    k_norm2 = []
    for h in range(N_KV_HEADS):
        sl = slice(h * LANES, (h + 1) * LANES)
        kr = _rope128(zkv[:, sl], c128, s128)
        kf_ref[:, sl] = kr
        kb = kr.astype(BF16)
        kb_ref[:, sl] = kb
        k_norm2.append(jnp.sum(kb.astype(F32) * kb.astype(F32), axis=-1, keepdims=True))
    v = zkv[:, N_KV_HEADS * LANES:]
    vf_ref[...] = v
    vb_ref[...] = v.astype(BF16)

    zqi = jnp.dot(xn, wqi_ref[...], preferred_element_type=F32)
    for p in range(N_IDX_HEADS // 2):
        sl = slice(p * LANES, (p + 1) * LANES)
        qi_ref[:, sl] = _rope64(zqi[:, sl], c64, sa64, sb64).astype(BF16)

    zkw = jnp.dot(xn, wkw_ref[...], preferred_element_type=F32)
    lane = lax.broadcasted_iota(jnp.int32, zkw.shape, 1)
    is_key = lane < IDX_DIM
    ckw = jnp.where(is_key, c64, jnp.where(lane < IDX_DIM + N_IDX_HEADS, N_IDX_HEADS ** -0.5, 1.0))
    kw = _rope64(zkw, ckw, jnp.where(is_key, sa64, 0.0), jnp.where(is_key, sb64, 0.0))
    for h in range(N_KV_HEADS):
        kw = jnp.where(lane == KNORM_LANE + h, k_norm2[h], kw)
    kw_ref[...] = kw

    u = jnp.dot(xn, wu_ref[...], preferred_element_type=F32)
    if not with_pool:
        u_ref[...] = u
        return

    @pl.when(pl.program_id(0) == 0)
    def _():
        ubuf[0:POOL_HALO, :] = halo_ref[...]

    ubuf[POOL_HALO:, :] = u
    for g, w in enumerate(POOL_WINDOWS):
        cols = slice(g * POOL_GROUP_WIDTH, (g + 1) * POOL_GROUP_WIDTH)
        cur = ubuf[POOL_HALO:POOL_HALO + tm, cols]
        tot = cur
        for j in range(1, w):
            tot = tot + ubuf[POOL_HALO - j:POOL_HALO - j + tm, cols]
        u_ref[:, cols] = tot * (1.0 / w) - cur
    tail = ubuf[tm:tm + POOL_HALO, :]
    ubuf[0:POOL_HALO, :] = tail
    utail_ref[...] = tail


def _project(x, pos_base, pos_rel, g_mix, w_parts, halo=None):
    tm = pos_rel.shape[0]
    n_tiles = pos_base.shape[0]
    rows = x.shape[0]
    assert rows == tm * n_tiles
    with_pool = halo is not None
    base_tabs = [t.reshape(n_tiles, 1, LANES) for t in _rope_tables(pos_base)]
    rel_tabs = _rope_tables(pos_rel)
    row_spec = lambda w: pl.BlockSpec((tm, w), lambda i: (i, 0))
    base_spec = pl.BlockSpec((None, 1, LANES), lambda i: (i, 0, 0))
    out_shapes = [
        jax.ShapeDtypeStruct((rows, ATTN_WIDTH), BF16),
        jax.ShapeDtypeStruct((rows, 256), F32),
        jax.ShapeDtypeStruct((rows, 256), F32),
        jax.ShapeDtypeStruct((rows, 256), BF16),
        jax.ShapeDtypeStruct((rows, 256), BF16),
        jax.ShapeDtypeStruct((rows, 1024), BF16),
        jax.ShapeDtypeStruct((rows, LANES), F32),
        jax.ShapeDtypeStruct((rows, POOL_WIDTH), F32),
    ]
    out_specs = [row_spec(s.shape[1]) for s in out_shapes]
    in_specs = ([row_spec(D_MODEL), _const_spec((1, D_MODEL))] + [base_spec] * 4
                + [_const_spec((tm, LANES))] * 4 + [_const_spec(w.shape) for w in w_parts])
    args = [x, g_mix.reshape(1, D_MODEL), *base_tabs, *rel_tabs, *w_parts]
    scratch = []
    if with_pool:
        in_specs.append(_const_spec((POOL_HALO, POOL_WIDTH)))
        args.append(halo)
        out_shapes.append(jax.ShapeDtypeStruct((POOL_HALO, POOL_WIDTH), F32))
        out_specs.append(pl.BlockSpec((POOL_HALO, POOL_WIDTH), lambda i: (0, 0)))
        scratch.append(pltpu.VMEM((tm + POOL_HALO, POOL_WIDTH), F32))
    return pl.pallas_call(
        functools.partial(_proj_kernel, with_pool=with_pool),
        grid=(n_tiles,),
        in_specs=in_specs,
        out_specs=tuple(out_specs),
        out_shape=tuple(out_shapes),
        scratch_shapes=scratch,
        compiler_params=pltpu.CompilerParams(
            dimension_semantics=("arbitrary",), vmem_limit_bytes=VMEM_LIMIT),
        name="proj",
    )(*args)


def _rope_tables(pos):
    posf = pos.astype(F32)[:, None]
    out = []
    for dim in (HEAD_DIM, IDX_DIM):
        inv = ROPE_THETA ** (-jnp.arange(0, dim, 2, dtype=F32) / dim)
        ang = posf * inv[None, :]
        reps = LANES // (dim // 2)
        out += [jnp.tile(jnp.cos(ang), (1, reps)), jnp.tile(jnp.sin(ang), (1, reps))]
    return out


def _key_count(st_ref, nchunks, chunk, pred):
    acc_rows = 8 * SUBLANES

    def body(c, part):
        c0 = pl.multiple_of(c * chunk, chunk)
        hit = pred(st_ref[pl.ds(c0, chunk), :], c0)
        for g in range(chunk // acc_rows):
            part = jnp.where(hit[g * acc_rows:(g + 1) * acc_rows], part + 1.0, part)
        return part

    part = lax.fori_loop(0, nchunks, body, jnp.zeros((acc_rows, LANES), F32))
    return jnp.sum(part, axis=0, keepdims=True)


def _topk_threshold(st_ref, nchunks, chunk, n_valid, s_min, s_max):
    kf = float(TOPK)
    take_all = n_valid <= kf

    def count_ge(t):
        return _key_count(st_ref, nchunks, chunk, lambda v, c0: v >= t)

    def cond(st):
        it, done = st[0], st[3]
        return jnp.logical_and(it < MAX_BISECT, jnp.min(done) < 0.5)

    def step(st):
        lo, hi, done, tau = st
        mid = lo + (hi - lo) * 0.5
        collapsed = jnp.logical_or(mid <= lo, mid >= hi)
        cnt = count_ge(mid)
        found = cnt == kf
        active = done < 0.5
        lo = jnp.where(jnp.logical_and(active, cnt > kf), mid, lo)
        hi = jnp.where(jnp.logical_and(active, cnt < kf), mid, hi)
        tau = jnp.where(jnp.logical_and(active, found), mid, tau)
        done = jnp.where(jnp.logical_or(found, collapsed), 1.0, done)
        return lo, hi, done, tau

    def body(st):
        return (st[0] + 1,) + step(step(st[1:]))

    done0 = jnp.where(take_all, 1.0, 0.0).astype(F32)
    tau0 = jnp.full((1, LANES), POS_INF, F32)
    _, lo, hi, _, tau = lax.while_loop(
        cond, body, (jnp.int32(0), s_min, s_max, done0, tau0))

    unresolved = jnp.logical_and(tau == POS_INF, jnp.logical_not(take_all))
    n_unres = jnp.sum(unresolved.astype(F32))
    tau = jnp.where(take_all, F32_LOWEST, tau)

    def tie_path(tau):
        cut = jnp.where(count_ge(hi) >= kf, hi, lo)
        cut = jnp.where(unresolved, cut, POS_INF)
        c_gt = _key_count(st_ref, nchunks, chunk, lambda v, c0: v > cut)
        keep = kf - c_gt

        def key_iota(c0):
            return c0 + lax.broadcasted_iota(jnp.int32, (chunk, LANES), 0)

        def idx_body(_, st):
            xlo, xhi = st
            xmid = (xlo + xhi) // 2
            cnt = _key_count(
                st_ref, nchunks, chunk,
                lambda v, c0: jnp.logical_and(v == cut, key_iota(c0) < xmid))
            ok = cnt >= keep
            return jnp.where(ok, xlo, xmid), jnp.where(ok, xmid, xhi)

        xlo0 = jnp.zeros((1, LANES), jnp.int32)
        xhi0 = jnp.full((1, LANES), nchunks * chunk, jnp.int32)
        n_idx_steps = 15
        _, xcut = lax.fori_loop(0, n_idx_steps, idx_body, (xlo0, xhi0))

        def drop_body(c, carry):
            c0 = pl.multiple_of(c * chunk, chunk)
            v = st_ref[pl.ds(c0, chunk), :]
            drop = jnp.logical_and(v == cut, key_iota(c0) >= xcut)
            st_ref[pl.ds(c0, chunk), :] = jnp.where(drop, NEG_INF, v)
            return carry

        lax.fori_loop(0, nchunks, drop_body, 0)
        return jnp.where(unresolved, cut, tau)

    return lax.cond(n_unres > 0.0, tie_path, lambda t: t, tau)


def _prompt_attn_kernel(kmax2_ref, qi_ref, kw_ref, q_ref, kilo_ref, kihi_ref, k_ref, vt_ref, o_ref,
                        st_ref, m_ref, l_ref, acc_ref, accf_ref):
    i = pl.program_id(0)
    tq, ck = Q_BLOCK, KEY_CHUNK
    n_blocks = i + 2
    nchunks = (n_blocks * Q_BLOCK + ck - 1) // ck

    wt = jnp.transpose(kw_ref[...])[IDX_DIM:IDX_DIM + N_IDX_HEADS, :] * (IDX_DIM ** -0.5)
    qcat = jnp.concatenate(
        [qi_ref[:, p * LANES:(p + 1) * LANES] for p in range(N_IDX_HEADS // 2)], axis=0)
    key = lax.broadcasted_iota(jnp.int32, (ck, tq), 0)
    qry = i * tq + lax.broadcasted_iota(jnp.int32, (ck, tq), 1)

    def idx_chunk(c, st):
        smin, smax = st
        c0 = pl.multiple_of(c * ck, ck)
        klo = kilo_ref[pl.ds(c0, ck), :]
        khi = kihi_ref[pl.ds(c0, ck), :]
        acc = jnp.zeros((ck, tq), F32)
        for t in range(N_IDX_HEADS // 4):
            rhs = qcat[2 * t * tq:(2 * t + 2) * tq]
            dlo = lax.dot_general(klo, rhs, CONTRACT_LAST, preferred_element_type=F32)
            dhi = lax.dot_general(khi, rhs, CONTRACT_LAST, preferred_element_type=F32)
            for j in range(2):
                p = 2 * t + j
                cs = slice(j * tq, (j + 1) * tq)
                acc = acc + wt[2 * p:2 * p + 1, :] * jnp.maximum(dlo[:, cs], 0.0)
                acc = acc + wt[2 * p + 1:2 * p + 2, :] * jnp.maximum(dhi[:, cs], 0.0)
        col = c0 + key
        valid = jnp.logical_or(
            col < N_META,
            jnp.logical_and(col >= Q_BLOCK, col - Q_BLOCK <= qry))
        st_ref[pl.ds(c0, ck), :] = jnp.where(valid, acc, NEG_INF)
        smin = jnp.minimum(smin, jnp.min(jnp.where(valid, acc, POS_INF), axis=0, keepdims=True))
        smax = jnp.maximum(smax, jnp.max(jnp.where(valid, acc, NEG_INF), axis=0, keepdims=True))
        return smin, smax

    st = lax.fori_loop(
        0, nchunks // 2, lambda j, st: idx_chunk(2 * j + 1, idx_chunk(2 * j, st)),
        (jnp.full((1, tq), POS_INF, F32), jnp.full((1, tq), NEG_INF, F32)))
    smin, smax = lax.cond(nchunks % 2 == 1, lambda st: idx_chunk(nchunks - 1, st), lambda st: st, st)

    n_valid = (N_META + 1 + i * tq + lax.broadcasted_iota(jnp.int32, (1, tq), 1)).astype(F32)
    tau = _topk_threshold(st_ref, nchunks, ck, n_valid, smin, smax)

    scale = HEAD_DIM ** -0.5

    def q_group(n):
        return jnp.concatenate(
            [q_ref[:, (n * GROUP + g) * LANES:(n * GROUP + g + 1) * LANES]
             for g in range(GROUP)], axis=0)

    def k_chunk(c0, n):
        return k_ref[pl.ds(c0, ck), n * LANES:(n + 1) * LANES]

    def write_out(n, ot):
        for g in range(GROUP):
            h = n * GROUP + g
            o_ref[:, h * LANES:(h + 1) * LANES] = jnp.transpose(
                ot[:, g * tq:(g + 1) * tq]).astype(o_ref.dtype)

    def ref_body(c, mx):
        c0 = pl.multiple_of(c * ck, ck)
        sel = st_ref[pl.ds(c0, ck), :] >= tau
        out = []
        for n in range(N_KV_HEADS):
            raw = lax.dot_general(k_chunk(c0, n), q_group(n), CONTRACT_LAST,
                                  preferred_element_type=F32)
            cur = jnp.concatenate(
                [jnp.max(jnp.where(sel, raw[:, g * tq:(g + 1) * tq], NEG), axis=0, keepdims=True)
                 for g in range(GROUP)], axis=1)
            out.append(jnp.maximum(mx[n], cur))
        return tuple(out)

    mx = ref_body(0, tuple(jnp.full((1, GROUP * tq), NEG, F32) for _ in range(N_KV_HEADS)))
    c2 = scale * LOG2E
    ones = jnp.ones((SUBLANES, HEAD_DIM), BF16)
    gap = jnp.float32(0.0)
    for n in range(N_KV_HEADS):
        qf = q_group(n).astype(F32)
        qn2 = lax.dot_general(ones, (qf * qf).astype(BF16), CONTRACT_LAST,
                              preferred_element_type=F32)[0:1]
        bound = jnp.sqrt(qn2 * kmax2_ref[n]) * (c2 * NORM_BOUND_SLACK)
        gap = jnp.maximum(gap, jnp.max(bound - mx[n] * c2))

    def fixed_reference_path():
        accf_ref[...] = jnp.zeros(accf_ref.shape, F32)

        def chunk_update(c, n):
            c0 = pl.multiple_of(c * ck, ck)
            sel = st_ref[pl.ds(c0, ck), :] >= tau
            raw = lax.dot_general(k_chunk(c0, n), q_group(n), CONTRACT_LAST,
                                  preferred_element_type=F32)
            neg_ref = mx[n] * (-c2)
            p = jnp.concatenate(
                [jnp.exp2(raw[:, g * tq:(g + 1) * tq] * c2
                          + jnp.where(sel, neg_ref[:, g * tq:(g + 1) * tq], NEG)).astype(BF16)
                 for g in range(GROUP)], axis=1)
            return jnp.dot(vt_ref[c, n], p, preferred_element_type=F32)

        def pair_body(j, carry):
            for n in range(N_KV_HEADS):
                accf_ref[n] += chunk_update(2 * j, n) + chunk_update(2 * j + 1, n)
            return carry

        lax.fori_loop(0, nchunks // 2, pair_body, 0)

        @pl.when(nchunks % 2 == 1)
        def _():
            for n in range(N_KV_HEADS):
                accf_ref[n] += chunk_update(nchunks - 1, n)

        for n in range(N_KV_HEADS):
            write_out(n, accf_ref[n, 0:HEAD_DIM, :] / accf_ref[n, HEAD_DIM:HEAD_DIM + 1, :])

    def running_max_path():
        m_ref[...] = jnp.full(m_ref.shape, NEG, F32)
        l_ref[...] = jnp.zeros(l_ref.shape, F32)
        acc_ref[...] = jnp.zeros(acc_ref.shape, F32)

        def body(c, carry):
            c0 = pl.multiple_of(c * ck, ck)
            bias = jnp.where(st_ref[pl.ds(c0, ck), :] >= tau, 0.0, NEG)
            bias = jnp.concatenate([bias] * GROUP, axis=1)
            for n in range(N_KV_HEADS):
                sc = lax.dot_general(k_chunk(c0, n), q_group(n), CONTRACT_LAST,
                                     preferred_element_type=F32)
                sc = sc * scale + bias
                m_old = m_ref[n]
                m_new = jnp.maximum(m_old, jnp.max(sc, axis=0, keepdims=True))
                alpha = jnp.exp(m_old - m_new)
                p = jnp.exp(sc - m_new)
                l_ref[n] = alpha * l_ref[n] + jnp.sum(
                    p.reshape(ck // SUBLANES, SUBLANES, GROUP * tq), axis=0)
                acc_ref[n] = alpha * acc_ref[n] + jnp.dot(
                    vt_ref[c, n, 0:HEAD_DIM, :], p.astype(BF16), preferred_element_type=F32)
                m_ref[n] = m_new
            return carry

        lax.fori_loop(0, nchunks, body, 0)
        for n in range(N_KV_HEADS):
            write_out(n, acc_ref[n] / jnp.sum(l_ref[n], axis=0, keepdims=True))

    lax.cond(gap <= MAX_EXP2_GAP, fixed_reference_path, running_max_path)


def _prompt_attention(kmax2, qi, kw, q, ki_lo, ki_hi, k_all, v_t):
    rows = q.shape[0]
    nk = k_all.shape[0]
    assert rows % Q_BLOCK == 0 and nk % KEY_CHUNK == 0
    assert nk >= (rows // Q_BLOCK + 1) * Q_BLOCK
    assert v_t.shape == (nk // KEY_CHUNK, N_KV_HEADS, HEAD_DIM + ONES_ROWS, KEY_CHUNK)
    row_spec = lambda w: pl.BlockSpec((Q_BLOCK, w), lambda i: (i, 0))
    return pl.pallas_call(
        _prompt_attn_kernel,
        grid=(rows // Q_BLOCK,),
        in_specs=[pl.BlockSpec(memory_space=pltpu.SMEM),
                  row_spec(1024), row_spec(LANES), row_spec(ATTN_WIDTH),
                  _const_spec(ki_lo.shape), _const_spec(ki_hi.shape),
                  _const_spec(k_all.shape), _const_spec(v_t.shape)],
        out_specs=row_spec(ATTN_WIDTH),
        out_shape=jax.ShapeDtypeStruct((rows, ATTN_WIDTH), BF16),
        scratch_shapes=[
            pltpu.VMEM((nk, Q_BLOCK), F32),
            pltpu.VMEM((N_KV_HEADS, 1, GROUP * Q_BLOCK), F32),
            pltpu.VMEM((N_KV_HEADS, SUBLANES, GROUP * Q_BLOCK), F32),
            pltpu.VMEM((N_KV_HEADS, HEAD_DIM, GROUP * Q_BLOCK), F32),
            pltpu.VMEM((N_KV_HEADS, HEAD_DIM + ONES_ROWS, GROUP * Q_BLOCK), F32),
        ],
        compiler_params=pltpu.CompilerParams(
            dimension_semantics=("arbitrary",), vmem_limit_bytes=VMEM_LIMIT),
        name="prompt_attn",
    )(kmax2, qi, kw, q, ki_lo, ki_hi, k_all, v_t)


SCORE_PAGE_SLOTS = 4
KV_PAGE_SLOTS = 3


def _prefetch_pages(copies, n_pages, n_slots, b, n_seq):
    depth = n_slots - 1

    def start(seq):
        for j in range(n_pages):
            for cp in copies(seq, seq % n_slots, j):
                cp.start()

    @pl.when(b == 0)
    def _():
        for s in range(depth):
            start(s)

    @pl.when(b + depth < n_seq)
    def _():
        start(b + depth)

    for j in range(n_pages):
        for cp in copies(b, b % n_slots, j):
            cp.wait()
    return b % n_slots


def _decode_scores_kernel(pt_ref, qi_ref, w_ref, kinew_ref, cache_ref, s_ref, buf, sem):
    b = pl.program_id(0)
    n_pages = pt_ref.shape[1]

    def copies(seq, slot, j):
        return (pltpu.make_async_copy(
            cache_ref.at[0, pt_ref[seq, j]], buf.at[slot, j], sem.at[slot]),)

    slot = _prefetch_pages(copies, n_pages, buf.shape[0], b, pl.num_programs(0))

    qi = qi_ref[...]
    w = w_ref[...] * (IDX_DIM ** -0.5)
    for j in range(n_pages):
        kpt = buf[slot, j].astype(BF16)
        d = jnp.dot(qi, kpt, preferred_element_type=F32)
        s_ref[:, j * PAGE_SIZE:(j + 1) * PAGE_SIZE] = jnp.sum(
            jnp.maximum(d, 0.0) * w, axis=0, keepdims=True)
    d_new = jnp.sum(qi.astype(F32) * kinew_ref[...].astype(F32), axis=-1, keepdims=True)
    s_new = jnp.sum(jnp.maximum(d_new, 0.0) * w, axis=0, keepdims=True)
    past = n_pages * PAGE_SIZE
    tail = s_ref.shape[1] - past
    lane = lax.broadcasted_iota(jnp.int32, (1, tail), 1)
    s_ref[:, past:] = jnp.where(lane == 0, s_new, NEG_INF)


def _decode_scores(page_table, qi_s, w_s, kinew_s, cache_kidx_t, width):
    db, n_pages = page_table.shape
    seq_spec = lambda a, c: pl.BlockSpec((None, a, c), lambda b, pt: (b, 0, 0))
    return pl.pallas_call(
        _decode_scores_kernel,
        grid_spec=pltpu.PrefetchScalarGridSpec(
            num_scalar_prefetch=1,
            grid=(db,),
            in_specs=[seq_spec(N_IDX_HEADS, IDX_DIM), seq_spec(N_IDX_HEADS, 1),
                      seq_spec(1, IDX_DIM), pl.BlockSpec(memory_space=pl.ANY)],
            out_specs=seq_spec(1, width),
            scratch_shapes=[pltpu.VMEM((SCORE_PAGE_SLOTS, n_pages, IDX_DIM, PAGE_SIZE), F32),
                            pltpu.SemaphoreType.DMA((SCORE_PAGE_SLOTS,))],
        ),
        out_shape=jax.ShapeDtypeStruct((db, 1, width), F32),
        compiler_params=pltpu.CompilerParams(dimension_semantics=("arbitrary",)),
        name="decode_scores",
    )(page_table, qi_s, w_s, kinew_s, cache_kidx_t)


def _decode_mask_kernel(s_in_ref, bias_ref, st_ref):
    nk = s_in_ref.shape[0]
    chunk = KEY_CHUNK
    s = s_in_ref[...]
    st_ref[...] = s
    valid = s > NEG_INF
    smin = jnp.min(jnp.where(valid, s, POS_INF), axis=0, keepdims=True)
    smax = jnp.max(s, axis=0, keepdims=True)
    n_valid = jnp.sum(valid.astype(F32), axis=0, keepdims=True)
    tau = _topk_threshold(st_ref, nk // chunk, chunk, n_valid, smin, smax)
    bias_ref[...] = jnp.where(st_ref[...] >= tau, 0.0, NEG)


def _decode_mask(scores_t):
    assert scores_t.shape[0] % KEY_CHUNK == 0 and scores_t.shape[1] == LANES
    return pl.pallas_call(
        _decode_mask_kernel,
        out_shape=jax.ShapeDtypeStruct(scores_t.shape, F32),
        scratch_shapes=[pltpu.VMEM(scores_t.shape, F32)],
        name="decode_mask",
    )(scores_t)


def _decode_attn_kernel(pt_ref, q_ref, bias_ref, knew_ref, vnew_ref, ck_ref, cv_ref, o_ref,
                        kbuf, vbuf, sem):
    b = pl.program_id(0)
    n_pages = pt_ref.shape[1]
    page_rows = N_KV_HEADS * PAGE_SIZE

    def copies(seq, slot, j):
        rows_j = pl.ds(j * page_rows, page_rows)
        page = pt_ref[seq, j]
        return (pltpu.make_async_copy(ck_ref.at[0, page], kbuf.at[slot, rows_j], sem.at[slot, 0]),
                pltpu.make_async_copy(cv_ref.at[0, page], vbuf.at[slot, rows_j], sem.at[slot, 1]))

    slot = _prefetch_pages(copies, n_pages, kbuf.shape[0], b, pl.num_programs(0))

    rows = n_pages * page_rows
    scale = HEAD_DIM ** -0.5
    q = q_ref[...]
    bias = jnp.concatenate(
        [jnp.broadcast_to(bias_ref[n:n + 1, :], (GROUP, bias_ref.shape[1]))
         for n in range(N_KV_HEADS)], axis=0)
    k_il = kbuf[slot].astype(BF16)
    v_il = vbuf[slot].astype(BF16)
    sc = lax.dot_general(q, k_il, CONTRACT_LAST, preferred_element_type=F32)
    sc = sc * scale + bias[:, :rows]
    sc_new = jnp.sum(q.astype(F32) * knew_ref[...].astype(F32), axis=-1, keepdims=True)
    sc_new = sc_new * scale + jnp.max(bias[:, rows:rows + N_KV_HEADS], axis=-1, keepdims=True)
    m = jnp.maximum(jnp.max(sc, axis=-1, keepdims=True), sc_new)
    p = jnp.exp(sc - m)
    p_new = jnp.exp(sc_new - m)
    denom = jnp.sum(p, axis=-1, keepdims=True) + p_new
    o = jnp.dot(p.astype(BF16), v_il, preferred_element_type=F32)
    o = o + p_new.astype(BF16).astype(F32) * vnew_ref[...].astype(F32)
    o_ref[...] = (o / denom).astype(o_ref.dtype)


def _decode_attention(page_table, q_s, bias_il, knew_s, vnew_s, cache_k, cache_v):
    db, n_pages = page_table.shape
    width2 = bias_il.shape[-1]
    rows = n_pages * N_KV_HEADS * PAGE_SIZE
    seq_spec = lambda a, c: pl.BlockSpec((None, a, c), lambda b, pt: (b, 0, 0))
    return pl.pallas_call(
        _decode_attn_kernel,
        grid_spec=pltpu.PrefetchScalarGridSpec(
            num_scalar_prefetch=1,
            grid=(db,),
            in_specs=[seq_spec(N_HEADS, HEAD_DIM), seq_spec(N_KV_HEADS, width2),
                      seq_spec(N_HEADS, HEAD_DIM), seq_spec(N_HEADS, HEAD_DIM),
                      pl.BlockSpec(memory_space=pl.ANY), pl.BlockSpec(memory_space=pl.ANY)],
            out_specs=seq_spec(N_HEADS, HEAD_DIM),
            scratch_shapes=[pltpu.VMEM((KV_PAGE_SLOTS, rows, HEAD_DIM), F32),
                            pltpu.VMEM((KV_PAGE_SLOTS, rows, HEAD_DIM), F32),
                            pltpu.SemaphoreType.DMA((KV_PAGE_SLOTS, 2))],
        ),
        out_shape=jax.ShapeDtypeStruct((db, N_HEADS, HEAD_DIM), BF16),
        compiler_params=pltpu.CompilerParams(
            dimension_semantics=("arbitrary",), vmem_limit_bytes=VMEM_LIMIT),
        name="decode_attn",
    )(page_table, q_s, bias_il, knew_s, vnew_s, cache_k, cache_v)


def _pool_decode_kernel(hist_ref, u_ref, d_ref):
    for g, w in enumerate(POOL_WINDOWS):
        cols = slice(g * POOL_GROUP_WIDTH, (g + 1) * POOL_GROUP_WIDTH)
        cur = u_ref[:, cols]
        tot = cur
        for j in range(1, w):
            tot = tot + hist_ref[POOL_HIST - j, :, cols]
        d_ref[:, cols] = tot * (1.0 / w) - cur


def _pool_decode(hist, u):
    return pl.pallas_call(
        _pool_decode_kernel,
        out_shape=jax.ShapeDtypeStruct(u.shape, F32),
        name="pool_decode",
    )(hist, u)


FF_TILE = 1024


def _rmsnorm(x, g):
    ms = jnp.mean(x * x, axis=-1, keepdims=True)
    return (x * lax.rsqrt(ms + EPS)) * g


def _post_kernel(x_ref, att_ref, d_ref, woa_ref, wop_ref, wpool_ref, pscale_ref,
                 gmlp_ref, gfin_ref, wup_ref, wdn_ref, y_ref, hn_ref):
    f = pl.program_id(1)

    @pl.when(f == 0)
    def _():
        d = d_ref[...]
        pooled = jnp.concatenate(
            [jnp.dot(d[:, g * POOL_GROUP_WIDTH:(g + 1) * POOL_GROUP_WIDTH].astype(BF16),
                     wpool_ref[g], preferred_element_type=F32)
             for g in range(len(POOL_WINDOWS))], axis=1) * pscale_ref[...]
        h = (x_ref[...]
             + jnp.dot(att_ref[...], woa_ref[...], preferred_element_type=F32)
             + jnp.dot(pooled.astype(BF16), wop_ref[...], preferred_element_type=F32))
        y_ref[...] = h
        hn_ref[...] = _rmsnorm(h, gmlp_ref[...]).astype(BF16)

    a = jnp.maximum(jnp.dot(hn_ref[...], wup_ref[...], preferred_element_type=F32), 0.0)
    y_ref[...] += jnp.dot((a * a).astype(BF16), wdn_ref[...], preferred_element_type=F32)

    @pl.when(f == pl.num_programs(1) - 1)
    def _():
        y_ref[...] = _rmsnorm(y_ref[...], gfin_ref[...])


def _post(x, att, d, wo_a, wo_p, w_pool, pool_scale, g_mlp, g_final, w_up, w_down, tm):
    rows = x.shape[0]
    assert rows % tm == 0 and D_FF % FF_TILE == 0
    row_spec = lambda w: pl.BlockSpec((tm, w), lambda i, f: (i, 0))
    return pl.pallas_call(
        _post_kernel,
        grid=(rows // tm, D_FF // FF_TILE),
        in_specs=[row_spec(D_MODEL), row_spec(ATTN_WIDTH), row_spec(POOL_WIDTH),
                  _const_spec(wo_a.shape), _const_spec(wo_p.shape), _const_spec(w_pool.shape),
                  _const_spec((1, POOL_WIDTH)), _const_spec((1, D_MODEL)), _const_spec((1, D_MODEL)),
                  pl.BlockSpec((D_MODEL, FF_TILE), lambda i, f: (0, f)),
                  pl.BlockSpec((FF_TILE, D_MODEL), lambda i, f: (f, 0))],
        out_specs=row_spec(D_MODEL),
        out_shape=jax.ShapeDtypeStruct((rows, D_MODEL), F32),
        scratch_shapes=[pltpu.VMEM((tm, D_MODEL), BF16)],
        compiler_params=pltpu.CompilerParams(
            dimension_semantics=("arbitrary", "arbitrary"), vmem_limit_bytes=VMEM_LIMIT),
        name="post",
    )(x, att, d, wo_a, wo_p, w_pool, pool_scale.reshape(1, POOL_WIDTH),
      g_mlp.reshape(1, D_MODEL), g_final.reshape(1, D_MODEL), w_up, w_down)


def kernel(x_prompt, x_sample, cache_k, cache_v, cache_kidx, state_pool, page_table, meta_tokens,
           g_mix, w_in, w_pool, pool_scale, w_out, g_mlp, w_up, w_down, g_final):
    assert x_prompt.shape[0] == 1 and x_sample.shape[1] == 1 and g_mix.shape[0] == 1
    seq = x_prompt.shape[1]
    db = x_sample.shape[0]
    n_pages = page_table.shape[1]
    past = n_pages * PAGE_SIZE
    assert db == LANES

    w = w_in[0]
    o = 0
    parts = []
    for width in (ATTN_WIDTH, 2 * N_KV_HEADS * HEAD_DIM, N_IDX_HEADS * IDX_DIM,
                  IDX_DIM + N_IDX_HEADS, POOL_WIDTH):
        parts.append(w[:, o:o + width].astype(BF16))
        o += width
    parts[3] = jnp.pad(parts[3], ((0, 0), (0, LANES - parts[3].shape[1])))
    wo_a = w_out[0, :ATTN_WIDTH].astype(BF16)
    wo_p = w_out[0, ATTN_WIDTH:].astype(BF16)
    w_pool_b = w_pool[0].astype(BF16)
    w_up_b = w_up[0].astype(BF16)
    w_down_b = w_down[0].astype(BF16)

    xs = jnp.concatenate([meta_tokens.astype(F32), x_sample[:, 0]], axis=0)
    pos_s = jnp.concatenate([jnp.arange(N_META, dtype=jnp.int32),
                             jnp.full((db,), past, jnp.int32)])
    q_s, kf_s, vf_s, kb_s, vb_s, qi_s, kw_s, u_s = _project(
        xs, jnp.zeros((1,), jnp.int32), pos_s, g_mix[0], parts)
    xp = x_prompt[0]
    assert seq % PROJ_TILE == 0
    tile_base = N_META + PROJ_TILE * jnp.arange(seq // PROJ_TILE, dtype=jnp.int32)
    q_p, kf_p, vf_p, kb_p, vb_p, qi_p, kw_p, d_p, u_tail = _project(
        xp, tile_base, jnp.arange(PROJ_TILE, dtype=jnp.int32), g_mix[0], parts, halo=u_s[:N_META])

    nk = -(-(seq + 2 * Q_BLOCK) // KEY_CHUNK) * KEY_CHUNK

    def key_rows(meta_part, prompt_part):
        width = meta_part.shape[1]
        return jnp.concatenate(
            [meta_part, jnp.zeros((Q_BLOCK - N_META, width), meta_part.dtype), prompt_part,
             jnp.zeros((nk - Q_BLOCK - seq, width), meta_part.dtype)], axis=0)

    ki_all = key_rows(kw_s[:N_META, :IDX_DIM], kw_p[:, :IDX_DIM]).astype(BF16)
    zk = jnp.zeros_like(ki_all)
    ki_lo = jnp.concatenate([ki_all, zk], axis=1)
    ki_hi = jnp.concatenate([zk, ki_all], axis=1)
    k_all = key_rows(kb_s[:N_META], kb_p)
    v_all = key_rows(vb_s[:N_META], vb_p)
    v_t = jnp.transpose(
        v_all.reshape(nk // KEY_CHUNK, KEY_CHUNK, N_KV_HEADS, HEAD_DIM), (0, 2, 3, 1))
    v_t = jnp.concatenate(
        [v_t, jnp.ones((nk // KEY_CHUNK, N_KV_HEADS, ONES_ROWS, KEY_CHUNK), BF16)], axis=2)
    knorm = lambda kw: kw[:, KNORM_LANE:KNORM_LANE + N_KV_HEADS]
    kmax2 = jnp.maximum(jnp.max(knorm(kw_p), axis=0), jnp.max(knorm(kw_s[:N_META]), axis=0))
    att_p = _prompt_attention(kmax2, qi_p, kw_p, q_p, ki_lo, ki_hi, k_all, v_t)

    width = -(-(past + 1) // KEY_CHUNK) * KEY_CHUNK
    qi_d = qi_s[N_META:].reshape(db, N_IDX_HEADS, IDX_DIM)
    w_d = kw_s[N_META:, IDX_DIM:IDX_DIM + N_IDX_HEADS].reshape(db, N_IDX_HEADS, 1)
    kinew = kw_s[N_META:, :IDX_DIM].astype(BF16).reshape(db, 1, IDX_DIM)
    scores = _decode_scores(page_table, qi_d, w_d, kinew, jnp.swapaxes(cache_kidx, 2, 3), width)
    bias_t = _decode_mask(jnp.transpose(scores.reshape(db, width)))
    bias = jnp.transpose(bias_t)
    negs = jnp.full_like(bias, NEG)
    bias_il = jnp.stack(
        [jnp.stack([bias, negs], axis=-1), jnp.stack([negs, bias], axis=-1)], axis=1,
    ).reshape(db, N_KV_HEADS, N_KV_HEADS * width)
    n_phys = cache_k.shape[1]
    il_shape = (1, n_phys, N_KV_HEADS * PAGE_SIZE, HEAD_DIM)
    per_head = lambda a: jnp.repeat(a.reshape(db, N_KV_HEADS, HEAD_DIM), GROUP, axis=1)
    att_s = _decode_attention(
        page_table, q_s[N_META:].reshape(db, N_HEADS, HEAD_DIM), bias_il,
        per_head(kb_s[N_META:]), per_head(vb_s[N_META:]),
        cache_k.reshape(il_shape), cache_v.reshape(il_shape))
    att_s = att_s.reshape(db, ATTN_WIDTH)

    d_s = _pool_decode(jnp.swapaxes(state_pool[0], 0, 1), u_s[N_META:])

    post = functools.partial(
        _post, wo_a=wo_a, wo_p=wo_p, w_pool=w_pool_b, pool_scale=pool_scale[0],
        g_mlp=g_mlp[0], g_final=g_final, w_up=w_up_b, w_down=w_down_b)
    y_p = post(xp, att_p, d_p, tm=512)
    y_s = post(x_sample[:, 0], att_s, d_s, tm=db)

    k_prompt = jnp.concatenate([kf_s[:N_META], kf_p], axis=0).reshape(1, 1, N_META + seq, N_KV_HEADS, HEAD_DIM)
    v_prompt = jnp.concatenate([vf_s[:N_META], vf_p], axis=0).reshape(1, 1, N_META + seq, N_KV_HEADS, HEAD_DIM)
    kidx_prompt = jnp.concatenate([kw_s[:N_META, :IDX_DIM], kw_p[:, :IDX_DIM]], axis=0).reshape(
        1, 1, N_META + seq, IDX_DIM)
    pool_prompt = u_tail[POOL_HALO - POOL_HIST:].reshape(1, 1, POOL_HIST, POOL_WIDTH)
    k_sample = kf_s[N_META:].reshape(1, db, 1, N_KV_HEADS, HEAD_DIM)
    v_sample = vf_s[N_META:].reshape(1, db, 1, N_KV_HEADS, HEAD_DIM)
    kidx_sample = kw_s[N_META:, :IDX_DIM].reshape(1, db, 1, IDX_DIM)
    pool_sample = jnp.concatenate(
        [state_pool[0][:, 1:], u_s[N_META:][:, None, :]], axis=1)[None]
    return (y_p.reshape(1, seq, D_MODEL), y_s.reshape(db, 1, D_MODEL),
            k_prompt, v_prompt, kidx_prompt, pool_prompt,
            k_sample, v_sample, kidx_sample, pool_sample)
```

```python
import functools

import jax
import jax.numpy as jnp
from jax import lax
from jax.experimental import pallas as pl
from jax.experimental.pallas import tpu as pltpu

F32 = jnp.float32
BF16 = jnp.bfloat16

D_MODEL = 2048
N_META = 16
ATTN_WIDTH = 1024
POOL_WIDTH = 1024
HEAD_DIM = 128
N_HEADS = 8
N_KV_HEADS = 2
GROUP = N_HEADS // N_KV_HEADS
N_IDX_HEADS = 16
IDX_DIM = 64
TOPK = 256
POOL_WINDOWS = (2, 4, 8, 16)
POOL_GROUP_WIDTH = 256
POOL_HIST = 15
D_FF = 8192
PAGE_SIZE = 128
ROPE_THETA = 10000.0
EPS = 1e-6
NEG = -1e30
NEG_INF = float("-inf")
POS_INF = float("inf")
F32_LOWEST = -3.0e38

LANES = 128
SUBLANES = 8
VMEM_LIMIT = 56 * 1024 * 1024

Q_BLOCK = 128
KEY_CHUNK = 512
MAX_BISECT = 320
LOG2E = 1.4426950408889634
MAX_EXP2_GAP = 80.0
NORM_BOUND_SLACK = 1.02
ONES_ROWS = 16
KNORM_LANE = IDX_DIM + N_IDX_HEADS
PROJ_TILE = 256
POOL_HALO = 16

CONTRACT_LAST = (((1,), (1,)), ((), ()))


def _const_spec(shape):
    nd = len(shape)
    return pl.BlockSpec(shape, lambda *_: (0,) * nd, pipeline_mode=pl.Buffered(1))


def _rope128(x, c, s):
    return x * c + pltpu.roll(x, 64, 1) * s


def _rope64(x, c, sa, sb):
    return x * c + pltpu.roll(x, 96, 1) * sa + pltpu.roll(x, 32, 1) * sb


def _angle_sum(cb_ref, sb_ref, cr_ref, sr_ref):
    cb, sb, cr, sr = cb_ref[...], sb_ref[...], cr_ref[...], sr_ref[...]
    return cb * cr - sb * sr, sb * cr + cb * sr


def _proj_kernel(x_ref, g_ref, cb128_ref, sb128_ref, cb64_ref, sb64_ref,
                 cr128_ref, sr128_ref, cr64_ref, sr64_ref,
                 wq_ref, wkv_ref, wqi_ref, wkw_ref, wu_ref, *rest, with_pool):
    if with_pool:
        (halo_ref, q_ref, kf_ref, vf_ref, kb_ref, vb_ref, qi_ref, kw_ref, u_ref, utail_ref,
         ubuf) = rest
    else:
        q_ref, kf_ref, vf_ref, kb_ref, vb_ref, qi_ref, kw_ref, u_ref = rest
    x = x_ref[...]
    ms = jnp.mean(x * x, axis=-1, keepdims=True)
    xn = ((x * lax.rsqrt(ms + EPS)) * g_ref[...]).astype(BF16)
    tm = x.shape[0]
    lane_t = lax.broadcasted_iota(jnp.int32, (tm, LANES), 1)
    c128, sfull = _angle_sum(cb128_ref, sb128_ref, cr128_ref, sr128_ref)
    s128 = jnp.where(lane_t < HEAD_DIM // 2, -sfull, sfull)
    c64, sfull = _angle_sum(cb64_ref, sb64_ref, cr64_ref, sr64_ref)
    first_half = jnp.bitwise_and(lane_t, IDX_DIM - 1) < IDX_DIM // 2
    sa64 = jnp.where(first_half, -sfull, 0.0)
    sb64 = jnp.where(first_half, 0.0, sfull)

    zq = jnp.dot(xn, wq_ref[...], preferred_element_type=F32)
    for h in range(N_HEADS):
        sl = slice(h * LANES, (h + 1) * LANES)
        q_ref[:, sl] = _rope128(zq[:, sl], c128, s128).astype(BF16)

    zkv = jnp.dot(xn, wkv_ref[...], preferred_element_type=F32)
    k_norm2 = []
    for h in range(N_KV_HEADS):
        sl = slice(h * LANES, (h + 1) * LANES)
        kr = _rope128(zkv[:, sl], c128, s128)
        kf_ref[:, sl] = kr
        kb = kr.astype(BF16)
        kb_ref[:, sl] = kb
        k_norm2.append(jnp.sum(kb.astype(F32) * kb.astype(F32), axis=-1, keepdims=True))
    v = zkv[:, N_KV_HEADS * LANES:]
    vf_ref[...] = v
    vb_ref[...] = v.astype(BF16)

    zqi = jnp.dot(xn, wqi_ref[...], preferred_element_type=F32)
    for p in range(N_IDX_HEADS // 2):
        sl = slice(p * LANES, (p + 1) * LANES)
        qi_ref[:, sl] = _rope64(zqi[:, sl], c64, sa64, sb64).astype(BF16)

    zkw = jnp.dot(xn, wkw_ref[...], preferred_element_type=F32)
    lane = lax.broadcasted_iota(jnp.int32, zkw.shape, 1)
    is_key = lane < IDX_DIM
    ckw = jnp.where(is_key, c64, jnp.where(lane < IDX_DIM + N_IDX_HEADS, N_IDX_HEADS ** -0.5, 1.0))
    kw = _rope64(zkw, ckw, jnp.where(is_key, sa64, 0.0), jnp.where(is_key, sb64, 0.0))
    for h in range(N_KV_HEADS):
        kw = jnp.where(lane == KNORM_LANE + h, k_norm2[h], kw)
    kw_ref[...] = kw

    u = jnp.dot(xn, wu_ref[...], preferred_element_type=F32)
    if not with_pool:
        u_ref[...] = u
        return

    @pl.when(pl.program_id(0) == 0)
    def _():
        ubuf[0:POOL_HALO, :] = halo_ref[...]

    ubuf[POOL_HALO:, :] = u
    for g, w in enumerate(POOL_WINDOWS):
        cols = slice(g * POOL_GROUP_WIDTH, (g + 1) * POOL_GROUP_WIDTH)
        cur = ubuf[POOL_HALO:POOL_HALO + tm, cols]
        tot = cur
        for j in range(1, w):
            tot = tot + ubuf[POOL_HALO - j:POOL_HALO - j + tm, cols]
        u_ref[:, cols] = tot * (1.0 / w) - cur
    tail = ubuf[tm:tm + POOL_HALO, :]
    ubuf[0:POOL_HALO, :] = tail
    utail_ref[...] = tail


def _project(x, pos_base, pos_rel, g_mix, w_parts, halo=None):
    tm = pos_rel.shape[0]
    n_tiles = pos_base.shape[0]
    rows = x.shape[0]
    assert rows == tm * n_tiles
    with_pool = halo is not None
    base_tabs = [t.reshape(n_tiles, 1, LANES) for t in _rope_tables(pos_base)]
    rel_tabs = _rope_tables(pos_rel)
    row_spec = lambda w: pl.BlockSpec((tm, w), lambda i: (i, 0))
    base_spec = pl.BlockSpec((None, 1, LANES), lambda i: (i, 0, 0))
    out_shapes = [
        jax.ShapeDtypeStruct((rows, ATTN_WIDTH), BF16),
        jax.ShapeDtypeStruct((rows, 256), F32),
        jax.ShapeDtypeStruct((rows, 256), F32),
        jax.ShapeDtypeStruct((rows, 256), BF16),
        jax.ShapeDtypeStruct((rows, 256), BF16),
        jax.ShapeDtypeStruct((rows, 1024), BF16),
        jax.ShapeDtypeStruct((rows, LANES), F32),
        jax.ShapeDtypeStruct((rows, POOL_WIDTH), F32),
    ]
    out_specs = [row_spec(s.shape[1]) for s in out_shapes]
    in_specs = ([row_spec(D_MODEL), _const_spec((1, D_MODEL))] + [base_spec] * 4
                + [_const_spec((tm, LANES))] * 4 + [_const_spec(w.shape) for w in w_parts])
    args = [x, g_mix.reshape(1, D_MODEL), *base_tabs, *rel_tabs, *w_parts]
    scratch = []
    if with_pool:
        in_specs.append(_const_spec((POOL_HALO, POOL_WIDTH)))
        args.append(halo)
        out_shapes.append(jax.ShapeDtypeStruct((POOL_HALO, POOL_WIDTH), F32))
        out_specs.append(pl.BlockSpec((POOL_HALO, POOL_WIDTH), lambda i: (0, 0)))
        scratch.append(pltpu.VMEM((tm + POOL_HALO, POOL_WIDTH), F32))
    return pl.pallas_call(
        functools.partial(_proj_kernel, with_pool=with_pool),
        grid=(n_tiles,),
        in_specs=in_specs,
        out_specs=tuple(out_specs),
        out_shape=tuple(out_shapes),
        scratch_shapes=scratch,
        compiler_params=pltpu.CompilerParams(
            dimension_semantics=("arbitrary",), vmem_limit_bytes=VMEM_LIMIT),
        name="proj",
    )(*args)


def _rope_tables(pos):
    posf = pos.astype(F32)[:, None]
    out = []
    for dim in (HEAD_DIM, IDX_DIM):
        inv = ROPE_THETA ** (-jnp.arange(0, dim, 2, dtype=F32) / dim)
        ang = posf * inv[None, :]
        reps = LANES // (dim // 2)
        out += [jnp.tile(jnp.cos(ang), (1, reps)), jnp.tile(jnp.sin(ang), (1, reps))]
    return out


def _key_count(st_ref, nchunks, chunk, pred):
    acc_rows = 8 * SUBLANES

    def body(c, part):
        c0 = pl.multiple_of(c * chunk, chunk)
        hit = pred(st_ref[pl.ds(c0, chunk), :], c0)
        for g in range(chunk // acc_rows):
            part = jnp.where(hit[g * acc_rows:(g + 1) * acc_rows], part + 1.0, part)
        return part

    part = lax.fori_loop(0, nchunks, body, jnp.zeros((acc_rows, LANES), F32))
    return jnp.sum(part, axis=0, keepdims=True)


def _topk_threshold(st_ref, nchunks, chunk, n_valid, s_min, s_max):
    kf = float(TOPK)
    take_all = n_valid <= kf

    def count_ge(t):
        return _key_count(st_ref, nchunks, chunk, lambda v, c0: v >= t)

    def cond(st):
        it, done = st[0], st[3]
        return jnp.logical_and(it < MAX_BISECT, jnp.min(done) < 0.5)

    def step(st):
        lo, hi, done, tau = st
        mid = lo + (hi - lo) * 0.5
        collapsed = jnp.logical_or(mid <= lo, mid >= hi)
        cnt = count_ge(mid)
        found = cnt == kf
        active = done < 0.5
        lo = jnp.where(jnp.logical_and(active, cnt > kf), mid, lo)
        hi = jnp.where(jnp.logical_and(active, cnt < kf), mid, hi)
        tau = jnp.where(jnp.logical_and(active, found), mid, tau)
        done = jnp.where(jnp.logical_or(found, collapsed), 1.0, done)
        return lo, hi, done, tau

    def body(st):
        return (st[0] + 1,) + step(step(st[1:]))

    done0 = jnp.where(take_all, 1.0, 0.0).astype(F32)
    tau0 = jnp.full((1, LANES), POS_INF, F32)
    _, lo, hi, _, tau = lax.while_loop(
        cond, body, (jnp.int32(0), s_min, s_max, done0, tau0))

    unresolved = jnp.logical_and(tau == POS_INF, jnp.logical_not(take_all))
    n_unres = jnp.sum(unresolved.astype(F32))
    tau = jnp.where(take_all, F32_LOWEST, tau)

    def tie_path(tau):
        cut = jnp.where(count_ge(hi) >= kf, hi, lo)
        cut = jnp.where(unresolved, cut, POS_INF)
        c_gt = _key_count(st_ref, nchunks, chunk, lambda v, c0: v > cut)
        keep = kf - c_gt

        def key_iota(c0):
            return c0 + lax.broadcasted_iota(jnp.int32, (chunk, LANES), 0)

        def idx_body(_, st):
            xlo, xhi = st
            xmid = (xlo + xhi) // 2
            cnt = _key_count(
                st_ref, nchunks, chunk,
                lambda v, c0: jnp.logical_and(v == cut, key_iota(c0) < xmid))
            ok = cnt >= keep
            return jnp.where(ok, xlo, xmid), jnp.where(ok, xmid, xhi)

        xlo0 = jnp.zeros((1, LANES), jnp.int32)
        xhi0 = jnp.full((1, LANES), nchunks * chunk, jnp.int32)
        n_idx_steps = 15
        _, xcut = lax.fori_loop(0, n_idx_steps, idx_body, (xlo0, xhi0))

        def drop_body(c, carry):
            c0 = pl.multiple_of(c * chunk, chunk)
            v = st_ref[pl.ds(c0, chunk), :]
            drop = jnp.logical_and(v == cut, key_iota(c0) >= xcut)
            st_ref[pl.ds(c0, chunk), :] = jnp.where(drop, NEG_INF, v)
            return carry

        lax.fori_loop(0, nchunks, drop_body, 0)
        return jnp.where(unresolved, cut, tau)

    return lax.cond(n_unres > 0.0, tie_path, lambda t: t, tau)


def _prompt_attn_kernel(kmax2_ref, qi_ref, kw_ref, q_ref, kilo_ref, kihi_ref, k_ref, vt_ref, o_ref,
                        st_ref, m_ref, l_ref, acc_ref, accf_ref):
    i = pl.program_id(0)
    tq, ck = Q_BLOCK, KEY_CHUNK
    n_blocks = i + 2
    nchunks = (n_blocks * Q_BLOCK + ck - 1) // ck

    wt = jnp.transpose(kw_ref[...])[IDX_DIM:IDX_DIM + N_IDX_HEADS, :] * (IDX_DIM ** -0.5)
    qcat = jnp.concatenate(
        [qi_ref[:, p * LANES:(p + 1) * LANES] for p in range(N_IDX_HEADS // 2)], axis=0)
    key = lax.broadcasted_iota(jnp.int32, (ck, tq), 0)
    qry = i * tq + lax.broadcasted_iota(jnp.int32, (ck, tq), 1)

    def idx_chunk(c, st):
        smin, smax = st
        c0 = pl.multiple_of(c * ck, ck)
        klo = kilo_ref[pl.ds(c0, ck), :]
        khi = kihi_ref[pl.ds(c0, ck), :]
        acc = jnp.zeros((ck, tq), F32)
        for t in range(N_IDX_HEADS // 4):
            rhs = qcat[2 * t * tq:(2 * t + 2) * tq]
            dlo = lax.dot_general(klo, rhs, CONTRACT_LAST, preferred_element_type=F32)
            dhi = lax.dot_general(khi, rhs, CONTRACT_LAST, preferred_element_type=F32)
            for j in range(2):
                p = 2 * t + j
                cs = slice(j * tq, (j + 1) * tq)
                acc = acc + wt[2 * p:2 * p + 1, :] * jnp.maximum(dlo[:, cs], 0.0)
                acc = acc + wt[2 * p + 1:2 * p + 2, :] * jnp.maximum(dhi[:, cs], 0.0)
        col = c0 + key
        valid = jnp.logical_or(
            col < N_META,
            jnp.logical_and(col >= Q_BLOCK, col - Q_BLOCK <= qry))
        st_ref[pl.ds(c0, ck), :] = jnp.where(valid, acc, NEG_INF)
        smin = jnp.minimum(smin, jnp.min(jnp.where(valid, acc, POS_INF), axis=0, keepdims=True))
        smax = jnp.maximum(smax, jnp.max(jnp.where(valid, acc, NEG_INF), axis=0, keepdims=True))
        return smin, smax

    st = lax.fori_loop(
        0, nchunks // 2, lambda j, st: idx_chunk(2 * j + 1, idx_chunk(2 * j, st)),
        (jnp.full((1, tq), POS_INF, F32), jnp.full((1, tq), NEG_INF, F32)))
    smin, smax = lax.cond(nchunks % 2 == 1, lambda st: idx_chunk(nchunks - 1, st), lambda st: st, st)

    n_valid = (N_META + 1 + i * tq + lax.broadcasted_iota(jnp.int32, (1, tq), 1)).astype(F32)
    tau = _topk_threshold(st_ref, nchunks, ck, n_valid, smin, smax)

    scale = HEAD_DIM ** -0.5

    def q_group(n):
        return jnp.concatenate(
            [q_ref[:, (n * GROUP + g) * LANES:(n * GROUP + g + 1) * LANES]
             for g in range(GROUP)], axis=0)

    def k_chunk(c0, n):
        return k_ref[pl.ds(c0, ck), n * LANES:(n + 1) * LANES]

    def write_out(n, ot):
        for g in range(GROUP):
            h = n * GROUP + g
            o_ref[:, h * LANES:(h + 1) * LANES] = jnp.transpose(
                ot[:, g * tq:(g + 1) * tq]).astype(o_ref.dtype)

    def ref_body(c, mx):
        c0 = pl.multiple_of(c * ck, ck)
        sel = st_ref[pl.ds(c0, ck), :] >= tau
        out = []
        for n in range(N_KV_HEADS):
            raw = lax.dot_general(k_chunk(c0, n), q_group(n), CONTRACT_LAST,
                                  preferred_element_type=F32)
            cur = jnp.concatenate(
                [jnp.max(jnp.where(sel, raw[:, g * tq:(g + 1) * tq], NEG), axis=0, keepdims=True)
                 for g in range(GROUP)], axis=1)
            out.append(jnp.maximum(mx[n], cur))
        return tuple(out)

    mx = ref_body(0, tuple(jnp.full((1, GROUP * tq), NEG, F32) for _ in range(N_KV_HEADS)))
    c2 = scale * LOG2E
    ones = jnp.ones((SUBLANES, HEAD_DIM), BF16)
    gap = jnp.float32(0.0)
    for n in range(N_KV_HEADS):
        qf = q_group(n).astype(F32)
        qn2 = lax.dot_general(ones, (qf * qf).astype(BF16), CONTRACT_LAST,
                              preferred_element_type=F32)[0:1]
        bound = jnp.sqrt(qn2 * kmax2_ref[n]) * (c2 * NORM_BOUND_SLACK)
        gap = jnp.maximum(gap, jnp.max(bound - mx[n] * c2))

    def fixed_reference_path():
        accf_ref[...] = jnp.zeros(accf_ref.shape, F32)

        def chunk_update(c, n):
            c0 = pl.multiple_of(c * ck, ck)
            sel = st_ref[pl.ds(c0, ck), :] >= tau
            raw = lax.dot_general(k_chunk(c0, n), q_group(n), CONTRACT_LAST,
                                  preferred_element_type=F32)
            neg_ref = mx[n] * (-c2)
            p = jnp.concatenate(
                [jnp.exp2(raw[:, g * tq:(g + 1) * tq] * c2
                          + jnp.where(sel, neg_ref[:, g * tq:(g + 1) * tq], NEG)).astype(BF16)
                 for g in range(GROUP)], axis=1)
            return jnp.dot(vt_ref[c, n], p, preferred_element_type=F32)

        def pair_body(j, carry):
            for n in range(N_KV_HEADS):
                accf_ref[n] += chunk_update(2 * j, n) + chunk_update(2 * j + 1, n)
            return carry

        lax.fori_loop(0, nchunks // 2, pair_body, 0)

        @pl.when(nchunks % 2 == 1)
        def _():
            for n in range(N_KV_HEADS):
                accf_ref[n] += chunk_update(nchunks - 1, n)

        for n in range(N_KV_HEADS):
            write_out(n, accf_ref[n, 0:HEAD_DIM, :] / accf_ref[n, HEAD_DIM:HEAD_DIM + 1, :])

    def running_max_path():
        m_ref[...] = jnp.full(m_ref.shape, NEG, F32)
        l_ref[...] = jnp.zeros(l_ref.shape, F32)
        acc_ref[...] = jnp.zeros(acc_ref.shape, F32)

        def body(c, carry):
            c0 = pl.multiple_of(c * ck, ck)
            bias = jnp.where(st_ref[pl.ds(c0, ck), :] >= tau, 0.0, NEG)
            bias = jnp.concatenate([bias] * GROUP, axis=1)
            for n in range(N_KV_HEADS):
                sc = lax.dot_general(k_chunk(c0, n), q_group(n), CONTRACT_LAST,
                                     preferred_element_type=F32)
                sc = sc * scale + bias
                m_old = m_ref[n]
                m_new = jnp.maximum(m_old, jnp.max(sc, axis=0, keepdims=True))
                alpha = jnp.exp(m_old - m_new)
                p = jnp.exp(sc - m_new)
                l_ref[n] = alpha * l_ref[n] + jnp.sum(
                    p.reshape(ck // SUBLANES, SUBLANES, GROUP * tq), axis=0)
                acc_ref[n] = alpha * acc_ref[n] + jnp.dot(
                    vt_ref[c, n, 0:HEAD_DIM, :], p.astype(BF16), preferred_element_type=F32)
                m_ref[n] = m_new
            return carry

        lax.fori_loop(0, nchunks, body, 0)
        for n in range(N_KV_HEADS):
            write_out(n, acc_ref[n] / jnp.sum(l_ref[n], axis=0, keepdims=True))

    lax.cond(gap <= MAX_EXP2_GAP, fixed_reference_path, running_max_path)


def _prompt_attention(kmax2, qi, kw, q, ki_lo, ki_hi, k_all, v_t):
    rows = q.shape[0]
    nk = k_all.shape[0]
    assert rows % Q_BLOCK == 0 and nk % KEY_CHUNK == 0
    assert nk >= (rows // Q_BLOCK + 1) * Q_BLOCK
    assert v_t.shape == (nk // KEY_CHUNK, N_KV_HEADS, HEAD_DIM + ONES_ROWS, KEY_CHUNK)
    row_spec = lambda w: pl.BlockSpec((Q_BLOCK, w), lambda i: (i, 0))
    return pl.pallas_call(
        _prompt_attn_kernel,
        grid=(rows // Q_BLOCK,),
        in_specs=[pl.BlockSpec(memory_space=pltpu.SMEM),
                  row_spec(1024), row_spec(LANES), row_spec(ATTN_WIDTH),
                  _const_spec(ki_lo.shape), _const_spec(ki_hi.shape),
                  _const_spec(k_all.shape), _const_spec(v_t.shape)],
        out_specs=row_spec(ATTN_WIDTH),
        out_shape=jax.ShapeDtypeStruct((rows, ATTN_WIDTH), BF16),
        scratch_shapes=[
            pltpu.VMEM((nk, Q_BLOCK), F32),
            pltpu.VMEM((N_KV_HEADS, 1, GROUP * Q_BLOCK), F32),
            pltpu.VMEM((N_KV_HEADS, SUBLANES, GROUP * Q_BLOCK), F32),
            pltpu.VMEM((N_KV_HEADS, HEAD_DIM, GROUP * Q_BLOCK), F32),
            pltpu.VMEM((N_KV_HEADS, HEAD_DIM + ONES_ROWS, GROUP * Q_BLOCK), F32),
        ],
        compiler_params=pltpu.CompilerParams(
            dimension_semantics=("arbitrary",), vmem_limit_bytes=VMEM_LIMIT),
        name="prompt_attn",
    )(kmax2, qi, kw, q, ki_lo, ki_hi, k_all, v_t)


SCORE_PAGE_SLOTS = 8
KV_PAGE_SLOTS = 4


def _prefetch_pages(copies, n_pages, n_slots, b, n_seq):
    depth = n_slots - 1

    def start(seq):
        for j in range(n_pages):
            for cp in copies(seq, seq % n_slots, j):
                cp.start()

    @pl.when(b == 0)
    def _():
        for s in range(depth):
            start(s)

    @pl.when(b + depth < n_seq)
    def _():
        start(b + depth)

    for j in range(n_pages):
        for cp in copies(b, b % n_slots, j):
            cp.wait()
    return b % n_slots


def _decode_scores_kernel(pt_ref, qi_ref, w_ref, kinew_ref, cache_ref, s_ref, buf, sem):
    b = pl.program_id(0)
    n_pages = pt_ref.shape[1]

    def copies(seq, slot, j):
        return (pltpu.make_async_copy(
            cache_ref.at[0, pt_ref[seq, j]], buf.at[slot, j], sem.at[slot]),)

    slot = _prefetch_pages(copies, n_pages, buf.shape[0], b, pl.num_programs(0))

    qi = qi_ref[...]
    w = w_ref[...] * (IDX_DIM ** -0.5)
    for j in range(n_pages):
        kpt = buf[slot, j].astype(BF16)
        d = jnp.dot(qi, kpt, preferred_element_type=F32)
        s_ref[:, j * PAGE_SIZE:(j + 1) * PAGE_SIZE] = jnp.sum(
            jnp.maximum(d, 0.0) * w, axis=0, keepdims=True)
    d_new = jnp.sum(qi.astype(F32) * kinew_ref[...].astype(F32), axis=-1, keepdims=True)
    s_new = jnp.sum(jnp.maximum(d_new, 0.0) * w, axis=0, keepdims=True)
    past = n_pages * PAGE_SIZE
    tail = s_ref.shape[1] - past
    lane = lax.broadcasted_iota(jnp.int32, (1, tail), 1)
    s_ref[:, past:] = jnp.where(lane == 0, s_new, NEG_INF)


def _decode_scores(page_table, qi_s, w_s, kinew_s, cache_kidx_t, width):
    db, n_pages = page_table.shape
    seq_spec = lambda a, c: pl.BlockSpec((None, a, c), lambda b, pt: (b, 0, 0))
    return pl.pallas_call(
        _decode_scores_kernel,
        grid_spec=pltpu.PrefetchScalarGridSpec(
            num_scalar_prefetch=1,
            grid=(db,),
            in_specs=[seq_spec(N_IDX_HEADS, IDX_DIM), seq_spec(N_IDX_HEADS, 1),
                      seq_spec(1, IDX_DIM), pl.BlockSpec(memory_space=pl.ANY)],
            out_specs=seq_spec(1, width),
            scratch_shapes=[pltpu.VMEM((SCORE_PAGE_SLOTS, n_pages, IDX_DIM, PAGE_SIZE), F32),
                            pltpu.SemaphoreType.DMA((SCORE_PAGE_SLOTS,))],
        ),
        out_shape=jax.ShapeDtypeStruct((db, 1, width), F32),
        compiler_params=pltpu.CompilerParams(dimension_semantics=("arbitrary",)),
        name="decode_scores",
    )(page_table, qi_s, w_s, kinew_s, cache_kidx_t)


def _decode_mask_kernel(s_in_ref, bias_ref, st_ref):
    nk = s_in_ref.shape[0]
    chunk = KEY_CHUNK
    s = s_in_ref[...]
    st_ref[...] = s
    valid = s > NEG_INF
    smin = jnp.min(jnp.where(valid, s, POS_INF), axis=0, keepdims=True)
    smax = jnp.max(s, axis=0, keepdims=True)
    n_valid = jnp.sum(valid.astype(F32), axis=0, keepdims=True)
    tau = _topk_threshold(st_ref, nk // chunk, chunk, n_valid, smin, smax)
    bias_ref[...] = jnp.where(st_ref[...] >= tau, 0.0, NEG)


def _decode_mask(scores_t):
    assert scores_t.shape[0] % KEY_CHUNK == 0 and scores_t.shape[1] == LANES
    return pl.pallas_call(
        _decode_mask_kernel,
        out_shape=jax.ShapeDtypeStruct(scores_t.shape, F32),
        scratch_shapes=[pltpu.VMEM(scores_t.shape, F32)],
        name="decode_mask",
    )(scores_t)


def _decode_attn_kernel(pt_ref, q_ref, bias_ref, knew_ref, vnew_ref, ck_ref, cv_ref, o_ref,
                        kbuf, vbuf, sem):
    b = pl.program_id(0)
    n_pages = pt_ref.shape[1]
    page_rows = N_KV_HEADS * PAGE_SIZE

    def copies(seq, slot, j):
        rows_j = pl.ds(j * page_rows, page_rows)
        page = pt_ref[seq, j]
        return (pltpu.make_async_copy(ck_ref.at[0, page], kbuf.at[slot, rows_j], sem.at[slot, 0]),
                pltpu.make_async_copy(cv_ref.at[0, page], vbuf.at[slot, rows_j], sem.at[slot, 1]))

    slot = _prefetch_pages(copies, n_pages, kbuf.shape[0], b, pl.num_programs(0))

    rows = n_pages * page_rows
    scale = HEAD_DIM ** -0.5
    q = q_ref[...]
    bias = jnp.concatenate(
        [jnp.broadcast_to(bias_ref[n:n + 1, :], (GROUP, bias_ref.shape[1]))
         for n in range(N_KV_HEADS)], axis=0)
    k_il = kbuf[slot].astype(BF16)
    v_il = vbuf[slot].astype(BF16)
    sc = lax.dot_general(q, k_il, CONTRACT_LAST, preferred_element_type=F32)
    sc = sc * scale + bias[:, :rows]
    sc_new = jnp.sum(q.astype(F32) * knew_ref[...].astype(F32), axis=-1, keepdims=True)
    sc_new = sc_new * scale + jnp.max(bias[:, rows:rows + N_KV_HEADS], axis=-1, keepdims=True)
    m = jnp.maximum(jnp.max(sc, axis=-1, keepdims=True), sc_new)
    p = jnp.exp(sc - m)
    p_new = jnp.exp(sc_new - m)
    denom = jnp.sum(p, axis=-1, keepdims=True) + p_new
    o = jnp.dot(p.astype(BF16), v_il, preferred_element_type=F32)
    o = o + p_new.astype(BF16).astype(F32) * vnew_ref[...].astype(F32)
    o_ref[...] = (o / denom).astype(o_ref.dtype)


def _decode_attention(page_table, q_s, bias_il, knew_s, vnew_s, cache_k, cache_v):
    db, n_pages = page_table.shape
    width2 = bias_il.shape[-1]
    rows = n_pages * N_KV_HEADS * PAGE_SIZE
    seq_spec = lambda a, c: pl.BlockSpec((None, a, c), lambda b, pt: (b, 0, 0))
    return pl.pallas_call(
        _decode_attn_kernel,
        grid_spec=pltpu.PrefetchScalarGridSpec(
            num_scalar_prefetch=1,
            grid=(db,),
            in_specs=[seq_spec(N_HEADS, HEAD_DIM), seq_spec(N_KV_HEADS, width2),
                      seq_spec(N_HEADS, HEAD_DIM), seq_spec(N_HEADS, HEAD_DIM),
                      pl.BlockSpec(memory_space=pl.ANY), pl.BlockSpec(memory_space=pl.ANY)],
            out_specs=seq_spec(N_HEADS, HEAD_DIM),
            scratch_shapes=[pltpu.VMEM((KV_PAGE_SLOTS, rows, HEAD_DIM), F32),
                            pltpu.VMEM((KV_PAGE_SLOTS, rows, HEAD_DIM), F32),
                            pltpu.SemaphoreType.DMA((KV_PAGE_SLOTS, 2))],
        ),
        out_shape=jax.ShapeDtypeStruct((db, N_HEADS, HEAD_DIM), BF16),
        compiler_params=pltpu.CompilerParams(
            dimension_semantics=("arbitrary",), vmem_limit_bytes=VMEM_LIMIT),
        name="decode_attn",
    )(page_table, q_s, bias_il, knew_s, vnew_s, cache_k, cache_v)


def _pool_decode_kernel(hist_ref, u_ref, d_ref):
    for g, w in enumerate(POOL_WINDOWS):
        cols = slice(g * POOL_GROUP_WIDTH, (g + 1) * POOL_GROUP_WIDTH)
        cur = u_ref[:, cols]
        tot = cur
        for j in range(1, w):
            tot = tot + hist_ref[POOL_HIST - j, :, cols]
        d_ref[:, cols] = tot * (1.0 / w) - cur


def _pool_decode(hist, u):
    return pl.pallas_call(
        _pool_decode_kernel,
        out_shape=jax.ShapeDtypeStruct(u.shape, F32),
        name="pool_decode",
    )(hist, u)


FF_TILE = 1024


def _rmsnorm(x, g):
    ms = jnp.mean(x * x, axis=-1, keepdims=True)
    return (x * lax.rsqrt(ms + EPS)) * g


def _post_kernel(x_ref, att_ref, d_ref, woa_ref, wop_ref, wpool_ref, pscale_ref,
                 gmlp_ref, gfin_ref, wup_ref, wdn_ref, y_ref, hn_ref):
    f = pl.program_id(1)

    @pl.when(f == 0)
    def _():
        d = d_ref[...]
        pooled = jnp.concatenate(
            [jnp.dot(d[:, g * POOL_GROUP_WIDTH:(g + 1) * POOL_GROUP_WIDTH].astype(BF16),
                     wpool_ref[g], preferred_element_type=F32)
             for g in range(len(POOL_WINDOWS))], axis=1) * pscale_ref[...]
        h = (x_ref[...]
             + jnp.dot(att_ref[...], woa_ref[...], preferred_element_type=F32)
             + jnp.dot(pooled.astype(BF16), wop_ref[...], preferred_element_type=F32))
        y_ref[...] = h
        hn_ref[...] = _rmsnorm(h, gmlp_ref[...]).astype(BF16)

    a = jnp.maximum(jnp.dot(hn_ref[...], wup_ref[...], preferred_element_type=F32), 0.0)
    y_ref[...] += jnp.dot((a * a).astype(BF16), wdn_ref[...], preferred_element_type=F32)

    @pl.when(f == pl.num_programs(1) - 1)
    def _():
        y_ref[...] = _rmsnorm(y_ref[...], gfin_ref[...])


def _post(x, att, d, wo_a, wo_p, w_pool, pool_scale, g_mlp, g_final, w_up, w_down, tm):
    rows = x.shape[0]
    assert rows % tm == 0 and D_FF % FF_TILE == 0
    row_spec = lambda w: pl.BlockSpec((tm, w), lambda i, f: (i, 0))
    return pl.pallas_call(
        _post_kernel,
        grid=(rows // tm, D_FF // FF_TILE),
        in_specs=[row_spec(D_MODEL), row_spec(ATTN_WIDTH), row_spec(POOL_WIDTH),
                  _const_spec(wo_a.shape), _const_spec(wo_p.shape), _const_spec(w_pool.shape),
                  _const_spec((1, POOL_WIDTH)), _const_spec((1, D_MODEL)), _const_spec((1, D_MODEL)),
                  pl.BlockSpec((D_MODEL, FF_TILE), lambda i, f: (0, f)),
                  pl.BlockSpec((FF_TILE, D_MODEL), lambda i, f: (f, 0))],
        out_specs=row_spec(D_MODEL),
        out_shape=jax.ShapeDtypeStruct((rows, D_MODEL), F32),
        scratch_shapes=[pltpu.VMEM((tm, D_MODEL), BF16)],
        compiler_params=pltpu.CompilerParams(
            dimension_semantics=("arbitrary", "arbitrary"), vmem_limit_bytes=VMEM_LIMIT),
        name="post",
    )(x, att, d, wo_a, wo_p, w_pool, pool_scale.reshape(1, POOL_WIDTH),
      g_mlp.reshape(1, D_MODEL), g_final.reshape(1, D_MODEL), w_up, w_down)


def kernel(x_prompt, x_sample, cache_k, cache_v, cache_kidx, state_pool, page_table, meta_tokens,
           g_mix, w_in, w_pool, pool_scale, w_out, g_mlp, w_up, w_down, g_final):
    assert x_prompt.shape[0] == 1 and x_sample.shape[1] == 1 and g_mix.shape[0] == 1
    seq = x_prompt.shape[1]
    db = x_sample.shape[0]
    n_pages = page_table.shape[1]
    past = n_pages * PAGE_SIZE
    assert db == LANES

    w = w_in[0]
    o = 0
    parts = []
    for width in (ATTN_WIDTH, 2 * N_KV_HEADS * HEAD_DIM, N_IDX_HEADS * IDX_DIM,
                  IDX_DIM + N_IDX_HEADS, POOL_WIDTH):
        parts.append(w[:, o:o + width].astype(BF16))
        o += width
    parts[3] = jnp.pad(parts[3], ((0, 0), (0, LANES - parts[3].shape[1])))
    wo_a = w_out[0, :ATTN_WIDTH].astype(BF16)
    wo_p = w_out[0, ATTN_WIDTH:].astype(BF16)
    w_pool_b = w_pool[0].astype(BF16)
    w_up_b = w_up[0].astype(BF16)
    w_down_b = w_down[0].astype(BF16)

    xs = jnp.concatenate([meta_tokens.astype(F32), x_sample[:, 0]], axis=0)
    pos_s = jnp.concatenate([jnp.arange(N_META, dtype=jnp.int32),
                             jnp.full((db,), past, jnp.int32)])
    q_s, kf_s, vf_s, kb_s, vb_s, qi_s, kw_s, u_s = _project(
        xs, jnp.zeros((1,), jnp.int32), pos_s, g_mix[0], parts)
    xp = x_prompt[0]
    assert seq % PROJ_TILE == 0
    tile_base = N_META + PROJ_TILE * jnp.arange(seq // PROJ_TILE, dtype=jnp.int32)
    q_p, kf_p, vf_p, kb_p, vb_p, qi_p, kw_p, d_p, u_tail = _project(
        xp, tile_base, jnp.arange(PROJ_TILE, dtype=jnp.int32), g_mix[0], parts, halo=u_s[:N_META])

    nk = -(-(seq + 2 * Q_BLOCK) // KEY_CHUNK) * KEY_CHUNK

    def key_rows(meta_part, prompt_part):
        width = meta_part.shape[1]
        return jnp.concatenate(
            [meta_part, jnp.zeros((Q_BLOCK - N_META, width), meta_part.dtype), prompt_part,
             jnp.zeros((nk - Q_BLOCK - seq, width), meta_part.dtype)], axis=0)

    ki_all = key_rows(kw_s[:N_META, :IDX_DIM], kw_p[:, :IDX_DIM]).astype(BF16)
    zk = jnp.zeros_like(ki_all)
    ki_lo = jnp.concatenate([ki_all, zk], axis=1)
    ki_hi = jnp.concatenate([zk, ki_all], axis=1)
    k_all = key_rows(kb_s[:N_META], kb_p)
    v_all = key_rows(vb_s[:N_META], vb_p)
    v_t = jnp.transpose(
        v_all.reshape(nk // KEY_CHUNK, KEY_CHUNK, N_KV_HEADS, HEAD_DIM), (0, 2, 3, 1))
    v_t = jnp.concatenate(
        [v_t, jnp.ones((nk // KEY_CHUNK, N_KV_HEADS, ONES_ROWS, KEY_CHUNK), BF16)], axis=2)
    knorm = lambda kw: kw[:, KNORM_LANE:KNORM_LANE + N_KV_HEADS]
    kmax2 = jnp.maximum(jnp.max(knorm(kw_p), axis=0), jnp.max(knorm(kw_s[:N_META]), axis=0))
    att_p = _prompt_attention(kmax2, qi_p, kw_p, q_p, ki_lo, ki_hi, k_all, v_t)

    width = -(-(past + 1) // KEY_CHUNK) * KEY_CHUNK
    qi_d = qi_s[N_META:].reshape(db, N_IDX_HEADS, IDX_DIM)
    w_d = kw_s[N_META:, IDX_DIM:IDX_DIM + N_IDX_HEADS].reshape(db, N_IDX_HEADS, 1)
    kinew = kw_s[N_META:, :IDX_DIM].astype(BF16).reshape(db, 1, IDX_DIM)
    scores = _decode_scores(page_table, qi_d, w_d, kinew, jnp.swapaxes(cache_kidx, 2, 3), width)
    bias_t = _decode_mask(jnp.transpose(scores.reshape(db, width)))
    bias = jnp.transpose(bias_t)
    negs = jnp.full_like(bias, NEG)
    bias_il = jnp.stack(
        [jnp.stack([bias, negs], axis=-1), jnp.stack([negs, bias], axis=-1)], axis=1,
    ).reshape(db, N_KV_HEADS, N_KV_HEADS * width)
    n_phys = cache_k.shape[1]
    il_shape = (1, n_phys, N_KV_HEADS * PAGE_SIZE, HEAD_DIM)
    per_head = lambda a: jnp.repeat(a.reshape(db, N_KV_HEADS, HEAD_DIM), GROUP, axis=1)
    att_s = _decode_attention(
        page_table, q_s[N_META:].reshape(db, N_HEADS, HEAD_DIM), bias_il,
        per_head(kb_s[N_META:]), per_head(vb_s[N_META:]),
        cache_k.reshape(il_shape), cache_v.reshape(il_shape))
    att_s = att_s.reshape(db, ATTN_WIDTH)

    d_s = _pool_decode(jnp.swapaxes(state_pool[0], 0, 1), u_s[N_META:])

    post = functools.partial(
        _post, wo_a=wo_a, wo_p=wo_p, w_pool=w_pool_b, pool_scale=pool_scale[0],
        g_mlp=g_mlp[0], g_final=g_final, w_up=w_up_b, w_down=w_down_b)
    y_p = post(xp, att_p, d_p, tm=512)
    y_s = post(x_sample[:, 0], att_s, d_s, tm=db)

    k_prompt = jnp.concatenate([kf_s[:N_META], kf_p], axis=0).reshape(1, 1, N_META + seq, N_KV_HEADS, HEAD_DIM)
    v_prompt = jnp.concatenate([vf_s[:N_META], vf_p], axis=0).reshape(1, 1, N_META + seq, N_KV_HEADS, HEAD_DIM)
    kidx_prompt = jnp.concatenate([kw_s[:N_META, :IDX_DIM], kw_p[:, :IDX_DIM]], axis=0).reshape(
        1, 1, N_META + seq, IDX_DIM)
    pool_prompt = u_tail[POOL_HALO - POOL_HIST:].reshape(1, 1, POOL_HIST, POOL_WIDTH)
    k_sample = kf_s[N_META:].reshape(1, db, 1, N_KV_HEADS, HEAD_DIM)
    v_sample = vf_s[N_META:].reshape(1, db, 1, N_KV_HEADS, HEAD_DIM)
    kidx_sample = kw_s[N_META:, :IDX_DIM].reshape(1, db, 1, IDX_DIM)
    pool_sample = jnp.concatenate(
        [state_pool[0][:, 1:], u_s[N_META:][:, None, :]], axis=1)[None]
    return (y_p.reshape(1, seq, D_MODEL), y_s.reshape(db, 1, D_MODEL),
            k_prompt, v_prompt, kidx_prompt, pool_prompt,
            k_sample, v_sample, kidx_sample, pool_sample)
```

```python
import functools

import jax
import jax.numpy as jnp
from jax import lax
from jax.experimental import pallas as pl
from jax.experimental.pallas import tpu as pltpu

F32 = jnp.float32
BF16 = jnp.bfloat16

D_MODEL = 2048
N_META = 16
ATTN_WIDTH = 1024
POOL_WIDTH = 1024
HEAD_DIM = 128
N_HEADS = 8
N_KV_HEADS = 2
GROUP = N_HEADS // N_KV_HEADS
N_IDX_HEADS = 16
IDX_DIM = 64
TOPK = 256
POOL_WINDOWS = (2, 4, 8, 16)
POOL_GROUP_WIDTH = 256
POOL_HIST = 15
D_FF = 8192
PAGE_SIZE = 128
ROPE_THETA = 10000.0
EPS = 1e-6
NEG = -1e30
NEG_INF = float("-inf")
POS_INF = float("inf")
F32_LOWEST = -3.0e38

LANES = 128
SUBLANES = 8
VMEM_LIMIT = 56 * 1024 * 1024

Q_BLOCK = 128
KEY_CHUNK = 512
MAX_BISECT = 320
N_UNCHECKED_ROUNDS = 7
LOG2E = 1.4426950408889634
MAX_EXP2_GAP = 80.0
NORM_BOUND_SLACK = 1.02
ONES_ROWS = 16
KNORM_LANE = IDX_DIM + N_IDX_HEADS
PROJ_TILE = 256
POOL_HALO = 16

CONTRACT_LAST = (((1,), (1,)), ((), ()))


def _const_spec(shape):
    nd = len(shape)
    return pl.BlockSpec(shape, lambda *_: (0,) * nd, pipeline_mode=pl.Buffered(1))


def _rope128(x, c, s):
    return x * c + pltpu.roll(x, 64, 1) * s


def _rope64(x, c, sa, sb):
    return x * c + pltpu.roll(x, 96, 1) * sa + pltpu.roll(x, 32, 1) * sb


def _angle_sum(cb_ref, sb_ref, cr_ref, sr_ref):
    cb, sb, cr, sr = cb_ref[...], sb_ref[...], cr_ref[...], sr_ref[...]
    return cb * cr - sb * sr, sb * cr + cb * sr


def _proj_kernel(x_ref, g_ref, cb128_ref, sb128_ref, cb64_ref, sb64_ref,
                 cr128_ref, sr128_ref, cr64_ref, sr64_ref,
                 wq_ref, wkv_ref, wqi_ref, wkw_ref, wu_ref, *rest, with_pool):
    if with_pool:
        (halo_ref, q_ref, kf_ref, vf_ref, kb_ref, vb_ref, qi_ref, kw_ref, u_ref, utail_ref,
         ubuf) = rest
    else:
        q_ref, kf_ref, vf_ref, kb_ref, vb_ref, qi_ref, kw_ref, u_ref = rest
    x = x_ref[...]
    ms = jnp.mean(x * x, axis=-1, keepdims=True)
    xn = ((x * lax.rsqrt(ms + EPS)) * g_ref[...]).astype(BF16)
    tm = x.shape[0]
    lane_t = lax.broadcasted_iota(jnp.int32, (tm, LANES), 1)
    c128, sfull = _angle_sum(cb128_ref, sb128_ref, cr128_ref, sr128_ref)
    s128 = jnp.where(lane_t < HEAD_DIM // 2, -sfull, sfull)
    c64, sfull = _angle_sum(cb64_ref, sb64_ref, cr64_ref, sr64_ref)
    first_half = jnp.bitwise_and(lane_t, IDX_DIM - 1) < IDX_DIM // 2
    sa64 = jnp.where(first_half, -sfull, 0.0)
    sb64 = jnp.where(first_half, 0.0, sfull)

    zq = jnp.dot(xn, wq_ref[...], preferred_element_type=F32)
    for h in range(N_HEADS):
        sl = slice(h * LANES, (h + 1) * LANES)
        q_ref[:, sl] = _rope128(zq[:, sl], c128, s128).astype(BF16)

    zkv = jnp.dot(xn, wkv_ref[...], preferred_element_type=F32)
    k_norm2 = []
    for h in range(N_KV_HEADS):
        sl = slice(h * LANES, (h + 1) * LANES)
        kr = _rope128(zkv[:, sl], c128, s128)
        kf_ref[:, sl] = kr
        kb = kr.astype(BF16)
        kb_ref[:, sl] = kb
        k_norm2.append(jnp.sum(kb.astype(F32) * kb.astype(F32), axis=-1, keepdims=True))
    v = zkv[:, N_KV_HEADS * LANES:]
    vf_ref[...] = v
    vb_ref[...] = v.astype(BF16)

    zqi = jnp.dot(xn, wqi_ref[...], preferred_element_type=F32)
    for p in range(N_IDX_HEADS // 2):
        sl = slice(p * LANES, (p + 1) * LANES)
        qi_ref[:, sl] = _rope64(zqi[:, sl], c64, sa64, sb64).astype(BF16)

    zkw = jnp.dot(xn, wkw_ref[...], preferred_element_type=F32)
    lane = lax.broadcasted_iota(jnp.int32, zkw.shape, 1)
    is_key = lane < IDX_DIM
    ckw = jnp.where(is_key, c64, jnp.where(lane < IDX_DIM + N_IDX_HEADS, N_IDX_HEADS ** -0.5, 1.0))
    kw = _rope64(zkw, ckw, jnp.where(is_key, sa64, 0.0), jnp.where(is_key, sb64, 0.0))
    for h in range(N_KV_HEADS):
        kw = jnp.where(lane == KNORM_LANE + h, k_norm2[h], kw)
    kw_ref[...] = kw

    u = jnp.dot(xn, wu_ref[...], preferred_element_type=F32)
    if not with_pool:
        u_ref[...] = u
        return

    @pl.when(pl.program_id(0) == 0)
    def _():
        ubuf[0:POOL_HALO, :] = halo_ref[...]

    ubuf[POOL_HALO:, :] = u
    for g, w in enumerate(POOL_WINDOWS):
        cols = slice(g * POOL_GROUP_WIDTH, (g + 1) * POOL_GROUP_WIDTH)
        cur = ubuf[POOL_HALO:POOL_HALO + tm, cols]
        tot = cur
        for j in range(1, w):
            tot = tot + ubuf[POOL_HALO - j:POOL_HALO - j + tm, cols]
        u_ref[:, cols] = tot * (1.0 / w) - cur
    tail = ubuf[tm:tm + POOL_HALO, :]
    ubuf[0:POOL_HALO, :] = tail
    utail_ref[...] = tail


def _project(x, pos_base, pos_rel, g_mix, w_parts, halo=None):
    tm = pos_rel.shape[0]
    n_tiles = pos_base.shape[0]
    rows = x.shape[0]
    assert rows == tm * n_tiles
    with_pool = halo is not None
    base_tabs = [t.reshape(n_tiles, 1, LANES) for t in _rope_tables(pos_base)]
    rel_tabs = _rope_tables(pos_rel)
    row_spec = lambda w: pl.BlockSpec((tm, w), lambda i: (i, 0))
    base_spec = pl.BlockSpec((None, 1, LANES), lambda i: (i, 0, 0))
    out_shapes = [
        jax.ShapeDtypeStruct((rows, ATTN_WIDTH), BF16),
        jax.ShapeDtypeStruct((rows, 256), F32),
        jax.ShapeDtypeStruct((rows, 256), F32),
        jax.ShapeDtypeStruct((rows, 256), BF16),
        jax.ShapeDtypeStruct((rows, 256), BF16),
        jax.ShapeDtypeStruct((rows, 1024), BF16),
        jax.ShapeDtypeStruct((rows, LANES), F32),
        jax.ShapeDtypeStruct((rows, POOL_WIDTH), F32),
    ]
    out_specs = [row_spec(s.shape[1]) for s in out_shapes]
    in_specs = ([row_spec(D_MODEL), _const_spec((1, D_MODEL))] + [base_spec] * 4
                + [_const_spec((tm, LANES))] * 4 + [_const_spec(w.shape) for w in w_parts])
    args = [x, g_mix.reshape(1, D_MODEL), *base_tabs, *rel_tabs, *w_parts]
    scratch = []
    if with_pool:
        in_specs.append(_const_spec((POOL_HALO, POOL_WIDTH)))
        args.append(halo)
        out_shapes.append(jax.ShapeDtypeStruct((POOL_HALO, POOL_WIDTH), F32))
        out_specs.append(pl.BlockSpec((POOL_HALO, POOL_WIDTH), lambda i: (0, 0)))
        scratch.append(pltpu.VMEM((tm + POOL_HALO, POOL_WIDTH), F32))
    return pl.pallas_call(
        functools.partial(_proj_kernel, with_pool=with_pool),
        grid=(n_tiles,),
        in_specs=in_specs,
        out_specs=tuple(out_specs),
        out_shape=tuple(out_shapes),
        scratch_shapes=scratch,
        compiler_params=pltpu.CompilerParams(
            dimension_semantics=("arbitrary",), vmem_limit_bytes=VMEM_LIMIT),
        name="proj",
    )(*args)


def _rope_tables(pos):
    posf = pos.astype(F32)[:, None]
    out = []
    for dim in (HEAD_DIM, IDX_DIM):
        inv = ROPE_THETA ** (-jnp.arange(0, dim, 2, dtype=F32) / dim)
        ang = posf * inv[None, :]
        reps = LANES // (dim // 2)
        out += [jnp.tile(jnp.cos(ang), (1, reps)), jnp.tile(jnp.sin(ang), (1, reps))]
    return out


def _key_count(st_ref, nchunks, chunk, pred):
    acc_rows = 8 * SUBLANES

    def body(c, part):
        c0 = pl.multiple_of(c * chunk, chunk)
        hit = pred(st_ref[pl.ds(c0, chunk), :], c0)
        for g in range(chunk // acc_rows):
            part = jnp.where(hit[g * acc_rows:(g + 1) * acc_rows], part + 1.0, part)
        return part

    part = lax.fori_loop(0, nchunks, body, jnp.zeros((acc_rows, LANES), F32))
    return jnp.sum(part, axis=0, keepdims=True)


def _topk_threshold(st_ref, nchunks, chunk, n_valid, s_min, s_max):
    kf = float(TOPK)
    take_all = n_valid <= kf

    def count_ge(t):
        return _key_count(st_ref, nchunks, chunk, lambda v, c0: v >= t)

    def cond(st):
        it, done = st[0], st[3]
        return jnp.logical_and(it < MAX_BISECT, jnp.min(done) < 0.5)

    def step(st):
        lo, hi, done, tau = st
        mid = lo + (hi - lo) * 0.5
        collapsed = jnp.logical_or(mid <= lo, mid >= hi)
        cnt = count_ge(mid)
        found = cnt == kf
        active = done < 0.5
        lo = jnp.where(jnp.logical_and(active, cnt > kf), mid, lo)
        hi = jnp.where(jnp.logical_and(active, cnt < kf), mid, hi)
        tau = jnp.where(jnp.logical_and(active, found), mid, tau)
        done = jnp.where(jnp.logical_or(found, collapsed), 1.0, done)
        return lo, hi, done, tau

    def body(st):
        return (st[0] + 1,) + step(step(st[1:]))

    done0 = jnp.where(take_all, 1.0, 0.0).astype(F32)
    tau0 = jnp.full((1, LANES), POS_INF, F32)
    st0 = (jnp.int32(0), s_min, s_max, done0, tau0)
    st0 = lax.cond(jnp.min(done0) < 0.5,
                   lambda st: lax.fori_loop(0, N_UNCHECKED_ROUNDS, lambda _, s: body(s), st),
                   lambda st: st, st0)
    _, lo, hi, _, tau = lax.while_loop(cond, body, st0)

    unresolved = jnp.logical_and(tau == POS_INF, jnp.logical_not(take_all))
    n_unres = jnp.sum(unresolved.astype(F32))
    tau = jnp.where(take_all, F32_LOWEST, tau)

    def tie_path(tau):
        cut = jnp.where(count_ge(hi) >= kf, hi, lo)
        cut = jnp.where(unresolved, cut, POS_INF)
        c_gt = _key_count(st_ref, nchunks, chunk, lambda v, c0: v > cut)
        keep = kf - c_gt

        def key_iota(c0):
            return c0 + lax.broadcasted_iota(jnp.int32, (chunk, LANES), 0)

        def idx_body(_, st):
            xlo, xhi = st
            xmid = (xlo + xhi) // 2
            cnt = _key_count(
                st_ref, nchunks, chunk,
                lambda v, c0: jnp.logical_and(v == cut, key_iota(c0) < xmid))
            ok = cnt >= keep
            return jnp.where(ok, xlo, xmid), jnp.where(ok, xmid, xhi)

        xlo0 = jnp.zeros((1, LANES), jnp.int32)
        xhi0 = jnp.full((1, LANES), nchunks * chunk, jnp.int32)
        n_idx_steps = 15
        _, xcut = lax.fori_loop(0, n_idx_steps, idx_body, (xlo0, xhi0))

        def drop_body(c, carry):
            c0 = pl.multiple_of(c * chunk, chunk)
            v = st_ref[pl.ds(c0, chunk), :]
            drop = jnp.logical_and(v == cut, key_iota(c0) >= xcut)
            st_ref[pl.ds(c0, chunk), :] = jnp.where(drop, NEG_INF, v)
            return carry

        lax.fori_loop(0, nchunks, drop_body, 0)
        return jnp.where(unresolved, cut, tau)

    return lax.cond(n_unres > 0.0, tie_path, lambda t: t, tau)


def _prompt_attn_kernel(kmax2_ref, qi_ref, kw_ref, q_ref, kilo_ref, kihi_ref, k_ref, vt_ref, o_ref,
                        st_ref, m_ref, l_ref, acc_ref, accf_ref):
    i = pl.program_id(0)
    tq, ck = Q_BLOCK, KEY_CHUNK
    n_blocks = i + 2
    nchunks = (n_blocks * Q_BLOCK + ck - 1) // ck

    wt = jnp.transpose(kw_ref[...])[IDX_DIM:IDX_DIM + N_IDX_HEADS, :] * (IDX_DIM ** -0.5)
    qcat = jnp.concatenate(
        [qi_ref[:, p * LANES:(p + 1) * LANES] for p in range(N_IDX_HEADS // 2)], axis=0)
    key = lax.broadcasted_iota(jnp.int32, (ck, tq), 0)
    qry = i * tq + lax.broadcasted_iota(jnp.int32, (ck, tq), 1)

    def idx_chunk(c, st):
        smin, smax = st
        c0 = pl.multiple_of(c * ck, ck)
        klo = kilo_ref[pl.ds(c0, ck), :]
        khi = kihi_ref[pl.ds(c0, ck), :]
        acc = jnp.zeros((ck, tq), F32)
        for t in range(N_IDX_HEADS // 4):
            rhs = qcat[2 * t * tq:(2 * t + 2) * tq]
            dlo = lax.dot_general(klo, rhs, CONTRACT_LAST, preferred_element_type=F32)
            dhi = lax.dot_general(khi, rhs, CONTRACT_LAST, preferred_element_type=F32)
            for j in range(2):
                p = 2 * t + j
                cs = slice(j * tq, (j + 1) * tq)
                acc = acc + wt[2 * p:2 * p + 1, :] * jnp.maximum(dlo[:, cs], 0.0)
                acc = acc + wt[2 * p + 1:2 * p + 2, :] * jnp.maximum(dhi[:, cs], 0.0)
        col = c0 + key
        valid = jnp.logical_or(
            col < N_META,
            jnp.logical_and(col >= Q_BLOCK, col - Q_BLOCK <= qry))
        st_ref[pl.ds(c0, ck), :] = jnp.where(valid, acc, NEG_INF)
        smin = jnp.minimum(smin, jnp.min(jnp.where(valid, acc, POS_INF), axis=0, keepdims=True))
        smax = jnp.maximum(smax, jnp.max(jnp.where(valid, acc, NEG_INF), axis=0, keepdims=True))
        return smin, smax

    st = lax.fori_loop(
        0, nchunks // 2, lambda j, st: idx_chunk(2 * j + 1, idx_chunk(2 * j, st)),
        (jnp.full((1, tq), POS_INF, F32), jnp.full((1, tq), NEG_INF, F32)))
    smin, smax = lax.cond(nchunks % 2 == 1, lambda st: idx_chunk(nchunks - 1, st), lambda st: st, st)

    n_valid = (N_META + 1 + i * tq + lax.broadcasted_iota(jnp.int32, (1, tq), 1)).astype(F32)
    tau = _topk_threshold(st_ref, nchunks, ck, n_valid, smin, smax)

    scale = HEAD_DIM ** -0.5

    def q_group(n):
        return jnp.concatenate(
            [q_ref[:, (n * GROUP + g) * LANES:(n * GROUP + g + 1) * LANES]
             for g in range(GROUP)], axis=0)

    def k_chunk(c0, n):
        return k_ref[pl.ds(c0, ck), n * LANES:(n + 1) * LANES]

    def write_out(n, ot):
        for g in range(GROUP):
            h = n * GROUP + g
            o_ref[:, h * LANES:(h + 1) * LANES] = jnp.transpose(
                ot[:, g * tq:(g + 1) * tq]).astype(o_ref.dtype)

    def ref_body(c, mx):
        c0 = pl.multiple_of(c * ck, ck)
        sel = st_ref[pl.ds(c0, ck), :] >= tau
        out = []
        for n in range(N_KV_HEADS):
            raw = lax.dot_general(k_chunk(c0, n), q_group(n), CONTRACT_LAST,
                                  preferred_element_type=F32)
            cur = jnp.concatenate(
                [jnp.max(jnp.where(sel, raw[:, g * tq:(g + 1) * tq], NEG), axis=0, keepdims=True)
                 for g in range(GROUP)], axis=1)
            out.append(jnp.maximum(mx[n], cur))
        return tuple(out)

    mx = ref_body(0, tuple(jnp.full((1, GROUP * tq), NEG, F32) for _ in range(N_KV_HEADS)))
    c2 = scale * LOG2E
    ones = jnp.ones((SUBLANES, HEAD_DIM), BF16)
    gap = jnp.float32(0.0)
    for n in range(N_KV_HEADS):
        qf = q_group(n).astype(F32)
        qn2 = lax.dot_general(ones, (qf * qf).astype(BF16), CONTRACT_LAST,
                              preferred_element_type=F32)[0:1]
        bound = jnp.sqrt(qn2 * kmax2_ref[n]) * (c2 * NORM_BOUND_SLACK)
        gap = jnp.maximum(gap, jnp.max(bound - mx[n] * c2))

    def fixed_reference_path():
        accf_ref[...] = jnp.zeros(accf_ref.shape, F32)

        def chunk_update(c, n):
            c0 = pl.multiple_of(c * ck, ck)
            sel = st_ref[pl.ds(c0, ck), :] >= tau
            raw = lax.dot_general(k_chunk(c0, n), q_group(n), CONTRACT_LAST,
                                  preferred_element_type=F32)
            neg_ref = mx[n] * (-c2)
            p = jnp.concatenate(
                [jnp.exp2(raw[:, g * tq:(g + 1) * tq] * c2
                          + jnp.where(sel, neg_ref[:, g * tq:(g + 1) * tq], NEG)).astype(BF16)
                 for g in range(GROUP)], axis=1)
            return jnp.dot(vt_ref[c, n], p, preferred_element_type=F32)

        def pair_body(j, carry):
            for n in range(N_KV_HEADS):
                accf_ref[n] += chunk_update(2 * j, n) + chunk_update(2 * j + 1, n)
            return carry

        lax.fori_loop(0, nchunks // 2, pair_body, 0)

        @pl.when(nchunks % 2 == 1)
        def _():
            for n in range(N_KV_HEADS):
                accf_ref[n] += chunk_update(nchunks - 1, n)

        for n in range(N_KV_HEADS):
            write_out(n, accf_ref[n, 0:HEAD_DIM, :] / accf_ref[n, HEAD_DIM:HEAD_DIM + 1, :])

    def running_max_path():
        m_ref[...] = jnp.full(m_ref.shape, NEG, F32)
        l_ref[...] = jnp.zeros(l_ref.shape, F32)
        acc_ref[...] = jnp.zeros(acc_ref.shape, F32)

        def body(c, carry):
            c0 = pl.multiple_of(c * ck, ck)
            bias = jnp.where(st_ref[pl.ds(c0, ck), :] >= tau, 0.0, NEG)
            bias = jnp.concatenate([bias] * GROUP, axis=1)
            for n in range(N_KV_HEADS):
                sc = lax.dot_general(k_chunk(c0, n), q_group(n), CONTRACT_LAST,
                                     preferred_element_type=F32)
                sc = sc * scale + bias
                m_old = m_ref[n]
                m_new = jnp.maximum(m_old, jnp.max(sc, axis=0, keepdims=True))
                alpha = jnp.exp(m_old - m_new)
                p = jnp.exp(sc - m_new)
                l_ref[n] = alpha * l_ref[n] + jnp.sum(
                    p.reshape(ck // SUBLANES, SUBLANES, GROUP * tq), axis=0)
                acc_ref[n] = alpha * acc_ref[n] + jnp.dot(
                    vt_ref[c, n, 0:HEAD_DIM, :], p.astype(BF16), preferred_element_type=F32)
                m_ref[n] = m_new
            return carry

        lax.fori_loop(0, nchunks, body, 0)
        for n in range(N_KV_HEADS):
            write_out(n, acc_ref[n] / jnp.sum(l_ref[n], axis=0, keepdims=True))

    lax.cond(gap <= MAX_EXP2_GAP, fixed_reference_path, running_max_path)


def _prompt_attention(kmax2, qi, kw, q, ki_lo, ki_hi, k_all, v_t):
    rows = q.shape[0]
    nk = k_all.shape[0]
    assert rows % Q_BLOCK == 0 and nk % KEY_CHUNK == 0
    assert nk >= (rows // Q_BLOCK + 1) * Q_BLOCK
    assert v_t.shape == (nk // KEY_CHUNK, N_KV_HEADS, HEAD_DIM + ONES_ROWS, KEY_CHUNK)
    row_spec = lambda w: pl.BlockSpec((Q_BLOCK, w), lambda i: (i, 0))
    return pl.pallas_call(
        _prompt_attn_kernel,
        grid=(rows // Q_BLOCK,),
        in_specs=[pl.BlockSpec(memory_space=pltpu.SMEM),
                  row_spec(1024), row_spec(LANES), row_spec(ATTN_WIDTH),
                  _const_spec(ki_lo.shape), _const_spec(ki_hi.shape),
                  _const_spec(k_all.shape), _const_spec(v_t.shape)],
        out_specs=row_spec(ATTN_WIDTH),
        out_shape=jax.ShapeDtypeStruct((rows, ATTN_WIDTH), BF16),
        scratch_shapes=[
            pltpu.VMEM((nk, Q_BLOCK), F32),
            pltpu.VMEM((N_KV_HEADS, 1, GROUP * Q_BLOCK), F32),
            pltpu.VMEM((N_KV_HEADS, SUBLANES, GROUP * Q_BLOCK), F32),
            pltpu.VMEM((N_KV_HEADS, HEAD_DIM, GROUP * Q_BLOCK), F32),
            pltpu.VMEM((N_KV_HEADS, HEAD_DIM + ONES_ROWS, GROUP * Q_BLOCK), F32),
        ],
        compiler_params=pltpu.CompilerParams(
            dimension_semantics=("arbitrary",), vmem_limit_bytes=VMEM_LIMIT),
        name="prompt_attn",
    )(kmax2, qi, kw, q, ki_lo, ki_hi, k_all, v_t)


SCORE_PAGE_SLOTS = 4
KV_PAGE_SLOTS = 3


def _prefetch_pages(copies, n_pages, n_slots, b, n_seq):
    depth = n_slots - 1

    def start(seq):
        for j in range(n_pages):
            for cp in copies(seq, seq % n_slots, j):
                cp.start()

    @pl.when(b == 0)
    def _():
        for s in range(depth):
            start(s)

    @pl.when(b + depth < n_seq)
    def _():
        start(b + depth)

    for j in range(n_pages):
        for cp in copies(b, b % n_slots, j):
            cp.wait()
    return b % n_slots


def _decode_scores_kernel(pt_ref, qi_ref, w_ref, kinew_ref, cache_ref, s_ref, buf, sem):
    b = pl.program_id(0)
    n_pages = pt_ref.shape[1]

    def copies(seq, slot, j):
        return (pltpu.make_async_copy(
            cache_ref.at[0, pt_ref[seq, j]], buf.at[slot, j], sem.at[slot]),)

    slot = _prefetch_pages(copies, n_pages, buf.shape[0], b, pl.num_programs(0))

    qi = qi_ref[...]
    w = w_ref[...] * (IDX_DIM ** -0.5)
    for j in range(n_pages):
        kpt = buf[slot, j].astype(BF16)
        d = jnp.dot(qi, kpt, preferred_element_type=F32)
        s_ref[:, j * PAGE_SIZE:(j + 1) * PAGE_SIZE] = jnp.sum(
            jnp.maximum(d, 0.0) * w, axis=0, keepdims=True)
    d_new = jnp.sum(qi.astype(F32) * kinew_ref[...].astype(F32), axis=-1, keepdims=True)
    s_new = jnp.sum(jnp.maximum(d_new, 0.0) * w, axis=0, keepdims=True)
    past = n_pages * PAGE_SIZE
    tail = s_ref.shape[1] - past
    lane = lax.broadcasted_iota(jnp.int32, (1, tail), 1)
    s_ref[:, past:] = jnp.where(lane == 0, s_new, NEG_INF)


def _decode_scores(page_table, qi_s, w_s, kinew_s, cache_kidx_t, width):
    db, n_pages = page_table.shape
    seq_spec = lambda a, c: pl.BlockSpec((None, a, c), lambda b, pt: (b, 0, 0))
    return pl.pallas_call(
        _decode_scores_kernel,
        grid_spec=pltpu.PrefetchScalarGridSpec(
            num_scalar_prefetch=1,
            grid=(db,),
            in_specs=[seq_spec(N_IDX_HEADS, IDX_DIM), seq_spec(N_IDX_HEADS, 1),
                      seq_spec(1, IDX_DIM), pl.BlockSpec(memory_space=pl.ANY)],
            out_specs=seq_spec(1, width),
            scratch_shapes=[pltpu.VMEM((SCORE_PAGE_SLOTS, n_pages, IDX_DIM, PAGE_SIZE), F32),
                            pltpu.SemaphoreType.DMA((SCORE_PAGE_SLOTS,))],
        ),
        out_shape=jax.ShapeDtypeStruct((db, 1, width), F32),
        compiler_params=pltpu.CompilerParams(dimension_semantics=("arbitrary",)),
        name="decode_scores",
    )(page_table, qi_s, w_s, kinew_s, cache_kidx_t)


def _decode_mask_kernel(s_in_ref, bias_ref, st_ref):
    nk = s_in_ref.shape[0]
    chunk = KEY_CHUNK
    s = s_in_ref[...]
    st_ref[...] = s
    valid = s > NEG_INF
    smin = jnp.min(jnp.where(valid, s, POS_INF), axis=0, keepdims=True)
    smax = jnp.max(s, axis=0, keepdims=True)
    n_valid = jnp.sum(valid.astype(F32), axis=0, keepdims=True)
    tau = _topk_threshold(st_ref, nk // chunk, chunk, n_valid, smin, smax)
    bias_ref[...] = jnp.where(st_ref[...] >= tau, 0.0, NEG)


def _decode_mask(scores_t):
    assert scores_t.shape[0] % KEY_CHUNK == 0 and scores_t.shape[1] == LANES
    return pl.pallas_call(
        _decode_mask_kernel,
        out_shape=jax.ShapeDtypeStruct(scores_t.shape, F32),
        scratch_shapes=[pltpu.VMEM(scores_t.shape, F32)],
        name="decode_mask",
    )(scores_t)


def _decode_attn_kernel(pt_ref, q_ref, bias_ref, knew_ref, vnew_ref, ck_ref, cv_ref, o_ref,
                        kbuf, vbuf, sem):
    b = pl.program_id(0)
    n_pages = pt_ref.shape[1]
    page_rows = N_KV_HEADS * PAGE_SIZE

    def copies(seq, slot, j):
        rows_j = pl.ds(j * page_rows, page_rows)
        page = pt_ref[seq, j]
        return (pltpu.make_async_copy(ck_ref.at[0, page], kbuf.at[slot, rows_j], sem.at[slot, 0]),
                pltpu.make_async_copy(cv_ref.at[0, page], vbuf.at[slot, rows_j], sem.at[slot, 1]))

    slot = _prefetch_pages(copies, n_pages, kbuf.shape[0], b, pl.num_programs(0))

    rows = n_pages * page_rows
    scale = HEAD_DIM ** -0.5
    q = q_ref[...]
    bias = jnp.concatenate(
        [jnp.broadcast_to(bias_ref[n:n + 1, :], (GROUP, bias_ref.shape[1]))
         for n in range(N_KV_HEADS)], axis=0)
    k_il = kbuf[slot].astype(BF16)
    v_il = vbuf[slot].astype(BF16)
    sc = lax.dot_general(q, k_il, CONTRACT_LAST, preferred_element_type=F32)
    sc = sc * scale + bias[:, :rows]
    sc_new = jnp.sum(q.astype(F32) * knew_ref[...].astype(F32), axis=-1, keepdims=True)
    sc_new = sc_new * scale + jnp.max(bias[:, rows:rows + N_KV_HEADS], axis=-1, keepdims=True)
    m = jnp.maximum(jnp.max(sc, axis=-1, keepdims=True), sc_new)
    p = jnp.exp(sc - m)
    p_new = jnp.exp(sc_new - m)
    denom = jnp.sum(p, axis=-1, keepdims=True) + p_new
    o = jnp.dot(p.astype(BF16), v_il, preferred_element_type=F32)
    o = o + p_new.astype(BF16).astype(F32) * vnew_ref[...].astype(F32)
    o_ref[...] = (o / denom).astype(o_ref.dtype)


def _decode_attention(page_table, q_s, bias_il, knew_s, vnew_s, cache_k, cache_v):
    db, n_pages = page_table.shape
    width2 = bias_il.shape[-1]
    rows = n_pages * N_KV_HEADS * PAGE_SIZE
    seq_spec = lambda a, c: pl.BlockSpec((None, a, c), lambda b, pt: (b, 0, 0))
    return pl.pallas_call(
        _decode_attn_kernel,
        grid_spec=pltpu.PrefetchScalarGridSpec(
            num_scalar_prefetch=1,
            grid=(db,),
            in_specs=[seq_spec(N_HEADS, HEAD_DIM), seq_spec(N_KV_HEADS, width2),
                      seq_spec(N_HEADS, HEAD_DIM), seq_spec(N_HEADS, HEAD_DIM),
                      pl.BlockSpec(memory_space=pl.ANY), pl.BlockSpec(memory_space=pl.ANY)],
            out_specs=seq_spec(N_HEADS, HEAD_DIM),
            scratch_shapes=[pltpu.VMEM((KV_PAGE_SLOTS, rows, HEAD_DIM), F32),
                            pltpu.VMEM((KV_PAGE_SLOTS, rows, HEAD_DIM), F32),
                            pltpu.SemaphoreType.DMA((KV_PAGE_SLOTS, 2))],
        ),
        out_shape=jax.ShapeDtypeStruct((db, N_HEADS, HEAD_DIM), BF16),
        compiler_params=pltpu.CompilerParams(
            dimension_semantics=("arbitrary",), vmem_limit_bytes=VMEM_LIMIT),
        name="decode_attn",
    )(page_table, q_s, bias_il, knew_s, vnew_s, cache_k, cache_v)


def _pool_decode_kernel(hist_ref, u_ref, d_ref):
    for g, w in enumerate(POOL_WINDOWS):
        cols = slice(g * POOL_GROUP_WIDTH, (g + 1) * POOL_GROUP_WIDTH)
        cur = u_ref[:, cols]
        tot = cur
        for j in range(1, w):
            tot = tot + hist_ref[POOL_HIST - j, :, cols]
        d_ref[:, cols] = tot * (1.0 / w) - cur


def _pool_decode(hist, u):
    return pl.pallas_call(
        _pool_decode_kernel,
        out_shape=jax.ShapeDtypeStruct(u.shape, F32),
        name="pool_decode",
    )(hist, u)


FF_TILE = 1024


def _rmsnorm(x, g):
    ms = jnp.mean(x * x, axis=-1, keepdims=True)
    return (x * lax.rsqrt(ms + EPS)) * g


def _post_kernel(x_ref, att_ref, d_ref, woa_ref, wop_ref, wpool_ref, pscale_ref,
                 gmlp_ref, gfin_ref, wup_ref, wdn_ref, y_ref, hn_ref):
    f = pl.program_id(1)

    @pl.when(f == 0)
    def _():
        d = d_ref[...]
        pooled = jnp.concatenate(
            [jnp.dot(d[:, g * POOL_GROUP_WIDTH:(g + 1) * POOL_GROUP_WIDTH].astype(BF16),
                     wpool_ref[g], preferred_element_type=F32)
             for g in range(len(POOL_WINDOWS))], axis=1) * pscale_ref[...]
        h = (x_ref[...]
             + jnp.dot(att_ref[...], woa_ref[...], preferred_element_type=F32)
             + jnp.dot(pooled.astype(BF16), wop_ref[...], preferred_element_type=F32))
        y_ref[...] = h
        hn_ref[...] = _rmsnorm(h, gmlp_ref[...]).astype(BF16)

    a = jnp.maximum(jnp.dot(hn_ref[...], wup_ref[...], preferred_element_type=F32), 0.0)
    y_ref[...] += jnp.dot((a * a).astype(BF16), wdn_ref[...], preferred_element_type=F32)

    @pl.when(f == pl.num_programs(1) - 1)
    def _():
        y_ref[...] = _rmsnorm(y_ref[...], gfin_ref[...])


def _post(x, att, d, wo_a, wo_p, w_pool, pool_scale, g_mlp, g_final, w_up, w_down, tm):
    rows = x.shape[0]
    assert rows % tm == 0 and D_FF % FF_TILE == 0
    row_spec = lambda w: pl.BlockSpec((tm, w), lambda i, f: (i, 0))
    return pl.pallas_call(
        _post_kernel,
        grid=(rows // tm, D_FF // FF_TILE),
        in_specs=[row_spec(D_MODEL), row_spec(ATTN_WIDTH), row_spec(POOL_WIDTH),
                  _const_spec(wo_a.shape), _const_spec(wo_p.shape), _const_spec(w_pool.shape),
                  _const_spec((1, POOL_WIDTH)), _const_spec((1, D_MODEL)), _const_spec((1, D_MODEL)),
                  pl.BlockSpec((D_MODEL, FF_TILE), lambda i, f: (0, f)),
                  pl.BlockSpec((FF_TILE, D_MODEL), lambda i, f: (f, 0))],
        out_specs=row_spec(D_MODEL),
        out_shape=jax.ShapeDtypeStruct((rows, D_MODEL), F32),
        scratch_shapes=[pltpu.VMEM((tm, D_MODEL), BF16)],
        compiler_params=pltpu.CompilerParams(
            dimension_semantics=("arbitrary", "arbitrary"), vmem_limit_bytes=VMEM_LIMIT),
        name="post",
    )(x, att, d, wo_a, wo_p, w_pool, pool_scale.reshape(1, POOL_WIDTH),
      g_mlp.reshape(1, D_MODEL), g_final.reshape(1, D_MODEL), w_up, w_down)


def kernel(x_prompt, x_sample, cache_k, cache_v, cache_kidx, state_pool, page_table, meta_tokens,
           g_mix, w_in, w_pool, pool_scale, w_out, g_mlp, w_up, w_down, g_final):
    assert x_prompt.shape[0] == 1 and x_sample.shape[1] == 1 and g_mix.shape[0] == 1
    seq = x_prompt.shape[1]
    db = x_sample.shape[0]
    n_pages = page_table.shape[1]
    past = n_pages * PAGE_SIZE
    assert db == LANES

    w = w_in[0]
    o = 0
    parts = []
    for width in (ATTN_WIDTH, 2 * N_KV_HEADS * HEAD_DIM, N_IDX_HEADS * IDX_DIM,
                  IDX_DIM + N_IDX_HEADS, POOL_WIDTH):
        parts.append(w[:, o:o + width].astype(BF16))
        o += width
    parts[3] = jnp.pad(parts[3], ((0, 0), (0, LANES - parts[3].shape[1])))
    wo_a = w_out[0, :ATTN_WIDTH].astype(BF16)
    wo_p = w_out[0, ATTN_WIDTH:].astype(BF16)
    w_pool_b = w_pool[0].astype(BF16)
    w_up_b = w_up[0].astype(BF16)
    w_down_b = w_down[0].astype(BF16)

    xs = jnp.concatenate([meta_tokens.astype(F32), x_sample[:, 0]], axis=0)
    pos_s = jnp.concatenate([jnp.arange(N_META, dtype=jnp.int32),
                             jnp.full((db,), past, jnp.int32)])
    q_s, kf_s, vf_s, kb_s, vb_s, qi_s, kw_s, u_s = _project(
        xs, jnp.zeros((1,), jnp.int32), pos_s, g_mix[0], parts)
    xp = x_prompt[0]
    assert seq % PROJ_TILE == 0
    tile_base = N_META + PROJ_TILE * jnp.arange(seq // PROJ_TILE, dtype=jnp.int32)
    q_p, kf_p, vf_p, kb_p, vb_p, qi_p, kw_p, d_p, u_tail = _project(
        xp, tile_base, jnp.arange(PROJ_TILE, dtype=jnp.int32), g_mix[0], parts, halo=u_s[:N_META])

    nk = -(-(seq + 2 * Q_BLOCK) // KEY_CHUNK) * KEY_CHUNK

    def key_rows(meta_part, prompt_part):
        width = meta_part.shape[1]
        return jnp.concatenate(
            [meta_part, jnp.zeros((Q_BLOCK - N_META, width), meta_part.dtype), prompt_part,
             jnp.zeros((nk - Q_BLOCK - seq, width), meta_part.dtype)], axis=0)

    ki_all = key_rows(kw_s[:N_META, :IDX_DIM], kw_p[:, :IDX_DIM]).astype(BF16)
    zk = jnp.zeros_like(ki_all)
    ki_lo = jnp.concatenate([ki_all, zk], axis=1)
    ki_hi = jnp.concatenate([zk, ki_all], axis=1)
    k_all = key_rows(kb_s[:N_META], kb_p)
    v_all = key_rows(vb_s[:N_META], vb_p)
    v_t = jnp.transpose(
        v_all.reshape(nk // KEY_CHUNK, KEY_CHUNK, N_KV_HEADS, HEAD_DIM), (0, 2, 3, 1))
    v_t = jnp.concatenate(
        [v_t, jnp.ones((nk // KEY_CHUNK, N_KV_HEADS, ONES_ROWS, KEY_CHUNK), BF16)], axis=2)
    knorm = lambda kw: kw[:, KNORM_LANE:KNORM_LANE + N_KV_HEADS]
    kmax2 = jnp.maximum(jnp.max(knorm(kw_p), axis=0), jnp.max(knorm(kw_s[:N_META]), axis=0))
    att_p = _prompt_attention(kmax2, qi_p, kw_p, q_p, ki_lo, ki_hi, k_all, v_t)

    width = -(-(past + 1) // KEY_CHUNK) * KEY_CHUNK
    qi_d = qi_s[N_META:].reshape(db, N_IDX_HEADS, IDX_DIM)
    w_d = kw_s[N_META:, IDX_DIM:IDX_DIM + N_IDX_HEADS].reshape(db, N_IDX_HEADS, 1)
    kinew = kw_s[N_META:, :IDX_DIM].astype(BF16).reshape(db, 1, IDX_DIM)
    scores = _decode_scores(page_table, qi_d, w_d, kinew, jnp.swapaxes(cache_kidx, 2, 3), width)
    bias_t = _decode_mask(jnp.transpose(scores.reshape(db, width)))
    bias = jnp.transpose(bias_t)
    negs = jnp.full_like(bias, NEG)
    bias_il = jnp.stack(
        [jnp.stack([bias, negs], axis=-1), jnp.stack([negs, bias], axis=-1)], axis=1,
    ).reshape(db, N_KV_HEADS, N_KV_HEADS * width)
    n_phys = cache_k.shape[1]
    il_shape = (1, n_phys, N_KV_HEADS * PAGE_SIZE, HEAD_DIM)
    per_head = lambda a: jnp.repeat(a.reshape(db, N_KV_HEADS, HEAD_DIM), GROUP, axis=1)
    att_s = _decode_attention(
        page_table, q_s[N_META:].reshape(db, N_HEADS, HEAD_DIM), bias_il,
        per_head(kb_s[N_META:]), per_head(vb_s[N_META:]),
        cache_k.reshape(il_shape), cache_v.reshape(il_shape))
    att_s = att_s.reshape(db, ATTN_WIDTH)

    d_s = _pool_decode(jnp.swapaxes(state_pool[0], 0, 1), u_s[N_META:])

    post = functools.partial(
        _post, wo_a=wo_a, wo_p=wo_p, w_pool=w_pool_b, pool_scale=pool_scale[0],
        g_mlp=g_mlp[0], g_final=g_final, w_up=w_up_b, w_down=w_down_b)
    y_p = post(xp, att_p, d_p, tm=512)
    y_s = post(x_sample[:, 0], att_s, d_s, tm=db)

    k_prompt = jnp.concatenate([kf_s[:N_META], kf_p], axis=0).reshape(1, 1, N_META + seq, N_KV_HEADS, HEAD_DIM)
    v_prompt = jnp.concatenate([vf_s[:N_META], vf_p], axis=0).reshape(1, 1, N_META + seq, N_KV_HEADS, HEAD_DIM)
    kidx_prompt = jnp.concatenate([kw_s[:N_META, :IDX_DIM], kw_p[:, :IDX_DIM]], axis=0).reshape(
        1, 1, N_META + seq, IDX_DIM)
    pool_prompt = u_tail[POOL_HALO - POOL_HIST:].reshape(1, 1, POOL_HIST, POOL_WIDTH)
    k_sample = kf_s[N_META:].reshape(1, db, 1, N_KV_HEADS, HEAD_DIM)
    v_sample = vf_s[N_META:].reshape(1, db, 1, N_KV_HEADS, HEAD_DIM)
    kidx_sample = kw_s[N_META:, :IDX_DIM].reshape(1, db, 1, IDX_DIM)
    pool_sample = jnp.concatenate(
        [state_pool[0][:, 1:], u_s[N_META:][:, None, :]], axis=1)[None]
    return (y_p.reshape(1, seq, D_MODEL), y_s.reshape(db, 1, D_MODEL),
            k_prompt, v_prompt, kidx_prompt, pool_prompt,
            k_sample, v_sample, kidx_sample, pool_sample)
```

```python
import functools

import jax
import jax.numpy as jnp
from jax import lax
from jax.experimental import pallas as pl
from jax.experimental.pallas import tpu as pltpu

F32 = jnp.float32
BF16 = jnp.bfloat16

D_MODEL = 2048
N_META = 16
ATTN_WIDTH = 1024
POOL_WIDTH = 1024
HEAD_DIM = 128
N_HEADS = 8
N_KV_HEADS = 2
GROUP = N_HEADS // N_KV_HEADS
N_IDX_HEADS = 16
IDX_DIM = 64
TOPK = 256
POOL_WINDOWS = (2, 4, 8, 16)
POOL_GROUP_WIDTH = 256
POOL_HIST = 15
D_FF = 8192
PAGE_SIZE = 128
ROPE_THETA = 10000.0
EPS = 1e-6
NEG = -1e30
NEG_INF = float("-inf")
POS_INF = float("inf")
F32_LOWEST = -3.0e38

LANES = 128
SUBLANES = 8
VMEM_LIMIT = 56 * 1024 * 1024

Q_BLOCK = 128
KEY_CHUNK = 512
MAX_BISECT = 320
N_UNCHECKED_ROUNDS = 7
LOG2E = 1.4426950408889634
MAX_EXP2_GAP = 80.0
NORM_BOUND_SLACK = 1.02
ONES_ROWS = 16
KNORM_LANE = IDX_DIM + N_IDX_HEADS
PROJ_TILE = 256
POOL_HALO = 16

CONTRACT_LAST = (((1,), (1,)), ((), ()))


def _const_spec(shape):
    nd = len(shape)
    return pl.BlockSpec(shape, lambda *_: (0,) * nd, pipeline_mode=pl.Buffered(1))


def _rope128(x, c, s):
    return x * c + pltpu.roll(x, 64, 1) * s


def _rope64(x, c, sa, sb):
    return x * c + pltpu.roll(x, 96, 1) * sa + pltpu.roll(x, 32, 1) * sb


def _angle_sum(cb_ref, sb_ref, cr_ref, sr_ref):
    cb, sb, cr, sr = cb_ref[...], sb_ref[...], cr_ref[...], sr_ref[...]
    return cb * cr - sb * sr, sb * cr + cb * sr


def _proj_kernel(x_ref, g_ref, cb128_ref, sb128_ref, cb64_ref, sb64_ref,
                 cr128_ref, sr128_ref, cr64_ref, sr64_ref,
                 wq_ref, wkv_ref, wqi_ref, wkw_ref, wu_ref, *rest, with_pool):
    if with_pool:
        (halo_ref, wup_ref, wdn_ref, q_ref, kf_ref, vf_ref, kb_ref, vb_ref, qi_ref, kw_ref, u_ref,
         utail_ref, wupb_ref, wdnb_ref, ubuf) = rest
        wupb_ref[...] = wup_ref[...].astype(BF16)
        wdnb_ref[...] = wdn_ref[...].astype(BF16)
    else:
        q_ref, kf_ref, vf_ref, kb_ref, vb_ref, qi_ref, kw_ref, u_ref = rest
    x = x_ref[...]
    ms = jnp.mean(x * x, axis=-1, keepdims=True)
    xn = ((x * lax.rsqrt(ms + EPS)) * g_ref[...]).astype(BF16)
    tm = x.shape[0]
    lane_t = lax.broadcasted_iota(jnp.int32, (tm, LANES), 1)
    c128, sfull = _angle_sum(cb128_ref, sb128_ref, cr128_ref, sr128_ref)
    s128 = jnp.where(lane_t < HEAD_DIM // 2, -sfull, sfull)
    c64, sfull = _angle_sum(cb64_ref, sb64_ref, cr64_ref, sr64_ref)
    first_half = jnp.bitwise_and(lane_t, IDX_DIM - 1) < IDX_DIM // 2
    sa64 = jnp.where(first_half, -sfull, 0.0)
    sb64 = jnp.where(first_half, 0.0, sfull)

    zq = jnp.dot(xn, wq_ref[...], preferred_element_type=F32)
    for h in range(N_HEADS):
        sl = slice(h * LANES, (h + 1) * LANES)
        q_ref[:, sl] = _rope128(zq[:, sl], c128, s128).astype(BF16)

    zkv = jnp.dot(xn, wkv_ref[...], preferred_element_type=F32)
    k_norm2 = []
    for h in range(N_KV_HEADS):
        sl = slice(h * LANES, (h + 1) * LANES)
        kr = _rope128(zkv[:, sl], c128, s128)
        kf_ref[:, sl] = kr
        kb = kr.astype(BF16)
        kb_ref[:, sl] = kb
        k_norm2.append(jnp.sum(kb.astype(F32) * kb.astype(F32), axis=-1, keepdims=True))
    v = zkv[:, N_KV_HEADS * LANES:]
    vf_ref[...] = v
    vb_ref[...] = v.astype(BF16)

    zqi = jnp.dot(xn, wqi_ref[...], preferred_element_type=F32)
    for p in range(N_IDX_HEADS // 2):
        sl = slice(p * LANES, (p + 1) * LANES)
        qi_ref[:, sl] = _rope64(zqi[:, sl], c64, sa64, sb64).astype(BF16)

    zkw = jnp.dot(xn, wkw_ref[...], preferred_element_type=F32)
    lane = lax.broadcasted_iota(jnp.int32, zkw.shape, 1)
    is_key = lane < IDX_DIM
    ckw = jnp.where(is_key, c64, jnp.where(lane < IDX_DIM + N_IDX_HEADS, N_IDX_HEADS ** -0.5, 1.0))
    kw = _rope64(zkw, ckw, jnp.where(is_key, sa64, 0.0), jnp.where(is_key, sb64, 0.0))
    for h in range(N_KV_HEADS):
        kw = jnp.where(lane == KNORM_LANE + h, k_norm2[h], kw)
    kw_ref[...] = kw

    u = jnp.dot(xn, wu_ref[...], preferred_element_type=F32)
    if not with_pool:
        u_ref[...] = u
        return

    @pl.when(pl.program_id(0) == 0)
    def _():
        ubuf[0:POOL_HALO, :] = halo_ref[...]

    ubuf[POOL_HALO:, :] = u
    for g, w in enumerate(POOL_WINDOWS):
        cols = slice(g * POOL_GROUP_WIDTH, (g + 1) * POOL_GROUP_WIDTH)
        cur = ubuf[POOL_HALO:POOL_HALO + tm, cols]
        tot = cur
        for j in range(1, w):
            tot = tot + ubuf[POOL_HALO - j:POOL_HALO - j + tm, cols]
        u_ref[:, cols] = tot * (1.0 / w) - cur
    tail = ubuf[tm:tm + POOL_HALO, :]
    ubuf[0:POOL_HALO, :] = tail
    utail_ref[...] = tail


def _project(x, pos_base, pos_rel, g_mix, w_parts, halo=None, mlp_weights=None):
    tm = pos_rel.shape[0]
    n_tiles = pos_base.shape[0]
    rows = x.shape[0]
    assert rows == tm * n_tiles
    with_pool = halo is not None
    assert with_pool == (mlp_weights is not None)
    base_tabs = [t.reshape(n_tiles, 1, LANES) for t in _rope_tables(pos_base)]
    rel_tabs = _rope_tables(pos_rel)
    row_spec = lambda w: pl.BlockSpec((tm, w), lambda i: (i, 0))
    base_spec = pl.BlockSpec((None, 1, LANES), lambda i: (i, 0, 0))
    out_shapes = [
        jax.ShapeDtypeStruct((rows, ATTN_WIDTH), BF16),
        jax.ShapeDtypeStruct((rows, 256), F32),
        jax.ShapeDtypeStruct((rows, 256), F32),
        jax.ShapeDtypeStruct((rows, 256), BF16),
        jax.ShapeDtypeStruct((rows, 256), BF16),
        jax.ShapeDtypeStruct((rows, 1024), BF16),
        jax.ShapeDtypeStruct((rows, LANES), F32),
        jax.ShapeDtypeStruct((rows, POOL_WIDTH), F32),
    ]
    out_specs = [row_spec(s.shape[1]) for s in out_shapes]
    in_specs = ([row_spec(D_MODEL), _const_spec((1, D_MODEL))] + [base_spec] * 4
                + [_const_spec((tm, LANES))] * 4 + [_const_spec(w.shape) for w in w_parts])
    args = [x, g_mix.reshape(1, D_MODEL), *base_tabs, *rel_tabs, *w_parts]
    scratch = []
    if with_pool:
        in_specs.append(_const_spec((POOL_HALO, POOL_WIDTH)))
        args.append(halo)
        out_shapes.append(jax.ShapeDtypeStruct((POOL_HALO, POOL_WIDTH), F32))
        out_specs.append(pl.BlockSpec((POOL_HALO, POOL_WIDTH), lambda i: (0, 0)))
        scratch.append(pltpu.VMEM((tm + POOL_HALO, POOL_WIDTH), F32))
        for wgt in mlp_weights:
            assert wgt.shape[0] % n_tiles == 0
            slab = pl.BlockSpec((wgt.shape[0] // n_tiles, wgt.shape[1]), lambda i: (i, 0))
            in_specs.append(slab)
            args.append(wgt)
            out_shapes.append(jax.ShapeDtypeStruct(wgt.shape, BF16))
            out_specs.append(slab)
    return pl.pallas_call(
        functools.partial(_proj_kernel, with_pool=with_pool),
        grid=(n_tiles,),
        in_specs=in_specs,
        out_specs=tuple(out_specs),
        out_shape=tuple(out_shapes),
        scratch_shapes=scratch,
        compiler_params=pltpu.CompilerParams(
            dimension_semantics=("arbitrary",), vmem_limit_bytes=VMEM_LIMIT),
        name="proj",
    )(*args)


def _rope_tables(pos):
    posf = pos.astype(F32)[:, None]
    out = []
    for dim in (HEAD_DIM, IDX_DIM):
        inv = ROPE_THETA ** (-jnp.arange(0, dim, 2, dtype=F32) / dim)
        ang = posf * inv[None, :]
        reps = LANES // (dim // 2)
        out += [jnp.tile(jnp.cos(ang), (1, reps)), jnp.tile(jnp.sin(ang), (1, reps))]
    return out


def _key_count(st_ref, nchunks, chunk, pred):
    acc_rows = 8 * SUBLANES

    def body(c, part):
        c0 = pl.multiple_of(c * chunk, chunk)
        hit = pred(st_ref[pl.ds(c0, chunk), :], c0)
        for g in range(chunk // acc_rows):
            part = jnp.where(hit[g * acc_rows:(g + 1) * acc_rows], part + 1.0, part)
        return part

    part = lax.fori_loop(0, nchunks, body, jnp.zeros((acc_rows, LANES), F32))
    return jnp.sum(part, axis=0, keepdims=True)


def _topk_threshold(st_ref, nchunks, chunk, n_valid, s_min, s_max):
    kf = float(TOPK)
    take_all = n_valid <= kf

    def count_ge(t):
        return _key_count(st_ref, nchunks, chunk, lambda v, c0: v >= t)

    def cond(st):
        it, done = st[0], st[3]
        return jnp.logical_and(it < MAX_BISECT, jnp.min(done) < 0.5)

    def step(st):
        lo, hi, done, tau = st
        mid = lo + (hi - lo) * 0.5
        collapsed = jnp.logical_or(mid <= lo, mid >= hi)
        cnt = count_ge(mid)
        found = cnt == kf
        active = done < 0.5
        lo = jnp.where(jnp.logical_and(active, cnt > kf), mid, lo)
        hi = jnp.where(jnp.logical_and(active, cnt < kf), mid, hi)
        tau = jnp.where(jnp.logical_and(active, found), mid, tau)
        done = jnp.where(jnp.logical_or(found, collapsed), 1.0, done)
        return lo, hi, done, tau

    def body(st):
        return (st[0] + 1,) + step(step(st[1:]))

    done0 = jnp.where(take_all, 1.0, 0.0).astype(F32)
    tau0 = jnp.full((1, LANES), POS_INF, F32)
    st0 = (jnp.int32(0), s_min, s_max, done0, tau0)
    st0 = lax.cond(jnp.min(done0) < 0.5,
                   lambda st: lax.fori_loop(0, N_UNCHECKED_ROUNDS, lambda _, s: body(s), st),
                   lambda st: st, st0)
    _, lo, hi, _, tau = lax.while_loop(cond, body, st0)

    unresolved = jnp.logical_and(tau == POS_INF, jnp.logical_not(take_all))
    n_unres = jnp.sum(unresolved.astype(F32))
    tau = jnp.where(take_all, F32_LOWEST, tau)

    def tie_path(tau):
        cut = jnp.where(count_ge(hi) >= kf, hi, lo)
        cut = jnp.where(unresolved, cut, POS_INF)
        c_gt = _key_count(st_ref, nchunks, chunk, lambda v, c0: v > cut)
        keep = kf - c_gt

        def key_iota(c0):
            return c0 + lax.broadcasted_iota(jnp.int32, (chunk, LANES), 0)

        def idx_body(_, st):
            xlo, xhi = st
            xmid = (xlo + xhi) // 2
            cnt = _key_count(
                st_ref, nchunks, chunk,
                lambda v, c0: jnp.logical_and(v == cut, key_iota(c0) < xmid))
            ok = cnt >= keep
            return jnp.where(ok, xlo, xmid), jnp.where(ok, xmid, xhi)

        xlo0 = jnp.zeros((1, LANES), jnp.int32)
        xhi0 = jnp.full((1, LANES), nchunks * chunk, jnp.int32)
        n_idx_steps = 15
        _, xcut = lax.fori_loop(0, n_idx_steps, idx_body, (xlo0, xhi0))

        def drop_body(c, carry):
            c0 = pl.multiple_of(c * chunk, chunk)
            v = st_ref[pl.ds(c0, chunk), :]
            drop = jnp.logical_and(v == cut, key_iota(c0) >= xcut)
            st_ref[pl.ds(c0, chunk), :] = jnp.where(drop, NEG_INF, v)
            return carry

        lax.fori_loop(0, nchunks, drop_body, 0)
        return jnp.where(unresolved, cut, tau)

    return lax.cond(n_unres > 0.0, tie_path, lambda t: t, tau)


def _prompt_attn_kernel(kmax2_ref, qi_ref, kw_ref, q_ref, kilo_ref, kihi_ref, k_ref, vt_ref, o_ref,
                        st_ref, m_ref, l_ref, acc_ref, accf_ref):
    i = pl.program_id(0)
    tq, ck = Q_BLOCK, KEY_CHUNK
    n_blocks = i + 2
    nchunks = (n_blocks * Q_BLOCK + ck - 1) // ck

    wt = jnp.transpose(kw_ref[...])[IDX_DIM:IDX_DIM + N_IDX_HEADS, :] * (IDX_DIM ** -0.5)
    qcat = jnp.concatenate(
        [qi_ref[:, p * LANES:(p + 1) * LANES] for p in range(N_IDX_HEADS // 2)], axis=0)
    key = lax.broadcasted_iota(jnp.int32, (ck, tq), 0)
    qry = i * tq + lax.broadcasted_iota(jnp.int32, (ck, tq), 1)

    def idx_chunk(c, st):
        smin, smax = st
        c0 = pl.multiple_of(c * ck, ck)
        klo = kilo_ref[pl.ds(c0, ck), :]
        khi = kihi_ref[pl.ds(c0, ck), :]
        acc = jnp.zeros((ck, tq), F32)
        for t in range(N_IDX_HEADS // 4):
            rhs = qcat[2 * t * tq:(2 * t + 2) * tq]
            dlo = lax.dot_general(klo, rhs, CONTRACT_LAST, preferred_element_type=F32)
            dhi = lax.dot_general(khi, rhs, CONTRACT_LAST, preferred_element_type=F32)
            for j in range(2):
                p = 2 * t + j
                cs = slice(j * tq, (j + 1) * tq)
                acc = acc + wt[2 * p:2 * p + 1, :] * jnp.maximum(dlo[:, cs], 0.0)
                acc = acc + wt[2 * p + 1:2 * p + 2, :] * jnp.maximum(dhi[:, cs], 0.0)
        col = c0 + key
        valid = jnp.logical_or(
            col < N_META,
            jnp.logical_and(col >= Q_BLOCK, col - Q_BLOCK <= qry))
        st_ref[pl.ds(c0, ck), :] = jnp.where(valid, acc, NEG_INF)
        smin = jnp.minimum(smin, jnp.min(jnp.where(valid, acc, POS_INF), axis=0, keepdims=True))
        smax = jnp.maximum(smax, jnp.max(jnp.where(valid, acc, NEG_INF), axis=0, keepdims=True))
        return smin, smax

    st = lax.fori_loop(
        0, nchunks // 2, lambda j, st: idx_chunk(2 * j + 1, idx_chunk(2 * j, st)),
        (jnp.full((1, tq), POS_INF, F32), jnp.full((1, tq), NEG_INF, F32)))
    smin, smax = lax.cond(nchunks % 2 == 1, lambda st: idx_chunk(nchunks - 1, st), lambda st: st, st)

    n_valid = (N_META + 1 + i * tq + lax.broadcasted_iota(jnp.int32, (1, tq), 1)).astype(F32)
    tau = _topk_threshold(st_ref, nchunks, ck, n_valid, smin, smax)

    scale = HEAD_DIM ** -0.5

    def q_group(n):
        return jnp.concatenate(
            [q_ref[:, (n * GROUP + g) * LANES:(n * GROUP + g + 1) * LANES]
             for g in range(GROUP)], axis=0)

    def k_chunk(c0, n):
        return k_ref[pl.ds(c0, ck), n * LANES:(n + 1) * LANES]

    def write_out(n, ot):
        for g in range(GROUP):
            h = n * GROUP + g
            o_ref[:, h * LANES:(h + 1) * LANES] = jnp.transpose(
                ot[:, g * tq:(g + 1) * tq]).astype(o_ref.dtype)

    def ref_body(c, mx):
        c0 = pl.multiple_of(c * ck, ck)
        sel = st_ref[pl.ds(c0, ck), :] >= tau
        out = []
        for n in range(N_KV_HEADS):
            raw = lax.dot_general(k_chunk(c0, n), q_group(n), CONTRACT_LAST,
                                  preferred_element_type=F32)
            cur = jnp.concatenate(
                [jnp.max(jnp.where(sel, raw[:, g * tq:(g + 1) * tq], NEG), axis=0, keepdims=True)
                 for g in range(GROUP)], axis=1)
            out.append(jnp.maximum(mx[n], cur))
        return tuple(out)

    mx = ref_body(0, tuple(jnp.full((1, GROUP * tq), NEG, F32) for _ in range(N_KV_HEADS)))
    c2 = scale * LOG2E
    ones = jnp.ones((SUBLANES, HEAD_DIM), BF16)
    gap = jnp.float32(0.0)
    for n in range(N_KV_HEADS):
        qf = q_group(n).astype(F32)
        qn2 = lax.dot_general(ones, (qf * qf).astype(BF16), CONTRACT_LAST,
                              preferred_element_type=F32)[0:1]
        bound = jnp.sqrt(qn2 * kmax2_ref[n]) * (c2 * NORM_BOUND_SLACK)
        gap = jnp.maximum(gap, jnp.max(bound - mx[n] * c2))

    def fixed_reference_path():
        accf_ref[...] = jnp.zeros(accf_ref.shape, F32)

        def chunk_update(c, n):
            c0 = pl.multiple_of(c * ck, ck)
            sel = st_ref[pl.ds(c0, ck), :] >= tau
            raw = lax.dot_general(k_chunk(c0, n), q_group(n), CONTRACT_LAST,
                                  preferred_element_type=F32)
            neg_ref = mx[n] * (-c2)
            p = jnp.concatenate(
                [jnp.exp2(raw[:, g * tq:(g + 1) * tq] * c2
                          + jnp.where(sel, neg_ref[:, g * tq:(g + 1) * tq], NEG)).astype(BF16)
                 for g in range(GROUP)], axis=1)
            return jnp.dot(vt_ref[c, n], p, preferred_element_type=F32)

        def pair_body(j, carry):
            for n in range(N_KV_HEADS):
                accf_ref[n] += chunk_update(2 * j, n) + chunk_update(2 * j + 1, n)
            return carry

        lax.fori_loop(0, nchunks // 2, pair_body, 0)

        @pl.when(nchunks % 2 == 1)
        def _():
            for n in range(N_KV_HEADS):
                accf_ref[n] += chunk_update(nchunks - 1, n)

        for n in range(N_KV_HEADS):
            write_out(n, accf_ref[n, 0:HEAD_DIM, :] / accf_ref[n, HEAD_DIM:HEAD_DIM + 1, :])

    def running_max_path():
        m_ref[...] = jnp.full(m_ref.shape, NEG, F32)
        l_ref[...] = jnp.zeros(l_ref.shape, F32)
        acc_ref[...] = jnp.zeros(acc_ref.shape, F32)

        def body(c, carry):
            c0 = pl.multiple_of(c * ck, ck)
            bias = jnp.where(st_ref[pl.ds(c0, ck), :] >= tau, 0.0, NEG)
            bias = jnp.concatenate([bias] * GROUP, axis=1)
            for n in range(N_KV_HEADS):
                sc = lax.dot_general(k_chunk(c0, n), q_group(n), CONTRACT_LAST,
                                     preferred_element_type=F32)
                sc = sc * scale + bias
                m_old = m_ref[n]
                m_new = jnp.maximum(m_old, jnp.max(sc, axis=0, keepdims=True))
                alpha = jnp.exp(m_old - m_new)
                p = jnp.exp(sc - m_new)
                l_ref[n] = alpha * l_ref[n] + jnp.sum(
                    p.reshape(ck // SUBLANES, SUBLANES, GROUP * tq), axis=0)
                acc_ref[n] = alpha * acc_ref[n] + jnp.dot(
                    vt_ref[c, n, 0:HEAD_DIM, :], p.astype(BF16), preferred_element_type=F32)
                m_ref[n] = m_new
            return carry

        lax.fori_loop(0, nchunks, body, 0)
        for n in range(N_KV_HEADS):
            write_out(n, acc_ref[n] / jnp.sum(l_ref[n], axis=0, keepdims=True))

    lax.cond(gap <= MAX_EXP2_GAP, fixed_reference_path, running_max_path)


def _prompt_attention(kmax2, qi, kw, q, ki_lo, ki_hi, k_all, v_t):
    rows = q.shape[0]
    nk = k_all.shape[0]
    assert rows % Q_BLOCK == 0 and nk % KEY_CHUNK == 0
    assert nk >= (rows // Q_BLOCK + 1) * Q_BLOCK
    assert v_t.shape == (nk // KEY_CHUNK, N_KV_HEADS, HEAD_DIM + ONES_ROWS, KEY_CHUNK)
    row_spec = lambda w: pl.BlockSpec((Q_BLOCK, w), lambda i: (i, 0))
    return pl.pallas_call(
        _prompt_attn_kernel,
        grid=(rows // Q_BLOCK,),
        in_specs=[pl.BlockSpec(memory_space=pltpu.SMEM),
                  row_spec(1024), row_spec(LANES), row_spec(ATTN_WIDTH),
                  _const_spec(ki_lo.shape), _const_spec(ki_hi.shape),
                  _const_spec(k_all.shape), _const_spec(v_t.shape)],
        out_specs=row_spec(ATTN_WIDTH),
        out_shape=jax.ShapeDtypeStruct((rows, ATTN_WIDTH), BF16),
        scratch_shapes=[
            pltpu.VMEM((nk, Q_BLOCK), F32),
            pltpu.VMEM((N_KV_HEADS, 1, GROUP * Q_BLOCK), F32),
            pltpu.VMEM((N_KV_HEADS, SUBLANES, GROUP * Q_BLOCK), F32),
            pltpu.VMEM((N_KV_HEADS, HEAD_DIM, GROUP * Q_BLOCK), F32),
            pltpu.VMEM((N_KV_HEADS, HEAD_DIM + ONES_ROWS, GROUP * Q_BLOCK), F32),
        ],
        compiler_params=pltpu.CompilerParams(
            dimension_semantics=("arbitrary",), vmem_limit_bytes=VMEM_LIMIT),
        name="prompt_attn",
    )(kmax2, qi, kw, q, ki_lo, ki_hi, k_all, v_t)


SCORE_PAGE_SLOTS = 4
KV_PAGE_SLOTS = 3


def _prefetch_pages(copies, n_pages, n_slots, b, n_seq):
    depth = n_slots - 1

    def start(seq):
        for j in range(n_pages):
            for cp in copies(seq, seq % n_slots, j):
                cp.start()

    @pl.when(b == 0)
    def _():
        for s in range(depth):
            start(s)

    @pl.when(b + depth < n_seq)
    def _():
        start(b + depth)

    for j in range(n_pages):
        for cp in copies(b, b % n_slots, j):
            cp.wait()
    return b % n_slots


def _decode_scores_kernel(pt_ref, qi_ref, w_ref, kinew_ref, cache_ref, s_ref, buf, sem):
    b = pl.program_id(0)
    n_pages = pt_ref.shape[1]

    def copies(seq, slot, j):
        return (pltpu.make_async_copy(
            cache_ref.at[0, pt_ref[seq, j]], buf.at[slot, j], sem.at[slot]),)

    slot = _prefetch_pages(copies, n_pages, buf.shape[0], b, pl.num_programs(0))

    qi = qi_ref[...]
    w = w_ref[...] * (IDX_DIM ** -0.5)
    for j in range(n_pages):
        kpt = buf[slot, j].astype(BF16)
        d = jnp.dot(qi, kpt, preferred_element_type=F32)
        s_ref[:, j * PAGE_SIZE:(j + 1) * PAGE_SIZE] = jnp.sum(
            jnp.maximum(d, 0.0) * w, axis=0, keepdims=True)
    d_new = jnp.sum(qi.astype(F32) * kinew_ref[...].astype(F32), axis=-1, keepdims=True)
    s_new = jnp.sum(jnp.maximum(d_new, 0.0) * w, axis=0, keepdims=True)
    past = n_pages * PAGE_SIZE
    tail = s_ref.shape[1] - past
    lane = lax.broadcasted_iota(jnp.int32, (1, tail), 1)
    s_ref[:, past:] = jnp.where(lane == 0, s_new, NEG_INF)


def _decode_scores(page_table, qi_s, w_s, kinew_s, cache_kidx_t, width):
    db, n_pages = page_table.shape
    seq_spec = lambda a, c: pl.BlockSpec((None, a, c), lambda b, pt: (b, 0, 0))
    return pl.pallas_call(
        _decode_scores_kernel,
        grid_spec=pltpu.PrefetchScalarGridSpec(
            num_scalar_prefetch=1,
            grid=(db,),
            in_specs=[seq_spec(N_IDX_HEADS, IDX_DIM), seq_spec(N_IDX_HEADS, 1),
                      seq_spec(1, IDX_DIM), pl.BlockSpec(memory_space=pl.ANY)],
            out_specs=seq_spec(1, width),
            scratch_shapes=[pltpu.VMEM((SCORE_PAGE_SLOTS, n_pages, IDX_DIM, PAGE_SIZE), F32),
                            pltpu.SemaphoreType.DMA((SCORE_PAGE_SLOTS,))],
        ),
        out_shape=jax.ShapeDtypeStruct((db, 1, width), F32),
        compiler_params=pltpu.CompilerParams(dimension_semantics=("arbitrary",)),
        name="decode_scores",
    )(page_table, qi_s, w_s, kinew_s, cache_kidx_t)


def _decode_mask_kernel(s_in_ref, bias_ref, st_ref):
    nk = s_in_ref.shape[0]
    chunk = KEY_CHUNK
    s = s_in_ref[...]
    st_ref[...] = s
    valid = s > NEG_INF
    smin = jnp.min(jnp.where(valid, s, POS_INF), axis=0, keepdims=True)
    smax = jnp.max(s, axis=0, keepdims=True)
    n_valid = jnp.sum(valid.astype(F32), axis=0, keepdims=True)
    tau = _topk_threshold(st_ref, nk // chunk, chunk, n_valid, smin, smax)
    bias_ref[...] = jnp.where(st_ref[...] >= tau, 0.0, NEG)


def _decode_mask(scores_t):
    assert scores_t.shape[0] % KEY_CHUNK == 0 and scores_t.shape[1] == LANES
    return pl.pallas_call(
        _decode_mask_kernel,
        out_shape=jax.ShapeDtypeStruct(scores_t.shape, F32),
        scratch_shapes=[pltpu.VMEM(scores_t.shape, F32)],
        name="decode_mask",
    )(scores_t)


def _decode_attn_kernel(pt_ref, q_ref, bias_ref, knew_ref, vnew_ref, ck_ref, cv_ref, o_ref,
                        kbuf, vbuf, sem):
    b = pl.program_id(0)
    n_pages = pt_ref.shape[1]
    page_rows = N_KV_HEADS * PAGE_SIZE

    def copies(seq, slot, j):
        rows_j = pl.ds(j * page_rows, page_rows)
        page = pt_ref[seq, j]
        return (pltpu.make_async_copy(ck_ref.at[0, page], kbuf.at[slot, rows_j], sem.at[slot, 0]),
                pltpu.make_async_copy(cv_ref.at[0, page], vbuf.at[slot, rows_j], sem.at[slot, 1]))

    slot = _prefetch_pages(copies, n_pages, kbuf.shape[0], b, pl.num_programs(0))

    rows = n_pages * page_rows
    scale = HEAD_DIM ** -0.5
    q = q_ref[...]
    bias = jnp.concatenate(
        [jnp.broadcast_to(bias_ref[n:n + 1, :], (GROUP, bias_ref.shape[1]))
         for n in range(N_KV_HEADS)], axis=0)
    k_il = kbuf[slot].astype(BF16)
    v_il = vbuf[slot].astype(BF16)
    sc = lax.dot_general(q, k_il, CONTRACT_LAST, preferred_element_type=F32)
    sc = sc * scale + bias[:, :rows]
    sc_new = jnp.sum(q.astype(F32) * knew_ref[...].astype(F32), axis=-1, keepdims=True)
    sc_new = sc_new * scale + jnp.max(bias[:, rows:rows + N_KV_HEADS], axis=-1, keepdims=True)
    m = jnp.maximum(jnp.max(sc, axis=-1, keepdims=True), sc_new)
    p = jnp.exp(sc - m)
    p_new = jnp.exp(sc_new - m)
    denom = jnp.sum(p, axis=-1, keepdims=True) + p_new
    o = jnp.dot(p.astype(BF16), v_il, preferred_element_type=F32)
    o = o + p_new.astype(BF16).astype(F32) * vnew_ref[...].astype(F32)
    o_ref[...] = (o / denom).astype(o_ref.dtype)


def _decode_attention(page_table, q_s, bias_il, knew_s, vnew_s, cache_k, cache_v):
    db, n_pages = page_table.shape
    width2 = bias_il.shape[-1]
    rows = n_pages * N_KV_HEADS * PAGE_SIZE
    seq_spec = lambda a, c: pl.BlockSpec((None, a, c), lambda b, pt: (b, 0, 0))
    return pl.pallas_call(
        _decode_attn_kernel,
        grid_spec=pltpu.PrefetchScalarGridSpec(
            num_scalar_prefetch=1,
            grid=(db,),
            in_specs=[seq_spec(N_HEADS, HEAD_DIM), seq_spec(N_KV_HEADS, width2),
                      seq_spec(N_HEADS, HEAD_DIM), seq_spec(N_HEADS, HEAD_DIM),
                      pl.BlockSpec(memory_space=pl.ANY), pl.BlockSpec(memory_space=pl.ANY)],
            out_specs=seq_spec(N_HEADS, HEAD_DIM),
            scratch_shapes=[pltpu.VMEM((KV_PAGE_SLOTS, rows, HEAD_DIM), F32),
                            pltpu.VMEM((KV_PAGE_SLOTS, rows, HEAD_DIM), F32),
                            pltpu.SemaphoreType.DMA((KV_PAGE_SLOTS, 2))],
        ),
        out_shape=jax.ShapeDtypeStruct((db, N_HEADS, HEAD_DIM), BF16),
        compiler_params=pltpu.CompilerParams(
            dimension_semantics=("arbitrary",), vmem_limit_bytes=VMEM_LIMIT),
        name="decode_attn",
    )(page_table, q_s, bias_il, knew_s, vnew_s, cache_k, cache_v)


def _pool_decode_kernel(hist_ref, u_ref, d_ref):
    for g, w in enumerate(POOL_WINDOWS):
        cols = slice(g * POOL_GROUP_WIDTH, (g + 1) * POOL_GROUP_WIDTH)
        cur = u_ref[:, cols]
        tot = cur
        for j in range(1, w):
            tot = tot + hist_ref[POOL_HIST - j, :, cols]
        d_ref[:, cols] = tot * (1.0 / w) - cur


def _pool_decode(hist, u):
    return pl.pallas_call(
        _pool_decode_kernel,
        out_shape=jax.ShapeDtypeStruct(u.shape, F32),
        name="pool_decode",
    )(hist, u)


FF_TILE = 1024


def _rmsnorm(x, g):
    ms = jnp.mean(x * x, axis=-1, keepdims=True)
    return (x * lax.rsqrt(ms + EPS)) * g


def _post_kernel(x_ref, att_ref, d_ref, woa_ref, wop_ref, wpool_ref, pscale_ref,
                 gmlp_ref, gfin_ref, wup_ref, wdn_ref, y_ref, hn_ref):
    f = pl.program_id(1)

    @pl.when(f == 0)
    def _():
        d = d_ref[...]
        pooled = jnp.concatenate(
            [jnp.dot(d[:, g * POOL_GROUP_WIDTH:(g + 1) * POOL_GROUP_WIDTH].astype(BF16),
                     wpool_ref[g], preferred_element_type=F32)
             for g in range(len(POOL_WINDOWS))], axis=1) * pscale_ref[...]
        h = (x_ref[...]
             + jnp.dot(att_ref[...], woa_ref[...], preferred_element_type=F32)
             + jnp.dot(pooled.astype(BF16), wop_ref[...], preferred_element_type=F32))
        y_ref[...] = h
        hn_ref[...] = _rmsnorm(h, gmlp_ref[...]).astype(BF16)

    a = jnp.maximum(jnp.dot(hn_ref[...], wup_ref[...], preferred_element_type=F32), 0.0)
    y_ref[...] += jnp.dot((a * a).astype(BF16), wdn_ref[...], preferred_element_type=F32)

    @pl.when(f == pl.num_programs(1) - 1)
    def _():
        y_ref[...] = _rmsnorm(y_ref[...], gfin_ref[...])


def _post(x, att, d, wo_a, wo_p, w_pool, pool_scale, g_mlp, g_final, w_up, w_down, tm):
    rows = x.shape[0]
    assert rows % tm == 0 and D_FF % FF_TILE == 0
    row_spec = lambda w: pl.BlockSpec((tm, w), lambda i, f: (i, 0))
    return pl.pallas_call(
        _post_kernel,
        grid=(rows // tm, D_FF // FF_TILE),
        in_specs=[row_spec(D_MODEL), row_spec(ATTN_WIDTH), row_spec(POOL_WIDTH),
                  _const_spec(wo_a.shape), _const_spec(wo_p.shape), _const_spec(w_pool.shape),
                  _const_spec((1, POOL_WIDTH)), _const_spec((1, D_MODEL)), _const_spec((1, D_MODEL)),
                  pl.BlockSpec((D_MODEL, FF_TILE), lambda i, f: (0, f)),
                  pl.BlockSpec((FF_TILE, D_MODEL), lambda i, f: (f, 0))],
        out_specs=row_spec(D_MODEL),
        out_shape=jax.ShapeDtypeStruct((rows, D_MODEL), F32),
        scratch_shapes=[pltpu.VMEM((tm, D_MODEL), BF16)],
        compiler_params=pltpu.CompilerParams(
            dimension_semantics=("arbitrary", "arbitrary"), vmem_limit_bytes=VMEM_LIMIT),
        name="post",
    )(x, att, d, wo_a, wo_p, w_pool, pool_scale.reshape(1, POOL_WIDTH),
      g_mlp.reshape(1, D_MODEL), g_final.reshape(1, D_MODEL), w_up, w_down)


def kernel(x_prompt, x_sample, cache_k, cache_v, cache_kidx, state_pool, page_table, meta_tokens,
           g_mix, w_in, w_pool, pool_scale, w_out, g_mlp, w_up, w_down, g_final):
    assert x_prompt.shape[0] == 1 and x_sample.shape[1] == 1 and g_mix.shape[0] == 1
    seq = x_prompt.shape[1]
    db = x_sample.shape[0]
    n_pages = page_table.shape[1]
    past = n_pages * PAGE_SIZE
    assert db == LANES

    w = w_in[0]
    o = 0
    parts = []
    for width in (ATTN_WIDTH, 2 * N_KV_HEADS * HEAD_DIM, N_IDX_HEADS * IDX_DIM,
                  IDX_DIM + N_IDX_HEADS, POOL_WIDTH):
        parts.append(w[:, o:o + width].astype(BF16))
        o += width
    parts[3] = jnp.pad(parts[3], ((0, 0), (0, LANES - parts[3].shape[1])))
    wo_a = w_out[0, :ATTN_WIDTH].astype(BF16)
    wo_p = w_out[0, ATTN_WIDTH:].astype(BF16)
    w_pool_b = w_pool[0].astype(BF16)

    xs = jnp.concatenate([meta_tokens.astype(F32), x_sample[:, 0]], axis=0)
    pos_s = jnp.concatenate([jnp.arange(N_META, dtype=jnp.int32),
                             jnp.full((db,), past, jnp.int32)])
    q_s, kf_s, vf_s, kb_s, vb_s, qi_s, kw_s, u_s = _project(
        xs, jnp.zeros((1,), jnp.int32), pos_s, g_mix[0], parts)
    xp = x_prompt[0]
    assert seq % PROJ_TILE == 0
    tile_base = N_META + PROJ_TILE * jnp.arange(seq // PROJ_TILE, dtype=jnp.int32)
    q_p, kf_p, vf_p, kb_p, vb_p, qi_p, kw_p, d_p, u_tail, w_up_b, w_down_b = _project(
        xp, tile_base, jnp.arange(PROJ_TILE, dtype=jnp.int32), g_mix[0], parts,
        halo=u_s[:N_META], mlp_weights=(w_up[0], w_down[0]))

    nk = -(-(seq + 2 * Q_BLOCK) // KEY_CHUNK) * KEY_CHUNK

    def key_rows(meta_part, prompt_part):
        width = meta_part.shape[1]
        return jnp.concatenate(
            [meta_part, jnp.zeros((Q_BLOCK - N_META, width), meta_part.dtype), prompt_part,
             jnp.zeros((nk - Q_BLOCK - seq, width), meta_part.dtype)], axis=0)

    ki_all = key_rows(kw_s[:N_META, :IDX_DIM], kw_p[:, :IDX_DIM]).astype(BF16)
    zk = jnp.zeros_like(ki_all)
    ki_lo = jnp.concatenate([ki_all, zk], axis=1)
    ki_hi = jnp.concatenate([zk, ki_all], axis=1)
    k_all = key_rows(kb_s[:N_META], kb_p)
    v_all = key_rows(vb_s[:N_META], vb_p)
    v_t = jnp.transpose(
        v_all.reshape(nk // KEY_CHUNK, KEY_CHUNK, N_KV_HEADS, HEAD_DIM), (0, 2, 3, 1))
    v_t = jnp.concatenate(
        [v_t, jnp.ones((nk // KEY_CHUNK, N_KV_HEADS, ONES_ROWS, KEY_CHUNK), BF16)], axis=2)
    knorm = lambda kw: kw[:, KNORM_LANE:KNORM_LANE + N_KV_HEADS]
    kmax2 = jnp.maximum(jnp.max(knorm(kw_p), axis=0), jnp.max(knorm(kw_s[:N_META]), axis=0))
    att_p = _prompt_attention(kmax2, qi_p, kw_p, q_p, ki_lo, ki_hi, k_all, v_t)

    width = -(-(past + 1) // KEY_CHUNK) * KEY_CHUNK
    qi_d = qi_s[N_META:].reshape(db, N_IDX_HEADS, IDX_DIM)
    w_d = kw_s[N_META:, IDX_DIM:IDX_DIM + N_IDX_HEADS].reshape(db, N_IDX_HEADS, 1)
    kinew = kw_s[N_META:, :IDX_DIM].astype(BF16).reshape(db, 1, IDX_DIM)
    scores = _decode_scores(page_table, qi_d, w_d, kinew, jnp.swapaxes(cache_kidx, 2, 3), width)
    bias_t = _decode_mask(jnp.transpose(scores.reshape(db, width)))
    bias = jnp.transpose(bias_t)
    negs = jnp.full_like(bias, NEG)
    bias_il = jnp.stack(
        [jnp.stack([bias, negs], axis=-1), jnp.stack([negs, bias], axis=-1)], axis=1,
    ).reshape(db, N_KV_HEADS, N_KV_HEADS * width)
    n_phys = cache_k.shape[1]
    il_shape = (1, n_phys, N_KV_HEADS * PAGE_SIZE, HEAD_DIM)
    per_head = lambda a: jnp.repeat(a.reshape(db, N_KV_HEADS, HEAD_DIM), GROUP, axis=1)
    att_s = _decode_attention(
        page_table, q_s[N_META:].reshape(db, N_HEADS, HEAD_DIM), bias_il,
        per_head(kb_s[N_META:]), per_head(vb_s[N_META:]),
        cache_k.reshape(il_shape), cache_v.reshape(il_shape))
    att_s = att_s.reshape(db, ATTN_WIDTH)

    d_s = _pool_decode(jnp.swapaxes(state_pool[0], 0, 1), u_s[N_META:])

    post = functools.partial(
        _post, wo_a=wo_a, wo_p=wo_p, w_pool=w_pool_b, pool_scale=pool_scale[0],
        g_mlp=g_mlp[0], g_final=g_final, w_up=w_up_b, w_down=w_down_b)
    y_p = post(xp, att_p, d_p, tm=512)
    y_s = post(x_sample[:, 0], att_s, d_s, tm=db)

    k_prompt = jnp.concatenate([kf_s[:N_META], kf_p], axis=0).reshape(1, 1, N_META + seq, N_KV_HEADS, HEAD_DIM)
    v_prompt = jnp.concatenate([vf_s[:N_META], vf_p], axis=0).reshape(1, 1, N_META + seq, N_KV_HEADS, HEAD_DIM)
    kidx_prompt = jnp.concatenate([kw_s[:N_META, :IDX_DIM], kw_p[:, :IDX_DIM]], axis=0).reshape(
        1, 1, N_META + seq, IDX_DIM)
    pool_prompt = u_tail[POOL_HALO - POOL_HIST:].reshape(1, 1, POOL_HIST, POOL_WIDTH)
    k_sample = kf_s[N_META:].reshape(1, db, 1, N_KV_HEADS, HEAD_DIM)
    v_sample = vf_s[N_META:].reshape(1, db, 1, N_KV_HEADS, HEAD_DIM)
    kidx_sample = kw_s[N_META:, :IDX_DIM].reshape(1, db, 1, IDX_DIM)
    pool_sample = jnp.concatenate(
        [state_pool[0][:, 1:], u_s[N_META:][:, None, :]], axis=1)[None]
    return (y_p.reshape(1, seq, D_MODEL), y_s.reshape(db, 1, D_MODEL),
            k_prompt, v_prompt, kidx_prompt, pool_prompt,
            k_sample, v_sample, kidx_sample, pool_sample)
```

```python
import functools

import jax
import jax.numpy as jnp
from jax import lax
from jax.experimental import pallas as pl
from jax.experimental.pallas import tpu as pltpu

F32 = jnp.float32
BF16 = jnp.bfloat16

D_MODEL = 2048
N_META = 16
ATTN_WIDTH = 1024
POOL_WIDTH = 1024
HEAD_DIM = 128
N_HEADS = 8
N_KV_HEADS = 2
GROUP = N_HEADS // N_KV_HEADS
N_IDX_HEADS = 16
IDX_DIM = 64
TOPK = 256
POOL_WINDOWS = (2, 4, 8, 16)
POOL_GROUP_WIDTH = 256
POOL_HIST = 15
D_FF = 8192
PAGE_SIZE = 128
ROPE_THETA = 10000.0
EPS = 1e-6
NEG = -1e30
NEG_INF = float("-inf")
POS_INF = float("inf")
F32_LOWEST = -3.0e38

LANES = 128
SUBLANES = 8
VMEM_LIMIT = 56 * 1024 * 1024

Q_BLOCK = 128
KEY_CHUNK = 512
MAX_BISECT = 320
N_UNCHECKED_ROUNDS = 8
LOG2E = 1.4426950408889634
MAX_EXP2_GAP = 80.0
NORM_BOUND_SLACK = 1.02
ONES_ROWS = 16
KNORM_LANE = IDX_DIM + N_IDX_HEADS
PROJ_TILE = 256
ATTN_GROUP = 2
POOL_HALO = 16

CONTRACT_LAST = (((1,), (1,)), ((), ()))


def _const_spec(shape):
    nd = len(shape)
    return pl.BlockSpec(shape, lambda *_: (0,) * nd, pipeline_mode=pl.Buffered(1))


def _rope128(x, c, s):
    return x * c + pltpu.roll(x, 64, 1) * s


def _rope64(x, c, sa, sb):
    return x * c + pltpu.roll(x, 96, 1) * sa + pltpu.roll(x, 32, 1) * sb


def _angle_sum(cb_ref, sb_ref, cr_ref, sr_ref):
    cb, sb, cr, sr = cb_ref[...], sb_ref[...], cr_ref[...], sr_ref[...]
    return cb * cr - sb * sr, sb * cr + cb * sr


def _proj_kernel(x_ref, g_ref, cb128_ref, sb128_ref, cb64_ref, sb64_ref,
                 cr128_ref, sr128_ref, cr64_ref, sr64_ref,
                 wq_ref, wkv_ref, wqi_ref, wkw_ref, wu_ref, *rest, with_pool):
    if with_pool:
        (halo_ref, wup_ref, wdn_ref, q_ref, kf_ref, vf_ref, kb_ref, vb_ref, qi_ref, kw_ref, u_ref,
         utail_ref, wupb_ref, wdnb_ref, ubuf) = rest
        wupb_ref[...] = wup_ref[...].astype(BF16)
        wdnb_ref[...] = wdn_ref[...].astype(BF16)
    else:
        q_ref, kf_ref, vf_ref, kb_ref, vb_ref, qi_ref, kw_ref, u_ref = rest
    x = x_ref[...]
    ms = jnp.mean(x * x, axis=-1, keepdims=True)
    xn = ((x * lax.rsqrt(ms + EPS)) * g_ref[...]).astype(BF16)
    tm = x.shape[0]
    lane_t = lax.broadcasted_iota(jnp.int32, (tm, LANES), 1)
    c128, sfull = _angle_sum(cb128_ref, sb128_ref, cr128_ref, sr128_ref)
    s128 = jnp.where(lane_t < HEAD_DIM // 2, -sfull, sfull)
    c64, sfull = _angle_sum(cb64_ref, sb64_ref, cr64_ref, sr64_ref)
    first_half = jnp.bitwise_and(lane_t, IDX_DIM - 1) < IDX_DIM // 2
    sa64 = jnp.where(first_half, -sfull, 0.0)
    sb64 = jnp.where(first_half, 0.0, sfull)

    zq = jnp.dot(xn, wq_ref[...], preferred_element_type=F32)
    for h in range(N_HEADS):
        sl = slice(h * LANES, (h + 1) * LANES)
        q_ref[:, sl] = _rope128(zq[:, sl], c128, s128).astype(BF16)

    zkv = jnp.dot(xn, wkv_ref[...], preferred_element_type=F32)
    k_norm2 = []
    for h in range(N_KV_HEADS):
        sl = slice(h * LANES, (h + 1) * LANES)
        kr = _rope128(zkv[:, sl], c128, s128)
        kf_ref[:, sl] = kr
        kb = kr.astype(BF16)
        kb_ref[:, sl] = kb
        k_norm2.append(jnp.sum(kb.astype(F32) * kb.astype(F32), axis=-1, keepdims=True))
    v = zkv[:, N_KV_HEADS * LANES:]
    vf_ref[...] = v
    vb_ref[...] = v.astype(BF16)

    zqi = jnp.dot(xn, wqi_ref[...], preferred_element_type=F32)
    for p in range(N_IDX_HEADS // 2):
        sl = slice(p * LANES, (p + 1) * LANES)
        qi_ref[:, sl] = _rope64(zqi[:, sl], c64, sa64, sb64).astype(BF16)

    zkw = jnp.dot(xn, wkw_ref[...], preferred_element_type=F32)
    lane = lax.broadcasted_iota(jnp.int32, zkw.shape, 1)
    is_key = lane < IDX_DIM
    ckw = jnp.where(is_key, c64, jnp.where(lane < IDX_DIM + N_IDX_HEADS, N_IDX_HEADS ** -0.5, 1.0))
    kw = _rope64(zkw, ckw, jnp.where(is_key, sa64, 0.0), jnp.where(is_key, sb64, 0.0))
    for h in range(N_KV_HEADS):
        kw = jnp.where(lane == KNORM_LANE + h, k_norm2[h], kw)
    kw_ref[...] = kw

    u = jnp.dot(xn, wu_ref[...], preferred_element_type=F32)
    if not with_pool:
        u_ref[...] = u
        return

    @pl.when(pl.program_id(0) == 0)
    def _():
        ubuf[0:POOL_HALO, :] = halo_ref[...]

    ubuf[POOL_HALO:, :] = u
    for g, w in enumerate(POOL_WINDOWS):
        cols = slice(g * POOL_GROUP_WIDTH, (g + 1) * POOL_GROUP_WIDTH)
        cur = ubuf[POOL_HALO:POOL_HALO + tm, cols]
        tot = cur
        for j in range(1, w):
            tot = tot + ubuf[POOL_HALO - j:POOL_HALO - j + tm, cols]
        u_ref[:, cols] = tot * (1.0 / w) - cur
    tail = ubuf[tm:tm + POOL_HALO, :]
    ubuf[0:POOL_HALO, :] = tail
    utail_ref[...] = tail


def _project(x, pos_base, pos_rel, g_mix, w_parts, halo=None, mlp_weights=None):
    tm = pos_rel.shape[0]
    n_tiles = pos_base.shape[0]
    rows = x.shape[0]
    assert rows == tm * n_tiles
    with_pool = halo is not None
    assert with_pool == (mlp_weights is not None)
    base_tabs = [t.reshape(n_tiles, 1, LANES) for t in _rope_tables(pos_base)]
    rel_tabs = _rope_tables(pos_rel)
    row_spec = lambda w: pl.BlockSpec((tm, w), lambda i: (i, 0))
    base_spec = pl.BlockSpec((None, 1, LANES), lambda i: (i, 0, 0))
    out_shapes = [
        jax.ShapeDtypeStruct((rows, ATTN_WIDTH), BF16),
        jax.ShapeDtypeStruct((rows, 256), F32),
        jax.ShapeDtypeStruct((rows, 256), F32),
        jax.ShapeDtypeStruct((rows, 256), BF16),
        jax.ShapeDtypeStruct((rows, 256), BF16),
        jax.ShapeDtypeStruct((rows, 1024), BF16),
        jax.ShapeDtypeStruct((rows, LANES), F32),
        jax.ShapeDtypeStruct((rows, POOL_WIDTH), F32),
    ]
    out_specs = [row_spec(s.shape[1]) for s in out_shapes]
    in_specs = ([row_spec(D_MODEL), _const_spec((1, D_MODEL))] + [base_spec] * 4
                + [_const_spec((tm, LANES))] * 4 + [_const_spec(w.shape) for w in w_parts])
    args = [x, g_mix.reshape(1, D_MODEL), *base_tabs, *rel_tabs, *w_parts]
    scratch = []
    if with_pool:
        in_specs.append(_const_spec((POOL_HALO, POOL_WIDTH)))
        args.append(halo)
        out_shapes.append(jax.ShapeDtypeStruct((POOL_HALO, POOL_WIDTH), F32))
        out_specs.append(pl.BlockSpec((POOL_HALO, POOL_WIDTH), lambda i: (0, 0)))
        scratch.append(pltpu.VMEM((tm + POOL_HALO, POOL_WIDTH), F32))
        for wgt in mlp_weights:
            assert wgt.shape[0] % n_tiles == 0
            slab = pl.BlockSpec((wgt.shape[0] // n_tiles, wgt.shape[1]), lambda i: (i, 0))
            in_specs.append(slab)
            args.append(wgt)
            out_shapes.append(jax.ShapeDtypeStruct(wgt.shape, BF16))
            out_specs.append(slab)
    return pl.pallas_call(
        functools.partial(_proj_kernel, with_pool=with_pool),
        grid=(n_tiles,),
        in_specs=in_specs,
        out_specs=tuple(out_specs),
        out_shape=tuple(out_shapes),
        scratch_shapes=scratch,
        compiler_params=pltpu.CompilerParams(
            dimension_semantics=("arbitrary",), vmem_limit_bytes=VMEM_LIMIT),
        name="proj",
    )(*args)


def _rope_tables(pos):
    posf = pos.astype(F32)[:, None]
    out = []
    for dim in (HEAD_DIM, IDX_DIM):
        inv = ROPE_THETA ** (-jnp.arange(0, dim, 2, dtype=F32) / dim)
        ang = posf * inv[None, :]
        reps = LANES // (dim // 2)
        out += [jnp.tile(jnp.cos(ang), (1, reps)), jnp.tile(jnp.sin(ang), (1, reps))]
    return out


def _key_count(st_ref, nchunks, chunk, pred):
    acc_rows = 8 * SUBLANES

    def body(c, part):
        c0 = pl.multiple_of(c * chunk, chunk)
        hit = pred(st_ref[pl.ds(c0, chunk), :], c0)
        for g in range(chunk // acc_rows):
            part = jnp.where(hit[g * acc_rows:(g + 1) * acc_rows], part + 1.0, part)
        return part

    part = lax.fori_loop(0, nchunks, body, jnp.zeros((acc_rows, LANES), F32))
    return jnp.sum(part, axis=0, keepdims=True)


def _topk_threshold(st_ref, nchunks, chunk, n_valid, s_min, s_max):
    kf = float(TOPK)
    take_all = n_valid <= kf

    def count_ge(t):
        return _key_count(st_ref, nchunks, chunk, lambda v, c0: v >= t)

    def cond(st):
        it, done = st[0], st[3]
        return jnp.logical_and(it < MAX_BISECT, jnp.min(done) < 0.5)

    def step(st):
        lo, hi, done, tau = st
        mid = lo + (hi - lo) * 0.5
        collapsed = jnp.logical_or(mid <= lo, mid >= hi)
        cnt = count_ge(mid)
        found = cnt == kf
        active = done < 0.5
        lo = jnp.where(jnp.logical_and(active, cnt > kf), mid, lo)
        hi = jnp.where(jnp.logical_and(active, cnt < kf), mid, hi)
        tau = jnp.where(jnp.logical_and(active, found), mid, tau)
        done = jnp.where(jnp.logical_or(found, collapsed), 1.0, done)
        return lo, hi, done, tau

    def body(st):
        return (st[0] + 1,) + step(step(st[1:]))

    done0 = jnp.where(take_all, 1.0, 0.0).astype(F32)
    tau0 = jnp.full((1, LANES), POS_INF, F32)
    st0 = (jnp.int32(0), s_min, s_max, done0, tau0)
    st0 = lax.cond(jnp.min(done0) < 0.5,
                   lambda st: lax.fori_loop(0, N_UNCHECKED_ROUNDS, lambda _, s: body(s), st),
                   lambda st: st, st0)
    _, lo, hi, _, tau = lax.while_loop(cond, body, st0)

    unresolved = jnp.logical_and(tau == POS_INF, jnp.logical_not(take_all))
    n_unres = jnp.sum(unresolved.astype(F32))
    tau = jnp.where(take_all, F32_LOWEST, tau)

    def tie_path(tau):
        cut = jnp.where(count_ge(hi) >= kf, hi, lo)
        cut = jnp.where(unresolved, cut, POS_INF)
        c_gt = _key_count(st_ref, nchunks, chunk, lambda v, c0: v > cut)
        keep = kf - c_gt

        def key_iota(c0):
            return c0 + lax.broadcasted_iota(jnp.int32, (chunk, LANES), 0)

        def idx_body(_, st):
            xlo, xhi = st
            xmid = (xlo + xhi) // 2
            cnt = _key_count(
                st_ref, nchunks, chunk,
                lambda v, c0: jnp.logical_and(v == cut, key_iota(c0) < xmid))
            ok = cnt >= keep
            return jnp.where(ok, xlo, xmid), jnp.where(ok, xmid, xhi)

        xlo0 = jnp.zeros((1, LANES), jnp.int32)
        xhi0 = jnp.full((1, LANES), nchunks * chunk, jnp.int32)
        n_idx_steps = 15
        _, xcut = lax.fori_loop(0, n_idx_steps, idx_body, (xlo0, xhi0))

        def drop_body(c, carry):
            c0 = pl.multiple_of(c * chunk, chunk)
            v = st_ref[pl.ds(c0, chunk), :]
            drop = jnp.logical_and(v == cut, key_iota(c0) >= xcut)
            st_ref[pl.ds(c0, chunk), :] = jnp.where(drop, NEG_INF, v)
            return carry

        lax.fori_loop(0, nchunks, drop_body, 0)
        return jnp.where(unresolved, cut, tau)

    return lax.cond(n_unres > 0.0, tie_path, lambda t: t, tau)


def _prompt_attn_kernel(kmax2_ref, qi_ref, kw_ref, q_ref, kilo_ref, kihi_ref, k_ref, vt_ref, o_ref,
                        st_ref, m_ref, l_ref, acc_ref, accf_ref):
    i = pl.program_id(0)
    tq, ck = Q_BLOCK, KEY_CHUNK
    n_blocks = i + 2
    nchunks = (n_blocks * Q_BLOCK + ck - 1) // ck

    wt = jnp.transpose(kw_ref[...])[IDX_DIM:IDX_DIM + N_IDX_HEADS, :] * (IDX_DIM ** -0.5)
    qcat = jnp.concatenate(
        [qi_ref[:, p * LANES:(p + 1) * LANES] for p in range(N_IDX_HEADS // 2)], axis=0)
    key = lax.broadcasted_iota(jnp.int32, (ck, tq), 0)
    qry = i * tq + lax.broadcasted_iota(jnp.int32, (ck, tq), 1)

    def idx_chunk(c, st):
        smin, smax = st
        c0 = pl.multiple_of(c * ck, ck)
        klo = kilo_ref[pl.ds(c0, ck), :]
        khi = kihi_ref[pl.ds(c0, ck), :]
        acc = jnp.zeros((ck, tq), F32)
        for t in range(N_IDX_HEADS // 4):
            rhs = qcat[2 * t * tq:(2 * t + 2) * tq]
            dlo = lax.dot_general(klo, rhs, CONTRACT_LAST, preferred_element_type=F32)
            dhi = lax.dot_general(khi, rhs, CONTRACT_LAST, preferred_element_type=F32)
            for j in range(2):
                p = 2 * t + j
                cs = slice(j * tq, (j + 1) * tq)
                acc = acc + wt[2 * p:2 * p + 1, :] * jnp.maximum(dlo[:, cs], 0.0)
                acc = acc + wt[2 * p + 1:2 * p + 2, :] * jnp.maximum(dhi[:, cs], 0.0)
        col = c0 + key
        valid = jnp.logical_or(
            col < N_META,
            jnp.logical_and(col >= Q_BLOCK, col - Q_BLOCK <= qry))
        st_ref[pl.ds(c0, ck), :] = jnp.where(valid, acc, NEG_INF)
        smin = jnp.minimum(smin, jnp.min(jnp.where(valid, acc, POS_INF), axis=0, keepdims=True))
        smax = jnp.maximum(smax, jnp.max(jnp.where(valid, acc, NEG_INF), axis=0, keepdims=True))
        return smin, smax

    st = lax.fori_loop(
        0, nchunks // 2, lambda j, st: idx_chunk(2 * j + 1, idx_chunk(2 * j, st)),
        (jnp.full((1, tq), POS_INF, F32), jnp.full((1, tq), NEG_INF, F32)))
    smin, smax = lax.cond(nchunks % 2 == 1, lambda st: idx_chunk(nchunks - 1, st), lambda st: st, st)

    n_valid = (N_META + 1 + i * tq + lax.broadcasted_iota(jnp.int32, (1, tq), 1)).astype(F32)
    tau = _topk_threshold(st_ref, nchunks, ck, n_valid, smin, smax)

    scale = HEAD_DIM ** -0.5

    def q_group(n):
        return jnp.concatenate(
            [q_ref[:, (n * GROUP + g) * LANES:(n * GROUP + g + 1) * LANES]
             for g in range(GROUP)], axis=0)

    def k_chunk(c0, n):
        return k_ref[pl.ds(c0, ck), n * LANES:(n + 1) * LANES]

    def write_out(n, ot):
        for g in range(GROUP):
            h = n * GROUP + g
            o_ref[:, h * LANES:(h + 1) * LANES] = jnp.transpose(
                ot[:, g * tq:(g + 1) * tq]).astype(o_ref.dtype)

    def ref_body(c, mx):
        c0 = pl.multiple_of(c * ck, ck)
        sel = st_ref[pl.ds(c0, ck), :] >= tau
        out = []
        for n in range(N_KV_HEADS):
            raw = lax.dot_general(k_chunk(c0, n), q_group(n), CONTRACT_LAST,
                                  preferred_element_type=F32)
            cur = jnp.concatenate(
                [jnp.max(jnp.where(sel, raw[:, g * tq:(g + 1) * tq], NEG), axis=0, keepdims=True)
                 for g in range(GROUP)], axis=1)
            out.append(jnp.maximum(mx[n], cur))
        return tuple(out)

    mx = ref_body(0, tuple(jnp.full((1, GROUP * tq), NEG, F32) for _ in range(N_KV_HEADS)))
    c2 = scale * LOG2E
    ones = jnp.ones((SUBLANES, HEAD_DIM), BF16)
    gap = jnp.float32(0.0)
    for n in range(N_KV_HEADS):
        qf = q_group(n).astype(F32)
        qn2 = lax.dot_general(ones, (qf * qf).astype(BF16), CONTRACT_LAST,
                              preferred_element_type=F32)[0:1]
        bound = jnp.sqrt(qn2 * kmax2_ref[n]) * (c2 * NORM_BOUND_SLACK)
        gap = jnp.maximum(gap, jnp.max(bound - mx[n] * c2))

    def fixed_reference_path():
        accf_ref[...] = jnp.zeros(accf_ref.shape, F32)

        def chunk_update(c, n):
            c0 = pl.multiple_of(c * ck, ck)
            sel = st_ref[pl.ds(c0, ck), :] >= tau
            raw = lax.dot_general(k_chunk(c0, n), q_group(n), CONTRACT_LAST,
                                  preferred_element_type=F32)
            neg_ref = mx[n] * (-c2)
            p = jnp.concatenate(
                [jnp.exp2(raw[:, g * tq:(g + 1) * tq] * c2
                          + jnp.where(sel, neg_ref[:, g * tq:(g + 1) * tq], NEG)).astype(BF16)
                 for g in range(GROUP)], axis=1)
            return jnp.dot(vt_ref[c, n], p, preferred_element_type=F32)

        def group_body(j, carry):
            for n in range(N_KV_HEADS):
                upd = chunk_update(ATTN_GROUP * j, n)
                for e in range(1, ATTN_GROUP):
                    upd = upd + chunk_update(ATTN_GROUP * j + e, n)
                accf_ref[n] += upd
            return carry

        n_groups = nchunks // ATTN_GROUP
        lax.fori_loop(0, n_groups, group_body, 0)

        def single_body(c, carry):
            for n in range(N_KV_HEADS):
                accf_ref[n] += chunk_update(c, n)
            return carry

        lax.fori_loop(n_groups * ATTN_GROUP, nchunks, single_body, 0)

        for n in range(N_KV_HEADS):
            write_out(n, accf_ref[n, 0:HEAD_DIM, :] / accf_ref[n, HEAD_DIM:HEAD_DIM + 1, :])

    def running_max_path():
        m_ref[...] = jnp.full(m_ref.shape, NEG, F32)
        l_ref[...] = jnp.zeros(l_ref.shape, F32)
        acc_ref[...] = jnp.zeros(acc_ref.shape, F32)

        def body(c, carry):
            c0 = pl.multiple_of(c * ck, ck)
            bias = jnp.where(st_ref[pl.ds(c0, ck), :] >= tau, 0.0, NEG)
            bias = jnp.concatenate([bias] * GROUP, axis=1)
            for n in range(N_KV_HEADS):
                sc = lax.dot_general(k_chunk(c0, n), q_group(n), CONTRACT_LAST,
                                     preferred_element_type=F32)
                sc = sc * scale + bias
                m_old = m_ref[n]
                m_new = jnp.maximum(m_old, jnp.max(sc, axis=0, keepdims=True))
                alpha = jnp.exp(m_old - m_new)
                p = jnp.exp(sc - m_new)
                l_ref[n] = alpha * l_ref[n] + jnp.sum(
                    p.reshape(ck // SUBLANES, SUBLANES, GROUP * tq), axis=0)
                acc_ref[n] = alpha * acc_ref[n] + jnp.dot(
                    vt_ref[c, n, 0:HEAD_DIM, :], p.astype(BF16), preferred_element_type=F32)
                m_ref[n] = m_new
            return carry

        lax.fori_loop(0, nchunks, body, 0)
        for n in range(N_KV_HEADS):
            write_out(n, acc_ref[n] / jnp.sum(l_ref[n], axis=0, keepdims=True))

    lax.cond(gap <= MAX_EXP2_GAP, fixed_reference_path, running_max_path)


def _prompt_attention(kmax2, qi, kw, q, ki_lo, ki_hi, k_all, v_t):
    rows = q.shape[0]
    nk = k_all.shape[0]
    assert rows % Q_BLOCK == 0 and nk % KEY_CHUNK == 0
    assert nk >= (rows // Q_BLOCK + 1) * Q_BLOCK
    assert v_t.shape == (nk // KEY_CHUNK, N_KV_HEADS, HEAD_DIM + ONES_ROWS, KEY_CHUNK)
    row_spec = lambda w: pl.BlockSpec((Q_BLOCK, w), lambda i: (i, 0))
    return pl.pallas_call(
        _prompt_attn_kernel,
        grid=(rows // Q_BLOCK,),
        in_specs=[pl.BlockSpec(memory_space=pltpu.SMEM),
                  row_spec(1024), row_spec(LANES), row_spec(ATTN_WIDTH),
                  _const_spec(ki_lo.shape), _const_spec(ki_hi.shape),
                  _const_spec(k_all.shape), _const_spec(v_t.shape)],
        out_specs=row_spec(ATTN_WIDTH),
        out_shape=jax.ShapeDtypeStruct((rows, ATTN_WIDTH), BF16),
        scratch_shapes=[
            pltpu.VMEM((nk, Q_BLOCK), F32),
            pltpu.VMEM((N_KV_HEADS, 1, GROUP * Q_BLOCK), F32),
            pltpu.VMEM((N_KV_HEADS, SUBLANES, GROUP * Q_BLOCK), F32),
            pltpu.VMEM((N_KV_HEADS, HEAD_DIM, GROUP * Q_BLOCK), F32),
            pltpu.VMEM((N_KV_HEADS, HEAD_DIM + ONES_ROWS, GROUP * Q_BLOCK), F32),
        ],
        compiler_params=pltpu.CompilerParams(
            dimension_semantics=("arbitrary",), vmem_limit_bytes=VMEM_LIMIT),
        name="prompt_attn",
    )(kmax2, qi, kw, q, ki_lo, ki_hi, k_all, v_t)


SCORE_PAGE_SLOTS = 4
SCORE_SEQS_PER_STEP = 4
KV_PAGE_SLOTS = 3


def _prefetch_pages(copies, n_pages, n_slots, b, n_seq):
    depth = n_slots - 1

    def start(seq):
        for j in range(n_pages):
            for cp in copies(seq, seq % n_slots, j):
                cp.start()

    @pl.when(b == 0)
    def _():
        for s in range(depth):
            start(s)

    @pl.when(b + depth < n_seq)
    def _():
        start(b + depth)

    for j in range(n_pages):
        for cp in copies(b, b % n_slots, j):
            cp.wait()
    return b % n_slots


def _decode_scores_kernel(pt_ref, qi_ref, w_ref, kinew_ref, cache_ref, s_ref, buf, sem):
    b = pl.program_id(0)
    n_pages = pt_ref.shape[1]

    n_seq = qi_ref.shape[0]

    def copies(unit, slot, j):
        page = pt_ref[n_seq * unit + j // n_pages, j % n_pages]
        return (pltpu.make_async_copy(cache_ref.at[0, page], buf.at[slot, j], sem.at[slot]),)

    slot = _prefetch_pages(copies, n_seq * n_pages, buf.shape[0], b, pl.num_programs(0))

    past = n_pages * PAGE_SIZE
    tail = s_ref.shape[2] - past
    lane = lax.broadcasted_iota(jnp.int32, (1, tail), 1)
    for e in range(n_seq):
        qi = qi_ref[e]
        w = w_ref[e] * (IDX_DIM ** -0.5)
        for j in range(n_pages):
            kpt = buf[slot, e * n_pages + j].astype(BF16)
            d = jnp.dot(qi, kpt, preferred_element_type=F32)
            s_ref[e, :, j * PAGE_SIZE:(j + 1) * PAGE_SIZE] = jnp.sum(
                jnp.maximum(d, 0.0) * w, axis=0, keepdims=True)
        d_new = jnp.sum(qi.astype(F32) * kinew_ref[e].astype(F32), axis=-1, keepdims=True)
        s_new = jnp.sum(jnp.maximum(d_new, 0.0) * w, axis=0, keepdims=True)
        s_ref[e, :, past:] = jnp.where(lane == 0, s_new, NEG_INF)


def _decode_scores(page_table, qi_s, w_s, kinew_s, cache_kidx_t, width):
    db, n_pages = page_table.shape
    per = SCORE_SEQS_PER_STEP
    assert db % per == 0
    seq_spec = lambda a, c: pl.BlockSpec((per, a, c), lambda b, pt: (b, 0, 0))
    return pl.pallas_call(
        _decode_scores_kernel,
        grid_spec=pltpu.PrefetchScalarGridSpec(
            num_scalar_prefetch=1,
            grid=(db // per,),
            in_specs=[seq_spec(N_IDX_HEADS, IDX_DIM), seq_spec(N_IDX_HEADS, 1),
                      seq_spec(1, IDX_DIM), pl.BlockSpec(memory_space=pl.ANY)],
            out_specs=seq_spec(1, width),
            scratch_shapes=[pltpu.VMEM((SCORE_PAGE_SLOTS, per * n_pages, IDX_DIM, PAGE_SIZE), F32),
                            pltpu.SemaphoreType.DMA((SCORE_PAGE_SLOTS,))],
        ),
        out_shape=jax.ShapeDtypeStruct((db, 1, width), F32),
        compiler_params=pltpu.CompilerParams(dimension_semantics=("arbitrary",)),
        name="decode_scores",
    )(page_table, qi_s, w_s, kinew_s, cache_kidx_t)


def _decode_mask_kernel(s_in_ref, bias_ref, st_ref):
    nk = s_in_ref.shape[0]
    chunk = KEY_CHUNK
    s = s_in_ref[...]
    st_ref[...] = s
    valid = s > NEG_INF
    smin = jnp.min(jnp.where(valid, s, POS_INF), axis=0, keepdims=True)
    smax = jnp.max(s, axis=0, keepdims=True)
    n_valid = jnp.sum(valid.astype(F32), axis=0, keepdims=True)
    tau = _topk_threshold(st_ref, nk // chunk, chunk, n_valid, smin, smax)
    bias_ref[...] = jnp.where(st_ref[...] >= tau, 0.0, NEG)


def _decode_mask(scores_t):
    assert scores_t.shape[0] % KEY_CHUNK == 0 and scores_t.shape[1] == LANES
    return pl.pallas_call(
        _decode_mask_kernel,
        out_shape=jax.ShapeDtypeStruct(scores_t.shape, F32),
        scratch_shapes=[pltpu.VMEM(scores_t.shape, F32)],
        name="decode_mask",
    )(scores_t)


def _decode_attn_kernel(pt_ref, q_ref, bias_ref, knew_ref, vnew_ref, ck_ref, cv_ref, o_ref,
                        kbuf, vbuf, sem):
    b = pl.program_id(0)
    n_pages = pt_ref.shape[1]
    page_rows = N_KV_HEADS * PAGE_SIZE

    def copies(seq, slot, j):
        rows_j = pl.ds(j * page_rows, page_rows)
        page = pt_ref[seq, j]
        return (pltpu.make_async_copy(ck_ref.at[0, page], kbuf.at[slot, rows_j], sem.at[slot, 0]),
                pltpu.make_async_copy(cv_ref.at[0, page], vbuf.at[slot, rows_j], sem.at[slot, 1]))

    slot = _prefetch_pages(copies, n_pages, kbuf.shape[0], b, pl.num_programs(0))

    rows = n_pages * page_rows
    scale = HEAD_DIM ** -0.5
    q = q_ref[...]
    bias = jnp.concatenate(
        [jnp.broadcast_to(bias_ref[n:n + 1, :], (GROUP, bias_ref.shape[1]))
         for n in range(N_KV_HEADS)], axis=0)
    k_il = kbuf[slot].astype(BF16)
    v_il = vbuf[slot].astype(BF16)
    sc = lax.dot_general(q, k_il, CONTRACT_LAST, preferred_element_type=F32)
    sc = sc * scale + bias[:, :rows]
    sc_new = jnp.sum(q.astype(F32) * knew_ref[...].astype(F32), axis=-1, keepdims=True)
    sc_new = sc_new * scale + jnp.max(bias[:, rows:rows + N_KV_HEADS], axis=-1, keepdims=True)
    m = jnp.maximum(jnp.max(sc, axis=-1, keepdims=True), sc_new)
    p = jnp.exp(sc - m)
    p_new = jnp.exp(sc_new - m)
    denom = jnp.sum(p, axis=-1, keepdims=True) + p_new
    o = jnp.dot(p.astype(BF16), v_il, preferred_element_type=F32)
    o = o + p_new.astype(BF16).astype(F32) * vnew_ref[...].astype(F32)
    o_ref[...] = (o / denom).astype(o_ref.dtype)


def _decode_attention(page_table, q_s, bias_il, knew_s, vnew_s, cache_k, cache_v):
    db, n_pages = page_table.shape
    width2 = bias_il.shape[-1]
    rows = n_pages * N_KV_HEADS * PAGE_SIZE
    seq_spec = lambda a, c: pl.BlockSpec((None, a, c), lambda b, pt: (b, 0, 0))
    return pl.pallas_call(
        _decode_attn_kernel,
        grid_spec=pltpu.PrefetchScalarGridSpec(
            num_scalar_prefetch=1,
            grid=(db,),
            in_specs=[seq_spec(N_HEADS, HEAD_DIM), seq_spec(N_KV_HEADS, width2),
                      seq_spec(N_HEADS, HEAD_DIM), seq_spec(N_HEADS, HEAD_DIM),
                      pl.BlockSpec(memory_space=pl.ANY), pl.BlockSpec(memory_space=pl.ANY)],
            out_specs=seq_spec(N_HEADS, HEAD_DIM),
            scratch_shapes=[pltpu.VMEM((KV_PAGE_SLOTS, rows, HEAD_DIM), F32),
                            pltpu.VMEM((KV_PAGE_SLOTS, rows, HEAD_DIM), F32),
                            pltpu.SemaphoreType.DMA((KV_PAGE_SLOTS, 2))],
        ),
        out_shape=jax.ShapeDtypeStruct((db, N_HEADS, HEAD_DIM), BF16),
        compiler_params=pltpu.CompilerParams(
            dimension_semantics=("arbitrary",), vmem_limit_bytes=VMEM_LIMIT),
        name="decode_attn",
    )(page_table, q_s, bias_il, knew_s, vnew_s, cache_k, cache_v)


def _pool_decode_kernel(hist_ref, u_ref, d_ref):
    for g, w in enumerate(POOL_WINDOWS):
        cols = slice(g * POOL_GROUP_WIDTH, (g + 1) * POOL_GROUP_WIDTH)
        cur = u_ref[:, cols]
        tot = cur
        for j in range(1, w):
            tot = tot + hist_ref[POOL_HIST - j, :, cols]
        d_ref[:, cols] = tot * (1.0 / w) - cur


def _pool_decode(hist, u):
    return pl.pallas_call(
        _pool_decode_kernel,
        out_shape=jax.ShapeDtypeStruct(u.shape, F32),
        name="pool_decode",
    )(hist, u)


FF_TILE = 1024


def _rmsnorm(x, g):
    ms = jnp.mean(x * x, axis=-1, keepdims=True)
    return (x * lax.rsqrt(ms + EPS)) * g


def _post_kernel(x_ref, att_ref, d_ref, woa_ref, wop_ref, wpool_ref, pscale_ref,
                 gmlp_ref, gfin_ref, wup_ref, wdn_ref, y_ref, hn_ref):
    f = pl.program_id(1)

    @pl.when(f == 0)
    def _():
        d = d_ref[...]
        pooled = jnp.concatenate(
            [jnp.dot(d[:, g * POOL_GROUP_WIDTH:(g + 1) * POOL_GROUP_WIDTH].astype(BF16),
                     wpool_ref[g], preferred_element_type=F32)
             for g in range(len(POOL_WINDOWS))], axis=1) * pscale_ref[...]
        h = (x_ref[...]
             + jnp.dot(att_ref[...], woa_ref[...], preferred_element_type=F32)
             + jnp.dot(pooled.astype(BF16), wop_ref[...], preferred_element_type=F32))
        y_ref[...] = h
        hn_ref[...] = _rmsnorm(h, gmlp_ref[...]).astype(BF16)

    a = jnp.maximum(jnp.dot(hn_ref[...], wup_ref[...], preferred_element_type=F32), 0.0)
    y_ref[...] += jnp.dot((a * a).astype(BF16), wdn_ref[...], preferred_element_type=F32)

    @pl.when(f == pl.num_programs(1) - 1)
    def _():
        y_ref[...] = _rmsnorm(y_ref[...], gfin_ref[...])


def _post(x, att, d, wo_a, wo_p, w_pool, pool_scale, g_mlp, g_final, w_up, w_down, tm):
    rows = x.shape[0]
    assert rows % tm == 0 and D_FF % FF_TILE == 0
    row_spec = lambda w: pl.BlockSpec((tm, w), lambda i, f: (i, 0))
    return pl.pallas_call(
        _post_kernel,
        grid=(rows // tm, D_FF // FF_TILE),
        in_specs=[row_spec(D_MODEL), row_spec(ATTN_WIDTH), row_spec(POOL_WIDTH),
                  _const_spec(wo_a.shape), _const_spec(wo_p.shape), _const_spec(w_pool.shape),
                  _const_spec((1, POOL_WIDTH)), _const_spec((1, D_MODEL)), _const_spec((1, D_MODEL)),
                  pl.BlockSpec((D_MODEL, FF_TILE), lambda i, f: (0, f)),
                  pl.BlockSpec((FF_TILE, D_MODEL), lambda i, f: (f, 0))],
        out_specs=row_spec(D_MODEL),
        out_shape=jax.ShapeDtypeStruct((rows, D_MODEL), F32),
        scratch_shapes=[pltpu.VMEM((tm, D_MODEL), BF16)],
        compiler_params=pltpu.CompilerParams(
            dimension_semantics=("arbitrary", "arbitrary"), vmem_limit_bytes=VMEM_LIMIT),
        name="post",
    )(x, att, d, wo_a, wo_p, w_pool, pool_scale.reshape(1, POOL_WIDTH),
      g_mlp.reshape(1, D_MODEL), g_final.reshape(1, D_MODEL), w_up, w_down)


def kernel(x_prompt, x_sample, cache_k, cache_v, cache_kidx, state_pool, page_table, meta_tokens,
           g_mix, w_in, w_pool, pool_scale, w_out, g_mlp, w_up, w_down, g_final):
    assert x_prompt.shape[0] == 1 and x_sample.shape[1] == 1 and g_mix.shape[0] == 1
    seq = x_prompt.shape[1]
    db = x_sample.shape[0]
    n_pages = page_table.shape[1]
    past = n_pages * PAGE_SIZE
    assert db == LANES

    w = w_in[0]
    o = 0
    parts = []
    for width in (ATTN_WIDTH, 2 * N_KV_HEADS * HEAD_DIM, N_IDX_HEADS * IDX_DIM,
                  IDX_DIM + N_IDX_HEADS, POOL_WIDTH):
        parts.append(w[:, o:o + width].astype(BF16))
        o += width
    parts[3] = jnp.pad(parts[3], ((0, 0), (0, LANES - parts[3].shape[1])))
    wo_a = w_out[0, :ATTN_WIDTH].astype(BF16)
    wo_p = w_out[0, ATTN_WIDTH:].astype(BF16)
    w_pool_b = w_pool[0].astype(BF16)

    xs = jnp.concatenate([meta_tokens.astype(F32), x_sample[:, 0]], axis=0)
    pos_s = jnp.concatenate([jnp.arange(N_META, dtype=jnp.int32),
                             jnp.full((db,), past, jnp.int32)])
    q_s, kf_s, vf_s, kb_s, vb_s, qi_s, kw_s, u_s = _project(
        xs, jnp.zeros((1,), jnp.int32), pos_s, g_mix[0], parts)
    xp = x_prompt[0]
    assert seq % PROJ_TILE == 0
    tile_base = N_META + PROJ_TILE * jnp.arange(seq // PROJ_TILE, dtype=jnp.int32)
    q_p, kf_p, vf_p, kb_p, vb_p, qi_p, kw_p, d_p, u_tail, w_up_b, w_down_b = _project(
        xp, tile_base, jnp.arange(PROJ_TILE, dtype=jnp.int32), g_mix[0], parts,
        halo=u_s[:N_META], mlp_weights=(w_up[0], w_down[0]))

    nk = -(-(seq + 2 * Q_BLOCK) // KEY_CHUNK) * KEY_CHUNK

    def key_rows(meta_part, prompt_part):
        width = meta_part.shape[1]
        return jnp.concatenate(
            [meta_part, jnp.zeros((Q_BLOCK - N_META, width), meta_part.dtype), prompt_part,
             jnp.zeros((nk - Q_BLOCK - seq, width), meta_part.dtype)], axis=0)

    ki_all = key_rows(kw_s[:N_META, :IDX_DIM], kw_p[:, :IDX_DIM]).astype(BF16)
    k_all = key_rows(kb_s[:N_META], kb_p)
    v_all = key_rows(vb_s[:N_META], vb_p)
    v_t = jnp.transpose(
        v_all.reshape(nk // KEY_CHUNK, KEY_CHUNK, N_KV_HEADS, HEAD_DIM), (0, 2, 3, 1))
    v_t = jnp.concatenate(
        [v_t, jnp.ones((nk // KEY_CHUNK, N_KV_HEADS, ONES_ROWS, KEY_CHUNK), BF16)], axis=2)
    knorm = lambda kw: kw[:, KNORM_LANE:KNORM_LANE + N_KV_HEADS]
    kmax2 = jnp.maximum(jnp.max(knorm(kw_p), axis=0), jnp.max(knorm(kw_s[:N_META]), axis=0))
    zk = jnp.zeros_like(ki_all)
    ki_lo = jnp.concatenate([ki_all, zk], axis=1)
    ki_hi = jnp.concatenate([zk, ki_all], axis=1)
    att_p = _prompt_attention(kmax2, qi_p, kw_p, q_p, ki_lo, ki_hi, k_all, v_t)

    width = -(-(past + 1) // KEY_CHUNK) * KEY_CHUNK
    qi_d = qi_s[N_META:].reshape(db, N_IDX_HEADS, IDX_DIM)
    w_d = kw_s[N_META:, IDX_DIM:IDX_DIM + N_IDX_HEADS].reshape(db, N_IDX_HEADS, 1)
    kinew = kw_s[N_META:, :IDX_DIM].astype(BF16).reshape(db, 1, IDX_DIM)
    scores = _decode_scores(page_table, qi_d, w_d, kinew, jnp.swapaxes(cache_kidx, 2, 3), width)
    bias_t = _decode_mask(jnp.transpose(scores.reshape(db, width)))
    bias = jnp.transpose(bias_t)
    negs = jnp.full_like(bias, NEG)
    bias_il = jnp.stack(
        [jnp.stack([bias, negs], axis=-1), jnp.stack([negs, bias], axis=-1)], axis=1,
    ).reshape(db, N_KV_HEADS, N_KV_HEADS * width)
    n_phys = cache_k.shape[1]
    il_shape = (1, n_phys, N_KV_HEADS * PAGE_SIZE, HEAD_DIM)
    per_head = lambda a: jnp.repeat(a.reshape(db, N_KV_HEADS, HEAD_DIM), GROUP, axis=1)
    att_s = _decode_attention(
        page_table, q_s[N_META:].reshape(db, N_HEADS, HEAD_DIM), bias_il,
        per_head(kb_s[N_META:]), per_head(vb_s[N_META:]),
        cache_k.reshape(il_shape), cache_v.reshape(il_shape))
    att_s = att_s.reshape(db, ATTN_WIDTH)

    d_s = _pool_decode(jnp.swapaxes(state_pool[0], 0, 1), u_s[N_META:])

    post = functools.partial(
        _post, wo_a=wo_a, wo_p=wo_p, w_pool=w_pool_b, pool_scale=pool_scale[0],
        g_mlp=g_mlp[0], g_final=g_final, w_up=w_up_b, w_down=w_down_b)
    y_p = post(xp, att_p, d_p, tm=512)
    y_s = post(x_sample[:, 0], att_s, d_s, tm=db)

    k_prompt = jnp.concatenate([kf_s[:N_META], kf_p], axis=0).reshape(1, 1, N_META + seq, N_KV_HEADS, HEAD_DIM)
    v_prompt = jnp.concatenate([vf_s[:N_META], vf_p], axis=0).reshape(1, 1, N_META + seq, N_KV_HEADS, HEAD_DIM)
    kidx_prompt = jnp.concatenate([kw_s[:N_META, :IDX_DIM], kw_p[:, :IDX_DIM]], axis=0).reshape(
        1, 1, N_META + seq, IDX_DIM)
    pool_prompt = u_tail[POOL_HALO - POOL_HIST:].reshape(1, 1, POOL_HIST, POOL_WIDTH)
    k_sample = kf_s[N_META:].reshape(1, db, 1, N_KV_HEADS, HEAD_DIM)
    v_sample = vf_s[N_META:].reshape(1, db, 1, N_KV_HEADS, HEAD_DIM)
    kidx_sample = kw_s[N_META:, :IDX_DIM].reshape(1, db, 1, IDX_DIM)
    pool_sample = jnp.concatenate(
        [state_pool[0][:, 1:], u_s[N_META:][:, None, :]], axis=1)[None]
    return (y_p.reshape(1, seq, D_MODEL), y_s.reshape(db, 1, D_MODEL),
            k_prompt, v_prompt, kidx_prompt, pool_prompt,
            k_sample, v_sample, kidx_sample, pool_sample)
```

```python
import functools

import jax
import jax.numpy as jnp
from jax import lax
from jax.experimental import pallas as pl
from jax.experimental.pallas import tpu as pltpu

F32 = jnp.float32
BF16 = jnp.bfloat16

D_MODEL = 2048
N_META = 16
ATTN_WIDTH = 1024
POOL_WIDTH = 1024
HEAD_DIM = 128
N_HEADS = 8
N_KV_HEADS = 2
GROUP = N_HEADS // N_KV_HEADS
N_IDX_HEADS = 16
IDX_DIM = 64
TOPK = 256
POOL_WINDOWS = (2, 4, 8, 16)
POOL_GROUP_WIDTH = 256
POOL_HIST = 15
D_FF = 8192
PAGE_SIZE = 128
ROPE_THETA = 10000.0
EPS = 1e-6
NEG = -1e30
NEG_INF = float("-inf")
POS_INF = float("inf")
F32_LOWEST = -3.0e38

LANES = 128
SUBLANES = 8
VMEM_LIMIT = 56 * 1024 * 1024

Q_BLOCK = 128
KEY_CHUNK = 512
MAX_BISECT = 320
N_UNCHECKED_ROUNDS = 8
LOG2E = 1.4426950408889634
MAX_EXP2_GAP = 80.0
NORM_BOUND_SLACK = 1.02
ONES_ROWS = 16
KNORM_LANE = IDX_DIM + N_IDX_HEADS
PROJ_TILE = 256
ATTN_GROUP = 2
POOL_HALO = 16

CONTRACT_LAST = (((1,), (1,)), ((), ()))


def _const_spec(shape):
    nd = len(shape)
    return pl.BlockSpec(shape, lambda *_: (0,) * nd, pipeline_mode=pl.Buffered(1))


def _rope128(x, c, s):
    return x * c + pltpu.roll(x, 64, 1) * s


def _rope64(x, c, sa, sb):
    return x * c + pltpu.roll(x, 96, 1) * sa + pltpu.roll(x, 32, 1) * sb


def _angle_sum(cb_ref, sb_ref, cr_ref, sr_ref):
    cb, sb, cr, sr = cb_ref[...], sb_ref[...], cr_ref[...], sr_ref[...]
    return cb * cr - sb * sr, sb * cr + cb * sr


def _proj_kernel(x_ref, g_ref, cb128_ref, sb128_ref, cb64_ref, sb64_ref,
                 cr128_ref, sr128_ref, cr64_ref, sr64_ref,
                 wq_ref, wkv_ref, wqi_ref, wkw_ref, wu_ref, *rest, with_pool):
    if with_pool:
        (halo_ref, wup_ref, wdn_ref, q_ref, kf_ref, vf_ref, kb_ref, vb_ref, qi_ref, kw_ref, u_ref,
         utail_ref, wupb_ref, wdnb_ref, ubuf) = rest
        wupb_ref[...] = wup_ref[...].astype(BF16)
        wdnb_ref[...] = wdn_ref[...].astype(BF16)
    else:
        q_ref, kf_ref, vf_ref, kb_ref, vb_ref, qi_ref, kw_ref, u_ref = rest
    x = x_ref[...]
    ms = jnp.mean(x * x, axis=-1, keepdims=True)
    xn = ((x * lax.rsqrt(ms + EPS)) * g_ref[...]).astype(BF16)
    tm = x.shape[0]
    lane_t = lax.broadcasted_iota(jnp.int32, (tm, LANES), 1)
    c128, sfull = _angle_sum(cb128_ref, sb128_ref, cr128_ref, sr128_ref)
    s128 = jnp.where(lane_t < HEAD_DIM // 2, -sfull, sfull)
    c64, sfull = _angle_sum(cb64_ref, sb64_ref, cr64_ref, sr64_ref)
    first_half = jnp.bitwise_and(lane_t, IDX_DIM - 1) < IDX_DIM // 2
    sa64 = jnp.where(first_half, -sfull, 0.0)
    sb64 = jnp.where(first_half, 0.0, sfull)

    zq = jnp.dot(xn, wq_ref[...], preferred_element_type=F32)
    for h in range(N_HEADS):
        sl = slice(h * LANES, (h + 1) * LANES)
        q_ref[:, sl] = _rope128(zq[:, sl], c128, s128).astype(BF16)

    zkv = jnp.dot(xn, wkv_ref[...], preferred_element_type=F32)
    k_norm2 = []
    for h in range(N_KV_HEADS):
        sl = slice(h * LANES, (h + 1) * LANES)
        kr = _rope128(zkv[:, sl], c128, s128)
        kf_ref[:, h, :] = kr
        kb = kr.astype(BF16)
        kb_ref[:, sl] = kb
        k_norm2.append(jnp.sum(kb.astype(F32) * kb.astype(F32), axis=-1, keepdims=True))
    v = zkv[:, N_KV_HEADS * LANES:]
    for h in range(N_KV_HEADS):
        vf_ref[:, h, :] = v[:, h * LANES:(h + 1) * LANES]
    vb_ref[...] = v.astype(BF16)

    zqi = jnp.dot(xn, wqi_ref[...], preferred_element_type=F32)
    for p in range(N_IDX_HEADS // 2):
        sl = slice(p * LANES, (p + 1) * LANES)
        qi_ref[:, sl] = _rope64(zqi[:, sl], c64, sa64, sb64).astype(BF16)

    zkw = jnp.dot(xn, wkw_ref[...], preferred_element_type=F32)
    lane = lax.broadcasted_iota(jnp.int32, zkw.shape, 1)
    is_key = lane < IDX_DIM
    ckw = jnp.where(is_key, c64, jnp.where(lane < IDX_DIM + N_IDX_HEADS, N_IDX_HEADS ** -0.5, 1.0))
    kw = _rope64(zkw, ckw, jnp.where(is_key, sa64, 0.0), jnp.where(is_key, sb64, 0.0))
    for h in range(N_KV_HEADS):
        kw = jnp.where(lane == KNORM_LANE + h, k_norm2[h], kw)
    kw_ref[...] = kw

    u = jnp.dot(xn, wu_ref[...], preferred_element_type=F32)
    if not with_pool:
        u_ref[...] = u
        return

    @pl.when(pl.program_id(0) == 0)
    def _():
        ubuf[0:POOL_HALO, :] = halo_ref[...]

    ubuf[POOL_HALO:, :] = u
    for g, w in enumerate(POOL_WINDOWS):
        cols = slice(g * POOL_GROUP_WIDTH, (g + 1) * POOL_GROUP_WIDTH)
        cur = ubuf[POOL_HALO:POOL_HALO + tm, cols]
        tot = cur
        for j in range(1, w):
            tot = tot + ubuf[POOL_HALO - j:POOL_HALO - j + tm, cols]
        u_ref[:, cols] = tot * (1.0 / w) - cur
    tail = ubuf[tm:tm + POOL_HALO, :]
    ubuf[0:POOL_HALO, :] = tail
    utail_ref[...] = tail


def _project(x, pos_base, pos_rel, g_mix, w_parts, halo=None, mlp_weights=None):
    tm = pos_rel.shape[0]
    n_tiles = pos_base.shape[0]
    rows = x.shape[0]
    assert rows == tm * n_tiles
    with_pool = halo is not None
    assert with_pool == (mlp_weights is not None)
    base_tabs = [t.reshape(n_tiles, 1, LANES) for t in _rope_tables(pos_base)]
    rel_tabs = _rope_tables(pos_rel)
    row_spec = lambda w: pl.BlockSpec((tm, w), lambda i: (i, 0))
    base_spec = pl.BlockSpec((None, 1, LANES), lambda i: (i, 0, 0))
    out_shapes = [
        jax.ShapeDtypeStruct((rows, ATTN_WIDTH), BF16),
        jax.ShapeDtypeStruct((rows, N_KV_HEADS, HEAD_DIM), F32),
        jax.ShapeDtypeStruct((rows, N_KV_HEADS, HEAD_DIM), F32),
        jax.ShapeDtypeStruct((rows, 256), BF16),
        jax.ShapeDtypeStruct((rows, 256), BF16),
        jax.ShapeDtypeStruct((rows, 1024), BF16),
        jax.ShapeDtypeStruct((rows, LANES), F32),
        jax.ShapeDtypeStruct((rows, POOL_WIDTH), F32),
    ]
    out_specs = [row_spec(s.shape[1]) if len(s.shape) == 2 else
                 pl.BlockSpec((tm,) + s.shape[1:], lambda i: (i, 0, 0)) for s in out_shapes]
    in_specs = ([row_spec(D_MODEL), _const_spec((1, D_MODEL))] + [base_spec] * 4
                + [_const_spec((tm, LANES))] * 4 + [_const_spec(w.shape) for w in w_parts])
    args = [x, g_mix.reshape(1, D_MODEL), *base_tabs, *rel_tabs, *w_parts]
    scratch = []
    if with_pool:
        in_specs.append(_const_spec((POOL_HALO, POOL_WIDTH)))
        args.append(halo)
        out_shapes.append(jax.ShapeDtypeStruct((POOL_HALO, POOL_WIDTH), F32))
        out_specs.append(pl.BlockSpec((POOL_HALO, POOL_WIDTH), lambda i: (0, 0)))
        scratch.append(pltpu.VMEM((tm + POOL_HALO, POOL_WIDTH), F32))
        for wgt in mlp_weights:
            assert wgt.shape[0] % n_tiles == 0
            slab = pl.BlockSpec((wgt.shape[0] // n_tiles, wgt.shape[1]), lambda i: (i, 0))
            in_specs.append(slab)
            args.append(wgt)
            out_shapes.append(jax.ShapeDtypeStruct(wgt.shape, BF16))
            out_specs.append(slab)
    return pl.pallas_call(
        functools.partial(_proj_kernel, with_pool=with_pool),
        grid=(n_tiles,),
        in_specs=in_specs,
        out_specs=tuple(out_specs),
        out_shape=tuple(out_shapes),
        scratch_shapes=scratch,
        compiler_params=pltpu.CompilerParams(
            dimension_semantics=("arbitrary",), vmem_limit_bytes=VMEM_LIMIT),
        name="proj",
    )(*args)


def _rope_tables(pos):
    posf = pos.astype(F32)[:, None]
    out = []
    for dim in (HEAD_DIM, IDX_DIM):
        inv = ROPE_THETA ** (-jnp.arange(0, dim, 2, dtype=F32) / dim)
        ang = posf * inv[None, :]
        reps = LANES // (dim // 2)
        out += [jnp.tile(jnp.cos(ang), (1, reps)), jnp.tile(jnp.sin(ang), (1, reps))]
    return out


def _key_count(st_ref, nchunks, chunk, pred):
    acc_rows = 8 * SUBLANES

    def body(c, part):
        c0 = pl.multiple_of(c * chunk, chunk)
        hit = pred(st_ref[pl.ds(c0, chunk), :], c0)
        for g in range(chunk // acc_rows):
            part = jnp.where(hit[g * acc_rows:(g + 1) * acc_rows], part + 1.0, part)
        return part

    part = lax.fori_loop(0, nchunks, body, jnp.zeros((acc_rows, LANES), F32))
    return jnp.sum(part, axis=0, keepdims=True)


def _topk_threshold(st_ref, nchunks, chunk, n_valid, s_min, s_max):
    kf = float(TOPK)
    take_all = n_valid <= kf

    def count_ge(t):
        return _key_count(st_ref, nchunks, chunk, lambda v, c0: v >= t)

    def cond(st):
        it, done = st[0], st[3]
        return jnp.logical_and(it < MAX_BISECT, jnp.min(done) < 0.5)

    def step(st):
        lo, hi, done, tau = st
        mid = lo + (hi - lo) * 0.5
        collapsed = jnp.logical_or(mid <= lo, mid >= hi)
        cnt = count_ge(mid)
        found = cnt == kf
        active = done < 0.5
        lo = jnp.where(jnp.logical_and(active, cnt > kf), mid, lo)
        hi = jnp.where(jnp.logical_and(active, cnt < kf), mid, hi)
        tau = jnp.where(jnp.logical_and(active, found), mid, tau)
        done = jnp.where(jnp.logical_or(found, collapsed), 1.0, done)
        return lo, hi, done, tau

    def body(st):
        return (st[0] + 1,) + step(step(st[1:]))

    done0 = jnp.where(take_all, 1.0, 0.0).astype(F32)
    tau0 = jnp.full((1, LANES), POS_INF, F32)
    st0 = (jnp.int32(0), s_min, s_max, done0, tau0)
    st0 = lax.cond(jnp.min(done0) < 0.5,
                   lambda st: lax.fori_loop(0, N_UNCHECKED_ROUNDS, lambda _, s: body(s), st),
                   lambda st: st, st0)
    _, lo, hi, _, tau = lax.while_loop(cond, body, st0)

    unresolved = jnp.logical_and(tau == POS_INF, jnp.logical_not(take_all))
    n_unres = jnp.sum(unresolved.astype(F32))
    tau = jnp.where(take_all, F32_LOWEST, tau)

    def tie_path(tau):
        cut = jnp.where(count_ge(hi) >= kf, hi, lo)
        cut = jnp.where(unresolved, cut, POS_INF)
        c_gt = _key_count(st_ref, nchunks, chunk, lambda v, c0: v > cut)
        keep = kf - c_gt

        def key_iota(c0):
            return c0 + lax.broadcasted_iota(jnp.int32, (chunk, LANES), 0)

        def idx_body(_, st):
            xlo, xhi = st
            xmid = (xlo + xhi) // 2
            cnt = _key_count(
                st_ref, nchunks, chunk,
                lambda v, c0: jnp.logical_and(v == cut, key_iota(c0) < xmid))
            ok = cnt >= keep
            return jnp.where(ok, xlo, xmid), jnp.where(ok, xmid, xhi)

        xlo0 = jnp.zeros((1, LANES), jnp.int32)
        xhi0 = jnp.full((1, LANES), nchunks * chunk, jnp.int32)
        n_idx_steps = 15
        _, xcut = lax.fori_loop(0, n_idx_steps, idx_body, (xlo0, xhi0))

        def drop_body(c, carry):
            c0 = pl.multiple_of(c * chunk, chunk)
            v = st_ref[pl.ds(c0, chunk), :]
            drop = jnp.logical_and(v == cut, key_iota(c0) >= xcut)
            st_ref[pl.ds(c0, chunk), :] = jnp.where(drop, NEG_INF, v)
            return carry

        lax.fori_loop(0, nchunks, drop_body, 0)
        return jnp.where(unresolved, cut, tau)

    return lax.cond(n_unres > 0.0, tie_path, lambda t: t, tau)


def _prompt_attn_kernel(kmax2_ref, qi_ref, kw_ref, q_ref, kilo_ref, kihi_ref, k_ref, vt_ref, o_ref,
                        st_ref, m_ref, l_ref, acc_ref, accf_ref):
    i = pl.program_id(0)
    tq, ck = Q_BLOCK, KEY_CHUNK
    n_blocks = i + 2
    nchunks = (n_blocks * Q_BLOCK + ck - 1) // ck

    wt = jnp.transpose(kw_ref[...])[IDX_DIM:IDX_DIM + N_IDX_HEADS, :] * (IDX_DIM ** -0.5)
    qcat = jnp.concatenate(
        [qi_ref[:, p * LANES:(p + 1) * LANES] for p in range(N_IDX_HEADS // 2)], axis=0)
    key = lax.broadcasted_iota(jnp.int32, (ck, tq), 0)
    qry = i * tq + lax.broadcasted_iota(jnp.int32, (ck, tq), 1)

    def idx_chunk(c, st):
        smin, smax = st
        c0 = pl.multiple_of(c * ck, ck)
        klo = kilo_ref[pl.ds(c0, ck), :]
        khi = kihi_ref[pl.ds(c0, ck), :]
        acc = jnp.zeros((ck, tq), F32)
        for t in range(N_IDX_HEADS // 4):
            rhs = qcat[2 * t * tq:(2 * t + 2) * tq]
            dlo = lax.dot_general(klo, rhs, CONTRACT_LAST, preferred_element_type=F32)
            dhi = lax.dot_general(khi, rhs, CONTRACT_LAST, preferred_element_type=F32)
            for j in range(2):
                p = 2 * t + j
                cs = slice(j * tq, (j + 1) * tq)
                acc = acc + wt[2 * p:2 * p + 1, :] * jnp.maximum(dlo[:, cs], 0.0)
                acc = acc + wt[2 * p + 1:2 * p + 2, :] * jnp.maximum(dhi[:, cs], 0.0)
        col = c0 + key
        valid = jnp.logical_or(
            col < N_META,
            jnp.logical_and(col >= Q_BLOCK, col - Q_BLOCK <= qry))
        st_ref[pl.ds(c0, ck), :] = jnp.where(valid, acc, NEG_INF)
        smin = jnp.minimum(smin, jnp.min(jnp.where(valid, acc, POS_INF), axis=0, keepdims=True))
        smax = jnp.maximum(smax, jnp.max(jnp.where(valid, acc, NEG_INF), axis=0, keepdims=True))
        return smin, smax

    st = lax.fori_loop(
        0, nchunks // 2, lambda j, st: idx_chunk(2 * j + 1, idx_chunk(2 * j, st)),
        (jnp.full((1, tq), POS_INF, F32), jnp.full((1, tq), NEG_INF, F32)))
    smin, smax = lax.cond(nchunks % 2 == 1, lambda st: idx_chunk(nchunks - 1, st), lambda st: st, st)

    n_valid = (N_META + 1 + i * tq + lax.broadcasted_iota(jnp.int32, (1, tq), 1)).astype(F32)
    tau = _topk_threshold(st_ref, nchunks, ck, n_valid, smin, smax)

    scale = HEAD_DIM ** -0.5

    def q_group(n):
        return jnp.concatenate(
            [q_ref[:, (n * GROUP + g) * LANES:(n * GROUP + g + 1) * LANES]
             for g in range(GROUP)], axis=0)

    def k_chunk(c0, n):
        return k_ref[pl.ds(c0, ck), n * LANES:(n + 1) * LANES]

    def write_out(n, ot):
        for g in range(GROUP):
            h = n * GROUP + g
            o_ref[:, h * LANES:(h + 1) * LANES] = jnp.transpose(
                ot[:, g * tq:(g + 1) * tq]).astype(o_ref.dtype)

    def ref_body(c, mx):
        c0 = pl.multiple_of(c * ck, ck)
        sel = st_ref[pl.ds(c0, ck), :] >= tau
        out = []
        for n in range(N_KV_HEADS):
            raw = lax.dot_general(k_chunk(c0, n), q_group(n), CONTRACT_LAST,
                                  preferred_element_type=F32)
            cur = jnp.concatenate(
                [jnp.max(jnp.where(sel, raw[:, g * tq:(g + 1) * tq], NEG), axis=0, keepdims=True)
                 for g in range(GROUP)], axis=1)
            out.append(jnp.maximum(mx[n], cur))
        return tuple(out)

    mx = ref_body(0, tuple(jnp.full((1, GROUP * tq), NEG, F32) for _ in range(N_KV_HEADS)))
    c2 = scale * LOG2E
    ones = jnp.ones((SUBLANES, HEAD_DIM), BF16)
    gap = jnp.float32(0.0)
    for n in range(N_KV_HEADS):
        qf = q_group(n).astype(F32)
        qn2 = lax.dot_general(ones, (qf * qf).astype(BF16), CONTRACT_LAST,
                              preferred_element_type=F32)[0:1]
        bound = jnp.sqrt(qn2 * kmax2_ref[n]) * (c2 * NORM_BOUND_SLACK)
        gap = jnp.maximum(gap, jnp.max(bound - mx[n] * c2))

    def fixed_reference_path():
        accf_ref[...] = jnp.zeros(accf_ref.shape, F32)

        def chunk_update(c, n):
            c0 = pl.multiple_of(c * ck, ck)
            sel = st_ref[pl.ds(c0, ck), :] >= tau
            raw = lax.dot_general(k_chunk(c0, n), q_group(n), CONTRACT_LAST,
                                  preferred_element_type=F32)
            neg_ref = mx[n] * (-c2)
            p = jnp.concatenate(
                [jnp.exp2(raw[:, g * tq:(g + 1) * tq] * c2
                          + jnp.where(sel, neg_ref[:, g * tq:(g + 1) * tq], NEG)).astype(BF16)
                 for g in range(GROUP)], axis=1)
            return jnp.dot(vt_ref[c, n], p, preferred_element_type=F32)

        def group_body(j, carry):
            for n in range(N_KV_HEADS):
                upd = chunk_update(ATTN_GROUP * j, n)
                for e in range(1, ATTN_GROUP):
                    upd = upd + chunk_update(ATTN_GROUP * j + e, n)
                accf_ref[n] += upd
            return carry

        n_groups = nchunks // ATTN_GROUP
        lax.fori_loop(0, n_groups, group_body, 0)

        def single_body(c, carry):
            for n in range(N_KV_HEADS):
                accf_ref[n] += chunk_update(c, n)
            return carry

        lax.fori_loop(n_groups * ATTN_GROUP, nchunks, single_body, 0)

        for n in range(N_KV_HEADS):
            write_out(n, accf_ref[n, 0:HEAD_DIM, :] / accf_ref[n, HEAD_DIM:HEAD_DIM + 1, :])

    def running_max_path():
        m_ref[...] = jnp.full(m_ref.shape, NEG, F32)
        l_ref[...] = jnp.zeros(l_ref.shape, F32)
        acc_ref[...] = jnp.zeros(acc_ref.shape, F32)

        def body(c, carry):
            c0 = pl.multiple_of(c * ck, ck)
            bias = jnp.where(st_ref[pl.ds(c0, ck), :] >= tau, 0.0, NEG)
            bias = jnp.concatenate([bias] * GROUP, axis=1)
            for n in range(N_KV_HEADS):
                sc = lax.dot_general(k_chunk(c0, n), q_group(n), CONTRACT_LAST,
                                     preferred_element_type=F32)
                sc = sc * scale + bias
                m_old = m_ref[n]
                m_new = jnp.maximum(m_old, jnp.max(sc, axis=0, keepdims=True))
                alpha = jnp.exp(m_old - m_new)
                p = jnp.exp(sc - m_new)
                l_ref[n] = alpha * l_ref[n] + jnp.sum(
                    p.reshape(ck // SUBLANES, SUBLANES, GROUP * tq), axis=0)
                acc_ref[n] = alpha * acc_ref[n] + jnp.dot(
                    vt_ref[c, n, 0:HEAD_DIM, :], p.astype(BF16), preferred_element_type=F32)
                m_ref[n] = m_new
            return carry

        lax.fori_loop(0, nchunks, body, 0)
        for n in range(N_KV_HEADS):
            write_out(n, acc_ref[n] / jnp.sum(l_ref[n], axis=0, keepdims=True))

    lax.cond(gap <= MAX_EXP2_GAP, fixed_reference_path, running_max_path)


def _prompt_attention(kmax2, qi, kw, q, ki_lo, ki_hi, k_all, v_t):
    rows = q.shape[0]
    nk = k_all.shape[0]
    assert rows % Q_BLOCK == 0 and nk % KEY_CHUNK == 0
    assert nk >= (rows // Q_BLOCK + 1) * Q_BLOCK
    assert v_t.shape == (nk // KEY_CHUNK, N_KV_HEADS, HEAD_DIM + ONES_ROWS, KEY_CHUNK)
    row_spec = lambda w: pl.BlockSpec((Q_BLOCK, w), lambda i: (i, 0))
    return pl.pallas_call(
        _prompt_attn_kernel,
        grid=(rows // Q_BLOCK,),
        in_specs=[pl.BlockSpec(memory_space=pltpu.SMEM),
                  row_spec(1024), row_spec(LANES), row_spec(ATTN_WIDTH),
                  _const_spec(ki_lo.shape), _const_spec(ki_hi.shape),
                  _const_spec(k_all.shape), _const_spec(v_t.shape)],
        out_specs=row_spec(ATTN_WIDTH),
        out_shape=jax.ShapeDtypeStruct((rows, ATTN_WIDTH), BF16),
        scratch_shapes=[
            pltpu.VMEM((nk, Q_BLOCK), F32),
            pltpu.VMEM((N_KV_HEADS, 1, GROUP * Q_BLOCK), F32),
            pltpu.VMEM((N_KV_HEADS, SUBLANES, GROUP * Q_BLOCK), F32),
            pltpu.VMEM((N_KV_HEADS, HEAD_DIM, GROUP * Q_BLOCK), F32),
            pltpu.VMEM((N_KV_HEADS, HEAD_DIM + ONES_ROWS, GROUP * Q_BLOCK), F32),
        ],
        compiler_params=pltpu.CompilerParams(
            dimension_semantics=("arbitrary",), vmem_limit_bytes=VMEM_LIMIT),
        name="prompt_attn",
    )(kmax2, qi, kw, q, ki_lo, ki_hi, k_all, v_t)


SCORE_PAGE_SLOTS = 4
SCORE_SEQS_PER_STEP = 4
KV_PAGE_SLOTS = 3


def _prefetch_pages(copies, n_pages, n_slots, b, n_seq):
    depth = n_slots - 1

    def start(seq):
        for j in range(n_pages):
            for cp in copies(seq, seq % n_slots, j):
                cp.start()

    @pl.when(b == 0)
    def _():
        for s in range(depth):
            start(s)

    @pl.when(b + depth < n_seq)
    def _():
        start(b + depth)

    for j in range(n_pages):
        for cp in copies(b, b % n_slots, j):
            cp.wait()
    return b % n_slots


def _decode_scores_kernel(pt_ref, qi_ref, w_ref, kinew_ref, cache_ref, s_ref, buf, sem):
    b = pl.program_id(0)
    n_pages = pt_ref.shape[1]

    n_seq = qi_ref.shape[0]

    def copies(unit, slot, j):
        page = pt_ref[n_seq * unit + j // n_pages, j % n_pages]
        return (pltpu.make_async_copy(cache_ref.at[0, page], buf.at[slot, j], sem.at[slot]),)

    slot = _prefetch_pages(copies, n_seq * n_pages, buf.shape[0], b, pl.num_programs(0))

    past = n_pages * PAGE_SIZE
    tail = s_ref.shape[2] - past
    lane = lax.broadcasted_iota(jnp.int32, (1, tail), 1)
    for e in range(n_seq):
        qi = qi_ref[e]
        w = w_ref[e] * (IDX_DIM ** -0.5)
        for j in range(n_pages):
            kpt = buf[slot, e * n_pages + j].astype(BF16)
            d = jnp.dot(qi, kpt, preferred_element_type=F32)
            s_ref[e, :, j * PAGE_SIZE:(j + 1) * PAGE_SIZE] = jnp.sum(
                jnp.maximum(d, 0.0) * w, axis=0, keepdims=True)
        d_new = jnp.sum(qi.astype(F32) * kinew_ref[e].astype(F32), axis=-1, keepdims=True)
        s_new = jnp.sum(jnp.maximum(d_new, 0.0) * w, axis=0, keepdims=True)
        s_ref[e, :, past:] = jnp.where(lane == 0, s_new, NEG_INF)


def _decode_scores(page_table, qi_s, w_s, kinew_s, cache_kidx_t, width):
    db, n_pages = page_table.shape
    per = SCORE_SEQS_PER_STEP
    assert db % per == 0
    seq_spec = lambda a, c: pl.BlockSpec((per, a, c), lambda b, pt: (b, 0, 0))
    return pl.pallas_call(
        _decode_scores_kernel,
        grid_spec=pltpu.PrefetchScalarGridSpec(
            num_scalar_prefetch=1,
            grid=(db // per,),
            in_specs=[seq_spec(N_IDX_HEADS, IDX_DIM), seq_spec(N_IDX_HEADS, 1),
                      seq_spec(1, IDX_DIM), pl.BlockSpec(memory_space=pl.ANY)],
            out_specs=seq_spec(1, width),
            scratch_shapes=[pltpu.VMEM((SCORE_PAGE_SLOTS, per * n_pages, IDX_DIM, PAGE_SIZE), F32),
                            pltpu.SemaphoreType.DMA((SCORE_PAGE_SLOTS,))],
        ),
        out_shape=jax.ShapeDtypeStruct((db, 1, width), F32),
        compiler_params=pltpu.CompilerParams(dimension_semantics=("arbitrary",)),
        name="decode_scores",
    )(page_table, qi_s, w_s, kinew_s, cache_kidx_t)


def _decode_mask_kernel(s_in_ref, bias_ref, st_ref):
    nk = s_in_ref.shape[0]
    chunk = KEY_CHUNK
    s = s_in_ref[...]
    st_ref[...] = s
    valid = s > NEG_INF
    smin = jnp.min(jnp.where(valid, s, POS_INF), axis=0, keepdims=True)
    smax = jnp.max(s, axis=0, keepdims=True)
    n_valid = jnp.sum(valid.astype(F32), axis=0, keepdims=True)
    tau = _topk_threshold(st_ref, nk // chunk, chunk, n_valid, smin, smax)
    bias_ref[...] = jnp.where(st_ref[...] >= tau, 0.0, NEG)


def _decode_mask(scores_t):
    assert scores_t.shape[0] % KEY_CHUNK == 0 and scores_t.shape[1] == LANES
    return pl.pallas_call(
        _decode_mask_kernel,
        out_shape=jax.ShapeDtypeStruct(scores_t.shape, F32),
        scratch_shapes=[pltpu.VMEM(scores_t.shape, F32)],
        name="decode_mask",
    )(scores_t)


def _decode_attn_kernel(pt_ref, q_ref, bias_ref, knew_ref, vnew_ref, ck_ref, cv_ref, o_ref,
                        kbuf, vbuf, sem):
    b = pl.program_id(0)
    n_pages = pt_ref.shape[1]
    page_rows = N_KV_HEADS * PAGE_SIZE

    def copies(seq, slot, j):
        rows_j = pl.ds(j * page_rows, page_rows)
        page = pt_ref[seq, j]
        return (pltpu.make_async_copy(ck_ref.at[0, page], kbuf.at[slot, rows_j], sem.at[slot, 0]),
                pltpu.make_async_copy(cv_ref.at[0, page], vbuf.at[slot, rows_j], sem.at[slot, 1]))

    slot = _prefetch_pages(copies, n_pages, kbuf.shape[0], b, pl.num_programs(0))

    rows = n_pages * page_rows
    scale = HEAD_DIM ** -0.5
    q = q_ref[...]
    bias = jnp.concatenate(
        [jnp.broadcast_to(bias_ref[n:n + 1, :], (GROUP, bias_ref.shape[1]))
         for n in range(N_KV_HEADS)], axis=0)
    k_il = kbuf[slot].astype(BF16)
    v_il = vbuf[slot].astype(BF16)
    sc = lax.dot_general(q, k_il, CONTRACT_LAST, preferred_element_type=F32)
    sc = sc * scale + bias[:, :rows]
    sc_new = jnp.sum(q.astype(F32) * knew_ref[...].astype(F32), axis=-1, keepdims=True)
    sc_new = sc_new * scale + jnp.max(bias[:, rows:rows + N_KV_HEADS], axis=-1, keepdims=True)
    m = jnp.maximum(jnp.max(sc, axis=-1, keepdims=True), sc_new)
    p = jnp.exp(sc - m)
    p_new = jnp.exp(sc_new - m)
    denom = jnp.sum(p, axis=-1, keepdims=True) + p_new
    o = jnp.dot(p.astype(BF16), v_il, preferred_element_type=F32)
    o = o + p_new.astype(BF16).astype(F32) * vnew_ref[...].astype(F32)
    o_ref[...] = (o / denom).astype(o_ref.dtype)


def _decode_attention(page_table, q_s, bias_il, knew_s, vnew_s, cache_k, cache_v):
    db, n_pages = page_table.shape
    width2 = bias_il.shape[-1]
    rows = n_pages * N_KV_HEADS * PAGE_SIZE
    seq_spec = lambda a, c: pl.BlockSpec((None, a, c), lambda b, pt: (b, 0, 0))
    return pl.pallas_call(
        _decode_attn_kernel,
        grid_spec=pltpu.PrefetchScalarGridSpec(
            num_scalar_prefetch=1,
            grid=(db,),
            in_specs=[seq_spec(N_HEADS, HEAD_DIM), seq_spec(N_KV_HEADS, width2),
                      seq_spec(N_HEADS, HEAD_DIM), seq_spec(N_HEADS, HEAD_DIM),
                      pl.BlockSpec(memory_space=pl.ANY), pl.BlockSpec(memory_space=pl.ANY)],
            out_specs=seq_spec(N_HEADS, HEAD_DIM),
            scratch_shapes=[pltpu.VMEM((KV_PAGE_SLOTS, rows, HEAD_DIM), F32),
                            pltpu.VMEM((KV_PAGE_SLOTS, rows, HEAD_DIM), F32),
                            pltpu.SemaphoreType.DMA((KV_PAGE_SLOTS, 2))],
        ),
        out_shape=jax.ShapeDtypeStruct((db, N_HEADS, HEAD_DIM), BF16),
        compiler_params=pltpu.CompilerParams(
            dimension_semantics=("arbitrary",), vmem_limit_bytes=VMEM_LIMIT),
        name="decode_attn",
    )(page_table, q_s, bias_il, knew_s, vnew_s, cache_k, cache_v)


def _pool_decode_kernel(hist_ref, u_ref, d_ref):
    for g, w in enumerate(POOL_WINDOWS):
        cols = slice(g * POOL_GROUP_WIDTH, (g + 1) * POOL_GROUP_WIDTH)
        cur = u_ref[:, cols]
        tot = cur
        for j in range(1, w):
            tot = tot + hist_ref[POOL_HIST - j, :, cols]
        d_ref[:, cols] = tot * (1.0 / w) - cur


def _pool_decode(hist, u):
    return pl.pallas_call(
        _pool_decode_kernel,
        out_shape=jax.ShapeDtypeStruct(u.shape, F32),
        name="pool_decode",
    )(hist, u)


FF_TILE = 1024


def _rmsnorm(x, g):
    ms = jnp.mean(x * x, axis=-1, keepdims=True)
    return (x * lax.rsqrt(ms + EPS)) * g


def _post_kernel(x_ref, att_ref, d_ref, woa_ref, wop_ref, wpool_ref, pscale_ref,
                 gmlp_ref, gfin_ref, wup_ref, wdn_ref, y_ref, hn_ref):
    f = pl.program_id(1)

    @pl.when(f == 0)
    def _():
        d = d_ref[...]
        pooled = jnp.concatenate(
            [jnp.dot(d[:, g * POOL_GROUP_WIDTH:(g + 1) * POOL_GROUP_WIDTH].astype(BF16),
                     wpool_ref[g], preferred_element_type=F32)
             for g in range(len(POOL_WINDOWS))], axis=1) * pscale_ref[...]
        h = (x_ref[...]
             + jnp.dot(att_ref[...], woa_ref[...], preferred_element_type=F32)
             + jnp.dot(pooled.astype(BF16), wop_ref[...], preferred_element_type=F32))
        y_ref[...] = h
        hn_ref[...] = _rmsnorm(h, gmlp_ref[...]).astype(BF16)

    a = jnp.maximum(jnp.dot(hn_ref[...], wup_ref[...], preferred_element_type=F32), 0.0)
    y_ref[...] += jnp.dot((a * a).astype(BF16), wdn_ref[...], preferred_element_type=F32)

    @pl.when(f == pl.num_programs(1) - 1)
    def _():
        y_ref[...] = _rmsnorm(y_ref[...], gfin_ref[...])


def _post(x, att, d, wo_a, wo_p, w_pool, pool_scale, g_mlp, g_final, w_up, w_down, tm):
    rows = x.shape[0]
    assert rows % tm == 0 and D_FF % FF_TILE == 0
    row_spec = lambda w: pl.BlockSpec((tm, w), lambda i, f: (i, 0))
    return pl.pallas_call(
        _post_kernel,
        grid=(rows // tm, D_FF // FF_TILE),
        in_specs=[row_spec(D_MODEL), row_spec(ATTN_WIDTH), row_spec(POOL_WIDTH),
                  _const_spec(wo_a.shape), _const_spec(wo_p.shape), _const_spec(w_pool.shape),
                  _const_spec((1, POOL_WIDTH)), _const_spec((1, D_MODEL)), _const_spec((1, D_MODEL)),
                  pl.BlockSpec((D_MODEL, FF_TILE), lambda i, f: (0, f)),
                  pl.BlockSpec((FF_TILE, D_MODEL), lambda i, f: (f, 0))],
        out_specs=row_spec(D_MODEL),
        out_shape=jax.ShapeDtypeStruct((rows, D_MODEL), F32),
        scratch_shapes=[pltpu.VMEM((tm, D_MODEL), BF16)],
        compiler_params=pltpu.CompilerParams(
            dimension_semantics=("arbitrary", "arbitrary"), vmem_limit_bytes=VMEM_LIMIT),
        name="post",
    )(x, att, d, wo_a, wo_p, w_pool, pool_scale.reshape(1, POOL_WIDTH),
      g_mlp.reshape(1, D_MODEL), g_final.reshape(1, D_MODEL), w_up, w_down)


def kernel(x_prompt, x_sample, cache_k, cache_v, cache_kidx, state_pool, page_table, meta_tokens,
           g_mix, w_in, w_pool, pool_scale, w_out, g_mlp, w_up, w_down, g_final):
    assert x_prompt.shape[0] == 1 and x_sample.shape[1] == 1 and g_mix.shape[0] == 1
    seq = x_prompt.shape[1]
    db = x_sample.shape[0]
    n_pages = page_table.shape[1]
    past = n_pages * PAGE_SIZE
    assert db == LANES

    w = w_in[0]
    o = 0
    parts = []
    for width in (ATTN_WIDTH, 2 * N_KV_HEADS * HEAD_DIM, N_IDX_HEADS * IDX_DIM,
                  IDX_DIM + N_IDX_HEADS, POOL_WIDTH):
        parts.append(w[:, o:o + width].astype(BF16))
        o += width
    parts[3] = jnp.pad(parts[3], ((0, 0), (0, LANES - parts[3].shape[1])))
    wo_a = w_out[0, :ATTN_WIDTH].astype(BF16)
    wo_p = w_out[0, ATTN_WIDTH:].astype(BF16)
    w_pool_b = w_pool[0].astype(BF16)

    xs = jnp.concatenate([meta_tokens.astype(F32), x_sample[:, 0]], axis=0)
    pos_s = jnp.concatenate([jnp.arange(N_META, dtype=jnp.int32),
                             jnp.full((db,), past, jnp.int32)])
    q_s, kf_s, vf_s, kb_s, vb_s, qi_s, kw_s, u_s = _project(
        xs, jnp.zeros((1,), jnp.int32), pos_s, g_mix[0], parts)
    xp = x_prompt[0]
    assert seq % PROJ_TILE == 0
    tile_base = N_META + PROJ_TILE * jnp.arange(seq // PROJ_TILE, dtype=jnp.int32)
    q_p, kf_p, vf_p, kb_p, vb_p, qi_p, kw_p, d_p, u_tail, w_up_b, w_down_b = _project(
        xp, tile_base, jnp.arange(PROJ_TILE, dtype=jnp.int32), g_mix[0], parts,
        halo=u_s[:N_META], mlp_weights=(w_up[0], w_down[0]))

    nk = -(-(seq + 2 * Q_BLOCK) // KEY_CHUNK) * KEY_CHUNK

    def key_rows(meta_part, prompt_part):
        width = meta_part.shape[1]
        return jnp.concatenate(
            [meta_part, jnp.zeros((Q_BLOCK - N_META, width), meta_part.dtype), prompt_part,
             jnp.zeros((nk - Q_BLOCK - seq, width), meta_part.dtype)], axis=0)

    ki_all = key_rows(kw_s[:N_META, :IDX_DIM], kw_p[:, :IDX_DIM]).astype(BF16)
    k_all = key_rows(kb_s[:N_META], kb_p)
    v_all = key_rows(vb_s[:N_META], vb_p)
    v_t = jnp.transpose(
        v_all.reshape(nk // KEY_CHUNK, KEY_CHUNK, N_KV_HEADS, HEAD_DIM), (0, 2, 3, 1))
    v_t = jnp.concatenate(
        [v_t, jnp.ones((nk // KEY_CHUNK, N_KV_HEADS, ONES_ROWS, KEY_CHUNK), BF16)], axis=2)
    knorm = lambda kw: kw[:, KNORM_LANE:KNORM_LANE + N_KV_HEADS]
    kmax2 = jnp.maximum(jnp.max(knorm(kw_p), axis=0), jnp.max(knorm(kw_s[:N_META]), axis=0))
    zk = jnp.zeros_like(ki_all)
    ki_lo = jnp.concatenate([ki_all, zk], axis=1)
    ki_hi = jnp.concatenate([zk, ki_all], axis=1)
    att_p = _prompt_attention(kmax2, qi_p, kw_p, q_p, ki_lo, ki_hi, k_all, v_t)

    width = -(-(past + 1) // KEY_CHUNK) * KEY_CHUNK
    qi_d = qi_s[N_META:].reshape(db, N_IDX_HEADS, IDX_DIM)
    w_d = kw_s[N_META:, IDX_DIM:IDX_DIM + N_IDX_HEADS].reshape(db, N_IDX_HEADS, 1)
    kinew = kw_s[N_META:, :IDX_DIM].astype(BF16).reshape(db, 1, IDX_DIM)
    scores = _decode_scores(page_table, qi_d, w_d, kinew, jnp.swapaxes(cache_kidx, 2, 3), width)
    bias_t = _decode_mask(jnp.transpose(scores.reshape(db, width)))
    bias = jnp.transpose(bias_t)
    negs = jnp.full_like(bias, NEG)
    bias_il = jnp.stack(
        [jnp.stack([bias, negs], axis=-1), jnp.stack([negs, bias], axis=-1)], axis=1,
    ).reshape(db, N_KV_HEADS, N_KV_HEADS * width)
    n_phys = cache_k.shape[1]
    il_shape = (1, n_phys, N_KV_HEADS * PAGE_SIZE, HEAD_DIM)
    per_head = lambda a: jnp.repeat(a.reshape(db, N_KV_HEADS, HEAD_DIM), GROUP, axis=1)
    att_s = _decode_attention(
        page_table, q_s[N_META:].reshape(db, N_HEADS, HEAD_DIM), bias_il,
        per_head(kb_s[N_META:]), per_head(vb_s[N_META:]),
        cache_k.reshape(il_shape), cache_v.reshape(il_shape))
    att_s = att_s.reshape(db, ATTN_WIDTH)

    d_s = _pool_decode(jnp.swapaxes(state_pool[0], 0, 1), u_s[N_META:])

    post = functools.partial(
        _post, wo_a=wo_a, wo_p=wo_p, w_pool=w_pool_b, pool_scale=pool_scale[0],
        g_mlp=g_mlp[0], g_final=g_final, w_up=w_up_b, w_down=w_down_b)
    y_p = post(xp, att_p, d_p, tm=512)
    y_s = post(x_sample[:, 0], att_s, d_s, tm=db)

    k_prompt = jnp.concatenate([kf_s[:N_META], kf_p], axis=0).reshape(1, 1, N_META + seq, N_KV_HEADS, HEAD_DIM)
    v_prompt = jnp.concatenate([vf_s[:N_META], vf_p], axis=0).reshape(1, 1, N_META + seq, N_KV_HEADS, HEAD_DIM)
    kidx_prompt = jnp.concatenate([kw_s[:N_META, :IDX_DIM], kw_p[:, :IDX_DIM]], axis=0).reshape(
        1, 1, N_META + seq, IDX_DIM)
    pool_prompt = u_tail[POOL_HALO - POOL_HIST:].reshape(1, 1, POOL_HIST, POOL_WIDTH)
    k_sample = kf_s[N_META:].reshape(1, db, 1, N_KV_HEADS, HEAD_DIM)
    v_sample = vf_s[N_META:].reshape(1, db, 1, N_KV_HEADS, HEAD_DIM)
    kidx_sample = kw_s[N_META:, :IDX_DIM].reshape(1, db, 1, IDX_DIM)
    pool_sample = jnp.concatenate(
        [state_pool[0][:, 1:], u_s[N_META:][:, None, :]], axis=1)[None]
    return (y_p.reshape(1, seq, D_MODEL), y_s.reshape(db, 1, D_MODEL),
            k_prompt, v_prompt, kidx_prompt, pool_prompt,
            k_sample, v_sample, kidx_sample, pool_sample)
```

```python
import functools

import jax
import jax.numpy as jnp
from jax import lax
from jax.experimental import pallas as pl
from jax.experimental.pallas import tpu as pltpu

F32 = jnp.float32
BF16 = jnp.bfloat16

D_MODEL = 2048
N_META = 16
ATTN_WIDTH = 1024
POOL_WIDTH = 1024
HEAD_DIM = 128
N_HEADS = 8
N_KV_HEADS = 2
GROUP = N_HEADS // N_KV_HEADS
N_IDX_HEADS = 16
IDX_DIM = 64
TOPK = 256
POOL_WINDOWS = (2, 4, 8, 16)
POOL_GROUP_WIDTH = 256
POOL_HIST = 15
D_FF = 8192
PAGE_SIZE = 128
ROPE_THETA = 10000.0
EPS = 1e-6
NEG = -1e30
NEG_INF = float("-inf")
POS_INF = float("inf")
F32_LOWEST = -3.0e38

LANES = 128
SUBLANES = 8
VMEM_LIMIT = 56 * 1024 * 1024

Q_BLOCK = 128
KEY_CHUNK = 512
MAX_BISECT = 320
N_UNCHECKED_ROUNDS = 8
LOG2E = 1.4426950408889634
MAX_EXP2_GAP = 80.0
NORM_BOUND_SLACK = 1.02
ONES_ROWS = 16
KNORM_LANE = IDX_DIM + N_IDX_HEADS
PROJ_TILE = 256
ATTN_GROUP = 2
IDX_GROUP = 3
POOL_HALO = 16

CONTRACT_LAST = (((1,), (1,)), ((), ()))


def _const_spec(shape):
    nd = len(shape)
    return pl.BlockSpec(shape, lambda *_: (0,) * nd, pipeline_mode=pl.Buffered(1))


def _rope128(x, c, s):
    return x * c + pltpu.roll(x, 64, 1) * s


def _rope64(x, c, sa, sb):
    return x * c + pltpu.roll(x, 96, 1) * sa + pltpu.roll(x, 32, 1) * sb


def _angle_sum(cb_ref, sb_ref, cr_ref, sr_ref):
    cb, sb, cr, sr = cb_ref[...], sb_ref[...], cr_ref[...], sr_ref[...]
    return cb * cr - sb * sr, sb * cr + cb * sr


def _proj_kernel(x_ref, g_ref, cb128_ref, sb128_ref, cb64_ref, sb64_ref,
                 cr128_ref, sr128_ref, cr64_ref, sr64_ref,
                 wq_ref, wkv_ref, wqi_ref, wkw_ref, wu_ref, *rest, with_pool):
    if with_pool:
        (halo_ref, wup_ref, wdn_ref, q_ref, kf_ref, vf_ref, kb_ref, vb_ref, qi_ref, kw_ref, u_ref,
         utail_ref, wupb_ref, wdnb_ref, ubuf) = rest
        wupb_ref[...] = wup_ref[...].astype(BF16)
        wdnb_ref[...] = wdn_ref[...].astype(BF16)
    else:
        q_ref, kf_ref, vf_ref, kb_ref, vb_ref, qi_ref, kw_ref, u_ref = rest
    x = x_ref[...]
    ms = jnp.mean(x * x, axis=-1, keepdims=True)
    xn = ((x * lax.rsqrt(ms + EPS)) * g_ref[...]).astype(BF16)
    tm = x.shape[0]
    lane_t = lax.broadcasted_iota(jnp.int32, (tm, LANES), 1)
    c128, sfull = _angle_sum(cb128_ref, sb128_ref, cr128_ref, sr128_ref)
    s128 = jnp.where(lane_t < HEAD_DIM // 2, -sfull, sfull)
    c64, sfull = _angle_sum(cb64_ref, sb64_ref, cr64_ref, sr64_ref)
    first_half = jnp.bitwise_and(lane_t, IDX_DIM - 1) < IDX_DIM // 2
    sa64 = jnp.where(first_half, -sfull, 0.0)
    sb64 = jnp.where(first_half, 0.0, sfull)

    zq = jnp.dot(xn, wq_ref[...], preferred_element_type=F32)
    for h in range(N_HEADS):
        sl = slice(h * LANES, (h + 1) * LANES)
        q_ref[:, sl] = _rope128(zq[:, sl], c128, s128).astype(BF16)

    zkv = jnp.dot(xn, wkv_ref[...], preferred_element_type=F32)
    k_norm2 = []
    for h in range(N_KV_HEADS):
        sl = slice(h * LANES, (h + 1) * LANES)
        kr = _rope128(zkv[:, sl], c128, s128)
        kf_ref[:, h, :] = kr
        kb = kr.astype(BF16)
        kb_ref[:, sl] = kb
        k_norm2.append(jnp.sum(kb.astype(F32) * kb.astype(F32), axis=-1, keepdims=True))
    v = zkv[:, N_KV_HEADS * LANES:]
    for h in range(N_KV_HEADS):
        vf_ref[:, h, :] = v[:, h * LANES:(h + 1) * LANES]
    vb_ref[...] = v.astype(BF16)

    zqi = jnp.dot(xn, wqi_ref[...], preferred_element_type=F32)
    for p in range(N_IDX_HEADS // 2):
        sl = slice(p * LANES, (p + 1) * LANES)
        qi_ref[:, sl] = _rope64(zqi[:, sl], c64, sa64, sb64).astype(BF16)

    zkw = jnp.dot(xn, wkw_ref[...], preferred_element_type=F32)
    lane = lax.broadcasted_iota(jnp.int32, zkw.shape, 1)
    is_key = lane < IDX_DIM
    ckw = jnp.where(is_key, c64, jnp.where(lane < IDX_DIM + N_IDX_HEADS, N_IDX_HEADS ** -0.5, 1.0))
    kw = _rope64(zkw, ckw, jnp.where(is_key, sa64, 0.0), jnp.where(is_key, sb64, 0.0))
    for h in range(N_KV_HEADS):
        kw = jnp.where(lane == KNORM_LANE + h, k_norm2[h], kw)
    kw_ref[...] = kw

    u = jnp.dot(xn, wu_ref[...], preferred_element_type=F32)
    if not with_pool:
        u_ref[...] = u
        return

    @pl.when(pl.program_id(0) == 0)
    def _():
        ubuf[0:POOL_HALO, :] = halo_ref[...]

    ubuf[POOL_HALO:, :] = u
    for g, w in enumerate(POOL_WINDOWS):
        cols = slice(g * POOL_GROUP_WIDTH, (g + 1) * POOL_GROUP_WIDTH)
        cur = ubuf[POOL_HALO:POOL_HALO + tm, cols]
        tot = cur
        for j in range(1, w):
            tot = tot + ubuf[POOL_HALO - j:POOL_HALO - j + tm, cols]
        u_ref[:, cols] = tot * (1.0 / w) - cur
    tail = ubuf[tm:tm + POOL_HALO, :]
    ubuf[0:POOL_HALO, :] = tail
    utail_ref[...] = tail


def _project(x, pos_base, pos_rel, g_mix, w_parts, halo=None, mlp_weights=None):
    tm = pos_rel.shape[0]
    n_tiles = pos_base.shape[0]
    rows = x.shape[0]
    assert rows == tm * n_tiles
    with_pool = halo is not None
    assert with_pool == (mlp_weights is not None)
    base_tabs = [t.reshape(n_tiles, 1, LANES) for t in _rope_tables(pos_base)]
    rel_tabs = _rope_tables(pos_rel)
    row_spec = lambda w: pl.BlockSpec((tm, w), lambda i: (i, 0))
    base_spec = pl.BlockSpec((None, 1, LANES), lambda i: (i, 0, 0))
    out_shapes = [
        jax.ShapeDtypeStruct((rows, ATTN_WIDTH), BF16),
        jax.ShapeDtypeStruct((rows, N_KV_HEADS, HEAD_DIM), F32),
        jax.ShapeDtypeStruct((rows, N_KV_HEADS, HEAD_DIM), F32),
        jax.ShapeDtypeStruct((rows, 256), BF16),
        jax.ShapeDtypeStruct((rows, 256), BF16),
        jax.ShapeDtypeStruct((rows, 1024), BF16),
        jax.ShapeDtypeStruct((rows, LANES), F32),
        jax.ShapeDtypeStruct((rows, POOL_WIDTH), F32),
    ]
    out_specs = [row_spec(s.shape[1]) if len(s.shape) == 2 else
                 pl.BlockSpec((tm,) + s.shape[1:], lambda i: (i, 0, 0)) for s in out_shapes]
    in_specs = ([row_spec(D_MODEL), _const_spec((1, D_MODEL))] + [base_spec] * 4
                + [_const_spec((tm, LANES))] * 4 + [_const_spec(w.shape) for w in w_parts])
    args = [x, g_mix.reshape(1, D_MODEL), *base_tabs, *rel_tabs, *w_parts]
    scratch = []
    if with_pool:
        in_specs.append(_const_spec((POOL_HALO, POOL_WIDTH)))
        args.append(halo)
        out_shapes.append(jax.ShapeDtypeStruct((POOL_HALO, POOL_WIDTH), F32))
        out_specs.append(pl.BlockSpec((POOL_HALO, POOL_WIDTH), lambda i: (0, 0)))
        scratch.append(pltpu.VMEM((tm + POOL_HALO, POOL_WIDTH), F32))
        for wgt in mlp_weights:
            assert wgt.shape[0] % n_tiles == 0
            slab = pl.BlockSpec((wgt.shape[0] // n_tiles, wgt.shape[1]), lambda i: (i, 0))
            in_specs.append(slab)
            args.append(wgt)
            out_shapes.append(jax.ShapeDtypeStruct(wgt.shape, BF16))
            out_specs.append(slab)
    return pl.pallas_call(
        functools.partial(_proj_kernel, with_pool=with_pool),
        grid=(n_tiles,),
        in_specs=in_specs,
        out_specs=tuple(out_specs),
        out_shape=tuple(out_shapes),
        scratch_shapes=scratch,
        compiler_params=pltpu.CompilerParams(
            dimension_semantics=("arbitrary",), vmem_limit_bytes=VMEM_LIMIT),
        name="proj",
    )(*args)


def _rope_tables(pos):
    posf = pos.astype(F32)[:, None]
    out = []
    for dim in (HEAD_DIM, IDX_DIM):
        inv = ROPE_THETA ** (-jnp.arange(0, dim, 2, dtype=F32) / dim)
        ang = posf * inv[None, :]
        reps = LANES // (dim // 2)
        out += [jnp.tile(jnp.cos(ang), (1, reps)), jnp.tile(jnp.sin(ang), (1, reps))]
    return out


def _key_count(st_ref, nchunks, chunk, pred):
    acc_rows = 8 * SUBLANES

    def body(c, part):
        c0 = pl.multiple_of(c * chunk, chunk)
        hit = pred(st_ref[pl.ds(c0, chunk), :], c0)
        for g in range(chunk // acc_rows):
            part = jnp.where(hit[g * acc_rows:(g + 1) * acc_rows], part + 1.0, part)
        return part

    part = lax.fori_loop(0, nchunks, body, jnp.zeros((acc_rows, LANES), F32))
    return jnp.sum(part, axis=0, keepdims=True)


def _topk_threshold(st_ref, nchunks, chunk, n_valid, s_min, s_max):
    kf = float(TOPK)
    take_all = n_valid <= kf

    def count_ge(t):
        return _key_count(st_ref, nchunks, chunk, lambda v, c0: v >= t)

    def cond(st):
        it, done = st[0], st[3]
        return jnp.logical_and(it < MAX_BISECT, jnp.min(done) < 0.5)

    def step(st):
        lo, hi, done, tau = st
        mid = lo + (hi - lo) * 0.5
        collapsed = jnp.logical_or(mid <= lo, mid >= hi)
        cnt = count_ge(mid)
        found = cnt == kf
        active = done < 0.5
        lo = jnp.where(jnp.logical_and(active, cnt > kf), mid, lo)
        hi = jnp.where(jnp.logical_and(active, cnt < kf), mid, hi)
        tau = jnp.where(jnp.logical_and(active, found), mid, tau)
        done = jnp.where(jnp.logical_or(found, collapsed), 1.0, done)
        return lo, hi, done, tau

    def body(st):
        return (st[0] + 1,) + step(step(st[1:]))

    done0 = jnp.where(take_all, 1.0, 0.0).astype(F32)
    tau0 = jnp.full((1, LANES), POS_INF, F32)
    st0 = (jnp.int32(0), s_min, s_max, done0, tau0)
    st0 = lax.cond(jnp.min(done0) < 0.5,
                   lambda st: lax.fori_loop(0, N_UNCHECKED_ROUNDS, lambda _, s: body(s), st),
                   lambda st: st, st0)
    _, lo, hi, _, tau = lax.while_loop(cond, body, st0)

    unresolved = jnp.logical_and(tau == POS_INF, jnp.logical_not(take_all))
    n_unres = jnp.sum(unresolved.astype(F32))
    tau = jnp.where(take_all, F32_LOWEST, tau)

    def tie_path(tau):
        cut = jnp.where(count_ge(hi) >= kf, hi, lo)
        cut = jnp.where(unresolved, cut, POS_INF)
        c_gt = _key_count(st_ref, nchunks, chunk, lambda v, c0: v > cut)
        keep = kf - c_gt

        def key_iota(c0):
            return c0 + lax.broadcasted_iota(jnp.int32, (chunk, LANES), 0)

        def idx_body(_, st):
            xlo, xhi = st
            xmid = (xlo + xhi) // 2
            cnt = _key_count(
                st_ref, nchunks, chunk,
                lambda v, c0: jnp.logical_and(v == cut, key_iota(c0) < xmid))
            ok = cnt >= keep
            return jnp.where(ok, xlo, xmid), jnp.where(ok, xmid, xhi)

        xlo0 = jnp.zeros((1, LANES), jnp.int32)
        xhi0 = jnp.full((1, LANES), nchunks * chunk, jnp.int32)
        n_idx_steps = 15
        _, xcut = lax.fori_loop(0, n_idx_steps, idx_body, (xlo0, xhi0))

        def drop_body(c, carry):
            c0 = pl.multiple_of(c * chunk, chunk)
            v = st_ref[pl.ds(c0, chunk), :]
            drop = jnp.logical_and(v == cut, key_iota(c0) >= xcut)
            st_ref[pl.ds(c0, chunk), :] = jnp.where(drop, NEG_INF, v)
            return carry

        lax.fori_loop(0, nchunks, drop_body, 0)
        return jnp.where(unresolved, cut, tau)

    return lax.cond(n_unres > 0.0, tie_path, lambda t: t, tau)


def _prompt_attn_kernel(kmax2_ref, qi_ref, kw_ref, q_ref, kilo_ref, kihi_ref, k_ref, vt_ref, o_ref,
                        st_ref, m_ref, l_ref, acc_ref, accf_ref):
    i = pl.program_id(0)
    tq, ck = Q_BLOCK, KEY_CHUNK
    n_blocks = i + 2
    nchunks = (n_blocks * Q_BLOCK + ck - 1) // ck

    wt = jnp.transpose(kw_ref[...])[IDX_DIM:IDX_DIM + N_IDX_HEADS, :] * (IDX_DIM ** -0.5)
    qcat = jnp.concatenate(
        [qi_ref[:, p * LANES:(p + 1) * LANES] for p in range(N_IDX_HEADS // 2)], axis=0)
    key = lax.broadcasted_iota(jnp.int32, (ck, tq), 0)
    qry = i * tq + lax.broadcasted_iota(jnp.int32, (ck, tq), 1)

    def idx_chunk(c, st):
        smin, smax = st
        c0 = pl.multiple_of(c * ck, ck)
        klo = kilo_ref[pl.ds(c0, ck), :]
        khi = kihi_ref[pl.ds(c0, ck), :]
        acc = jnp.zeros((ck, tq), F32)
        for t in range(N_IDX_HEADS // 4):
            rhs = qcat[2 * t * tq:(2 * t + 2) * tq]
            dlo = lax.dot_general(klo, rhs, CONTRACT_LAST, preferred_element_type=F32)
            dhi = lax.dot_general(khi, rhs, CONTRACT_LAST, preferred_element_type=F32)
            for j in range(2):
                p = 2 * t + j
                cs = slice(j * tq, (j + 1) * tq)
                acc = acc + wt[2 * p:2 * p + 1, :] * jnp.maximum(dlo[:, cs], 0.0)
                acc = acc + wt[2 * p + 1:2 * p + 2, :] * jnp.maximum(dhi[:, cs], 0.0)
        col = c0 + key
        valid = jnp.logical_or(
            col < N_META,
            jnp.logical_and(col >= Q_BLOCK, col - Q_BLOCK <= qry))
        st_ref[pl.ds(c0, ck), :] = jnp.where(valid, acc, NEG_INF)
        smin = jnp.minimum(smin, jnp.min(jnp.where(valid, acc, POS_INF), axis=0, keepdims=True))
        smax = jnp.maximum(smax, jnp.max(jnp.where(valid, acc, NEG_INF), axis=0, keepdims=True))
        return smin, smax

    def idx_group(j, st):
        for e in range(IDX_GROUP):
            st = idx_chunk(IDX_GROUP * j + e, st)
        return st

    n_idx_groups = nchunks // IDX_GROUP
    st = lax.fori_loop(
        0, n_idx_groups, idx_group,
        (jnp.full((1, tq), POS_INF, F32), jnp.full((1, tq), NEG_INF, F32)))
    smin, smax = lax.fori_loop(n_idx_groups * IDX_GROUP, nchunks, idx_chunk, st)

    n_valid = (N_META + 1 + i * tq + lax.broadcasted_iota(jnp.int32, (1, tq), 1)).astype(F32)
    tau = _topk_threshold(st_ref, nchunks, ck, n_valid, smin, smax)

    scale = HEAD_DIM ** -0.5

    def q_group(n):
        return jnp.concatenate(
            [q_ref[:, (n * GROUP + g) * LANES:(n * GROUP + g + 1) * LANES]
             for g in range(GROUP)], axis=0)

    def k_chunk(c0, n):
        return k_ref[pl.ds(c0, ck), n * LANES:(n + 1) * LANES]

    def write_out(n, ot):
        for g in range(GROUP):
            h = n * GROUP + g
            o_ref[:, h * LANES:(h + 1) * LANES] = jnp.transpose(
                ot[:, g * tq:(g + 1) * tq]).astype(o_ref.dtype)

    def ref_body(c, mx):
        c0 = pl.multiple_of(c * ck, ck)
        sel = st_ref[pl.ds(c0, ck), :] >= tau
        out = []
        for n in range(N_KV_HEADS):
            raw = lax.dot_general(k_chunk(c0, n), q_group(n), CONTRACT_LAST,
                                  preferred_element_type=F32)
            cur = jnp.concatenate(
                [jnp.max(jnp.where(sel, raw[:, g * tq:(g + 1) * tq], NEG), axis=0, keepdims=True)
                 for g in range(GROUP)], axis=1)
            out.append(jnp.maximum(mx[n], cur))
        return tuple(out)

    mx = ref_body(0, tuple(jnp.full((1, GROUP * tq), NEG, F32) for _ in range(N_KV_HEADS)))
    c2 = scale * LOG2E
    ones = jnp.ones((SUBLANES, HEAD_DIM), BF16)
    gap = jnp.float32(0.0)
    for n in range(N_KV_HEADS):
        qf = q_group(n).astype(F32)
        qn2 = lax.dot_general(ones, (qf * qf).astype(BF16), CONTRACT_LAST,
                              preferred_element_type=F32)[0:1]
        bound = jnp.sqrt(qn2 * kmax2_ref[n]) * (c2 * NORM_BOUND_SLACK)
        gap = jnp.maximum(gap, jnp.max(bound - mx[n] * c2))

    def fixed_reference_path():
        accf_ref[...] = jnp.zeros(accf_ref.shape, F32)

        def chunk_update(c, n):
            c0 = pl.multiple_of(c * ck, ck)
            sel = st_ref[pl.ds(c0, ck), :] >= tau
            raw = lax.dot_general(k_chunk(c0, n), q_group(n), CONTRACT_LAST,
                                  preferred_element_type=F32)
            neg_ref = mx[n] * (-c2)
            p = jnp.concatenate(
                [jnp.exp2(raw[:, g * tq:(g + 1) * tq] * c2
                          + jnp.where(sel, neg_ref[:, g * tq:(g + 1) * tq], NEG)).astype(BF16)
                 for g in range(GROUP)], axis=1)
            return jnp.dot(vt_ref[c, n], p, preferred_element_type=F32)

        def group_body(j, carry):
            for n in range(N_KV_HEADS):
                upd = chunk_update(ATTN_GROUP * j, n)
                for e in range(1, ATTN_GROUP):
                    upd = upd + chunk_update(ATTN_GROUP * j + e, n)
                accf_ref[n] += upd
            return carry

        n_groups = nchunks // ATTN_GROUP
        lax.fori_loop(0, n_groups, group_body, 0)

        def single_body(c, carry):
            for n in range(N_KV_HEADS):
                accf_ref[n] += chunk_update(c, n)
            return carry

        lax.fori_loop(n_groups * ATTN_GROUP, nchunks, single_body, 0)

        for n in range(N_KV_HEADS):
            write_out(n, accf_ref[n, 0:HEAD_DIM, :] / accf_ref[n, HEAD_DIM:HEAD_DIM + 1, :])

    def running_max_path():
        m_ref[...] = jnp.full(m_ref.shape, NEG, F32)
        l_ref[...] = jnp.zeros(l_ref.shape, F32)
        acc_ref[...] = jnp.zeros(acc_ref.shape, F32)

        def body(c, carry):
            c0 = pl.multiple_of(c * ck, ck)
            bias = jnp.where(st_ref[pl.ds(c0, ck), :] >= tau, 0.0, NEG)
            bias = jnp.concatenate([bias] * GROUP, axis=1)
            for n in range(N_KV_HEADS):
                sc = lax.dot_general(k_chunk(c0, n), q_group(n), CONTRACT_LAST,
                                     preferred_element_type=F32)
                sc = sc * scale + bias
                m_old = m_ref[n]
                m_new = jnp.maximum(m_old, jnp.max(sc, axis=0, keepdims=True))
                alpha = jnp.exp(m_old - m_new)
                p = jnp.exp(sc - m_new)
                l_ref[n] = alpha * l_ref[n] + jnp.sum(
                    p.reshape(ck // SUBLANES, SUBLANES, GROUP * tq), axis=0)
                acc_ref[n] = alpha * acc_ref[n] + jnp.dot(
                    vt_ref[c, n, 0:HEAD_DIM, :], p.astype(BF16), preferred_element_type=F32)
                m_ref[n] = m_new
            return carry

        lax.fori_loop(0, nchunks, body, 0)
        for n in range(N_KV_HEADS):
            write_out(n, acc_ref[n] / jnp.sum(l_ref[n], axis=0, keepdims=True))

    lax.cond(gap <= MAX_EXP2_GAP, fixed_reference_path, running_max_path)


def _prompt_attention(kmax2, qi, kw, q, ki_lo, ki_hi, k_all, v_t):
    rows = q.shape[0]
    nk = k_all.shape[0]
    assert rows % Q_BLOCK == 0 and nk % KEY_CHUNK == 0
    assert nk >= (rows // Q_BLOCK + 1) * Q_BLOCK
    assert v_t.shape == (nk // KEY_CHUNK, N_KV_HEADS, HEAD_DIM + ONES_ROWS, KEY_CHUNK)
    row_spec = lambda w: pl.BlockSpec((Q_BLOCK, w), lambda i: (i, 0))
    return pl.pallas_call(
        _prompt_attn_kernel,
        grid=(rows // Q_BLOCK,),
        in_specs=[pl.BlockSpec(memory_space=pltpu.SMEM),
                  row_spec(1024), row_spec(LANES), row_spec(ATTN_WIDTH),
                  _const_spec(ki_lo.shape), _const_spec(ki_hi.shape),
                  _const_spec(k_all.shape), _const_spec(v_t.shape)],
        out_specs=row_spec(ATTN_WIDTH),
        out_shape=jax.ShapeDtypeStruct((rows, ATTN_WIDTH), BF16),
        scratch_shapes=[
            pltpu.VMEM((nk, Q_BLOCK), F32),
            pltpu.VMEM((N_KV_HEADS, 1, GROUP * Q_BLOCK), F32),
            pltpu.VMEM((N_KV_HEADS, SUBLANES, GROUP * Q_BLOCK), F32),
            pltpu.VMEM((N_KV_HEADS, HEAD_DIM, GROUP * Q_BLOCK), F32),
            pltpu.VMEM((N_KV_HEADS, HEAD_DIM + ONES_ROWS, GROUP * Q_BLOCK), F32),
        ],
        compiler_params=pltpu.CompilerParams(
            dimension_semantics=("arbitrary",), vmem_limit_bytes=VMEM_LIMIT),
        name="prompt_attn",
    )(kmax2, qi, kw, q, ki_lo, ki_hi, k_all, v_t)


SCORE_PAGE_SLOTS = 4
SCORE_SEQS_PER_STEP = 8
KV_PAGE_SLOTS = 3


def _prefetch_pages(copies, n_pages, n_slots, b, n_seq):
    depth = n_slots - 1

    def start(seq):
        for j in range(n_pages):
            for cp in copies(seq, seq % n_slots, j):
                cp.start()

    @pl.when(b == 0)
    def _():
        for s in range(depth):
            start(s)

    @pl.when(b + depth < n_seq)
    def _():
        start(b + depth)

    for j in range(n_pages):
        for cp in copies(b, b % n_slots, j):
            cp.wait()
    return b % n_slots


def _decode_scores_kernel(pt_ref, qi_ref, w_ref, kinew_ref, cache_ref, s_ref, buf, sem):
    b = pl.program_id(0)
    n_pages = pt_ref.shape[1]

    n_seq = qi_ref.shape[0]

    def copies(unit, slot, j):
        page = pt_ref[n_seq * unit + j // n_pages, j % n_pages]
        return (pltpu.make_async_copy(cache_ref.at[0, page], buf.at[slot, j], sem.at[slot]),)

    slot = _prefetch_pages(copies, n_seq * n_pages, buf.shape[0], b, pl.num_programs(0))

    past = n_pages * PAGE_SIZE
    tail = s_ref.shape[2] - past
    lane = lax.broadcasted_iota(jnp.int32, (1, tail), 1)
    for e in range(n_seq):
        qi = qi_ref[e]
        w = w_ref[e] * (IDX_DIM ** -0.5)
        for j in range(n_pages):
            kpt = buf[slot, e * n_pages + j].astype(BF16)
            d = jnp.dot(qi, kpt, preferred_element_type=F32)
            s_ref[e, :, j * PAGE_SIZE:(j + 1) * PAGE_SIZE] = jnp.sum(
                jnp.maximum(d, 0.0) * w, axis=0, keepdims=True)
        d_new = jnp.sum(qi.astype(F32) * kinew_ref[e].astype(F32), axis=-1, keepdims=True)
        s_new = jnp.sum(jnp.maximum(d_new, 0.0) * w, axis=0, keepdims=True)
        s_ref[e, :, past:] = jnp.where(lane == 0, s_new, NEG_INF)


def _decode_scores(page_table, qi_s, w_s, kinew_s, cache_kidx_t, width):
    db, n_pages = page_table.shape
    per = SCORE_SEQS_PER_STEP
    assert db % per == 0
    seq_spec = lambda a, c: pl.BlockSpec((per, a, c), lambda b, pt: (b, 0, 0))
    return pl.pallas_call(
        _decode_scores_kernel,
        grid_spec=pltpu.PrefetchScalarGridSpec(
            num_scalar_prefetch=1,
            grid=(db // per,),
            in_specs=[seq_spec(N_IDX_HEADS, IDX_DIM), seq_spec(N_IDX_HEADS, 1),
                      seq_spec(1, IDX_DIM), pl.BlockSpec(memory_space=pl.ANY)],
            out_specs=seq_spec(1, width),
            scratch_shapes=[pltpu.VMEM((SCORE_PAGE_SLOTS, per * n_pages, IDX_DIM, PAGE_SIZE), F32),
                            pltpu.SemaphoreType.DMA((SCORE_PAGE_SLOTS,))],
        ),
        out_shape=jax.ShapeDtypeStruct((db, 1, width), F32),
        compiler_params=pltpu.CompilerParams(dimension_semantics=("arbitrary",)),
        name="decode_scores",
    )(page_table, qi_s, w_s, kinew_s, cache_kidx_t)


def _decode_mask_kernel(s_in_ref, bias_ref, st_ref):
    nk = s_in_ref.shape[0]
    chunk = KEY_CHUNK
    s = s_in_ref[...]
    st_ref[...] = s
    valid = s > NEG_INF
    smin = jnp.min(jnp.where(valid, s, POS_INF), axis=0, keepdims=True)
    smax = jnp.max(s, axis=0, keepdims=True)
    n_valid = jnp.sum(valid.astype(F32), axis=0, keepdims=True)
    tau = _topk_threshold(st_ref, nk // chunk, chunk, n_valid, smin, smax)
    bias_ref[...] = jnp.where(st_ref[...] >= tau, 0.0, NEG)


def _decode_mask(scores_t):
    assert scores_t.shape[0] % KEY_CHUNK == 0 and scores_t.shape[1] == LANES
    return pl.pallas_call(
        _decode_mask_kernel,
        out_shape=jax.ShapeDtypeStruct(scores_t.shape, F32),
        scratch_shapes=[pltpu.VMEM(scores_t.shape, F32)],
        name="decode_mask",
    )(scores_t)


def _decode_attn_kernel(pt_ref, q_ref, bias_ref, knew_ref, vnew_ref, ck_ref, cv_ref, o_ref,
                        kbuf, vbuf, sem):
    b = pl.program_id(0)
    n_pages = pt_ref.shape[1]
    page_rows = N_KV_HEADS * PAGE_SIZE

    def copies(seq, slot, j):
        rows_j = pl.ds(j * page_rows, page_rows)
        page = pt_ref[seq, j]
        return (pltpu.make_async_copy(ck_ref.at[0, page], kbuf.at[slot, rows_j], sem.at[slot, 0]),
                pltpu.make_async_copy(cv_ref.at[0, page], vbuf.at[slot, rows_j], sem.at[slot, 1]))

    slot = _prefetch_pages(copies, n_pages, kbuf.shape[0], b, pl.num_programs(0))

    rows = n_pages * page_rows
    scale = HEAD_DIM ** -0.5
    q = q_ref[...]
    bias = jnp.concatenate(
        [jnp.broadcast_to(bias_ref[n:n + 1, :], (GROUP, bias_ref.shape[1]))
         for n in range(N_KV_HEADS)], axis=0)
    k_il = kbuf[slot].astype(BF16)
    v_il = vbuf[slot].astype(BF16)
    sc = lax.dot_general(q, k_il, CONTRACT_LAST, preferred_element_type=F32)
    sc = sc * scale + bias[:, :rows]
    sc_new = jnp.sum(q.astype(F32) * knew_ref[...].astype(F32), axis=-1, keepdims=True)
    sc_new = sc_new * scale + jnp.max(bias[:, rows:rows + N_KV_HEADS], axis=-1, keepdims=True)
    m = jnp.maximum(jnp.max(sc, axis=-1, keepdims=True), sc_new)
    p = jnp.exp(sc - m)
    p_new = jnp.exp(sc_new - m)
    denom = jnp.sum(p, axis=-1, keepdims=True) + p_new
    o = jnp.dot(p.astype(BF16), v_il, preferred_element_type=F32)
    o = o + p_new.astype(BF16).astype(F32) * vnew_ref[...].astype(F32)
    o_ref[...] = (o / denom).astype(o_ref.dtype)


def _decode_attention(page_table, q_s, bias_il, knew_s, vnew_s, cache_k, cache_v):
    db, n_pages = page_table.shape
    width2 = bias_il.shape[-1]
    rows = n_pages * N_KV_HEADS * PAGE_SIZE
    seq_spec = lambda a, c: pl.BlockSpec((None, a, c), lambda b, pt: (b, 0, 0))
    return pl.pallas_call(
        _decode_attn_kernel,
        grid_spec=pltpu.PrefetchScalarGridSpec(
            num_scalar_prefetch=1,
            grid=(db,),
            in_specs=[seq_spec(N_HEADS, HEAD_DIM), seq_spec(N_KV_HEADS, width2),
                      seq_spec(N_HEADS, HEAD_DIM), seq_spec(N_HEADS, HEAD_DIM),
                      pl.BlockSpec(memory_space=pl.ANY), pl.BlockSpec(memory_space=pl.ANY)],
            out_specs=seq_spec(N_HEADS, HEAD_DIM),
            scratch_shapes=[pltpu.VMEM((KV_PAGE_SLOTS, rows, HEAD_DIM), F32),
                            pltpu.VMEM((KV_PAGE_SLOTS, rows, HEAD_DIM), F32),
                            pltpu.SemaphoreType.DMA((KV_PAGE_SLOTS, 2))],
        ),
        out_shape=jax.ShapeDtypeStruct((db, N_HEADS, HEAD_DIM), BF16),
        compiler_params=pltpu.CompilerParams(
            dimension_semantics=("arbitrary",), vmem_limit_bytes=VMEM_LIMIT),
        name="decode_attn",
    )(page_table, q_s, bias_il, knew_s, vnew_s, cache_k, cache_v)


def _pool_decode_kernel(hist_ref, u_ref, d_ref):
    for g, w in enumerate(POOL_WINDOWS):
        cols = slice(g * POOL_GROUP_WIDTH, (g + 1) * POOL_GROUP_WIDTH)
        cur = u_ref[:, cols]
        tot = cur
        for j in range(1, w):
            tot = tot + hist_ref[POOL_HIST - j, :, cols]
        d_ref[:, cols] = tot * (1.0 / w) - cur


def _pool_decode(hist, u):
    return pl.pallas_call(
        _pool_decode_kernel,
        out_shape=jax.ShapeDtypeStruct(u.shape, F32),
        name="pool_decode",
    )(hist, u)


FF_TILE = 1024


def _rmsnorm(x, g):
    ms = jnp.mean(x * x, axis=-1, keepdims=True)
    return (x * lax.rsqrt(ms + EPS)) * g


def _post_kernel(x_ref, att_ref, d_ref, woa_ref, wop_ref, wpool_ref, pscale_ref,
                 gmlp_ref, gfin_ref, wup_ref, wdn_ref, y_ref, hn_ref):
    f = pl.program_id(1)

    @pl.when(f == 0)
    def _():
        d = d_ref[...]
        pooled = jnp.concatenate(
            [jnp.dot(d[:, g * POOL_GROUP_WIDTH:(g + 1) * POOL_GROUP_WIDTH].astype(BF16),
                     wpool_ref[g], preferred_element_type=F32)
             for g in range(len(POOL_WINDOWS))], axis=1) * pscale_ref[...]
        h = (x_ref[...]
             + jnp.dot(att_ref[...], woa_ref[...], preferred_element_type=F32)
             + jnp.dot(pooled.astype(BF16), wop_ref[...], preferred_element_type=F32))
        y_ref[...] = h
        hn_ref[...] = _rmsnorm(h, gmlp_ref[...]).astype(BF16)

    a = jnp.maximum(jnp.dot(hn_ref[...], wup_ref[...], preferred_element_type=F32), 0.0)
    y_ref[...] += jnp.dot((a * a).astype(BF16), wdn_ref[...], preferred_element_type=F32)

    @pl.when(f == pl.num_programs(1) - 1)
    def _():
        y_ref[...] = _rmsnorm(y_ref[...], gfin_ref[...])


def _post(x, att, d, wo_a, wo_p, w_pool, pool_scale, g_mlp, g_final, w_up, w_down, tm):
    rows = x.shape[0]
    assert rows % tm == 0 and D_FF % FF_TILE == 0
    row_spec = lambda w: pl.BlockSpec((tm, w), lambda i, f: (i, 0))
    return pl.pallas_call(
        _post_kernel,
        grid=(rows // tm, D_FF // FF_TILE),
        in_specs=[row_spec(D_MODEL), row_spec(ATTN_WIDTH), row_spec(POOL_WIDTH),
                  _const_spec(wo_a.shape), _const_spec(wo_p.shape), _const_spec(w_pool.shape),
                  _const_spec((1, POOL_WIDTH)), _const_spec((1, D_MODEL)), _const_spec((1, D_MODEL)),
                  pl.BlockSpec((D_MODEL, FF_TILE), lambda i, f: (0, f)),
                  pl.BlockSpec((FF_TILE, D_MODEL), lambda i, f: (f, 0))],
        out_specs=row_spec(D_MODEL),
        out_shape=jax.ShapeDtypeStruct((rows, D_MODEL), F32),
        scratch_shapes=[pltpu.VMEM((tm, D_MODEL), BF16)],
        compiler_params=pltpu.CompilerParams(
            dimension_semantics=("arbitrary", "arbitrary"), vmem_limit_bytes=VMEM_LIMIT),
        name="post",
    )(x, att, d, wo_a, wo_p, w_pool, pool_scale.reshape(1, POOL_WIDTH),
      g_mlp.reshape(1, D_MODEL), g_final.reshape(1, D_MODEL), w_up, w_down)


def kernel(x_prompt, x_sample, cache_k, cache_v, cache_kidx, state_pool, page_table, meta_tokens,
           g_mix, w_in, w_pool, pool_scale, w_out, g_mlp, w_up, w_down, g_final):
    assert x_prompt.shape[0] == 1 and x_sample.shape[1] == 1 and g_mix.shape[0] == 1
    seq = x_prompt.shape[1]
    db = x_sample.shape[0]
    n_pages = page_table.shape[1]
    past = n_pages * PAGE_SIZE
    assert db == LANES

    w = w_in[0]
    o = 0
    parts = []
    for width in (ATTN_WIDTH, 2 * N_KV_HEADS * HEAD_DIM, N_IDX_HEADS * IDX_DIM,
                  IDX_DIM + N_IDX_HEADS, POOL_WIDTH):
        parts.append(w[:, o:o + width].astype(BF16))
        o += width
    parts[3] = jnp.pad(parts[3], ((0, 0), (0, LANES - parts[3].shape[1])))
    wo_a = w_out[0, :ATTN_WIDTH].astype(BF16)
    wo_p = w_out[0, ATTN_WIDTH:].astype(BF16)
    w_pool_b = w_pool[0].astype(BF16)

    xs = jnp.concatenate([meta_tokens.astype(F32), x_sample[:, 0]], axis=0)
    pos_s = jnp.concatenate([jnp.arange(N_META, dtype=jnp.int32),
                             jnp.full((db,), past, jnp.int32)])
    q_s, kf_s, vf_s, kb_s, vb_s, qi_s, kw_s, u_s = _project(
        xs, jnp.zeros((1,), jnp.int32), pos_s, g_mix[0], parts)
    xp = x_prompt[0]
    assert seq % PROJ_TILE == 0
    tile_base = N_META + PROJ_TILE * jnp.arange(seq // PROJ_TILE, dtype=jnp.int32)
    q_p, kf_p, vf_p, kb_p, vb_p, qi_p, kw_p, d_p, u_tail, w_up_b, w_down_b = _project(
        xp, tile_base, jnp.arange(PROJ_TILE, dtype=jnp.int32), g_mix[0], parts,
        halo=u_s[:N_META], mlp_weights=(w_up[0], w_down[0]))

    nk = -(-(seq + 2 * Q_BLOCK) // KEY_CHUNK) * KEY_CHUNK

    def key_rows(meta_part, prompt_part):
        width = meta_part.shape[1]
        return jnp.concatenate(
            [meta_part, jnp.zeros((Q_BLOCK - N_META, width), meta_part.dtype), prompt_part,
             jnp.zeros((nk - Q_BLOCK - seq, width), meta_part.dtype)], axis=0)

    ki_all = key_rows(kw_s[:N_META, :IDX_DIM], kw_p[:, :IDX_DIM]).astype(BF16)
    k_all = key_rows(kb_s[:N_META], kb_p)
    v_all = key_rows(vb_s[:N_META], vb_p)
    v_t = jnp.transpose(
        v_all.reshape(nk // KEY_CHUNK, KEY_CHUNK, N_KV_HEADS, HEAD_DIM), (0, 2, 3, 1))
    v_t = jnp.concatenate(
        [v_t, jnp.ones((nk // KEY_CHUNK, N_KV_HEADS, ONES_ROWS, KEY_CHUNK), BF16)], axis=2)
    knorm = lambda kw: kw[:, KNORM_LANE:KNORM_LANE + N_KV_HEADS]
    kmax2 = jnp.maximum(jnp.max(knorm(kw_p), axis=0), jnp.max(knorm(kw_s[:N_META]), axis=0))
    zk = jnp.zeros_like(ki_all)
    ki_lo = jnp.concatenate([ki_all, zk], axis=1)
    ki_hi = jnp.concatenate([zk, ki_all], axis=1)
    att_p = _prompt_attention(kmax2, qi_p, kw_p, q_p, ki_lo, ki_hi, k_all, v_t)

    width = -(-(past + 1) // KEY_CHUNK) * KEY_CHUNK
    qi_d = qi_s[N_META:].reshape(db, N_IDX_HEADS, IDX_DIM)
    w_d = kw_s[N_META:, IDX_DIM:IDX_DIM + N_IDX_HEADS].reshape(db, N_IDX_HEADS, 1)
    kinew = kw_s[N_META:, :IDX_DIM].astype(BF16).reshape(db, 1, IDX_DIM)
    scores = _decode_scores(page_table, qi_d, w_d, kinew, jnp.swapaxes(cache_kidx, 2, 3), width)
    bias_t = _decode_mask(jnp.transpose(scores.reshape(db, width)))
    bias = jnp.transpose(bias_t)
    negs = jnp.full_like(bias, NEG)
    bias_il = jnp.stack(
        [jnp.stack([bias, negs], axis=-1), jnp.stack([negs, bias], axis=-1)], axis=1,
    ).reshape(db, N_KV_HEADS, N_KV_HEADS * width)
    n_phys = cache_k.shape[1]
    il_shape = (1, n_phys, N_KV_HEADS * PAGE_SIZE, HEAD_DIM)
    per_head = lambda a: jnp.repeat(a.reshape(db, N_KV_HEADS, HEAD_DIM), GROUP, axis=1)
    att_s = _decode_attention(
        page_table, q_s[N_META:].reshape(db, N_HEADS, HEAD_DIM), bias_il,
        per_head(kb_s[N_META:]), per_head(vb_s[N_META:]),
        cache_k.reshape(il_shape), cache_v.reshape(il_shape))
    att_s = att_s.reshape(db, ATTN_WIDTH)

    d_s = _pool_decode(jnp.swapaxes(state_pool[0], 0, 1), u_s[N_META:])

    post = functools.partial(
        _post, wo_a=wo_a, wo_p=wo_p, w_pool=w_pool_b, pool_scale=pool_scale[0],
        g_mlp=g_mlp[0], g_final=g_final, w_up=w_up_b, w_down=w_down_b)
    y_p = post(xp, att_p, d_p, tm=512)
    y_s = post(x_sample[:, 0], att_s, d_s, tm=db)

    k_prompt = jnp.concatenate([kf_s[:N_META], kf_p], axis=0).reshape(1, 1, N_META + seq, N_KV_HEADS, HEAD_DIM)
    v_prompt = jnp.concatenate([vf_s[:N_META], vf_p], axis=0).reshape(1, 1, N_META + seq, N_KV_HEADS, HEAD_DIM)
    kidx_prompt = jnp.concatenate([kw_s[:N_META, :IDX_DIM], kw_p[:, :IDX_DIM]], axis=0).reshape(
        1, 1, N_META + seq, IDX_DIM)
    pool_prompt = u_tail[POOL_HALO - POOL_HIST:].reshape(1, 1, POOL_HIST, POOL_WIDTH)
    k_sample = kf_s[N_META:].reshape(1, db, 1, N_KV_HEADS, HEAD_DIM)
    v_sample = vf_s[N_META:].reshape(1, db, 1, N_KV_HEADS, HEAD_DIM)
    kidx_sample = kw_s[N_META:, :IDX_DIM].reshape(1, db, 1, IDX_DIM)
    pool_sample = jnp.concatenate(
        [state_pool[0][:, 1:], u_s[N_META:][:, None, :]], axis=1)[None]
    return (y_p.reshape(1, seq, D_MODEL), y_s.reshape(db, 1, D_MODEL),
            k_prompt, v_prompt, kidx_prompt, pool_prompt,
            k_sample, v_sample, kidx_sample, pool_sample)
```

```python
import functools

import jax
import jax.numpy as jnp
from jax import lax
from jax.experimental import pallas as pl
from jax.experimental.pallas import tpu as pltpu

F32 = jnp.float32
BF16 = jnp.bfloat16

D_MODEL = 2048
N_META = 16
ATTN_WIDTH = 1024
POOL_WIDTH = 1024
HEAD_DIM = 128
N_HEADS = 8
N_KV_HEADS = 2
GROUP = N_HEADS // N_KV_HEADS
N_IDX_HEADS = 16
IDX_DIM = 64
TOPK = 256
POOL_WINDOWS = (2, 4, 8, 16)
POOL_GROUP_WIDTH = 256
POOL_HIST = 15
D_FF = 8192
PAGE_SIZE = 128
ROPE_THETA = 10000.0
EPS = 1e-6
NEG = -1e30
NEG_INF = float("-inf")
POS_INF = float("inf")
F32_LOWEST = -3.0e38

LANES = 128
SUBLANES = 8
VMEM_LIMIT = 56 * 1024 * 1024

Q_BLOCK = 128
KEY_CHUNK = 512
MAX_BISECT = 320
N_UNCHECKED_ROUNDS = 8
LOG2E = 1.4426950408889634
MAX_EXP2_GAP = 80.0
NORM_BOUND_SLACK = 1.02
ONES_ROWS = 16
KNORM_LANE = IDX_DIM + N_IDX_HEADS
PROJ_TILE = 256
ATTN_GROUP = 2
IDX_GROUP = 3
POOL_HALO = 16

CONTRACT_LAST = (((1,), (1,)), ((), ()))


def _const_spec(shape):
    nd = len(shape)
    return pl.BlockSpec(shape, lambda *_: (0,) * nd, pipeline_mode=pl.Buffered(1))


def _rope128(x, c, s):
    return x * c + pltpu.roll(x, 64, 1) * s


def _rope64(x, c, sa, sb):
    return x * c + pltpu.roll(x, 96, 1) * sa + pltpu.roll(x, 32, 1) * sb


def _angle_sum(cb_ref, sb_ref, cr_ref, sr_ref):
    cb, sb, cr, sr = cb_ref[...], sb_ref[...], cr_ref[...], sr_ref[...]
    return cb * cr - sb * sr, sb * cr + cb * sr


def _proj_kernel(x_ref, g_ref, cb128_ref, sb128_ref, cb64_ref, sb64_ref,
                 cr128_ref, sr128_ref, cr64_ref, sr64_ref,
                 wq_ref, wkv_ref, wqi_ref, wkw_ref, wu_ref, *rest, with_pool):
    if with_pool:
        (halo_ref, wup_ref, wdn_ref, q_ref, kf_ref, vf_ref, kb_ref, vb_ref, qi_ref, kw_ref, u_ref,
         utail_ref, wupb_ref, wdnb_ref, ubuf) = rest
        wupb_ref[...] = wup_ref[...].astype(BF16)
        wdnb_ref[...] = wdn_ref[...].astype(BF16)
    else:
        q_ref, kf_ref, vf_ref, kb_ref, vb_ref, qi_ref, kw_ref, u_ref = rest
    x = x_ref[...]
    ms = jnp.mean(x * x, axis=-1, keepdims=True)
    xn = ((x * lax.rsqrt(ms + EPS)) * g_ref[...]).astype(BF16)
    tm = x.shape[0]
    lane_t = lax.broadcasted_iota(jnp.int32, (tm, LANES), 1)
    c128, sfull = _angle_sum(cb128_ref, sb128_ref, cr128_ref, sr128_ref)
    s128 = jnp.where(lane_t < HEAD_DIM // 2, -sfull, sfull)
    c64, sfull = _angle_sum(cb64_ref, sb64_ref, cr64_ref, sr64_ref)
    first_half = jnp.bitwise_and(lane_t, IDX_DIM - 1) < IDX_DIM // 2
    sa64 = jnp.where(first_half, -sfull, 0.0)
    sb64 = jnp.where(first_half, 0.0, sfull)

    zq = jnp.dot(xn, wq_ref[...], preferred_element_type=F32)
    for h in range(N_HEADS):
        sl = slice(h * LANES, (h + 1) * LANES)
        q_ref[:, sl] = _rope128(zq[:, sl], c128, s128).astype(BF16)

    zkv = jnp.dot(xn, wkv_ref[...], preferred_element_type=F32)
    k_norm2 = []
    for h in range(N_KV_HEADS):
        sl = slice(h * LANES, (h + 1) * LANES)
        kr = _rope128(zkv[:, sl], c128, s128)
        kf_ref[:, h, :] = kr
        kb = kr.astype(BF16)
        kb_ref[:, sl] = kb
        k_norm2.append(jnp.sum(kb.astype(F32) * kb.astype(F32), axis=-1, keepdims=True))
    v = zkv[:, N_KV_HEADS * LANES:]
    for h in range(N_KV_HEADS):
        vf_ref[:, h, :] = v[:, h * LANES:(h + 1) * LANES]
    vb_ref[...] = v.astype(BF16)

    zqi = jnp.dot(xn, wqi_ref[...], preferred_element_type=F32)
    for p in range(N_IDX_HEADS // 2):
        sl = slice(p * LANES, (p + 1) * LANES)
        qi_ref[:, sl] = _rope64(zqi[:, sl], c64, sa64, sb64).astype(BF16)

    zkw = jnp.dot(xn, wkw_ref[...], preferred_element_type=F32)
    lane = lax.broadcasted_iota(jnp.int32, zkw.shape, 1)
    is_key = lane < IDX_DIM
    ckw = jnp.where(is_key, c64, jnp.where(lane < IDX_DIM + N_IDX_HEADS, N_IDX_HEADS ** -0.5, 1.0))
    kw = _rope64(zkw, ckw, jnp.where(is_key, sa64, 0.0), jnp.where(is_key, sb64, 0.0))
    for h in range(N_KV_HEADS):
        kw = jnp.where(lane == KNORM_LANE + h, k_norm2[h], kw)
    kw_ref[...] = kw

    u = jnp.dot(xn, wu_ref[...], preferred_element_type=F32)
    if not with_pool:
        u_ref[...] = u
        return

    @pl.when(pl.program_id(0) == 0)
    def _():
        ubuf[0:POOL_HALO, :] = halo_ref[...]

    ubuf[POOL_HALO:, :] = u
    for g, w in enumerate(POOL_WINDOWS):
        cols = slice(g * POOL_GROUP_WIDTH, (g + 1) * POOL_GROUP_WIDTH)
        cur = ubuf[POOL_HALO:POOL_HALO + tm, cols]
        tot = cur
        for j in range(1, w):
            tot = tot + ubuf[POOL_HALO - j:POOL_HALO - j + tm, cols]
        u_ref[:, cols] = tot * (1.0 / w) - cur
    tail = ubuf[tm:tm + POOL_HALO, :]
    ubuf[0:POOL_HALO, :] = tail
    utail_ref[...] = tail


def _project(x, pos_base, pos_rel, g_mix, w_parts, halo=None, mlp_weights=None):
    tm = pos_rel.shape[0]
    n_tiles = pos_base.shape[0]
    rows = x.shape[0]
    assert rows == tm * n_tiles
    with_pool = halo is not None
    assert with_pool == (mlp_weights is not None)
    base_tabs = [t.reshape(n_tiles, 1, LANES) for t in _rope_tables(pos_base)]
    rel_tabs = _rope_tables(pos_rel)
    row_spec = lambda w: pl.BlockSpec((tm, w), lambda i: (i, 0))
    base_spec = pl.BlockSpec((None, 1, LANES), lambda i: (i, 0, 0))
    out_shapes = [
        jax.ShapeDtypeStruct((rows, ATTN_WIDTH), BF16),
        jax.ShapeDtypeStruct((rows, N_KV_HEADS, HEAD_DIM), F32),
        jax.ShapeDtypeStruct((rows, N_KV_HEADS, HEAD_DIM), F32),
        jax.ShapeDtypeStruct((rows, 256), BF16),
        jax.ShapeDtypeStruct((rows, 256), BF16),
        jax.ShapeDtypeStruct((rows, 1024), BF16),
        jax.ShapeDtypeStruct((rows, LANES), F32),
        jax.ShapeDtypeStruct((rows, POOL_WIDTH), F32),
    ]
    out_specs = [row_spec(s.shape[1]) if len(s.shape) == 2 else
                 pl.BlockSpec((tm,) + s.shape[1:], lambda i: (i, 0, 0)) for s in out_shapes]
    in_specs = ([row_spec(D_MODEL), _const_spec((1, D_MODEL))] + [base_spec] * 4
                + [_const_spec((tm, LANES))] * 4 + [_const_spec(w.shape) for w in w_parts])
    args = [x, g_mix.reshape(1, D_MODEL), *base_tabs, *rel_tabs, *w_parts]
    scratch = []
    if with_pool:
        in_specs.append(_const_spec((POOL_HALO, POOL_WIDTH)))
        args.append(halo)
        out_shapes.append(jax.ShapeDtypeStruct((POOL_HALO, POOL_WIDTH), F32))
        out_specs.append(pl.BlockSpec((POOL_HALO, POOL_WIDTH), lambda i: (0, 0)))
        scratch.append(pltpu.VMEM((tm + POOL_HALO, POOL_WIDTH), F32))
        for wgt in mlp_weights:
            assert wgt.shape[0] % n_tiles == 0
            slab = pl.BlockSpec((wgt.shape[0] // n_tiles, wgt.shape[1]), lambda i: (i, 0))
            in_specs.append(slab)
            args.append(wgt)
            out_shapes.append(jax.ShapeDtypeStruct(wgt.shape, BF16))
            out_specs.append(slab)
    return pl.pallas_call(
        functools.partial(_proj_kernel, with_pool=with_pool),
        grid=(n_tiles,),
        in_specs=in_specs,
        out_specs=tuple(out_specs),
        out_shape=tuple(out_shapes),
        scratch_shapes=scratch,
        compiler_params=pltpu.CompilerParams(
            dimension_semantics=("arbitrary",), vmem_limit_bytes=VMEM_LIMIT),
        name="proj",
    )(*args)


def _rope_tables(pos):
    posf = pos.astype(F32)[:, None]
    out = []
    for dim in (HEAD_DIM, IDX_DIM):
        inv = ROPE_THETA ** (-jnp.arange(0, dim, 2, dtype=F32) / dim)
        ang = posf * inv[None, :]
        reps = LANES // (dim // 2)
        out += [jnp.tile(jnp.cos(ang), (1, reps)), jnp.tile(jnp.sin(ang), (1, reps))]
    return out


def _key_count(st_ref, nchunks, chunk, pred):
    acc_rows = 8 * SUBLANES

    def body(c, part):
        c0 = pl.multiple_of(c * chunk, chunk)
        hit = pred(st_ref[pl.ds(c0, chunk), :], c0)
        for g in range(chunk // acc_rows):
            part = jnp.where(hit[g * acc_rows:(g + 1) * acc_rows], part + 1.0, part)
        return part

    part = lax.fori_loop(0, nchunks, body, jnp.zeros((acc_rows, LANES), F32))
    return jnp.sum(part, axis=0, keepdims=True)


def _topk_threshold(st_ref, nchunks, chunk, n_valid, s_min, s_max):
    kf = float(TOPK)
    take_all = n_valid <= kf

    def count_ge(t):
        return _key_count(st_ref, nchunks, chunk, lambda v, c0: v >= t)

    def cond(st):
        it, done = st[0], st[3]
        return jnp.logical_and(it < MAX_BISECT, jnp.min(done) < 0.5)

    def step(st):
        lo, hi, done, tau = st
        mid = lo + (hi - lo) * 0.5
        collapsed = jnp.logical_or(mid <= lo, mid >= hi)
        cnt = count_ge(mid)
        found = cnt == kf
        active = done < 0.5
        lo = jnp.where(jnp.logical_and(active, cnt > kf), mid, lo)
        hi = jnp.where(jnp.logical_and(active, cnt < kf), mid, hi)
        tau = jnp.where(jnp.logical_and(active, found), mid, tau)
        done = jnp.where(jnp.logical_or(found, collapsed), 1.0, done)
        return lo, hi, done, tau

    def body(st):
        return (st[0] + 1,) + step(step(st[1:]))

    done0 = jnp.where(take_all, 1.0, 0.0).astype(F32)
    tau0 = jnp.full((1, LANES), POS_INF, F32)
    st0 = (jnp.int32(0), s_min, s_max, done0, tau0)
    st0 = lax.cond(jnp.min(done0) < 0.5,
                   lambda st: lax.fori_loop(0, N_UNCHECKED_ROUNDS, lambda _, s: body(s), st),
                   lambda st: st, st0)
    _, lo, hi, _, tau = lax.while_loop(cond, body, st0)

    unresolved = jnp.logical_and(tau == POS_INF, jnp.logical_not(take_all))
    n_unres = jnp.sum(unresolved.astype(F32))
    tau = jnp.where(take_all, F32_LOWEST, tau)

    def tie_path(tau):
        cut = jnp.where(count_ge(hi) >= kf, hi, lo)
        cut = jnp.where(unresolved, cut, POS_INF)
        c_gt = _key_count(st_ref, nchunks, chunk, lambda v, c0: v > cut)
        keep = kf - c_gt

        def key_iota(c0):
            return c0 + lax.broadcasted_iota(jnp.int32, (chunk, LANES), 0)

        def idx_body(_, st):
            xlo, xhi = st
            xmid = (xlo + xhi) // 2
            cnt = _key_count(
                st_ref, nchunks, chunk,
                lambda v, c0: jnp.logical_and(v == cut, key_iota(c0) < xmid))
            ok = cnt >= keep
            return jnp.where(ok, xlo, xmid), jnp.where(ok, xmid, xhi)

        xlo0 = jnp.zeros((1, LANES), jnp.int32)
        xhi0 = jnp.full((1, LANES), nchunks * chunk, jnp.int32)
        n_idx_steps = 15
        _, xcut = lax.fori_loop(0, n_idx_steps, idx_body, (xlo0, xhi0))

        def drop_body(c, carry):
            c0 = pl.multiple_of(c * chunk, chunk)
            v = st_ref[pl.ds(c0, chunk), :]
            drop = jnp.logical_and(v == cut, key_iota(c0) >= xcut)
            st_ref[pl.ds(c0, chunk), :] = jnp.where(drop, NEG_INF, v)
            return carry

        lax.fori_loop(0, nchunks, drop_body, 0)
        return jnp.where(unresolved, cut, tau)

    return lax.cond(n_unres > 0.0, tie_path, lambda t: t, tau)


def _prompt_attn_kernel(kmax2_ref, qi_ref, kw_ref, q_ref, kilo_ref, kihi_ref, k_ref, vt_ref, o_ref,
                        st_ref, m_ref, l_ref, acc_ref, accf_ref):
    i = pl.program_id(0)
    tq, ck = Q_BLOCK, KEY_CHUNK
    n_blocks = i + 2
    nchunks = (n_blocks * Q_BLOCK + ck - 1) // ck

    wt = jnp.transpose(kw_ref[...])[IDX_DIM:IDX_DIM + N_IDX_HEADS, :] * (IDX_DIM ** -0.5)
    qcat = jnp.concatenate(
        [qi_ref[:, p * LANES:(p + 1) * LANES] for p in range(N_IDX_HEADS // 2)], axis=0)
    key = lax.broadcasted_iota(jnp.int32, (ck, tq), 0)
    qry = i * tq + lax.broadcasted_iota(jnp.int32, (ck, tq), 1)

    def idx_chunk(c, st):
        smin, smax = st
        c0 = pl.multiple_of(c * ck, ck)
        klo = kilo_ref[pl.ds(c0, ck), :]
        khi = kihi_ref[pl.ds(c0, ck), :]
        acc = jnp.zeros((ck, tq), F32)
        for t in range(N_IDX_HEADS // 4):
            rhs = qcat[2 * t * tq:(2 * t + 2) * tq]
            dlo = lax.dot_general(klo, rhs, CONTRACT_LAST, preferred_element_type=F32)
            dhi = lax.dot_general(khi, rhs, CONTRACT_LAST, preferred_element_type=F32)
            for j in range(2):
                p = 2 * t + j
                cs = slice(j * tq, (j + 1) * tq)
                acc = acc + wt[2 * p:2 * p + 1, :] * jnp.maximum(dlo[:, cs], 0.0)
                acc = acc + wt[2 * p + 1:2 * p + 2, :] * jnp.maximum(dhi[:, cs], 0.0)
        col = c0 + key
        valid = jnp.logical_or(
            col < N_META,
            jnp.logical_and(col >= Q_BLOCK, col - Q_BLOCK <= qry))
        st_ref[pl.ds(c0, ck), :] = jnp.where(valid, acc, NEG_INF)
        smin = jnp.minimum(smin, jnp.min(jnp.where(valid, acc, POS_INF), axis=0, keepdims=True))
        smax = jnp.maximum(smax, jnp.max(jnp.where(valid, acc, NEG_INF), axis=0, keepdims=True))
        return smin, smax

    def idx_group(j, st):
        for e in range(IDX_GROUP):
            st = idx_chunk(IDX_GROUP * j + e, st)
        return st

    n_idx_groups = nchunks // IDX_GROUP
    st = lax.fori_loop(
        0, n_idx_groups, idx_group,
        (jnp.full((1, tq), POS_INF, F32), jnp.full((1, tq), NEG_INF, F32)))
    smin, smax = lax.fori_loop(n_idx_groups * IDX_GROUP, nchunks, idx_chunk, st)

    n_valid = (N_META + 1 + i * tq + lax.broadcasted_iota(jnp.int32, (1, tq), 1)).astype(F32)
    tau = _topk_threshold(st_ref, nchunks, ck, n_valid, smin, smax)

    scale = HEAD_DIM ** -0.5

    def q_group(n):
        return jnp.concatenate(
            [q_ref[:, (n * GROUP + g) * LANES:(n * GROUP + g + 1) * LANES]
             for g in range(GROUP)], axis=0)

    def k_chunk(c0, n):
        return k_ref[pl.ds(c0, ck), n * LANES:(n + 1) * LANES]

    def write_out(n, ot):
        for g in range(GROUP):
            h = n * GROUP + g
            o_ref[:, h * LANES:(h + 1) * LANES] = jnp.transpose(
                ot[:, g * tq:(g + 1) * tq]).astype(o_ref.dtype)

    def ref_body(c, mx):
        c0 = pl.multiple_of(c * ck, ck)
        sel = st_ref[pl.ds(c0, ck), :] >= tau
        out = []
        for n in range(N_KV_HEADS):
            raw = lax.dot_general(k_chunk(c0, n), q_group(n), CONTRACT_LAST,
                                  preferred_element_type=F32)
            cur = jnp.concatenate(
                [jnp.max(jnp.where(sel, raw[:, g * tq:(g + 1) * tq], NEG), axis=0, keepdims=True)
                 for g in range(GROUP)], axis=1)
            out.append(jnp.maximum(mx[n], cur))
        return tuple(out)

    mx = ref_body(0, tuple(jnp.full((1, GROUP * tq), NEG, F32) for _ in range(N_KV_HEADS)))
    c2 = scale * LOG2E
    ones = jnp.ones((SUBLANES, HEAD_DIM), BF16)
    gap = jnp.float32(0.0)
    for n in range(N_KV_HEADS):
        qf = q_group(n).astype(F32)
        qn2 = lax.dot_general(ones, (qf * qf).astype(BF16), CONTRACT_LAST,
                              preferred_element_type=F32)[0:1]
        bound = jnp.sqrt(qn2 * kmax2_ref[n]) * (c2 * NORM_BOUND_SLACK)
        gap = jnp.maximum(gap, jnp.max(bound - mx[n] * c2))

    def fixed_reference_path():
        accf_ref[...] = jnp.zeros(accf_ref.shape, F32)

        def chunk_update(c, n):
            c0 = pl.multiple_of(c * ck, ck)
            sel = st_ref[pl.ds(c0, ck), :] >= tau
            raw = lax.dot_general(k_chunk(c0, n), q_group(n), CONTRACT_LAST,
                                  preferred_element_type=F32)
            neg_ref = mx[n] * (-c2)
            p = jnp.concatenate(
                [jnp.exp2(raw[:, g * tq:(g + 1) * tq] * c2
                          + jnp.where(sel, neg_ref[:, g * tq:(g + 1) * tq], NEG)).astype(BF16)
                 for g in range(GROUP)], axis=1)
            return jnp.dot(vt_ref[c, n], p, preferred_element_type=F32)

        def group_body(j, carry):
            for n in range(N_KV_HEADS):
                upd = chunk_update(ATTN_GROUP * j, n)
                for e in range(1, ATTN_GROUP):
                    upd = upd + chunk_update(ATTN_GROUP * j + e, n)
                accf_ref[n] += upd
            return carry

        n_groups = nchunks // ATTN_GROUP
        lax.fori_loop(0, n_groups, group_body, 0)

        def single_body(c, carry):
            for n in range(N_KV_HEADS):
                accf_ref[n] += chunk_update(c, n)
            return carry

        lax.fori_loop(n_groups * ATTN_GROUP, nchunks, single_body, 0)

        for n in range(N_KV_HEADS):
            write_out(n, accf_ref[n, 0:HEAD_DIM, :] / accf_ref[n, HEAD_DIM:HEAD_DIM + 1, :])

    def running_max_path():
        m_ref[...] = jnp.full(m_ref.shape, NEG, F32)
        l_ref[...] = jnp.zeros(l_ref.shape, F32)
        acc_ref[...] = jnp.zeros(acc_ref.shape, F32)

        def body(c, carry):
            c0 = pl.multiple_of(c * ck, ck)
            bias = jnp.where(st_ref[pl.ds(c0, ck), :] >= tau, 0.0, NEG)
            bias = jnp.concatenate([bias] * GROUP, axis=1)
            for n in range(N_KV_HEADS):
                sc = lax.dot_general(k_chunk(c0, n), q_group(n), CONTRACT_LAST,
                                     preferred_element_type=F32)
                sc = sc * scale + bias
                m_old = m_ref[n]
                m_new = jnp.maximum(m_old, jnp.max(sc, axis=0, keepdims=True))
                alpha = jnp.exp(m_old - m_new)
                p = jnp.exp(sc - m_new)
                l_ref[n] = alpha * l_ref[n] + jnp.sum(
                    p.reshape(ck // SUBLANES, SUBLANES, GROUP * tq), axis=0)
                acc_ref[n] = alpha * acc_ref[n] + jnp.dot(
                    vt_ref[c, n, 0:HEAD_DIM, :], p.astype(BF16), preferred_element_type=F32)
                m_ref[n] = m_new
            return carry

        lax.fori_loop(0, nchunks, body, 0)
        for n in range(N_KV_HEADS):
            write_out(n, acc_ref[n] / jnp.sum(l_ref[n], axis=0, keepdims=True))

    lax.cond(gap <= MAX_EXP2_GAP, fixed_reference_path, running_max_path)


def _prompt_attention(kmax2, qi, kw, q, ki_lo, ki_hi, k_all, v_t):
    rows = q.shape[0]
    nk = k_all.shape[0]
    assert rows % Q_BLOCK == 0 and nk % KEY_CHUNK == 0
    assert nk >= (rows // Q_BLOCK + 1) * Q_BLOCK
    assert v_t.shape == (nk // KEY_CHUNK, N_KV_HEADS, HEAD_DIM + ONES_ROWS, KEY_CHUNK)
    row_spec = lambda w: pl.BlockSpec((Q_BLOCK, w), lambda i: (i, 0))
    return pl.pallas_call(
        _prompt_attn_kernel,
        grid=(rows // Q_BLOCK,),
        in_specs=[pl.BlockSpec(memory_space=pltpu.SMEM),
                  row_spec(1024), row_spec(LANES), row_spec(ATTN_WIDTH),
                  _const_spec(ki_lo.shape), _const_spec(ki_hi.shape),
                  _const_spec(k_all.shape), _const_spec(v_t.shape)],
        out_specs=row_spec(ATTN_WIDTH),
        out_shape=jax.ShapeDtypeStruct((rows, ATTN_WIDTH), BF16),
        scratch_shapes=[
            pltpu.VMEM((nk, Q_BLOCK), F32),
            pltpu.VMEM((N_KV_HEADS, 1, GROUP * Q_BLOCK), F32),
            pltpu.VMEM((N_KV_HEADS, SUBLANES, GROUP * Q_BLOCK), F32),
            pltpu.VMEM((N_KV_HEADS, HEAD_DIM, GROUP * Q_BLOCK), F32),
            pltpu.VMEM((N_KV_HEADS, HEAD_DIM + ONES_ROWS, GROUP * Q_BLOCK), F32),
        ],
        compiler_params=pltpu.CompilerParams(
            dimension_semantics=("arbitrary",), vmem_limit_bytes=VMEM_LIMIT),
        name="prompt_attn",
    )(kmax2, qi, kw, q, ki_lo, ki_hi, k_all, v_t)


SCORE_PAGE_SLOTS = 4
SCORE_SEQS_PER_STEP = 8
KV_PAGE_SLOTS = 3


def _prefetch_pages(copies, n_pages, n_slots, b, n_seq):
    depth = n_slots - 1

    def start(seq):
        for j in range(n_pages):
            for cp in copies(seq, seq % n_slots, j):
                cp.start(priority=j % 2)

    @pl.when(b == 0)
    def _():
        for s in range(depth):
            start(s)

    @pl.when(b + depth < n_seq)
    def _():
        start(b + depth)

    for j in range(n_pages):
        for cp in copies(b, b % n_slots, j):
            cp.wait()
    return b % n_slots


def _decode_scores_kernel(pt_ref, qi_ref, w_ref, kinew_ref, cache_ref, s_ref, buf, sem):
    b = pl.program_id(0)
    n_pages = pt_ref.shape[1]

    n_seq = qi_ref.shape[0]

    def copies(unit, slot, j):
        page = pt_ref[n_seq * unit + j // n_pages, j % n_pages]
        return (pltpu.make_async_copy(cache_ref.at[0, page], buf.at[slot, j], sem.at[slot]),)

    slot = _prefetch_pages(copies, n_seq * n_pages, buf.shape[0], b, pl.num_programs(0))

    past = n_pages * PAGE_SIZE
    tail = s_ref.shape[2] - past
    lane = lax.broadcasted_iota(jnp.int32, (1, tail), 1)
    for e in range(n_seq):
        qi = qi_ref[e]
        w = w_ref[e] * (IDX_DIM ** -0.5)
        for j in range(n_pages):
            kpt = buf[slot, e * n_pages + j].astype(BF16)
            d = jnp.dot(qi, kpt, preferred_element_type=F32)
            s_ref[e, :, j * PAGE_SIZE:(j + 1) * PAGE_SIZE] = jnp.sum(
                jnp.maximum(d, 0.0) * w, axis=0, keepdims=True)
        d_new = jnp.sum(qi.astype(F32) * kinew_ref[e].astype(F32), axis=-1, keepdims=True)
        s_new = jnp.sum(jnp.maximum(d_new, 0.0) * w, axis=0, keepdims=True)
        s_ref[e, :, past:] = jnp.where(lane == 0, s_new, NEG_INF)


def _decode_scores(page_table, qi_s, w_s, kinew_s, cache_kidx_t, width):
    db, n_pages = page_table.shape
    per = SCORE_SEQS_PER_STEP
    assert db % per == 0
    seq_spec = lambda a, c: pl.BlockSpec((per, a, c), lambda b, pt: (b, 0, 0))
    return pl.pallas_call(
        _decode_scores_kernel,
        grid_spec=pltpu.PrefetchScalarGridSpec(
            num_scalar_prefetch=1,
            grid=(db // per,),
            in_specs=[seq_spec(N_IDX_HEADS, IDX_DIM), seq_spec(N_IDX_HEADS, 1),
                      seq_spec(1, IDX_DIM), pl.BlockSpec(memory_space=pl.ANY)],
            out_specs=seq_spec(1, width),
            scratch_shapes=[pltpu.VMEM((SCORE_PAGE_SLOTS, per * n_pages, IDX_DIM, PAGE_SIZE), F32),
                            pltpu.SemaphoreType.DMA((SCORE_PAGE_SLOTS,))],
        ),
        out_shape=jax.ShapeDtypeStruct((db, 1, width), F32),
        compiler_params=pltpu.CompilerParams(dimension_semantics=("arbitrary",)),
        name="decode_scores",
    )(page_table, qi_s, w_s, kinew_s, cache_kidx_t)


def _decode_mask_kernel(s_in_ref, bias_ref, st_ref):
    nk = s_in_ref.shape[0]
    chunk = KEY_CHUNK
    s = s_in_ref[...]
    st_ref[...] = s
    valid = s > NEG_INF
    smin = jnp.min(jnp.where(valid, s, POS_INF), axis=0, keepdims=True)
    smax = jnp.max(s, axis=0, keepdims=True)
    n_valid = jnp.sum(valid.astype(F32), axis=0, keepdims=True)
    tau = _topk_threshold(st_ref, nk // chunk, chunk, n_valid, smin, smax)
    bias_ref[...] = jnp.where(st_ref[...] >= tau, 0.0, NEG)


def _decode_mask(scores_t):
    assert scores_t.shape[0] % KEY_CHUNK == 0 and scores_t.shape[1] == LANES
    return pl.pallas_call(
        _decode_mask_kernel,
        out_shape=jax.ShapeDtypeStruct(scores_t.shape, F32),
        scratch_shapes=[pltpu.VMEM(scores_t.shape, F32)],
        name="decode_mask",
    )(scores_t)


def _decode_attn_kernel(pt_ref, q_ref, bias_ref, knew_ref, vnew_ref, ck_ref, cv_ref, o_ref,
                        kbuf, vbuf, sem):
    b = pl.program_id(0)
    n_pages = pt_ref.shape[1]
    page_rows = N_KV_HEADS * PAGE_SIZE

    def copies(seq, slot, j):
        rows_j = pl.ds(j * page_rows, page_rows)
        page = pt_ref[seq, j]
        return (pltpu.make_async_copy(ck_ref.at[0, page], kbuf.at[slot, rows_j], sem.at[slot, 0]),
                pltpu.make_async_copy(cv_ref.at[0, page], vbuf.at[slot, rows_j], sem.at[slot, 1]))

    slot = _prefetch_pages(copies, n_pages, kbuf.shape[0], b, pl.num_programs(0))

    rows = n_pages * page_rows
    scale = HEAD_DIM ** -0.5
    q = q_ref[...]
    bias = jnp.concatenate(
        [jnp.broadcast_to(bias_ref[n:n + 1, :], (GROUP, bias_ref.shape[1]))
         for n in range(N_KV_HEADS)], axis=0)
    k_il = kbuf[slot].astype(BF16)
    v_il = vbuf[slot].astype(BF16)
    sc = lax.dot_general(q, k_il, CONTRACT_LAST, preferred_element_type=F32)
    sc = sc * scale + bias[:, :rows]
    sc_new = jnp.sum(q.astype(F32) * knew_ref[...].astype(F32), axis=-1, keepdims=True)
    sc_new = sc_new * scale + jnp.max(bias[:, rows:rows + N_KV_HEADS], axis=-1, keepdims=True)
    m = jnp.maximum(jnp.max(sc, axis=-1, keepdims=True), sc_new)
    p = jnp.exp(sc - m)
    p_new = jnp.exp(sc_new - m)
    denom = jnp.sum(p, axis=-1, keepdims=True) + p_new
    o = jnp.dot(p.astype(BF16), v_il, preferred_element_type=F32)
    o = o + p_new.astype(BF16).astype(F32) * vnew_ref[...].astype(F32)
    o_ref[...] = (o / denom).astype(o_ref.dtype)


def _decode_attention(page_table, q_s, bias_il, knew_s, vnew_s, cache_k, cache_v):
    db, n_pages = page_table.shape
    width2 = bias_il.shape[-1]
    rows = n_pages * N_KV_HEADS * PAGE_SIZE
    seq_spec = lambda a, c: pl.BlockSpec((None, a, c), lambda b, pt: (b, 0, 0))
    return pl.pallas_call(
        _decode_attn_kernel,
        grid_spec=pltpu.PrefetchScalarGridSpec(
            num_scalar_prefetch=1,
            grid=(db,),
            in_specs=[seq_spec(N_HEADS, HEAD_DIM), seq_spec(N_KV_HEADS, width2),
                      seq_spec(N_HEADS, HEAD_DIM), seq_spec(N_HEADS, HEAD_DIM),
                      pl.BlockSpec(memory_space=pl.ANY), pl.BlockSpec(memory_space=pl.ANY)],
            out_specs=seq_spec(N_HEADS, HEAD_DIM),
            scratch_shapes=[pltpu.VMEM((KV_PAGE_SLOTS, rows, HEAD_DIM), F32),
                            pltpu.VMEM((KV_PAGE_SLOTS, rows, HEAD_DIM), F32),
                            pltpu.SemaphoreType.DMA((KV_PAGE_SLOTS, 2))],
        ),
        out_shape=jax.ShapeDtypeStruct((db, N_HEADS, HEAD_DIM), BF16),
        compiler_params=pltpu.CompilerParams(
            dimension_semantics=("arbitrary",), vmem_limit_bytes=VMEM_LIMIT),
        name="decode_attn",
    )(page_table, q_s, bias_il, knew_s, vnew_s, cache_k, cache_v)


def _pool_decode_kernel(hist_ref, u_ref, d_ref):
    for g, w in enumerate(POOL_WINDOWS):
        cols = slice(g * POOL_GROUP_WIDTH, (g + 1) * POOL_GROUP_WIDTH)
        cur = u_ref[:, cols]
        tot = cur
        for j in range(1, w):
            tot = tot + hist_ref[POOL_HIST - j, :, cols]
        d_ref[:, cols] = tot * (1.0 / w) - cur


def _pool_decode(hist, u):
    return pl.pallas_call(
        _pool_decode_kernel,
        out_shape=jax.ShapeDtypeStruct(u.shape, F32),
        name="pool_decode",
    )(hist, u)


FF_TILE = 1024


def _rmsnorm(x, g):
    ms = jnp.mean(x * x, axis=-1, keepdims=True)
    return (x * lax.rsqrt(ms + EPS)) * g


def _post_kernel(x_ref, att_ref, d_ref, woa_ref, wop_ref, wpool_ref, pscale_ref,
                 gmlp_ref, gfin_ref, wup_ref, wdn_ref, y_ref, hn_ref):
    f = pl.program_id(1)

    @pl.when(f == 0)
    def _():
        d = d_ref[...]
        pooled = jnp.concatenate(
            [jnp.dot(d[:, g * POOL_GROUP_WIDTH:(g + 1) * POOL_GROUP_WIDTH].astype(BF16),
                     wpool_ref[g], preferred_element_type=F32)
             for g in range(len(POOL_WINDOWS))], axis=1) * pscale_ref[...]
        h = (x_ref[...]
             + jnp.dot(att_ref[...], woa_ref[...], preferred_element_type=F32)
             + jnp.dot(pooled.astype(BF16), wop_ref[...], preferred_element_type=F32))
        y_ref[...] = h
        hn_ref[...] = _rmsnorm(h, gmlp_ref[...]).astype(BF16)

    a = jnp.maximum(jnp.dot(hn_ref[...], wup_ref[...], preferred_element_type=F32), 0.0)
    y_ref[...] += jnp.dot((a * a).astype(BF16), wdn_ref[...], preferred_element_type=F32)

    @pl.when(f == pl.num_programs(1) - 1)
    def _():
        y_ref[...] = _rmsnorm(y_ref[...], gfin_ref[...])


def _post(x, att, d, wo_a, wo_p, w_pool, pool_scale, g_mlp, g_final, w_up, w_down, tm):
    rows = x.shape[0]
    assert rows % tm == 0 and D_FF % FF_TILE == 0
    row_spec = lambda w: pl.BlockSpec((tm, w), lambda i, f: (i, 0))
    return pl.pallas_call(
        _post_kernel,
        grid=(rows // tm, D_FF // FF_TILE),
        in_specs=[row_spec(D_MODEL), row_spec(ATTN_WIDTH), row_spec(POOL_WIDTH),
                  _const_spec(wo_a.shape), _const_spec(wo_p.shape), _const_spec(w_pool.shape),
                  _const_spec((1, POOL_WIDTH)), _const_spec((1, D_MODEL)), _const_spec((1, D_MODEL)),
                  pl.BlockSpec((D_MODEL, FF_TILE), lambda i, f: (0, f)),
                  pl.BlockSpec((FF_TILE, D_MODEL), lambda i, f: (f, 0))],
        out_specs=row_spec(D_MODEL),
        out_shape=jax.ShapeDtypeStruct((rows, D_MODEL), F32),
        scratch_shapes=[pltpu.VMEM((tm, D_MODEL), BF16)],
        compiler_params=pltpu.CompilerParams(
            dimension_semantics=("arbitrary", "arbitrary"), vmem_limit_bytes=VMEM_LIMIT),
        name="post",
    )(x, att, d, wo_a, wo_p, w_pool, pool_scale.reshape(1, POOL_WIDTH),
      g_mlp.reshape(1, D_MODEL), g_final.reshape(1, D_MODEL), w_up, w_down)


def kernel(x_prompt, x_sample, cache_k, cache_v, cache_kidx, state_pool, page_table, meta_tokens,
           g_mix, w_in, w_pool, pool_scale, w_out, g_mlp, w_up, w_down, g_final):
    assert x_prompt.shape[0] == 1 and x_sample.shape[1] == 1 and g_mix.shape[0] == 1
    seq = x_prompt.shape[1]
    db = x_sample.shape[0]
    n_pages = page_table.shape[1]
    past = n_pages * PAGE_SIZE
    assert db == LANES

    w = w_in[0]
    o = 0
    parts = []
    for width in (ATTN_WIDTH, 2 * N_KV_HEADS * HEAD_DIM, N_IDX_HEADS * IDX_DIM,
                  IDX_DIM + N_IDX_HEADS, POOL_WIDTH):
        parts.append(w[:, o:o + width].astype(BF16))
        o += width
    parts[3] = jnp.pad(parts[3], ((0, 0), (0, LANES - parts[3].shape[1])))
    wo_a = w_out[0, :ATTN_WIDTH].astype(BF16)
    wo_p = w_out[0, ATTN_WIDTH:].astype(BF16)
    w_pool_b = w_pool[0].astype(BF16)

    xs = jnp.concatenate([meta_tokens.astype(F32), x_sample[:, 0]], axis=0)
    pos_s = jnp.concatenate([jnp.arange(N_META, dtype=jnp.int32),
                             jnp.full((db,), past, jnp.int32)])
    q_s, kf_s, vf_s, kb_s, vb_s, qi_s, kw_s, u_s = _project(
        xs, jnp.zeros((1,), jnp.int32), pos_s, g_mix[0], parts)
    xp = x_prompt[0]
    assert seq % PROJ_TILE == 0
    tile_base = N_META + PROJ_TILE * jnp.arange(seq // PROJ_TILE, dtype=jnp.int32)
    q_p, kf_p, vf_p, kb_p, vb_p, qi_p, kw_p, d_p, u_tail, w_up_b, w_down_b = _project(
        xp, tile_base, jnp.arange(PROJ_TILE, dtype=jnp.int32), g_mix[0], parts,
        halo=u_s[:N_META], mlp_weights=(w_up[0], w_down[0]))

    nk = -(-(seq + 2 * Q_BLOCK) // KEY_CHUNK) * KEY_CHUNK

    def key_rows(meta_part, prompt_part):
        width = meta_part.shape[1]
        return jnp.concatenate(
            [meta_part, jnp.zeros((Q_BLOCK - N_META, width), meta_part.dtype), prompt_part,
             jnp.zeros((nk - Q_BLOCK - seq, width), meta_part.dtype)], axis=0)

    ki_all = key_rows(kw_s[:N_META, :IDX_DIM], kw_p[:, :IDX_DIM]).astype(BF16)
    k_all = key_rows(kb_s[:N_META], kb_p)
    v_all = key_rows(vb_s[:N_META], vb_p)
    v_t = jnp.transpose(
        v_all.reshape(nk // KEY_CHUNK, KEY_CHUNK, N_KV_HEADS, HEAD_DIM), (0, 2, 3, 1))
    v_t = jnp.concatenate(
        [v_t, jnp.ones((nk // KEY_CHUNK, N_KV_HEADS, ONES_ROWS, KEY_CHUNK), BF16)], axis=2)
    knorm = lambda kw: kw[:, KNORM_LANE:KNORM_LANE + N_KV_HEADS]
    kmax2 = jnp.maximum(jnp.max(knorm(kw_p), axis=0), jnp.max(knorm(kw_s[:N_META]), axis=0))
    zk = jnp.zeros_like(ki_all)
    ki_lo = jnp.concatenate([ki_all, zk], axis=1)
    ki_hi = jnp.concatenate([zk, ki_all], axis=1)
    att_p = _prompt_attention(kmax2, qi_p, kw_p, q_p, ki_lo, ki_hi, k_all, v_t)

    width = -(-(past + 1) // KEY_CHUNK) * KEY_CHUNK
    qi_d = qi_s[N_META:].reshape(db, N_IDX_HEADS, IDX_DIM)
    w_d = kw_s[N_META:, IDX_DIM:IDX_DIM + N_IDX_HEADS].reshape(db, N_IDX_HEADS, 1)
    kinew = kw_s[N_META:, :IDX_DIM].astype(BF16).reshape(db, 1, IDX_DIM)
    scores = _decode_scores(page_table, qi_d, w_d, kinew, jnp.swapaxes(cache_kidx, 2, 3), width)
    bias_t = _decode_mask(jnp.transpose(scores.reshape(db, width)))
    bias = jnp.transpose(bias_t)
    negs = jnp.full_like(bias, NEG)
    bias_il = jnp.stack(
        [jnp.stack([bias, negs], axis=-1), jnp.stack([negs, bias], axis=-1)], axis=1,
    ).reshape(db, N_KV_HEADS, N_KV_HEADS * width)
    n_phys = cache_k.shape[1]
    il_shape = (1, n_phys, N_KV_HEADS * PAGE_SIZE, HEAD_DIM)
    per_head = lambda a: jnp.repeat(a.reshape(db, N_KV_HEADS, HEAD_DIM), GROUP, axis=1)
    att_s = _decode_attention(
        page_table, q_s[N_META:].reshape(db, N_HEADS, HEAD_DIM), bias_il,
        per_head(kb_s[N_META:]), per_head(vb_s[N_META:]),
        cache_k.reshape(il_shape), cache_v.reshape(il_shape))
    att_s = att_s.reshape(db, ATTN_WIDTH)

    d_s = _pool_decode(jnp.swapaxes(state_pool[0], 0, 1), u_s[N_META:])

    post = functools.partial(
        _post, wo_a=wo_a, wo_p=wo_p, w_pool=w_pool_b, pool_scale=pool_scale[0],
        g_mlp=g_mlp[0], g_final=g_final, w_up=w_up_b, w_down=w_down_b)
    y_p = post(xp, att_p, d_p, tm=512)
    y_s = post(x_sample[:, 0], att_s, d_s, tm=db)

    k_prompt = jnp.concatenate([kf_s[:N_META], kf_p], axis=0).reshape(1, 1, N_META + seq, N_KV_HEADS, HEAD_DIM)
    v_prompt = jnp.concatenate([vf_s[:N_META], vf_p], axis=0).reshape(1, 1, N_META + seq, N_KV_HEADS, HEAD_DIM)
    kidx_prompt = jnp.concatenate([kw_s[:N_META, :IDX_DIM], kw_p[:, :IDX_DIM]], axis=0).reshape(
        1, 1, N_META + seq, IDX_DIM)
    pool_prompt = u_tail[POOL_HALO - POOL_HIST:].reshape(1, 1, POOL_HIST, POOL_WIDTH)
    k_sample = kf_s[N_META:].reshape(1, db, 1, N_KV_HEADS, HEAD_DIM)
    v_sample = vf_s[N_META:].reshape(1, db, 1, N_KV_HEADS, HEAD_DIM)
    kidx_sample = kw_s[N_META:, :IDX_DIM].reshape(1, db, 1, IDX_DIM)
    pool_sample = jnp.concatenate(
        [state_pool[0][:, 1:], u_s[N_META:][:, None, :]], axis=1)[None]
    return (y_p.reshape(1, seq, D_MODEL), y_s.reshape(db, 1, D_MODEL),
            k_prompt, v_prompt, kidx_prompt, pool_prompt,
            k_sample, v_sample, kidx_sample, pool_sample)
```
